```python
import math
import jax, jax.numpy as jnp
from jax import lax
import numpy as np

D_MODEL = 1024
BATCH = 32
SEQ = 256
DEPTH = 2
DEC_BATCH = 2
DEC_SEQ = 1024
PAST_LEN = 512

GRID_W = 64
HEAD_DIM = 64
FOURIER_CH = D_MODEL // 4
FOURIER_GROUPS = 4
FOURIER_GW = FOURIER_CH // FOURIER_GROUPS
NA_DIM = D_MODEL - FOURIER_CH
NA_HEADS = NA_DIM // HEAD_DIM
NA_MAX_ROWS = 8
NA_COLS = 16
EVEN_IN = FOURIER_CH + 3 * NA_DIM
POOL_WINDOWS = (2, 4, 8, 16)
POOL_CH = D_MODEL // 4
POOL_GW = POOL_CH // len(POOL_WINDOWS)
RWKV_DIM = D_MODEL - POOL_CH
RWKV_HEADS = RWKV_DIM // HEAD_DIM
DECAY_LORA = 64
ICLR_LORA = 64
GATE_LORA = 128
RWKV_IN = 3 * RWKV_DIM + 2 * DECAY_LORA + 2 * ICLR_LORA + GATE_LORA
ODD_IN = POOL_CH + RWKV_IN
RW_SPLITS = (RWKV_DIM, 2 * RWKV_DIM, 3 * RWKV_DIM, 3 * RWKV_DIM + DECAY_LORA, 3 * RWKV_DIM + 2 * DECAY_LORA,
             3 * RWKV_DIM + 2 * DECAY_LORA + ICLR_LORA, 3 * RWKV_DIM + 2 * DECAY_LORA + 2 * ICLR_LORA)
D_FF = 2816
N_EXPERTS = 8
TOP_K = 2
D_FF_EXPERT = 1408
N_EVEN = (DEPTH + 1) // 2
N_ODD = DEPTH // 2
Q_BLOCK = 128
RMS_EPS = 1e-6
GN_EPS = 64e-5
L2_EPS = 1e-12
DECAY_SCALE = math.exp(-0.5)
NEG_INF = -1e30

kernel_name = "hybrid_flow_prefix_fourier_natten_pool_rwkv7_step"


def rmsnorm(x, g):
    xf = x.astype(jnp.float32)
    y = xf * lax.rsqrt(jnp.mean(xf * xf, axis=-1, keepdims=True) + RMS_EPS)
    return (y * g.astype(jnp.float32)).astype(x.dtype)


def adaln(cond, w, b):
    m = jax.nn.silu(cond) @ w + b
    return jnp.split(m[:, None, :], 6, axis=-1)


def modulate(x, g, shift, scale):
    return rmsnorm(x, g) * (1.0 + scale) + shift


def swiglu(x, wg, wu, wd):
    return (jax.nn.silu(x @ wg) * (x @ wu)) @ wd


def fourier_mix(f, w):
    b, n, _ = f.shape
    fg = f.astype(jnp.float32).reshape(b, n, FOURIER_GROUPS, FOURIER_GW)
    spec = jnp.fft.fft2(fg, axes=(1, 3), norm="ortho").real
    y = jnp.einsum("bngc,gcd->bngd", spec.astype(f.dtype), w)
    return y.reshape(b, n, FOURIER_CH)


def context_attention(q, k, v):
    b, n, h, d = q.shape
    scale = d ** -0.5
    qb = q.reshape(b, n // Q_BLOCK, Q_BLOCK, h, d).swapaxes(0, 1)

    def block(qi):
        s = jnp.einsum("bqhd,bkhd->bhqk", qi, k).astype(jnp.float32) * scale
        p = jax.nn.softmax(s, axis=-1).astype(v.dtype)
        return jnp.einsum("bhqk,bkhd->bqhd", p, v)

    o = lax.map(block, qb)
    return o.swapaxes(0, 1).reshape(b, n, h, d)


def neighbourhood_attention(q, k, v, ck, cv, rel_bias):
    b, n, h, d = q.shape
    rows = n // GRID_W
    wr = min(NA_MAX_ROWS, rows)
    scale = d ** -0.5
    qg = q.reshape(b, rows, GRID_W, h, d)
    kg = k.reshape(b, rows, GRID_W, h, d)
    vg = v.reshape(b, rows, GRID_W, h, d)
    cols = jnp.arange(GRID_W)
    c0 = jnp.clip(cols - NA_COLS // 2, 0, GRID_W - NA_COLS)
    col_ok = (cols[None, :] >= c0[:, None]) & (cols[None, :] < c0[:, None] + NA_COLS)
    dc = jnp.clip(cols[None, :] - cols[:, None] + NA_COLS - 1, 0, 2 * NA_COLS - 2)

    def row(i):
        r0 = jnp.clip(i - wr // 2, 0, rows - wr)
        kb = lax.dynamic_slice_in_dim(kg, r0, wr, axis=1)
        vb = lax.dynamic_slice_in_dim(vg, r0, wr, axis=1)
        qi = lax.dynamic_index_in_dim(qg, i, axis=1, keepdims=False)
        dr = r0 + jnp.arange(wr) - i + NA_MAX_ROWS - 1
        bias = rel_bias[:, dr[None, :, None], dc[:, None, :]]
        s_loc = jnp.einsum("bqhd,brkhd->bhqrk", qi, kb).astype(jnp.float32) * scale
        s_loc = s_loc + bias[None].astype(jnp.float32)
        s_loc = jnp.where(col_ok[:, None, :], s_loc, NEG_INF).reshape(b, h, GRID_W, wr * GRID_W)
        s_ctx = jnp.einsum("bqhd,bkhd->bhqk", qi, ck).astype(jnp.float32) * scale
        p = jax.nn.softmax(jnp.concatenate([s_loc, s_ctx], axis=-1), axis=-1).astype(v.dtype)
        p_loc = p[..., : wr * GRID_W].reshape(b, h, GRID_W, wr, GRID_W)
        p_ctx = p[..., wr * GRID_W:]
        return (jnp.einsum("bhqrk,brkhd->bqhd", p_loc, vb)
                + jnp.einsum("bhqk,bkhd->bqhd", p_ctx, cv.astype(v.dtype)))

    o = lax.map(row, jnp.arange(rows))
    return o.swapaxes(0, 1).reshape(b, n, h, d)


def even_mixer(h, w_in, f_w, rel_bias, w_out, ctx_kv):
    b, n, _ = h.shape
    proj = h @ w_in
    q, k, v = [t.reshape(b, n, NA_HEADS, HEAD_DIM) for t in jnp.split(proj[..., FOURIER_CH:], 3, axis=-1)]
    a_out = fourier_mix(proj[..., :FOURIER_CH], f_w)
    if ctx_kv is None:
        b_out = context_attention(q, k, v)
    else:
        b_out = neighbourhood_attention(q, k, v, ctx_kv[0], ctx_kv[1], rel_bias)
    out = jnp.concatenate([a_out, b_out.reshape(b, n, NA_DIM)], axis=-1) @ w_out
    return out, k, v


def pool_mix(x, w, scale):
    b, n, _ = x.shape
    xf = x.astype(jnp.float32)
    cs = jnp.concatenate([jnp.zeros((b, 1, POOL_CH), jnp.float32), jnp.cumsum(xf, axis=1)], axis=1)
    t = jnp.arange(n)
    groups = []
    for g, win in enumerate(POOL_WINDOWS):
        lo = jnp.clip(t - win // 2, 0, n)
        hi = jnp.clip(t + win - win // 2, 0, n)
        sl = slice(g * POOL_GW, (g + 1) * POOL_GW)
        mean = (cs[:, hi, sl] - cs[:, lo, sl]) / (hi - lo).astype(jnp.float32)[None, :, None]
        groups.append(mean - xf[:, :, sl])
    y = jnp.stack(groups, axis=2).astype(x.dtype)
    y = jnp.einsum("bngc,gcd->bngd", y, w).reshape(b, n, POOL_CH)
    return y * scale


def centred_shift(x, mu):
    prev = jnp.pad(x[:, :-1], ((0, 0), (1, 0), (0, 0)))
    nxt = jnp.pad(x[:, 1:], ((0, 0), (0, 1), (0, 0)))
    return x + mu[0] * (prev - x) + mu[1] * (nxt - x)


def wkv_scan(r, w, k, v, kk, a, s0, reverse):
    xs = tuple(jnp.swapaxes(t, 0, 1) for t in (r, w, k, v, kk, a))

    def step(s, inp):
        r_t, w_t, k_t, v_t, kk_t, a_t = inp
        sa = jnp.einsum("bhvk,bhk->bhv", s, kk_t)
        s = (s * w_t[:, :, None, :] - sa[..., None] * (kk_t * a_t)[:, :, None, :]
             + v_t[..., None] * k_t[:, :, None, :])
        return s, jnp.einsum("bhvk,bhk->bhv", s, r_t)

    s_fin, o = lax.scan(step, s0, xs, reverse=reverse)
    return jnp.swapaxes(o, 0, 1), s_fin


def rwkv7_bidir(z, s0, decay_w0, decay_up, iclr_a0, iclr_up, gate_up, k_k, k_a, r_k, gn_w, gn_b):
    b, n, _ = z.shape
    f32 = jnp.float32
    r, k, v, wl_f, wl_b, al_f, al_b, gl = jnp.split(z.astype(f32), RW_SPLITS, axis=-1)
    hd = lambda t: t.reshape(b, n, RWKV_HEADS, HEAD_DIM)
    rh, kh, vh = hd(r), hd(k), hd(v)
    kk = hd(k * k_k.astype(f32))
    kk = kk * lax.rsqrt(jnp.sum(kk * kk, axis=-1, keepdims=True) + L2_EPS)
    k_a_h = k_a.astype(f32).reshape(RWKV_HEADS, HEAD_DIM)
    outs, finals = [], []
    for dr, (wl, al) in enumerate(((wl_f, al_f), (wl_b, al_b))):
        w = jnp.exp(-DECAY_SCALE * jax.nn.sigmoid(decay_w0[dr].astype(f32) + jnp.tanh(wl) @ decay_up[dr].astype(f32)))
        a = hd(jax.nn.sigmoid(iclr_a0[dr].astype(f32) + al @ iclr_up[dr].astype(f32)))
        k_mod = kh * (1.0 + (a - 1.0) * k_a_h)
        o_d, s_d = wkv_scan(rh, hd(w), k_mod, vh, kk, a, s0[:, dr].astype(f32), dr == 1)
        outs.append(o_d)
        finals.append(s_d)
    o = outs[0] + outs[1]
    mu = jnp.mean(o, axis=-1, keepdims=True)
    var = jnp.mean(jnp.square(o - mu), axis=-1, keepdims=True)
    o = ((o - mu) * lax.rsqrt(var + GN_EPS)).reshape(b, n, RWKV_DIM) * gn_w.astype(f32) + gn_b.astype(f32)
    bonus = jnp.sum(rh * kh * r_k.astype(f32), axis=-1, keepdims=True) * vh
    g = jax.nn.sigmoid(gl) @ gate_up.astype(f32)
    y = (o + bonus.reshape(b, n, RWKV_DIM)) * g
    return y.astype(z.dtype), jnp.stack(finals, axis=1)


def odd_mixer(h, s0, w_in, pool_w, pool_scale, shift_mu, decay_w0, decay_up, iclr_a0, iclr_up,
              gate_up, k_k, k_a, r_k, gn_w, gn_b, w_out):
    proj = h @ w_in
    c_out = pool_mix(proj[..., :POOL_CH], pool_w, pool_scale)
    zr = centred_shift(proj[..., POOL_CH:], shift_mu)
    d_out, s_fin = rwkv7_bidir(zr, s0, decay_w0, decay_up, iclr_a0, iclr_up, gate_up, k_k, k_a, r_k, gn_w, gn_b)
    out = jnp.concatenate([c_out, d_out.astype(c_out.dtype)], axis=-1) @ w_out
    return out, s_fin


def moe_swiglu(x, router_w, router_b, wg, wu, wd):
    b, n, dm = x.shape
    t = x.reshape(b * n, dm)
    logits = (t @ router_w).astype(jnp.float32) + router_b.astype(jnp.float32)
    top_v, top_i = lax.top_k(logits, TOP_K)
    top_g = jax.nn.softmax(top_v, axis=-1)
    combine = jnp.sum(jax.nn.one_hot(top_i, N_EXPERTS, dtype=jnp.float32) * top_g[..., None], axis=1).astype(x.dtype)
    out = jnp.zeros_like(t)
    for e in range(N_EXPERTS):
        out = out + combine[:, e:e + 1] * swiglu(t, wg[e], wu[e], wd[e])
    return out.reshape(b, n, dm)


def setup_inputs(seed: int = 0) -> dict:
    key = jax.random.key(seed)
    ks = iter(jax.random.split(key, 64))
    f32 = jnp.float32
    nrm = lambda shape, s: jax.random.normal(next(ks), shape, f32) * s
    uni = lambda shape, lo, hi: jax.random.uniform(next(ks), shape, f32, lo, hi)
    D = D_MODEL
    inp = {}
    inp["x_prompt"] = nrm((BATCH, SEQ, D), 1.0)
    inp["x_sample"] = nrm((DEC_BATCH, DEC_SEQ, D), 1.0)
    inp["cache_na_k"] = nrm((DEC_BATCH, N_EVEN, PAST_LEN, NA_HEADS, HEAD_DIM), 1.0)
    inp["cache_na_v"] = nrm((DEC_BATCH, N_EVEN, PAST_LEN, NA_HEADS, HEAD_DIM), 1.0)
    inp["state_wkv"] = nrm((DEC_BATCH, N_ODD, 2, RWKV_HEADS, HEAD_DIM, HEAD_DIM), 1.0)
    inp["c"] = nrm((DEC_BATCH, D), 1.0)
    inp["c_ctx"] = nrm((D,), 1.0)
    inp["mod_w"] = nrm((DEPTH, D, 6 * D), 0.5 * D ** -0.5)
    inp["mod_b"] = nrm((DEPTH, 6 * D), 0.02)
    inp["norm_mix"] = 1.0 + nrm((DEPTH, D), 0.05)
    inp["norm_ffn"] = 1.0 + nrm((DEPTH, D), 0.05)
    inp["norm_final"] = 1.0 + nrm((D,), 0.05)
    inp["na_w_in"] = nrm((N_EVEN, D, EVEN_IN), D ** -0.5)
    inp["fourier_w"] = nrm((N_EVEN, FOURIER_GROUPS, FOURIER_GW, FOURIER_GW), FOURIER_GW ** -0.5)
    inp["na_rel_bias"] = nrm((N_EVEN, NA_HEADS, 2 * NA_MAX_ROWS - 1, 2 * NA_COLS - 1), 0.3)
    inp["na_w_out"] = nrm((N_EVEN, FOURIER_CH + NA_DIM, D), D ** -0.5)
    inp["ffn_w_gate"] = nrm((N_EVEN, D, D_FF), D ** -0.5)
    inp["ffn_w_up"] = nrm((N_EVEN, D, D_FF), D ** -0.5)
    inp["ffn_w_down"] = nrm((N_EVEN, D_FF, D), D_FF ** -0.5)
    inp["rw_w_in"] = nrm((N_ODD, D, ODD_IN), D ** -0.5)
    inp["pool_w"] = nrm((N_ODD, len(POOL_WINDOWS), POOL_GW, POOL_GW), POOL_GW ** -0.5)
    inp["pool_scale"] = 1.0 + nrm((N_ODD, POOL_CH), 0.1)
    inp["shift_mu"] = uni((N_ODD, 2, RWKV_IN), 0.0, 0.5)
    inp["decay_w0"] = uni((N_ODD, 2, RWKV_DIM), -4.0, 2.0)
    inp["decay_up"] = nrm((N_ODD, 2, DECAY_LORA, RWKV_DIM), 0.5 * DECAY_LORA ** -0.5)
    inp["iclr_a0"] = nrm((N_ODD, 2, RWKV_DIM), 0.5)
    inp["iclr_up"] = nrm((N_ODD, 2, ICLR_LORA, RWKV_DIM), 0.5 * ICLR_LORA ** -0.5)
    inp["gate_up"] = nrm((N_ODD, GATE_LORA, RWKV_DIM), GATE_LORA ** -0.5)
    inp["k_k"] = 1.0 + nrm((N_ODD, RWKV_DIM), 0.1)
    inp["k_a"] = 1.0 + nrm((N_ODD, RWKV_DIM), 0.1)
    inp["r_k"] = nrm((N_ODD, RWKV_HEADS, HEAD_DIM), 0.1)
    inp["gn_w"] = 1.0 + nrm((N_ODD, RWKV_DIM), 0.05)
    inp["gn_b"] = nrm((N_ODD, RWKV_DIM), 0.02)
    inp["rw_w_out"] = nrm((N_ODD, POOL_CH + RWKV_DIM, D), D ** -0.5)
    inp["router_w"] = nrm((N_ODD, D, N_EXPERTS), D ** -0.5)
    inp["router_b"] = nrm((N_ODD, N_EXPERTS), 0.01)
    inp["moe_w_gate"] = nrm((N_ODD, N_EXPERTS, D, D_FF_EXPERT), D ** -0.5)
    inp["moe_w_up"] = nrm((N_ODD, N_EXPERTS, D, D_FF_EXPERT), D ** -0.5)
    inp["moe_w_down"] = nrm((N_ODD, N_EXPERTS, D_FF_EXPERT, D), D_FF_EXPERT ** -0.5)
    return inp


def reference(x_prompt, x_sample, cache_na_k, cache_na_v, state_wkv, c, c_ctx, mod_w, mod_b, norm_mix,
              norm_ffn, norm_final, na_w_in, fourier_w, na_rel_bias, na_w_out, ffn_w_gate, ffn_w_up, ffn_w_down,
              rw_w_in, pool_w, pool_scale, shift_mu, decay_w0, decay_up, iclr_a0, iclr_up, gate_up, k_k, k_a,
              r_k, gn_w, gn_b, rw_w_out, router_w, router_b, moe_w_gate, moe_w_up, moe_w_down):
    nb = x_prompt.shape[0]
    cond_ctx = jnp.broadcast_to(c_ctx, (nb, D_MODEL))
    xp, xs = x_prompt, x_sample
    new_k, new_v, new_s = [], [], []
    for l in range(DEPTH):
        j = l // 2
        sh1p, sc1p, g1p, sh2p, sc2p, g2p = adaln(cond_ctx, mod_w[l], mod_b[l])
        sh1s, sc1s, g1s, sh2s, sc2s, g2s = adaln(c, mod_w[l], mod_b[l])
        hp = modulate(xp, norm_mix[l], sh1p, sc1p)
        hs = modulate(xs, norm_mix[l], sh1s, sc1s)
        if l % 2 == 0:
            ep = (na_w_in[j], fourier_w[j], na_rel_bias[j], na_w_out[j])
            mp, kp, vp = even_mixer(hp, *ep, None)
            ms, _, _ = even_mixer(hs, *ep, (cache_na_k[:, j], cache_na_v[:, j]))
            new_k.append(kp)
            new_v.append(vp)
        else:
            op = (rw_w_in[j], pool_w[j], pool_scale[j], shift_mu[j], decay_w0[j], decay_up[j], iclr_a0[j],
                  iclr_up[j], gate_up[j], k_k[j], k_a[j], r_k[j], gn_w[j], gn_b[j], rw_w_out[j])
            s0 = jnp.zeros((nb, 2, RWKV_HEADS, HEAD_DIM, HEAD_DIM), jnp.float32)
            mp, sp = odd_mixer(hp, s0, *op)
            ms, _ = odd_mixer(hs, state_wkv[:, j], *op)
            new_s.append(sp)
        xp = xp + g1p * mp
        xs = xs + g1s * ms
        hp = modulate(xp, norm_ffn[l], sh2p, sc2p)
        hs = modulate(xs, norm_ffn[l], sh2s, sc2s)
        if l % 2 == 0:
            fw = (ffn_w_gate[j], ffn_w_up[j], ffn_w_down[j])
            xp = xp + g2p * swiglu(hp, *fw)
            xs = xs + g2s * swiglu(hs, *fw)
        else:
            mw = (router_w[j], router_b[j], moe_w_gate[j], moe_w_up[j], moe_w_down[j])
            xp = xp + g2p * moe_swiglu(hp, *mw)
            xs = xs + g2s * moe_swiglu(hs, *mw)
    y_prompt = rmsnorm(xp, norm_final)
    y_sample = rmsnorm(xs, norm_final)
    new_cache_na_k = jnp.stack(new_k, axis=1)
    new_cache_na_v = jnp.stack(new_v, axis=1)
    new_state_wkv = jnp.stack(new_s, axis=1)
    return (y_prompt, y_sample, new_cache_na_k, new_cache_na_v, new_state_wkv)
```

```python
import functools
import math

import numpy as np
import jax
import jax.numpy as jnp
from jax import lax
from jax.experimental import pallas as pl
from jax.experimental.pallas import tpu as pltpu

F32 = jnp.float32
BF16 = jnp.bfloat16

D_MODEL = 1024
BATCH = 32
SEQ = 256
DEPTH = 2
DEC_BATCH = 2
DEC_SEQ = 1024
PAST_LEN = 512
GRID_W = 64
HEAD_DIM = 64
FOURIER_CH = D_MODEL // 4
FOURIER_GROUPS = 4
FOURIER_GW = FOURIER_CH // FOURIER_GROUPS
NA_DIM = D_MODEL - FOURIER_CH
NA_HEADS = NA_DIM // HEAD_DIM
NA_MAX_ROWS = 8
NA_COLS = 16
POOL_WINDOWS = (2, 4, 8, 16)
POOL_CH = D_MODEL // 4
POOL_GW = POOL_CH // len(POOL_WINDOWS)
RWKV_DIM = D_MODEL - POOL_CH
RWKV_HEADS = RWKV_DIM // HEAD_DIM
DECAY_LORA = 64
ICLR_LORA = 64
GATE_LORA = 128
RWKV_IN = 3 * RWKV_DIM + 2 * DECAY_LORA + 2 * ICLR_LORA + GATE_LORA
D_FF = 2816
N_EXPERTS = 8
D_FF_EXPERT = 1408
RMS_EPS = 1e-6
GN_EPS = 64e-5
L2_EPS = 1e-12
DECAY_SCALE = math.exp(-0.5)
NEG_INF = -1e30

P_ROWS = BATCH * SEQ
S_ROWS = DEC_BATCH * DEC_SEQ
N_ROWS = P_ROWS + S_ROWS
N_SETS = 1 + DEC_BATCH
LANES = 128
VMEM_LIMIT = 56 * 1024 * 1024


def _cparams(sem):
    return pltpu.CompilerParams(dimension_semantics=sem, vmem_limit_bytes=VMEM_LIMIT)


def _sigmoid(x):
    return 1.0 / (1.0 + jnp.exp(-x))


def _dot(a, b):
    return jnp.dot(a, b, preferred_element_type=F32)


def _dot_nt(a, b):
    return lax.dot_general(a, b, (((1,), (1,)), ((), ())), preferred_element_type=F32)


def _set_index(tm):
    q = DEC_SEQ // tm
    p = P_ROWS // tm
    return lambda i: jnp.maximum(i - p + q, 0) // q


def _modulated(x, g, sh, sc):
    ms = jnp.mean(x * x, axis=-1, keepdims=True)
    return (x * lax.rsqrt(ms + RMS_EPS) * g) * (1.0 + sc) + sh


def _adaln_kernel(c_ref, w_ref, b_ref, o_ref):
    c = c_ref[...]
    s = (c * _sigmoid(c)).astype(BF16)
    o_ref[0] = _dot(s, w_ref[0].astype(BF16)) + b_ref[0]


def adaln_all(cond, mod_w, mod_b):
    tn = 1536
    n = 6 * D_MODEL
    return pl.pallas_call(
        _adaln_kernel,
        grid=(DEPTH, n // tn),
        in_specs=[pl.BlockSpec((8, D_MODEL), lambda l, j: (0, 0)),
                  pl.BlockSpec((1, D_MODEL, tn), lambda l, j: (l, 0, j)),
                  pl.BlockSpec((1, 1, tn), lambda l, j: (l, 0, j))],
        out_specs=pl.BlockSpec((1, 8, tn), lambda l, j: (l, 0, j)),
        out_shape=jax.ShapeDtypeStruct((DEPTH, 8, n), F32),
        compiler_params=_cparams(("parallel", "parallel")),
    )(cond, mod_w, mod_b.reshape(DEPTH, 1, n))


def _modmm_kernel(x_ref, g_ref, sh_ref, sc_ref, w_ref, *o_refs, splits):
    h = _modulated(x_ref[...], g_ref[...], sh_ref[0], sc_ref[0]).astype(BF16)
    for o_ref, (a, b) in zip(o_refs, splits):
        o_ref[...] = _dot(h, w_ref[:, a:b]).astype(o_ref.dtype)


def modulated_matmul(x, g, sh, sc, w, splits, tm=512):
    n_out = w.shape[1]
    si = _set_index(tm)
    vec = pl.BlockSpec((1, 1, D_MODEL), lambda i: (si(i), 0, 0))
    return pl.pallas_call(
        functools.partial(_modmm_kernel, splits=splits),
        grid=(N_ROWS // tm,),
        in_specs=[pl.BlockSpec((tm, D_MODEL), lambda i: (i, 0)),
                  pl.BlockSpec((1, D_MODEL), lambda i: (0, 0)),
                  vec, vec,
                  pl.BlockSpec((D_MODEL, n_out), lambda i: (0, 0))],
        out_specs=[pl.BlockSpec((tm, b - a), lambda i: (i, 0)) for a, b in splits],
        out_shape=[jax.ShapeDtypeStruct((N_ROWS, b - a), F32) for a, b in splits],
        compiler_params=_cparams(("parallel",)),
    )(x, g.reshape(1, D_MODEL), sh, sc, w)


def _dft_mats(n):
    t = np.arange(n)
    ang = 2.0 * np.pi * ((t[:, None] * t[None, :]) % n) / n
    cn, sn = np.cos(ang) / np.sqrt(n), np.sin(ang) / np.sqrt(n)
    c = np.arange(FOURIER_GW)
    angc = 2.0 * np.pi * ((c[:, None] * c[None, :]) % FOURIER_GW) / FOURIER_GW
    eye = np.eye(FOURIER_GROUPS)
    cc = np.kron(eye, np.cos(angc) / np.sqrt(FOURIER_GW))
    sc = np.kron(eye, np.sin(angc) / np.sqrt(FOURIER_GW))
    as_bf = lambda a: jnp.asarray(a, dtype=F32).astype(BF16)
    return as_bf(cn), as_bf(sn), as_bf(cc), as_bf(sc)


def _fourier_kernel(f_ref, cn_ref, sn_ref, cc_ref, sc_ref, w_ref, o_ref):
    x = f_ref[...].astype(BF16)
    a = _dot(x, cc_ref[...]).astype(BF16)
    b = _dot(x, sc_ref[...]).astype(BF16)
    re = _dot(cn_ref[...], a) - _dot(sn_ref[...], b)
    o_ref[...] = _dot(re.astype(BF16), w_ref[...])


def _block_diag(w):
    g, c, _ = w.shape
    eye = jnp.eye(g, dtype=w.dtype)
    return (eye[:, None, :, None] * w[:, :, None, :]).reshape(g * c, g * c)


def fourier_mix(f, n, w_bd):
    rows = f.shape[0]
    cn, sn, cc, sc = _dft_mats(n)
    full = lambda shape: pl.BlockSpec(shape, lambda b: (0, 0))
    return pl.pallas_call(
        _fourier_kernel,
        grid=(rows // n,),
        in_specs=[pl.BlockSpec((n, FOURIER_CH), lambda b: (b, 0)),
                  full((n, n)), full((n, n)),
                  full((FOURIER_CH, FOURIER_CH)), full((FOURIER_CH, FOURIER_CH)),
                  full((FOURIER_CH, FOURIER_CH))],
        out_specs=pl.BlockSpec((n, FOURIER_CH), lambda b: (b, 0)),
        out_shape=jax.ShapeDtypeStruct((rows, FOURIER_CH), F32),
        compiler_params=_cparams(("parallel",)),
    )(f, cn, sn, cc, sc, w_bd)


def _ctx_attn_kernel(q_ref, k_ref, v_ref, o_ref):
    scale = HEAD_DIM ** -0.5
    outs = []
    for h in range(NA_HEADS):
        sl = slice(h * HEAD_DIM, (h + 1) * HEAD_DIM)
        q = q_ref[:, sl].astype(BF16)
        k = k_ref[:, sl].astype(BF16)
        v = v_ref[:, sl].astype(BF16)
        s = _dot_nt(q, k) * scale
        p = jnp.exp(s - jnp.max(s, axis=-1, keepdims=True))
        p = p / jnp.sum(p, axis=-1, keepdims=True)
        outs.append(_dot(p.astype(BF16), v))
    o_ref[...] = jnp.concatenate(outs, axis=-1)


def context_attention(q, k, v):
    blk = pl.BlockSpec((SEQ, NA_DIM), lambda b: (b, 0))
    return pl.pallas_call(
        _ctx_attn_kernel,
        grid=(BATCH,),
        in_specs=[blk, blk, blk],
        out_specs=blk,
        out_shape=jax.ShapeDtypeStruct((P_ROWS, NA_DIM), F32),
        compiler_params=_cparams(("parallel",)),
    )(q, k, v)


NA_ROWS = DEC_SEQ // GRID_W
NA_WIN = NA_MAX_ROWS * GRID_W


def _na_bias_table(rel_bias):
    cols = np.arange(GRID_W)
    c0 = np.clip(cols - NA_COLS // 2, 0, GRID_W - NA_COLS)
    col_ok = (cols[None, :] >= c0[:, None]) & (cols[None, :] < c0[:, None] + NA_COLS)
    dc = np.clip(cols[None, :] - cols[:, None] + NA_COLS - 1, 0, 2 * NA_COLS - 2)
    o = np.arange(NA_MAX_ROWS)
    r = np.arange(NA_MAX_ROWS)
    dr = r[None, :] - o[:, None] + NA_MAX_ROWS - 1
    tab = rel_bias[:, dr[:, :, None, None], dc[None, None, :, :]]
    tab = jnp.where(col_ok[None, None, None], tab, NEG_INF)
    tab = jnp.transpose(tab, (1, 0, 3, 2, 4))
    return tab.reshape(NA_MAX_ROWS, NA_HEADS, GRID_W, NA_WIN)


def _na_row_start(i):
    return jnp.clip(i - NA_MAX_ROWS // 2, 0, NA_ROWS - NA_MAX_ROWS)


def _na_kernel(q_ref, k_ref, v_ref, ck_ref, cv_ref, bias_ref, o_ref):
    scale = HEAD_DIM ** -0.5
    i = pl.program_id(1)
    start = pl.multiple_of(_na_row_start(i) * GRID_W, GRID_W)
    kw = k_ref[pl.ds(start, NA_WIN), :]
    vw = v_ref[pl.ds(start, NA_WIN), :]
    outs = []
    for h in range(NA_HEADS):
        sl = slice(h * HEAD_DIM, (h + 1) * HEAD_DIM)
        q = q_ref[:, sl].astype(BF16)
        s_loc = _dot_nt(q, kw[:, sl].astype(BF16)) * scale + bias_ref[0, h]
        s_ctx = _dot_nt(q, ck_ref[:, sl].astype(BF16)) * scale
        m = jnp.maximum(jnp.max(s_loc, axis=-1, keepdims=True), jnp.max(s_ctx, axis=-1, keepdims=True))
        p_loc = jnp.exp(s_loc - m)
        p_ctx = jnp.exp(s_ctx - m)
        den = jnp.sum(p_loc, axis=-1, keepdims=True) + jnp.sum(p_ctx, axis=-1, keepdims=True)
        outs.append(_dot((p_loc / den).astype(BF16), vw[:, sl].astype(BF16))
                    + _dot((p_ctx / den).astype(BF16), cv_ref[:, sl].astype(BF16)))
    o_ref[...] = jnp.concatenate(outs, axis=-1)


def neighbourhood_attention(q, k, v, ck, cv, bias_tab):
    seq = pl.BlockSpec((DEC_SEQ, NA_DIM), lambda b, i: (b, 0))
    ctx = pl.BlockSpec((PAST_LEN, NA_DIM), lambda b, i: (b, 0))
    row = pl.BlockSpec((GRID_W, NA_DIM), lambda b, i: (b * NA_ROWS + i, 0))
    return pl.pallas_call(
        _na_kernel,
        grid=(DEC_BATCH, NA_ROWS),
        in_specs=[row, seq, seq, ctx, ctx,
                  pl.BlockSpec((1, NA_HEADS, GRID_W, NA_WIN), lambda b, i: (i - _na_row_start(i), 0, 0, 0))],
        out_specs=row,
        out_shape=jax.ShapeDtypeStruct((S_ROWS, NA_DIM), F32),
        compiler_params=_cparams(("parallel", "arbitrary")),
    )(q, k, v, ck, cv, bias_tab)


def _proj_res_kernel(a_ref, b_ref, x_ref, gate_ref, w_ref, o_ref, *, na):
    y = _dot(a_ref[...].astype(BF16), w_ref[:na, :]) + _dot(b_ref[...].astype(BF16), w_ref[na:, :])
    o_ref[...] = x_ref[...] + gate_ref[0] * y


def proj_residual(a, b, x, gate, w, tm=512):
    na, nb = a.shape[1], b.shape[1]
    si = _set_index(tm)
    return pl.pallas_call(
        functools.partial(_proj_res_kernel, na=na),
        grid=(N_ROWS // tm,),
        in_specs=[pl.BlockSpec((tm, na), lambda i: (i, 0)),
                  pl.BlockSpec((tm, nb), lambda i: (i, 0)),
                  pl.BlockSpec((tm, D_MODEL), lambda i: (i, 0)),
                  pl.BlockSpec((1, 1, D_MODEL), lambda i: (si(i), 0, 0)),
                  pl.BlockSpec((na + nb, D_MODEL), lambda i: (0, 0))],
        out_specs=pl.BlockSpec((tm, D_MODEL), lambda i: (i, 0)),
        out_shape=jax.ShapeDtypeStruct((N_ROWS, D_MODEL), F32),
        compiler_params=_cparams(("parallel",)),
    )(a, b, x, gate, w)


def _ffn_kernel(x_ref, g_ref, sh_ref, sc_ref, gate_ref, wg_ref, wu_ref, wd_ref, o_ref, h_scr, acc_scr):
    j = pl.program_id(1)

    @pl.when(j == 0)
    def _():
        h_scr[...] = _modulated(x_ref[...], g_ref[...], sh_ref[0], sc_ref[0]).astype(BF16)
        acc_scr[...] = jnp.zeros_like(acc_scr)

    h = h_scr[...]
    gt = _dot(h, wg_ref[...])
    act = (gt * _sigmoid(gt)) * _dot(h, wu_ref[...])
    acc_scr[...] += _dot(act.astype(BF16), wd_ref[...])

    @pl.when(j == pl.num_programs(1) - 1)
    def _():
        o_ref[...] = x_ref[...] + gate_ref[0] * acc_scr[...]


def ffn_residual(x, g, sh, sc, gate, wg, wu, wd, tm=1024, tf=256):
    si = _set_index(tm)
    vec = pl.BlockSpec((1, 1, D_MODEL), lambda i, j: (si(i), 0, 0))
    return pl.pallas_call(
        _ffn_kernel,
        grid=(N_ROWS // tm, D_FF // tf),
        in_specs=[pl.BlockSpec((tm, D_MODEL), lambda i, j: (i, 0)),
                  pl.BlockSpec((1, D_MODEL), lambda i, j: (0, 0)),
                  vec, vec, vec,
                  pl.BlockSpec((D_MODEL, tf), lambda i, j: (0, j)),
                  pl.BlockSpec((D_MODEL, tf), lambda i, j: (0, j)),
                  pl.BlockSpec((tf, D_MODEL), lambda i, j: (j, 0))],
        out_specs=pl.BlockSpec((tm, D_MODEL), lambda i, j: (i, 0)),
        out_shape=jax.ShapeDtypeStruct((N_ROWS, D_MODEL), F32),
        scratch_shapes=[pltpu.VMEM((tm, D_MODEL), BF16), pltpu.VMEM((tm, D_MODEL), F32)],
        compiler_params=_cparams(("parallel", "arbitrary")),
    )(x, g.reshape(1, D_MODEL), sh, sc, gate, wg, wu, wd)


def _pool_consts(n):
    t = np.arange(n)
    mats, cnts = [], []
    for win in POOL_WINDOWS:
        lo = np.clip(t - win // 2, 0, n)
        hi = np.clip(t + win - win // 2, 0, n)
        mats.append(((t[None, :] >= lo[:, None]) & (t[None, :] < hi[:, None])).astype(np.float32))
        cnts.append(np.repeat((hi - lo).astype(np.float32)[:, None], POOL_GW, axis=1))
    return jnp.asarray(np.stack(mats)).astype(BF16), jnp.asarray(np.concatenate(cnts, axis=1))


def _pool_kernel(x_ref, pm_ref, cnt_ref, w_ref, scale_ref, o_ref):
    x = x_ref[...]
    hi = x.astype(BF16)
    lo = (x - hi.astype(F32)).astype(BF16)
    sums = []
    for g in range(len(POOL_WINDOWS)):
        sl = slice(g * POOL_GW, (g + 1) * POOL_GW)
        sums.append(_dot(pm_ref[g], hi[:, sl]) + _dot(pm_ref[g], lo[:, sl]))
    y = jnp.concatenate(sums, axis=-1) / cnt_ref[...] - x
    o_ref[...] = _dot(y.astype(BF16), w_ref[...]) * scale_ref[...]


def pool_mix(x, n, w_bd, scale):
    rows = x.shape[0]
    pm, cnt = _pool_consts(n)
    return pl.pallas_call(
        _pool_kernel,
        grid=(rows // n,),
        in_specs=[pl.BlockSpec((n, POOL_CH), lambda b: (b, 0)),
                  pl.BlockSpec((len(POOL_WINDOWS), n, n), lambda b: (0, 0, 0)),
                  pl.BlockSpec((n, POOL_CH), lambda b: (0, 0)),
                  pl.BlockSpec((POOL_CH, POOL_CH), lambda b: (0, 0)),
                  pl.BlockSpec((1, POOL_CH), lambda b: (0, 0))],
        out_specs=pl.BlockSpec((n, POOL_CH), lambda b: (b, 0)),
        out_shape=jax.ShapeDtypeStruct((rows, POOL_CH), F32),
        compiler_params=_cparams(("parallel",)),
    )(x, pm, cnt, w_bd, scale.reshape(1, POOL_CH))


RW_TILE = 256
HALO = 8


def _head_ones():
    h = np.arange(RWKV_DIM) // HEAD_DIM
    return jnp.asarray((h[:, None] == h[None, :]).astype(np.float32)).astype(BF16)


def _head_sum(x, ones):
    hi = x.astype(BF16)
    lo = (x - hi.astype(F32)).astype(BF16)
    return _dot(hi, ones) + _dot(lo, ones)


def _rwkv_prep_kernel(z_ref, zp_ref, zn_ref, mu_ref, kk_w_ref, ka_ref, rk_ref, w0_ref, a0_ref,
                      dup_ref, iup_ref, gup_ref, ones_ref,
                      r_ref, v_ref, kk_ref, wf_ref, kmf_ref, bf_ref, wb_ref, kmb_ref, bb_ref, g_ref, bonus_ref,
                      *, tiles_per_seq):
    i = pl.program_id(0)
    pos = i % tiles_per_seq
    z = z_ref[...]
    row = lax.broadcasted_iota(jnp.int32, (RW_TILE, 1), 0)
    prev_edge = jnp.where(pos == 0, 0.0, zp_ref[HALO - 1:HALO, :])
    next_edge = jnp.where(pos == tiles_per_seq - 1, 0.0, zn_ref[0:1, :])
    prev = jnp.where(row == 0, prev_edge, pltpu.roll(z, 1, 0))
    nxt = jnp.where(row == RW_TILE - 1, next_edge, pltpu.roll(z, RW_TILE - 1, 0))
    zr = z + mu_ref[0:1, :] * (prev - z) + mu_ref[1:2, :] * (nxt - z)

    d = RWKV_DIM
    r, k, v = zr[:, :d], zr[:, d:2 * d], zr[:, 2 * d:3 * d]
    lora = 3 * d
    ones = ones_ref[...]
    kk = k * kk_w_ref[...]
    kk = kk * lax.rsqrt(_head_sum(kk * kk, ones) + L2_EPS)
    r_ref[...] = r
    v_ref[...] = v
    kk_ref[...] = kk
    outs = ((wf_ref, kmf_ref, bf_ref), (wb_ref, kmb_ref, bb_ref))
    for dr, (w_o, km_o, b_o) in enumerate(outs):
        wl = zr[:, lora + dr * DECAY_LORA: lora + (dr + 1) * DECAY_LORA]
        al = zr[:, lora + 2 * DECAY_LORA + dr * ICLR_LORA: lora + 2 * DECAY_LORA + (dr + 1) * ICLR_LORA]
        lw = w0_ref[dr:dr + 1, :] + _dot(jnp.tanh(wl).astype(BF16), dup_ref[dr])
        w_o[...] = jnp.exp(-DECAY_SCALE * _sigmoid(lw))
        a = _sigmoid(a0_ref[dr:dr + 1, :] + _dot(al.astype(BF16), iup_ref[dr]))
        km_o[...] = k * (1.0 + (a - 1.0) * ka_ref[...])
        b_o[...] = kk * a
    gl = zr[:, lora + 2 * DECAY_LORA + 2 * ICLR_LORA:]
    g_ref[...] = _dot(_sigmoid(gl).astype(BF16), gup_ref[...])
    bonus_ref[...] = _head_sum(r * k * rk_ref[...], ones) * v


def rwkv_prep(z, n, mu, k_k, k_a, r_k, w0, a0, dup, iup, gup):
    rows = z.shape[0]
    tps = n // RW_TILE
    hb = RW_TILE // HALO
    last = rows // HALO - 1
    d = RWKV_DIM
    full2 = lambda shape: pl.BlockSpec(shape, lambda i: (0, 0))
    full3 = lambda shape: pl.BlockSpec(shape, lambda i: (0, 0, 0))
    tile = pl.BlockSpec((RW_TILE, d), lambda i: (i, 0))
    return pl.pallas_call(
        functools.partial(_rwkv_prep_kernel, tiles_per_seq=tps),
        grid=(rows // RW_TILE,),
        in_specs=[pl.BlockSpec((RW_TILE, RWKV_IN), lambda i: (i, 0)),
                  pl.BlockSpec((HALO, RWKV_IN), lambda i: (jnp.maximum(i * hb - 1, 0), 0)),
                  pl.BlockSpec((HALO, RWKV_IN), lambda i: (jnp.minimum((i + 1) * hb, last), 0)),
                  full2((2, RWKV_IN)), full2((1, d)), full2((1, d)), full2((1, d)),
                  full2((2, d)), full2((2, d)),
                  full3((2, DECAY_LORA, d)), full3((2, ICLR_LORA, d)), full2((GATE_LORA, d)),
                  full2((d, d))],
        out_specs=[tile] * 11,
        out_shape=[jax.ShapeDtypeStruct((rows, d), F32)] * 11,
        compiler_params=_cparams(("parallel",)),
    )(z, z, z, mu, k_k.reshape(1, d), k_a.reshape(1, d), r_k.reshape(1, d), w0, a0,
      dup.astype(BF16), iup.astype(BF16), gup.astype(BF16), _head_ones())


SCAN_TC = 32


def _scan_kernel(r_ref, w_ref, km_ref, kk_ref, bb_ref, v_ref, s0_ref, o_ref, st_ref, s_scr, *, vs):
    c = pl.program_id(1)

    @pl.when(c == 0)
    def _():
        s_scr[...] = s0_ref[...]

    def step(t, carry):
        sa = jnp.zeros((vs, LANES), F32)
        for k in range(HEAD_DIM):
            sa = sa + s_scr[k] * kk_ref[t, k:k + 1, :]
        v_t = v_ref[t]
        o = jnp.zeros((vs, LANES), F32)
        for k in range(HEAD_DIM):
            s_new = (s_scr[k] * w_ref[t, k:k + 1, :] - sa * bb_ref[t, k:k + 1, :]
                     + v_t * km_ref[t, k:k + 1, :])
            s_scr[k] = s_new
            o = o + s_new * r_ref[t, k:k + 1, :]
        o_ref[t] = o
        return carry

    lax.fori_loop(0, SCAN_TC, step, 0)

    @pl.when(c == pl.num_programs(1) - 1)
    def _():
        st_ref[...] = s_scr[...]


def wkv_scan(r, w, km, kk, bb, v, s0):
    t_len, vs, c = v.shape
    kin = pl.BlockSpec((SCAN_TC, HEAD_DIM, LANES), lambda g, s: (s, 0, g))
    vio = pl.BlockSpec((SCAN_TC, vs, LANES), lambda g, s: (s, 0, g))
    st = pl.BlockSpec((HEAD_DIM, vs, LANES), lambda g, s: (0, 0, g))
    return pl.pallas_call(
        functools.partial(_scan_kernel, vs=vs),
        grid=(c // LANES, t_len // SCAN_TC),
        in_specs=[kin, kin, kin, kin, kin, vio, st],
        out_specs=[vio, st],
        out_shape=[jax.ShapeDtypeStruct((t_len, vs, c), F32),
                   jax.ShapeDtypeStruct((HEAD_DIM, vs, c), F32)],
        scratch_shapes=[pltpu.VMEM((HEAD_DIM, vs, LANES), F32)],
        compiler_params=_cparams(("parallel", "arbitrary")),
    )(r, w, km, kk, bb, v, s0)


def _to_chain_lanes(x, b, n):
    x = x.reshape(b, n, RWKV_HEADS, HEAD_DIM)
    return jnp.transpose(x, (1, 3, 0, 2)).reshape(n, HEAD_DIM, b * RWKV_HEADS)


def _from_chain_lanes(x, b, n):
    x = x.reshape(n, HEAD_DIM, b, RWKV_HEADS)
    return jnp.transpose(x, (2, 0, 3, 1)).reshape(b * n, RWKV_DIM)


def _bidir(fwd, bwd):
    return jnp.concatenate([fwd, jnp.flip(bwd, axis=0)], axis=-1)


def rwkv_scan_stream(terms, b, n, s0, split_v):
    r, v, kk, wf, kmf, bf, wb, kmb, bb = [_to_chain_lanes(t, b, n) for t in terms]
    nc = 2 * b * RWKV_HEADS
    ks = [_bidir(r, r), _bidir(wf, wb), _bidir(kmf, kmb), _bidir(kk, kk), _bidir(bf, bb)]
    vv = _bidir(v, v)
    if s0 is None:
        st0 = jnp.zeros((HEAD_DIM, HEAD_DIM, nc), F32)
    else:
        st0 = jnp.transpose(s0, (4, 3, 1, 0, 2)).reshape(HEAD_DIM, HEAD_DIM, nc)
    vs = HEAD_DIM
    if split_v:
        vs = HEAD_DIM // 2
        ks = [jnp.concatenate([t, t], axis=-1) for t in ks]
        halves = lambda t: jnp.concatenate([t[:, :vs], t[:, vs:]], axis=-1)
        vv, st0 = halves(vv), halves(st0)
    lanes = ks[0].shape[-1]
    pad = (-lanes) % LANES
    padl = lambda t: jnp.pad(t, ((0, 0), (0, 0), (0, pad)))
    o, st = wkv_scan(*[padl(t) for t in ks], padl(vv), padl(st0))
    o, st = o[..., :lanes], st[..., :lanes]
    if split_v:
        o = jnp.concatenate([o[..., :nc], o[..., nc:]], axis=1)
        st = jnp.concatenate([st[..., :nc], st[..., nc:]], axis=1)
    half = nc // 2
    o_f = _from_chain_lanes(o[..., :half], b, n)
    o_b = _from_chain_lanes(jnp.flip(o[..., half:], axis=0), b, n)
    st = jnp.transpose(st.reshape(HEAD_DIM, HEAD_DIM, 2, b, RWKV_HEADS), (3, 2, 4, 1, 0))
    return o_f, o_b, st


def _rwkv_post_kernel(of_ref, ob_ref, g_ref, bonus_ref, gw_ref, gb_ref, ones_ref, y_ref):
    ones = ones_ref[...]
    o = of_ref[...] + ob_ref[...]
    mu = _head_sum(o, ones) / HEAD_DIM
    oc = o - mu
    var = _head_sum(oc * oc, ones) / HEAD_DIM
    on = (oc * lax.rsqrt(var + GN_EPS)) * gw_ref[...] + gb_ref[...]
    y_ref[...] = (on + bonus_ref[...]) * g_ref[...]


def rwkv_post(o_f, o_b, g, bonus, gn_w, gn_b):
    rows, d = o_f.shape
    tile = pl.BlockSpec((RW_TILE, d), lambda i: (i, 0))
    vec = pl.BlockSpec((1, d), lambda i: (0, 0))
    return pl.pallas_call(
        _rwkv_post_kernel,
        grid=(rows // RW_TILE,),
        in_specs=[tile, tile, tile, tile, vec, vec, pl.BlockSpec((d, d), lambda i: (0, 0))],
        out_specs=tile,
        out_shape=jax.ShapeDtypeStruct((rows, d), F32),
        compiler_params=_cparams(("parallel",)),
    )(o_f, o_b, g, bonus, gn_w.reshape(1, d), gn_b.reshape(1, d), _head_ones())


def _split3(x):
    a = x.astype(BF16)
    r1 = x - a.astype(F32)
    b = r1.astype(BF16)
    c = (r1 - b.astype(F32)).astype(BF16)
    return a, b, c


def _router_kernel(x_ref, g_ref, sh_ref, sc_ref, w_ref, b_ref, h_ref, comb_ref):
    h = _modulated(x_ref[...], g_ref[...], sh_ref[0], sc_ref[0])
    h_ref[...] = h.astype(BF16)
    h1, h2, h3 = _split3(h)
    w1, w2, w3 = _split3(w_ref[...])
    logits = (_dot(h1, w1) + (_dot(h1, w2) + _dot(h2, w1))
              + (_dot(h1, w3) + _dot(h2, w2) + _dot(h3, w1))) + b_ref[...]
    col = lax.broadcasted_iota(jnp.int32, logits.shape, 1)
    logits = jnp.where(col < N_EXPERTS, logits, -jnp.inf)
    m1 = jnp.max(logits, axis=-1, keepdims=True)
    i1 = jnp.min(jnp.where(logits == m1, col, LANES), axis=-1, keepdims=True)
    rest = jnp.where(col == i1, -jnp.inf, logits)
    m2 = jnp.max(rest, axis=-1, keepdims=True)
    i2 = jnp.min(jnp.where(rest == m2, col, LANES), axis=-1, keepdims=True)
    e2 = jnp.exp(m2 - m1)
    den = 1.0 + e2
    comb_ref[...] = jnp.where(col == i1, 1.0 / den, 0.0) + jnp.where(col == i2, e2 / den, 0.0)


def moe_router(x, g, sh, sc, router_w, router_b, tm=512):
    si = _set_index(tm)
    vec = pl.BlockSpec((1, 1, D_MODEL), lambda i: (si(i), 0, 0))
    w = jnp.pad(router_w, ((0, 0), (0, LANES - N_EXPERTS)))
    b = jnp.pad(router_b, (0, LANES - N_EXPERTS)).reshape(1, LANES)
    return pl.pallas_call(
        _router_kernel,
        grid=(N_ROWS // tm,),
        in_specs=[pl.BlockSpec((tm, D_MODEL), lambda i: (i, 0)),
                  pl.BlockSpec((1, D_MODEL), lambda i: (0, 0)),
                  vec, vec,
                  pl.BlockSpec((D_MODEL, LANES), lambda i: (0, 0)),
                  pl.BlockSpec((1, LANES), lambda i: (0, 0))],
        out_specs=[pl.BlockSpec((tm, D_MODEL), lambda i: (i, 0)),
                   pl.BlockSpec((tm, LANES), lambda i: (i, 0))],
        out_shape=[jax.ShapeDtypeStruct((N_ROWS, D_MODEL), BF16),
                   jax.ShapeDtypeStruct((N_ROWS, LANES), F32)],
        compiler_params=_cparams(("parallel",)),
    )(x, g.reshape(1, D_MODEL), sh, sc, w, b)


def _moe_kernel(x_ref, h_ref, comb_ref, gate_ref, gfin_ref, wg_ref, wu_ref, wd_ref, o_ref, acc_scr):
    e = pl.program_id(1)

    @pl.when(e == 0)
    def _():
        acc_scr[...] = jnp.zeros_like(acc_scr)

    comb = comb_ref[...]
    col = lax.broadcasted_iota(jnp.int32, comb.shape, 1)
    ce = jnp.sum(jnp.where(col == e, comb, 0.0), axis=-1, keepdims=True)
    h = h_ref[...]
    gt = _dot(h, wg_ref[0])
    act = (gt * _sigmoid(gt)) * _dot(h, wu_ref[0])
    acc_scr[...] += ce * _dot(act.astype(BF16), wd_ref[0])

    @pl.when(e == pl.num_programs(1) - 1)
    def _():
        y = x_ref[...] + gate_ref[0] * acc_scr[...]
        ms = jnp.mean(y * y, axis=-1, keepdims=True)
        o_ref[...] = y * lax.rsqrt(ms + RMS_EPS) * gfin_ref[...]


def moe_residual_norm(x, h, comb, gate, g_final, wg, wu, wd, tm=512):
    si = _set_index(tm)
    return pl.pallas_call(
        _moe_kernel,
        grid=(N_ROWS // tm, N_EXPERTS),
        in_specs=[pl.BlockSpec((tm, D_MODEL), lambda i, e: (i, 0)),
                  pl.BlockSpec((tm, D_MODEL), lambda i, e: (i, 0)),
                  pl.BlockSpec((tm, LANES), lambda i, e: (i, 0)),
                  pl.BlockSpec((1, 1, D_MODEL), lambda i, e: (si(i), 0, 0)),
                  pl.BlockSpec((1, D_MODEL), lambda i, e: (0, 0)),
                  pl.BlockSpec((1, D_MODEL, D_FF_EXPERT), lambda i, e: (e, 0, 0)),
                  pl.BlockSpec((1, D_MODEL, D_FF_EXPERT), lambda i, e: (e, 0, 0)),
                  pl.BlockSpec((1, D_FF_EXPERT, D_MODEL), lambda i, e: (e, 0, 0))],
        out_specs=pl.BlockSpec((tm, D_MODEL), lambda i, e: (i, 0)),
        out_shape=jax.ShapeDtypeStruct((N_ROWS, D_MODEL), F32),
        scratch_shapes=[pltpu.VMEM((tm, D_MODEL), F32)],
        compiler_params=_cparams(("parallel", "arbitrary")),
    )(x, h, comb, gate, g_final.reshape(1, D_MODEL), wg, wu, wd)


def kernel(x_prompt, x_sample, cache_na_k, cache_na_v, state_wkv, c, c_ctx, mod_w, mod_b, norm_mix, norm_ffn, norm_final, na_w_in, fourier_w, na_rel_bias, na_w_out, ffn_w_gate, ffn_w_up, ffn_w_down, rw_w_in, pool_w, pool_scale, shift_mu, decay_w0, decay_up, iclr_a0, iclr_up, gate_up, k_k, k_a, r_k, gn_w, gn_b, rw_w_out, router_w, router_b, moe_w_gate, moe_w_up, moe_w_down):
    x = jnp.concatenate([x_prompt.reshape(P_ROWS, D_MODEL), x_sample.reshape(S_ROWS, D_MODEL)], axis=0)
    cond = jnp.concatenate([c_ctx[None, :], c, jnp.zeros((8 - N_SETS, D_MODEL), F32)], axis=0)
    mods = adaln_all(cond, mod_w, mod_b)[:, :N_SETS].reshape(DEPTH, N_SETS, 6, 1, D_MODEL)

    sh1, sc1, g1, sh2, sc2, g2 = [mods[0, :, m] for m in range(6)]
    splits = ((0, FOURIER_CH), (FOURIER_CH, FOURIER_CH + NA_DIM),
              (FOURIER_CH + NA_DIM, FOURIER_CH + 2 * NA_DIM), (FOURIER_CH + 2 * NA_DIM, FOURIER_CH + 3 * NA_DIM))
    f, q, k, v = modulated_matmul(x, norm_mix[0], sh1, sc1, na_w_in[0].astype(BF16), splits)
    f_bd = _block_diag(fourier_w[0]).astype(BF16)
    a_out = jnp.concatenate([fourier_mix(f[:P_ROWS], SEQ, f_bd), fourier_mix(f[P_ROWS:], DEC_SEQ, f_bd)], axis=0)
    ck = cache_na_k[:, 0].reshape(DEC_BATCH * PAST_LEN, NA_DIM)
    cv = cache_na_v[:, 0].reshape(DEC_BATCH * PAST_LEN, NA_DIM)
    b_out = jnp.concatenate([
        context_attention(q[:P_ROWS], k[:P_ROWS], v[:P_ROWS]),
        neighbourhood_attention(q[P_ROWS:], k[P_ROWS:], v[P_ROWS:], ck, cv, _na_bias_table(na_rel_bias[0])),
    ], axis=0)
    new_k = k[:P_ROWS].reshape(BATCH, 1, SEQ, NA_HEADS, HEAD_DIM)
    new_v = v[:P_ROWS].reshape(BATCH, 1, SEQ, NA_HEADS, HEAD_DIM)
    x = proj_residual(a_out, b_out, x, g1, na_w_out[0].astype(BF16))
    x = ffn_residual(x, norm_ffn[0], sh2, sc2, g2, ffn_w_gate[0].astype(BF16), ffn_w_up[0].astype(BF16),
                     ffn_w_down[0].astype(BF16))

    sh1, sc1, g1, sh2, sc2, g2 = [mods[1, :, m] for m in range(6)]
    pc, z = modulated_matmul(x, norm_mix[1], sh1, sc1, rw_w_in[0].astype(BF16),
                             ((0, POOL_CH), (POOL_CH, POOL_CH + RWKV_IN)))
    p_bd = _block_diag(pool_w[0]).astype(BF16)
    c_out = jnp.concatenate([pool_mix(pc[:P_ROWS], SEQ, p_bd, pool_scale[0]),
                             pool_mix(pc[P_ROWS:], DEC_SEQ, p_bd, pool_scale[0])], axis=0)
    rw = (shift_mu[0], k_k[0], k_a[0], r_k[0], decay_w0[0], iclr_a0[0], decay_up[0], iclr_up[0], gate_up[0])
    tp = rwkv_prep(z[:P_ROWS], SEQ, *rw)
    ts = rwkv_prep(z[P_ROWS:], DEC_SEQ, *rw)
    ofp, obp, st_p = rwkv_scan_stream(tp[:9], BATCH, SEQ, None, split_v=False)
    ofs, obs, _ = rwkv_scan_stream(ts[:9], DEC_BATCH, DEC_SEQ, state_wkv[:, 0], split_v=True)
    d_out = jnp.concatenate([rwkv_post(ofp, obp, tp[9], tp[10], gn_w[0], gn_b[0]),
                             rwkv_post(ofs, obs, ts[9], ts[10], gn_w[0], gn_b[0])], axis=0)
    x = proj_residual(c_out, d_out, x, g1, rw_w_out[0].astype(BF16))
    h, comb = moe_router(x, norm_ffn[1], sh2, sc2, router_w[0], router_b[0])
    y = moe_residual_norm(x, h, comb, g2, norm_final, moe_w_gate[0].astype(BF16), moe_w_up[0].astype(BF16),
                          moe_w_down[0].astype(BF16))

    y_prompt = y[:P_ROWS].reshape(BATCH, SEQ, D_MODEL)
    y_sample = y[P_ROWS:].reshape(DEC_BATCH, DEC_SEQ, D_MODEL)
    return (y_prompt, y_sample, new_k, new_v, st_p[:, None])
```

```python
import functools
import math

import numpy as np
import jax
import jax.numpy as jnp
from jax import lax
from jax.experimental import pallas as pl
from jax.experimental.pallas import tpu as pltpu

F32 = jnp.float32
BF16 = jnp.bfloat16

D_MODEL = 1024
BATCH = 32
SEQ = 256
DEPTH = 2
DEC_BATCH = 2
DEC_SEQ = 1024
PAST_LEN = 512
GRID_W = 64
HEAD_DIM = 64
FOURIER_CH = D_MODEL // 4
FOURIER_GROUPS = 4
FOURIER_GW = FOURIER_CH // FOURIER_GROUPS
NA_DIM = D_MODEL - FOURIER_CH
NA_HEADS = NA_DIM // HEAD_DIM
NA_MAX_ROWS = 8
NA_COLS = 16
POOL_WINDOWS = (2, 4, 8, 16)
POOL_CH = D_MODEL // 4
POOL_GW = POOL_CH // len(POOL_WINDOWS)
RWKV_DIM = D_MODEL - POOL_CH
RWKV_HEADS = RWKV_DIM // HEAD_DIM
DECAY_LORA = 64
ICLR_LORA = 64
GATE_LORA = 128
RWKV_IN = 3 * RWKV_DIM + 2 * DECAY_LORA + 2 * ICLR_LORA + GATE_LORA
D_FF = 2816
N_EXPERTS = 8
D_FF_EXPERT = 1408
RMS_EPS = 1e-6
GN_EPS = 64e-5
L2_EPS = 1e-12
DECAY_SCALE = math.exp(-0.5)
NEG_INF = -1e30

P_ROWS = BATCH * SEQ
S_ROWS = DEC_BATCH * DEC_SEQ
N_ROWS = P_ROWS + S_ROWS
N_SETS = 1 + DEC_BATCH
LANES = 128
VMEM_LIMIT = 56 * 1024 * 1024


def _cparams(sem):
    return pltpu.CompilerParams(dimension_semantics=sem, vmem_limit_bytes=VMEM_LIMIT)


def _sigmoid(x):
    return 1.0 / (1.0 + jnp.exp(-x))


def _dot(a, b):
    return jnp.dot(a, b, preferred_element_type=F32)


def _dot_nt(a, b):
    return lax.dot_general(a, b, (((1,), (1,)), ((), ())), preferred_element_type=F32)


def _set_index(tm):
    q = DEC_SEQ // tm
    p = P_ROWS // tm
    return lambda i: jnp.maximum(i - p + q, 0) // q


def _modulated(x, g, sh, sc):
    ms = jnp.mean(x * x, axis=-1, keepdims=True)
    return (x * lax.rsqrt(ms + RMS_EPS) * g) * (1.0 + sc) + sh


def _adaln_kernel(c_ref, w_ref, b_ref, o_ref):
    c = c_ref[...]
    s = (c * _sigmoid(c)).astype(BF16)
    o_ref[0] = _dot(s, w_ref[0].astype(BF16)) + b_ref[0]


def adaln_all(cond, mod_w, mod_b):
    tn = 1536
    n = 6 * D_MODEL
    return pl.pallas_call(
        _adaln_kernel, name="adaln",
        grid=(DEPTH, n // tn),
        in_specs=[pl.BlockSpec((8, D_MODEL), lambda l, j: (0, 0)),
                  pl.BlockSpec((1, D_MODEL, tn), lambda l, j: (l, 0, j)),
                  pl.BlockSpec((1, 1, tn), lambda l, j: (l, 0, j))],
        out_specs=pl.BlockSpec((1, 8, tn), lambda l, j: (l, 0, j)),
        out_shape=jax.ShapeDtypeStruct((DEPTH, 8, n), F32),
        compiler_params=_cparams(("parallel", "parallel")),
    )(cond, mod_w, mod_b.reshape(DEPTH, 1, n))


def _modmm_kernel(x_ref, g_ref, sh_ref, sc_ref, w_ref, *o_refs, splits):
    h = _modulated(x_ref[...], g_ref[...], sh_ref[0], sc_ref[0]).astype(BF16)
    for o_ref, (a, b) in zip(o_refs, splits):
        o_ref[...] = _dot(h, w_ref[:, a:b]).astype(o_ref.dtype)


def modulated_matmul(x, g, sh, sc, w, splits, tm=512):
    n_out = w.shape[1]
    si = _set_index(tm)
    vec = pl.BlockSpec((1, 1, D_MODEL), lambda i: (si(i), 0, 0))
    return pl.pallas_call(
        functools.partial(_modmm_kernel, splits=splits), name="modulated_matmul",
        grid=(N_ROWS // tm,),
        in_specs=[pl.BlockSpec((tm, D_MODEL), lambda i: (i, 0)),
                  pl.BlockSpec((1, D_MODEL), lambda i: (0, 0)),
                  vec, vec,
                  pl.BlockSpec((D_MODEL, n_out), lambda i: (0, 0))],
        out_specs=[pl.BlockSpec((tm, b - a), lambda i: (i, 0)) for a, b in splits],
        out_shape=[jax.ShapeDtypeStruct((N_ROWS, b - a), F32) for a, b in splits],
        compiler_params=_cparams(("parallel",)),
    )(x, g.reshape(1, D_MODEL), sh, sc, w)


def _dft_mats(n):
    t = np.arange(n)
    ang = 2.0 * np.pi * ((t[:, None] * t[None, :]) % n) / n
    cn, sn = np.cos(ang) / np.sqrt(n), np.sin(ang) / np.sqrt(n)
    c = np.arange(FOURIER_GW)
    angc = 2.0 * np.pi * ((c[:, None] * c[None, :]) % FOURIER_GW) / FOURIER_GW
    eye = np.eye(FOURIER_GROUPS)
    cc = np.kron(eye, np.cos(angc) / np.sqrt(FOURIER_GW))
    sc = np.kron(eye, np.sin(angc) / np.sqrt(FOURIER_GW))
    as_bf = lambda a: jnp.asarray(a, dtype=F32).astype(BF16)
    return as_bf(cn), as_bf(sn), as_bf(cc), as_bf(sc)


def _fourier_kernel(f_ref, cn_ref, sn_ref, cc_ref, sc_ref, w_ref, o_ref):
    x = f_ref[...].astype(BF16)
    a = _dot(x, cc_ref[...]).astype(BF16)
    b = _dot(x, sc_ref[...]).astype(BF16)
    re = _dot(cn_ref[...], a) - _dot(sn_ref[...], b)
    o_ref[...] = _dot(re.astype(BF16), w_ref[...])


def _block_diag(w):
    g, c, _ = w.shape
    eye = jnp.eye(g, dtype=w.dtype)
    return (eye[:, None, :, None] * w[:, :, None, :]).reshape(g * c, g * c)


def fourier_mix(f, n, w_bd):
    rows = f.shape[0]
    cn, sn, cc, sc = _dft_mats(n)
    full = lambda shape: pl.BlockSpec(shape, lambda b: (0, 0))
    return pl.pallas_call(
        _fourier_kernel, name="fourier",
        grid=(rows // n,),
        in_specs=[pl.BlockSpec((n, FOURIER_CH), lambda b: (b, 0)),
                  full((n, n)), full((n, n)),
                  full((FOURIER_CH, FOURIER_CH)), full((FOURIER_CH, FOURIER_CH)),
                  full((FOURIER_CH, FOURIER_CH))],
        out_specs=pl.BlockSpec((n, FOURIER_CH), lambda b: (b, 0)),
        out_shape=jax.ShapeDtypeStruct((rows, FOURIER_CH), F32),
        compiler_params=_cparams(("parallel",)),
    )(f, cn, sn, cc, sc, w_bd)


def _ctx_attn_kernel(q_ref, k_ref, v_ref, o_ref):
    scale = HEAD_DIM ** -0.5
    outs = []
    for h in range(NA_HEADS):
        sl = slice(h * HEAD_DIM, (h + 1) * HEAD_DIM)
        q = q_ref[:, sl].astype(BF16)
        k = k_ref[:, sl].astype(BF16)
        v = v_ref[:, sl].astype(BF16)
        s = _dot_nt(q, k) * scale
        p = jnp.exp(s - jnp.max(s, axis=-1, keepdims=True))
        p = p / jnp.sum(p, axis=-1, keepdims=True)
        outs.append(_dot(p.astype(BF16), v))
    o_ref[...] = jnp.concatenate(outs, axis=-1)


def context_attention(q, k, v):
    blk = pl.BlockSpec((SEQ, NA_DIM), lambda b: (b, 0))
    return pl.pallas_call(
        _ctx_attn_kernel, name="ctx_attn",
        grid=(BATCH,),
        in_specs=[blk, blk, blk],
        out_specs=blk,
        out_shape=jax.ShapeDtypeStruct((P_ROWS, NA_DIM), F32),
        compiler_params=_cparams(("parallel",)),
    )(q, k, v)


NA_ROWS = DEC_SEQ // GRID_W
NA_WIN = NA_MAX_ROWS * GRID_W


def _na_bias_table(rel_bias):
    cols = np.arange(GRID_W)
    c0 = np.clip(cols - NA_COLS // 2, 0, GRID_W - NA_COLS)
    col_ok = (cols[None, :] >= c0[:, None]) & (cols[None, :] < c0[:, None] + NA_COLS)
    dc = np.clip(cols[None, :] - cols[:, None] + NA_COLS - 1, 0, 2 * NA_COLS - 2)
    onehot = (dc[None] == np.arange(2 * NA_COLS - 1)[:, None, None]).astype(np.float32)
    toe = jnp.einsum("hrj,jqk->hrqk", rel_bias, jnp.asarray(onehot), precision=lax.Precision.HIGHEST)
    toe = jnp.where(col_ok[None, None], toe, NEG_INF)
    tabs = [jnp.transpose(toe[:, NA_MAX_ROWS - 1 - o: 2 * NA_MAX_ROWS - 1 - o], (0, 2, 1, 3))
            for o in range(NA_MAX_ROWS)]
    return jnp.stack(tabs).reshape(NA_MAX_ROWS, NA_HEADS, GRID_W, NA_WIN)


def _na_row_start(i):
    return jnp.clip(i - NA_MAX_ROWS // 2, 0, NA_ROWS - NA_MAX_ROWS)


def _na_kernel(q_ref, k_ref, v_ref, ck_ref, cv_ref, bias_ref, o_ref):
    scale = HEAD_DIM ** -0.5
    i = pl.program_id(1)
    start = pl.multiple_of(_na_row_start(i) * GRID_W, GRID_W)
    kw = k_ref[pl.ds(start, NA_WIN), :]
    vw = v_ref[pl.ds(start, NA_WIN), :]
    outs = []
    for h in range(NA_HEADS):
        sl = slice(h * HEAD_DIM, (h + 1) * HEAD_DIM)
        q = q_ref[:, sl].astype(BF16)
        s_loc = _dot_nt(q, kw[:, sl].astype(BF16)) * scale + bias_ref[0, h]
        s_ctx = _dot_nt(q, ck_ref[:, sl].astype(BF16)) * scale
        m = jnp.maximum(jnp.max(s_loc, axis=-1, keepdims=True), jnp.max(s_ctx, axis=-1, keepdims=True))
        p_loc = jnp.exp(s_loc - m)
        p_ctx = jnp.exp(s_ctx - m)
        den = jnp.sum(p_loc, axis=-1, keepdims=True) + jnp.sum(p_ctx, axis=-1, keepdims=True)
        outs.append(_dot((p_loc / den).astype(BF16), vw[:, sl].astype(BF16))
                    + _dot((p_ctx / den).astype(BF16), cv_ref[:, sl].astype(BF16)))
    o_ref[...] = jnp.concatenate(outs, axis=-1)


def neighbourhood_attention(q, k, v, ck, cv, bias_tab):
    seq = pl.BlockSpec((DEC_SEQ, NA_DIM), lambda b, i: (b, 0))
    ctx = pl.BlockSpec((PAST_LEN, NA_DIM), lambda b, i: (b, 0))
    row = pl.BlockSpec((GRID_W, NA_DIM), lambda b, i: (b * NA_ROWS + i, 0))
    return pl.pallas_call(
        _na_kernel, name="na_attn",
        grid=(DEC_BATCH, NA_ROWS),
        in_specs=[row, seq, seq, ctx, ctx,
                  pl.BlockSpec((1, NA_HEADS, GRID_W, NA_WIN), lambda b, i: (i - _na_row_start(i), 0, 0, 0))],
        out_specs=row,
        out_shape=jax.ShapeDtypeStruct((S_ROWS, NA_DIM), F32),
        compiler_params=_cparams(("parallel", "arbitrary")),
    )(q, k, v, ck, cv, bias_tab)


def _proj_res_kernel(a_ref, b_ref, x_ref, gate_ref, w_ref, o_ref, *, na):
    y = _dot(a_ref[...].astype(BF16), w_ref[:na, :]) + _dot(b_ref[...].astype(BF16), w_ref[na:, :])
    o_ref[...] = x_ref[...] + gate_ref[0] * y


def proj_residual(a, b, x, gate, w, tm=512):
    na, nb = a.shape[1], b.shape[1]
    si = _set_index(tm)
    return pl.pallas_call(
        functools.partial(_proj_res_kernel, na=na), name="proj_residual",
        grid=(N_ROWS // tm,),
        in_specs=[pl.BlockSpec((tm, na), lambda i: (i, 0)),
                  pl.BlockSpec((tm, nb), lambda i: (i, 0)),
                  pl.BlockSpec((tm, D_MODEL), lambda i: (i, 0)),
                  pl.BlockSpec((1, 1, D_MODEL), lambda i: (si(i), 0, 0)),
                  pl.BlockSpec((na + nb, D_MODEL), lambda i: (0, 0))],
        out_specs=pl.BlockSpec((tm, D_MODEL), lambda i: (i, 0)),
        out_shape=jax.ShapeDtypeStruct((N_ROWS, D_MODEL), F32),
        compiler_params=_cparams(("parallel",)),
    )(a, b, x, gate, w)


def _ffn_kernel(x_ref, g_ref, sh_ref, sc_ref, gate_ref, wg_ref, wu_ref, wd_ref, o_ref, h_scr, acc_scr):
    j = pl.program_id(1)

    @pl.when(j == 0)
    def _():
        h_scr[...] = _modulated(x_ref[...], g_ref[...], sh_ref[0], sc_ref[0]).astype(BF16)
        acc_scr[...] = jnp.zeros_like(acc_scr)

    h = h_scr[...]
    gt = _dot(h, wg_ref[...])
    act = (gt * _sigmoid(gt)) * _dot(h, wu_ref[...])
    acc_scr[...] += _dot(act.astype(BF16), wd_ref[...])

    @pl.when(j == pl.num_programs(1) - 1)
    def _():
        o_ref[...] = x_ref[...] + gate_ref[0] * acc_scr[...]


def ffn_residual(x, g, sh, sc, gate, wg, wu, wd, tm=1024, tf=256):
    si = _set_index(tm)
    vec = pl.BlockSpec((1, 1, D_MODEL), lambda i, j: (si(i), 0, 0))
    return pl.pallas_call(
        _ffn_kernel, name="ffn",
        grid=(N_ROWS // tm, D_FF // tf),
        in_specs=[pl.BlockSpec((tm, D_MODEL), lambda i, j: (i, 0)),
                  pl.BlockSpec((1, D_MODEL), lambda i, j: (0, 0)),
                  vec, vec, vec,
                  pl.BlockSpec((D_MODEL, tf), lambda i, j: (0, j)),
                  pl.BlockSpec((D_MODEL, tf), lambda i, j: (0, j)),
                  pl.BlockSpec((tf, D_MODEL), lambda i, j: (j, 0))],
        out_specs=pl.BlockSpec((tm, D_MODEL), lambda i, j: (i, 0)),
        out_shape=jax.ShapeDtypeStruct((N_ROWS, D_MODEL), F32),
        scratch_shapes=[pltpu.VMEM((tm, D_MODEL), BF16), pltpu.VMEM((tm, D_MODEL), F32)],
        compiler_params=_cparams(("parallel", "arbitrary")),
    )(x, g.reshape(1, D_MODEL), sh, sc, gate, wg, wu, wd)


def _pool_consts(n):
    t = np.arange(n)
    mats, cnts = [], []
    for win in POOL_WINDOWS:
        lo = np.clip(t - win // 2, 0, n)
        hi = np.clip(t + win - win // 2, 0, n)
        mats.append(((t[None, :] >= lo[:, None]) & (t[None, :] < hi[:, None])).astype(np.float32))
        cnts.append(np.repeat((hi - lo).astype(np.float32)[:, None], POOL_GW, axis=1))
    return jnp.asarray(np.stack(mats)).astype(BF16), jnp.asarray(np.concatenate(cnts, axis=1))


def _pool_kernel(x_ref, pm_ref, cnt_ref, w_ref, scale_ref, o_ref):
    x = x_ref[...]
    hi = x.astype(BF16)
    lo = (x - hi.astype(F32)).astype(BF16)
    sums = []
    for g in range(len(POOL_WINDOWS)):
        sl = slice(g * POOL_GW, (g + 1) * POOL_GW)
        sums.append(_dot(pm_ref[g], hi[:, sl]) + _dot(pm_ref[g], lo[:, sl]))
    y = jnp.concatenate(sums, axis=-1) / cnt_ref[...] - x
    o_ref[...] = _dot(y.astype(BF16), w_ref[...]) * scale_ref[...]


def pool_mix(x, n, w_bd, scale):
    rows = x.shape[0]
    pm, cnt = _pool_consts(n)
    return pl.pallas_call(
        _pool_kernel, name="pool",
        grid=(rows // n,),
        in_specs=[pl.BlockSpec((n, POOL_CH), lambda b: (b, 0)),
                  pl.BlockSpec((len(POOL_WINDOWS), n, n), lambda b: (0, 0, 0)),
                  pl.BlockSpec((n, POOL_CH), lambda b: (0, 0)),
                  pl.BlockSpec((POOL_CH, POOL_CH), lambda b: (0, 0)),
                  pl.BlockSpec((1, POOL_CH), lambda b: (0, 0))],
        out_specs=pl.BlockSpec((n, POOL_CH), lambda b: (b, 0)),
        out_shape=jax.ShapeDtypeStruct((rows, POOL_CH), F32),
        compiler_params=_cparams(("parallel",)),
    )(x, pm, cnt, w_bd, scale.reshape(1, POOL_CH))


RW_TILE = 256
HALO = 8


def _head_ones():
    h = np.arange(RWKV_DIM) // HEAD_DIM
    return jnp.asarray((h[:, None] == h[None, :]).astype(np.float32)).astype(BF16)


def _head_sum(x, ones):
    hi = x.astype(BF16)
    lo = (x - hi.astype(F32)).astype(BF16)
    return _dot(hi, ones) + _dot(lo, ones)


def _rwkv_prep_kernel(z_ref, zp_ref, zn_ref, mu_ref, kk_w_ref, ka_ref, rk_ref, w0_ref, a0_ref,
                      dup_ref, iup_ref, gup_ref, ones_ref,
                      r_ref, v_ref, kk_ref, wf_ref, kmf_ref, bf_ref, wb_ref, kmb_ref, bb_ref, g_ref, bonus_ref,
                      *, tiles_per_seq):
    i = pl.program_id(0)
    pos = i % tiles_per_seq
    z = z_ref[...]
    row = lax.broadcasted_iota(jnp.int32, (RW_TILE, 1), 0)
    prev_edge = jnp.where(pos == 0, 0.0, zp_ref[HALO - 1:HALO, :])
    next_edge = jnp.where(pos == tiles_per_seq - 1, 0.0, zn_ref[0:1, :])
    prev = jnp.where(row == 0, prev_edge, pltpu.roll(z, 1, 0))
    nxt = jnp.where(row == RW_TILE - 1, next_edge, pltpu.roll(z, RW_TILE - 1, 0))
    zr = z + mu_ref[0:1, :] * (prev - z) + mu_ref[1:2, :] * (nxt - z)

    d = RWKV_DIM
    r, k, v = zr[:, :d], zr[:, d:2 * d], zr[:, 2 * d:3 * d]
    lora = 3 * d
    ones = ones_ref[...]
    kk = k * kk_w_ref[...]
    kk = kk * lax.rsqrt(_head_sum(kk * kk, ones) + L2_EPS)
    r_ref[...] = r
    v_ref[...] = v
    kk_ref[...] = kk
    outs = ((wf_ref, kmf_ref, bf_ref), (wb_ref, kmb_ref, bb_ref))
    for dr, (w_o, km_o, b_o) in enumerate(outs):
        wl = zr[:, lora + dr * DECAY_LORA: lora + (dr + 1) * DECAY_LORA]
        al = zr[:, lora + 2 * DECAY_LORA + dr * ICLR_LORA: lora + 2 * DECAY_LORA + (dr + 1) * ICLR_LORA]
        lw = w0_ref[dr:dr + 1, :] + _dot(jnp.tanh(wl).astype(BF16), dup_ref[dr])
        w_o[...] = jnp.exp(-DECAY_SCALE * _sigmoid(lw))
        a = _sigmoid(a0_ref[dr:dr + 1, :] + _dot(al.astype(BF16), iup_ref[dr]))
        km_o[...] = k * (1.0 + (a - 1.0) * ka_ref[...])
        b_o[...] = kk * a
    gl = zr[:, lora + 2 * DECAY_LORA + 2 * ICLR_LORA:]
    g_ref[...] = _dot(_sigmoid(gl).astype(BF16), gup_ref[...])
    bonus_ref[...] = _head_sum(r * k * rk_ref[...], ones) * v


def rwkv_prep(z, n, mu, k_k, k_a, r_k, w0, a0, dup, iup, gup):
    rows = z.shape[0]
    tps = n // RW_TILE
    hb = RW_TILE // HALO
    last = rows // HALO - 1
    d = RWKV_DIM
    full2 = lambda shape: pl.BlockSpec(shape, lambda i: (0, 0))
    full3 = lambda shape: pl.BlockSpec(shape, lambda i: (0, 0, 0))
    tile = pl.BlockSpec((RW_TILE, d), lambda i: (i, 0))
    return pl.pallas_call(
        functools.partial(_rwkv_prep_kernel, tiles_per_seq=tps), name="rwkv_prep",
        grid=(rows // RW_TILE,),
        in_specs=[pl.BlockSpec((RW_TILE, RWKV_IN), lambda i: (i, 0)),
                  pl.BlockSpec((HALO, RWKV_IN), lambda i: (jnp.maximum(i * hb - 1, 0), 0)),
                  pl.BlockSpec((HALO, RWKV_IN), lambda i: (jnp.minimum((i + 1) * hb, last), 0)),
                  full2((2, RWKV_IN)), full2((1, d)), full2((1, d)), full2((1, d)),
                  full2((2, d)), full2((2, d)),
                  full3((2, DECAY_LORA, d)), full3((2, ICLR_LORA, d)), full2((GATE_LORA, d)),
                  full2((d, d))],
        out_specs=[tile] * 11,
        out_shape=[jax.ShapeDtypeStruct((rows, d), F32)] * 11,
        compiler_params=_cparams(("parallel",)),
    )(z, z, z, mu, k_k.reshape(1, d), k_a.reshape(1, d), r_k.reshape(1, d), w0, a0,
      dup.astype(BF16), iup.astype(BF16), gup.astype(BF16), _head_ones())


SCAN_TC = 32


def _scan_kernel(r_ref, w_ref, km_ref, kk_ref, bb_ref, v_ref, s0_ref, o_ref, st_ref, s_scr, *, vs):
    c = pl.program_id(1)

    @pl.when(c == 0)
    def _():
        s_scr[...] = s0_ref[...]

    def step(t, carry):
        sa = jnp.zeros((vs, LANES), F32)
        for k in range(HEAD_DIM):
            sa = sa + s_scr[k] * kk_ref[t, k:k + 1, :]
        v_t = v_ref[t]
        o = jnp.zeros((vs, LANES), F32)
        for k in range(HEAD_DIM):
            s_new = (s_scr[k] * w_ref[t, k:k + 1, :] - sa * bb_ref[t, k:k + 1, :]
                     + v_t * km_ref[t, k:k + 1, :])
            s_scr[k] = s_new
            o = o + s_new * r_ref[t, k:k + 1, :]
        o_ref[t] = o
        return carry

    lax.fori_loop(0, SCAN_TC, step, 0)

    @pl.when(c == pl.num_programs(1) - 1)
    def _():
        st_ref[...] = s_scr[...]


def wkv_scan(r, w, km, kk, bb, v, s0):
    t_len, vs, c = v.shape
    kin = pl.BlockSpec((SCAN_TC, HEAD_DIM, LANES), lambda g, s: (s, 0, g))
    vio = pl.BlockSpec((SCAN_TC, vs, LANES), lambda g, s: (s, 0, g))
    st = pl.BlockSpec((HEAD_DIM, vs, LANES), lambda g, s: (0, 0, g))
    return pl.pallas_call(
        functools.partial(_scan_kernel, vs=vs), name="wkv_scan",
        grid=(c // LANES, t_len // SCAN_TC),
        in_specs=[kin, kin, kin, kin, kin, vio, st],
        out_specs=[vio, st],
        out_shape=[jax.ShapeDtypeStruct((t_len, vs, c), F32),
                   jax.ShapeDtypeStruct((HEAD_DIM, vs, c), F32)],
        scratch_shapes=[pltpu.VMEM((HEAD_DIM, vs, LANES), F32)],
        compiler_params=_cparams(("parallel", "arbitrary")),
    )(r, w, km, kk, bb, v, s0)


def _to_chain_lanes(x, b, n):
    x = x.reshape(b, n, RWKV_HEADS, HEAD_DIM)
    return jnp.transpose(x, (1, 3, 0, 2)).reshape(n, HEAD_DIM, b * RWKV_HEADS)


def _from_chain_lanes(x, b, n):
    x = x.reshape(n, HEAD_DIM, b, RWKV_HEADS)
    return jnp.transpose(x, (2, 0, 3, 1)).reshape(b * n, RWKV_DIM)


def _bidir(fwd, bwd):
    return jnp.concatenate([fwd, jnp.flip(bwd, axis=0)], axis=-1)


def rwkv_scan_stream(terms, b, n, s0, split_v):
    r, v, kk, wf, kmf, bf, wb, kmb, bb = [_to_chain_lanes(t, b, n) for t in terms]
    nc = 2 * b * RWKV_HEADS
    ks = [_bidir(r, r), _bidir(wf, wb), _bidir(kmf, kmb), _bidir(kk, kk), _bidir(bf, bb)]
    vv = _bidir(v, v)
    if s0 is None:
        st0 = jnp.zeros((HEAD_DIM, HEAD_DIM, nc), F32)
    else:
        st0 = jnp.transpose(s0, (4, 3, 1, 0, 2)).reshape(HEAD_DIM, HEAD_DIM, nc)
    vs = HEAD_DIM
    if split_v:
        vs = HEAD_DIM // 2
        ks = [jnp.concatenate([t, t], axis=-1) for t in ks]
        halves = lambda t: jnp.concatenate([t[:, :vs], t[:, vs:]], axis=-1)
        vv, st0 = halves(vv), halves(st0)
    lanes = ks[0].shape[-1]
    pad = (-lanes) % LANES
    padl = lambda t: jnp.pad(t, ((0, 0), (0, 0), (0, pad)))
    o, st = wkv_scan(*[padl(t) for t in ks], padl(vv), padl(st0))
    o, st = o[..., :lanes], st[..., :lanes]
    if split_v:
        o = jnp.concatenate([o[..., :nc], o[..., nc:]], axis=1)
        st = jnp.concatenate([st[..., :nc], st[..., nc:]], axis=1)
    half = nc // 2
    o_f = _from_chain_lanes(o[..., :half], b, n)
    o_b = _from_chain_lanes(jnp.flip(o[..., half:], axis=0), b, n)
    st = jnp.transpose(st.reshape(HEAD_DIM, HEAD_DIM, 2, b, RWKV_HEADS), (3, 2, 4, 1, 0))
    return o_f, o_b, st


def _rwkv_post_kernel(of_ref, ob_ref, g_ref, bonus_ref, gw_ref, gb_ref, ones_ref, y_ref):
    ones = ones_ref[...]
    o = of_ref[...] + ob_ref[...]
    mu = _head_sum(o, ones) / HEAD_DIM
    oc = o - mu
    var = _head_sum(oc * oc, ones) / HEAD_DIM
    on = (oc * lax.rsqrt(var + GN_EPS)) * gw_ref[...] + gb_ref[...]
    y_ref[...] = (on + bonus_ref[...]) * g_ref[...]


def rwkv_post(o_f, o_b, g, bonus, gn_w, gn_b):
    rows, d = o_f.shape
    tile = pl.BlockSpec((RW_TILE, d), lambda i: (i, 0))
    vec = pl.BlockSpec((1, d), lambda i: (0, 0))
    return pl.pallas_call(
        _rwkv_post_kernel, name="rwkv_post",
        grid=(rows // RW_TILE,),
        in_specs=[tile, tile, tile, tile, vec, vec, pl.BlockSpec((d, d), lambda i: (0, 0))],
        out_specs=tile,
        out_shape=jax.ShapeDtypeStruct((rows, d), F32),
        compiler_params=_cparams(("parallel",)),
    )(o_f, o_b, g, bonus, gn_w.reshape(1, d), gn_b.reshape(1, d), _head_ones())


def _split3(x):
    a = x.astype(BF16)
    r1 = x - a.astype(F32)
    b = r1.astype(BF16)
    c = (r1 - b.astype(F32)).astype(BF16)
    return a, b, c


def _router_kernel(x_ref, g_ref, sh_ref, sc_ref, w_ref, b_ref, h_ref, comb_ref):
    h = _modulated(x_ref[...], g_ref[...], sh_ref[0], sc_ref[0])
    h_ref[...] = h.astype(BF16)
    h1, h2, h3 = _split3(h)
    w1, w2, w3 = _split3(w_ref[...])
    logits = (_dot(h1, w1) + (_dot(h1, w2) + _dot(h2, w1))
              + (_dot(h1, w3) + _dot(h2, w2) + _dot(h3, w1))) + b_ref[...]
    col = lax.broadcasted_iota(jnp.int32, logits.shape, 1)
    logits = jnp.where(col < N_EXPERTS, logits, -jnp.inf)
    m1 = jnp.max(logits, axis=-1, keepdims=True)
    i1 = jnp.min(jnp.where(logits == m1, col, LANES), axis=-1, keepdims=True)
    rest = jnp.where(col == i1, -jnp.inf, logits)
    m2 = jnp.max(rest, axis=-1, keepdims=True)
    i2 = jnp.min(jnp.where(rest == m2, col, LANES), axis=-1, keepdims=True)
    e2 = jnp.exp(m2 - m1)
    den = 1.0 + e2
    comb_ref[...] = jnp.where(col == i1, 1.0 / den, 0.0) + jnp.where(col == i2, e2 / den, 0.0)


def moe_router(x, g, sh, sc, router_w, router_b, tm=512):
    si = _set_index(tm)
    vec = pl.BlockSpec((1, 1, D_MODEL), lambda i: (si(i), 0, 0))
    w = jnp.pad(router_w, ((0, 0), (0, LANES - N_EXPERTS)))
    b = jnp.pad(router_b, (0, LANES - N_EXPERTS)).reshape(1, LANES)
    return pl.pallas_call(
        _router_kernel, name="moe_router",
        grid=(N_ROWS // tm,),
        in_specs=[pl.BlockSpec((tm, D_MODEL), lambda i: (i, 0)),
                  pl.BlockSpec((1, D_MODEL), lambda i: (0, 0)),
                  vec, vec,
                  pl.BlockSpec((D_MODEL, LANES), lambda i: (0, 0)),
                  pl.BlockSpec((1, LANES), lambda i: (0, 0))],
        out_specs=[pl.BlockSpec((tm, D_MODEL), lambda i: (i, 0)),
                   pl.BlockSpec((tm, LANES), lambda i: (i, 0))],
        out_shape=[jax.ShapeDtypeStruct((N_ROWS, D_MODEL), BF16),
                   jax.ShapeDtypeStruct((N_ROWS, LANES), F32)],
        compiler_params=_cparams(("parallel",)),
    )(x, g.reshape(1, D_MODEL), sh, sc, w, b)


def _moe_kernel(x_ref, h_ref, comb_ref, gate_ref, gfin_ref, wg_ref, wu_ref, wd_ref, o_ref, acc_scr):
    e = pl.program_id(1)

    @pl.when(e == 0)
    def _():
        acc_scr[...] = jnp.zeros_like(acc_scr)

    comb = comb_ref[...]
    col = lax.broadcasted_iota(jnp.int32, comb.shape, 1)
    ce = jnp.sum(jnp.where(col == e, comb, 0.0), axis=-1, keepdims=True)
    h = h_ref[...]
    gt = _dot(h, wg_ref[0])
    act = (gt * _sigmoid(gt)) * _dot(h, wu_ref[0])
    acc_scr[...] += ce * _dot(act.astype(BF16), wd_ref[0])

    @pl.when(e == pl.num_programs(1) - 1)
    def _():
        y = x_ref[...] + gate_ref[0] * acc_scr[...]
        ms = jnp.mean(y * y, axis=-1, keepdims=True)
        o_ref[...] = y * lax.rsqrt(ms + RMS_EPS) * gfin_ref[...]


def moe_residual_norm(x, h, comb, gate, g_final, wg, wu, wd, tm=512):
    si = _set_index(tm)
    return pl.pallas_call(
        _moe_kernel, name="moe_experts",
        grid=(N_ROWS // tm, N_EXPERTS),
        in_specs=[pl.BlockSpec((tm, D_MODEL), lambda i, e: (i, 0)),
                  pl.BlockSpec((tm, D_MODEL), lambda i, e: (i, 0)),
                  pl.BlockSpec((tm, LANES), lambda i, e: (i, 0)),
                  pl.BlockSpec((1, 1, D_MODEL), lambda i, e: (si(i), 0, 0)),
                  pl.BlockSpec((1, D_MODEL), lambda i, e: (0, 0)),
                  pl.BlockSpec((1, D_MODEL, D_FF_EXPERT), lambda i, e: (e, 0, 0)),
                  pl.BlockSpec((1, D_MODEL, D_FF_EXPERT), lambda i, e: (e, 0, 0)),
                  pl.BlockSpec((1, D_FF_EXPERT, D_MODEL), lambda i, e: (e, 0, 0))],
        out_specs=pl.BlockSpec((tm, D_MODEL), lambda i, e: (i, 0)),
        out_shape=jax.ShapeDtypeStruct((N_ROWS, D_MODEL), F32),
        scratch_shapes=[pltpu.VMEM((tm, D_MODEL), F32)],
        compiler_params=_cparams(("parallel", "arbitrary")),
    )(x, h, comb, gate, g_final.reshape(1, D_MODEL), wg, wu, wd)


def kernel(x_prompt, x_sample, cache_na_k, cache_na_v, state_wkv, c, c_ctx, mod_w, mod_b, norm_mix, norm_ffn, norm_final, na_w_in, fourier_w, na_rel_bias, na_w_out, ffn_w_gate, ffn_w_up, ffn_w_down, rw_w_in, pool_w, pool_scale, shift_mu, decay_w0, decay_up, iclr_a0, iclr_up, gate_up, k_k, k_a, r_k, gn_w, gn_b, rw_w_out, router_w, router_b, moe_w_gate, moe_w_up, moe_w_down):
    x = jnp.concatenate([x_prompt.reshape(P_ROWS, D_MODEL), x_sample.reshape(S_ROWS, D_MODEL)], axis=0)
    cond = jnp.concatenate([c_ctx[None, :], c, jnp.zeros((8 - N_SETS, D_MODEL), F32)], axis=0)
    mods = adaln_all(cond, mod_w, mod_b)[:, :N_SETS].reshape(DEPTH, N_SETS, 6, 1, D_MODEL)

    sh1, sc1, g1, sh2, sc2, g2 = [mods[0, :, m] for m in range(6)]
    splits = ((0, FOURIER_CH), (FOURIER_CH, FOURIER_CH + NA_DIM),
              (FOURIER_CH + NA_DIM, FOURIER_CH + 2 * NA_DIM), (FOURIER_CH + 2 * NA_DIM, FOURIER_CH + 3 * NA_DIM))
    f, q, k, v = modulated_matmul(x, norm_mix[0], sh1, sc1, na_w_in[0].astype(BF16), splits)
    f_bd = _block_diag(fourier_w[0]).astype(BF16)
    a_out = jnp.concatenate([fourier_mix(f[:P_ROWS], SEQ, f_bd), fourier_mix(f[P_ROWS:], DEC_SEQ, f_bd)], axis=0)
    ck = cache_na_k[:, 0].reshape(DEC_BATCH * PAST_LEN, NA_DIM)
    cv = cache_na_v[:, 0].reshape(DEC_BATCH * PAST_LEN, NA_DIM)
    b_out = jnp.concatenate([
        context_attention(q[:P_ROWS], k[:P_ROWS], v[:P_ROWS]),
        neighbourhood_attention(q[P_ROWS:], k[P_ROWS:], v[P_ROWS:], ck, cv, _na_bias_table(na_rel_bias[0])),
    ], axis=0)
    new_k = k[:P_ROWS].reshape(BATCH, 1, SEQ, NA_HEADS, HEAD_DIM)
    new_v = v[:P_ROWS].reshape(BATCH, 1, SEQ, NA_HEADS, HEAD_DIM)
    x = proj_residual(a_out, b_out, x, g1, na_w_out[0].astype(BF16))
    x = ffn_residual(x, norm_ffn[0], sh2, sc2, g2, ffn_w_gate[0].astype(BF16), ffn_w_up[0].astype(BF16),
                     ffn_w_down[0].astype(BF16))

    sh1, sc1, g1, sh2, sc2, g2 = [mods[1, :, m] for m in range(6)]
    pc, z = modulated_matmul(x, norm_mix[1], sh1, sc1, rw_w_in[0].astype(BF16),
                             ((0, POOL_CH), (POOL_CH, POOL_CH + RWKV_IN)))
    p_bd = _block_diag(pool_w[0]).astype(BF16)
    c_out = jnp.concatenate([pool_mix(pc[:P_ROWS], SEQ, p_bd, pool_scale[0]),
                             pool_mix(pc[P_ROWS:], DEC_SEQ, p_bd, pool_scale[0])], axis=0)
    rw = (shift_mu[0], k_k[0], k_a[0], r_k[0], decay_w0[0], iclr_a0[0], decay_up[0], iclr_up[0], gate_up[0])
    tp = rwkv_prep(z[:P_ROWS], SEQ, *rw)
    ts = rwkv_prep(z[P_ROWS:], DEC_SEQ, *rw)
    ofp, obp, st_p = rwkv_scan_stream(tp[:9], BATCH, SEQ, None, split_v=False)
    ofs, obs, _ = rwkv_scan_stream(ts[:9], DEC_BATCH, DEC_SEQ, state_wkv[:, 0], split_v=True)
    d_out = jnp.concatenate([rwkv_post(ofp, obp, tp[9], tp[10], gn_w[0], gn_b[0]),
                             rwkv_post(ofs, obs, ts[9], ts[10], gn_w[0], gn_b[0])], axis=0)
    x = proj_residual(c_out, d_out, x, g1, rw_w_out[0].astype(BF16))
    h, comb = moe_router(x, norm_ffn[1], sh2, sc2, router_w[0], router_b[0])
    y = moe_residual_norm(x, h, comb, g2, norm_final, moe_w_gate[0].astype(BF16), moe_w_up[0].astype(BF16),
                          moe_w_down[0].astype(BF16))

    y_prompt = y[:P_ROWS].reshape(BATCH, SEQ, D_MODEL)
    y_sample = y[P_ROWS:].reshape(DEC_BATCH, DEC_SEQ, D_MODEL)
    return (y_prompt, y_sample, new_k, new_v, st_p[:, None])
```

```python
import functools
import math

import numpy as np
import jax
import jax.numpy as jnp
from jax import lax
from jax.experimental import pallas as pl
from jax.experimental.pallas import tpu as pltpu

F32 = jnp.float32
BF16 = jnp.bfloat16

D_MODEL = 1024
BATCH = 32
SEQ = 256
DEPTH = 2
DEC_BATCH = 2
DEC_SEQ = 1024
PAST_LEN = 512
GRID_W = 64
HEAD_DIM = 64
FOURIER_CH = D_MODEL // 4
FOURIER_GROUPS = 4
FOURIER_GW = FOURIER_CH // FOURIER_GROUPS
NA_DIM = D_MODEL - FOURIER_CH
NA_HEADS = NA_DIM // HEAD_DIM
NA_MAX_ROWS = 8
NA_COLS = 16
POOL_WINDOWS = (2, 4, 8, 16)
POOL_CH = D_MODEL // 4
POOL_GW = POOL_CH // len(POOL_WINDOWS)
RWKV_DIM = D_MODEL - POOL_CH
RWKV_HEADS = RWKV_DIM // HEAD_DIM
DECAY_LORA = 64
ICLR_LORA = 64
GATE_LORA = 128
RWKV_IN = 3 * RWKV_DIM + 2 * DECAY_LORA + 2 * ICLR_LORA + GATE_LORA
D_FF = 2816
N_EXPERTS = 8
D_FF_EXPERT = 1408
RMS_EPS = 1e-6
GN_EPS = 64e-5
L2_EPS = 1e-12
DECAY_SCALE = math.exp(-0.5)
NEG_INF = -1e30

P_ROWS = BATCH * SEQ
S_ROWS = DEC_BATCH * DEC_SEQ
N_ROWS = P_ROWS + S_ROWS
N_SETS = 1 + DEC_BATCH
LANES = 128
VMEM_LIMIT = 56 * 1024 * 1024


def _cparams(sem):
    return pltpu.CompilerParams(dimension_semantics=sem, vmem_limit_bytes=VMEM_LIMIT)


def _sigmoid(x):
    return 1.0 / (1.0 + jnp.exp(-x))


def _dot(a, b):
    return jnp.dot(a, b, preferred_element_type=F32)


def _dot_nt(a, b):
    return lax.dot_general(a, b, (((1,), (1,)), ((), ())), preferred_element_type=F32)


def _set_index(tm):
    q = DEC_SEQ // tm
    p = P_ROWS // tm
    return lambda i: jnp.maximum(i - p + q, 0) // q


def _modulated(x, g, sh, sc):
    ms = jnp.mean(x * x, axis=-1, keepdims=True)
    return (x * lax.rsqrt(ms + RMS_EPS) * g) * (1.0 + sc) + sh


def _adaln_kernel(c_ref, w_ref, b_ref, o_ref):
    c = c_ref[...]
    s = (c * _sigmoid(c)).astype(BF16)
    o_ref[0] = _dot(s, w_ref[0].astype(BF16)) + b_ref[0]


def adaln_all(cond, mod_w, mod_b):
    tn = 1536
    n = 6 * D_MODEL
    return pl.pallas_call(
        _adaln_kernel, name="adaln",
        grid=(DEPTH, n // tn),
        in_specs=[pl.BlockSpec((8, D_MODEL), lambda l, j: (0, 0)),
                  pl.BlockSpec((1, D_MODEL, tn), lambda l, j: (l, 0, j)),
                  pl.BlockSpec((1, 1, tn), lambda l, j: (l, 0, j))],
        out_specs=pl.BlockSpec((1, 8, tn), lambda l, j: (l, 0, j)),
        out_shape=jax.ShapeDtypeStruct((DEPTH, 8, n), F32),
        compiler_params=_cparams(("parallel", "parallel")),
    )(cond, mod_w, mod_b.reshape(DEPTH, 1, n))


def _modmm_kernel(x_ref, g_ref, sh_ref, sc_ref, w_ref, *o_refs, splits):
    h = _modulated(x_ref[...], g_ref[...], sh_ref[0], sc_ref[0]).astype(BF16)
    for o_ref, (a, b) in zip(o_refs, splits):
        o_ref[...] = _dot(h, w_ref[:, a:b]).astype(o_ref.dtype)


def modulated_matmul(x, g, sh, sc, w, splits, tm=512):
    n_out = w.shape[1]
    si = _set_index(tm)
    vec = pl.BlockSpec((1, 1, D_MODEL), lambda i: (si(i), 0, 0))
    return pl.pallas_call(
        functools.partial(_modmm_kernel, splits=splits), name="modulated_matmul",
        grid=(N_ROWS // tm,),
        in_specs=[pl.BlockSpec((tm, D_MODEL), lambda i: (i, 0)),
                  pl.BlockSpec((1, D_MODEL), lambda i: (0, 0)),
                  vec, vec,
                  pl.BlockSpec((D_MODEL, n_out), lambda i: (0, 0))],
        out_specs=[pl.BlockSpec((tm, b - a), lambda i: (i, 0)) for a, b in splits],
        out_shape=[jax.ShapeDtypeStruct((N_ROWS, b - a), F32) for a, b in splits],
        compiler_params=_cparams(("parallel",)),
    )(x, g.reshape(1, D_MODEL), sh, sc, w)


def _dft_mats(n):
    t = np.arange(n)
    ang = 2.0 * np.pi * ((t[:, None] * t[None, :]) % n) / n
    cn, sn = np.cos(ang) / np.sqrt(n), np.sin(ang) / np.sqrt(n)
    c = np.arange(FOURIER_GW)
    angc = 2.0 * np.pi * ((c[:, None] * c[None, :]) % FOURIER_GW) / FOURIER_GW
    eye = np.eye(FOURIER_GROUPS)
    cc = np.kron(eye, np.cos(angc) / np.sqrt(FOURIER_GW))
    sc = np.kron(eye, np.sin(angc) / np.sqrt(FOURIER_GW))
    as_bf = lambda a: jnp.asarray(a, dtype=F32).astype(BF16)
    return as_bf(cn), as_bf(sn), as_bf(cc), as_bf(sc)


def _fourier_kernel(f_ref, cn_ref, sn_ref, cc_ref, sc_ref, w_ref, o_ref):
    x = f_ref[...].astype(BF16)
    a = _dot(x, cc_ref[...]).astype(BF16)
    b = _dot(x, sc_ref[...]).astype(BF16)
    re = _dot(cn_ref[...], a) - _dot(sn_ref[...], b)
    o_ref[...] = _dot(re.astype(BF16), w_ref[...])


def _block_diag(w):
    g, c, _ = w.shape
    eye = jnp.eye(g, dtype=w.dtype)
    return (eye[:, None, :, None] * w[:, :, None, :]).reshape(g * c, g * c)


def fourier_mix(f, row0, nb, n, w_bd):
    rows = nb * n
    b0 = row0 // n
    cn, sn, cc, sc = _dft_mats(n)
    full = lambda shape: pl.BlockSpec(shape, lambda b: (0, 0))
    return pl.pallas_call(
        _fourier_kernel, name="fourier",
        grid=(rows // n,),
        in_specs=[pl.BlockSpec((n, FOURIER_CH), lambda b: (b + b0, 0)),
                  full((n, n)), full((n, n)),
                  full((FOURIER_CH, FOURIER_CH)), full((FOURIER_CH, FOURIER_CH)),
                  full((FOURIER_CH, FOURIER_CH))],
        out_specs=pl.BlockSpec((n, FOURIER_CH), lambda b: (b, 0)),
        out_shape=jax.ShapeDtypeStruct((rows, FOURIER_CH), F32),
        compiler_params=_cparams(("parallel",)),
    )(f, cn, sn, cc, sc, w_bd)


def _ctx_attn_kernel(q_ref, k_ref, v_ref, o_ref):
    scale = HEAD_DIM ** -0.5
    outs = []
    for h in range(NA_HEADS):
        sl = slice(h * HEAD_DIM, (h + 1) * HEAD_DIM)
        q = q_ref[:, sl].astype(BF16)
        k = k_ref[:, sl].astype(BF16)
        v = v_ref[:, sl].astype(BF16)
        s = _dot_nt(q, k) * scale
        p = jnp.exp(s - jnp.max(s, axis=-1, keepdims=True))
        p = p / jnp.sum(p, axis=-1, keepdims=True)
        outs.append(_dot(p.astype(BF16), v))
    o_ref[...] = jnp.concatenate(outs, axis=-1)


def context_attention(q, k, v):
    blk = pl.BlockSpec((SEQ, NA_DIM), lambda b: (b, 0))
    return pl.pallas_call(
        _ctx_attn_kernel, name="ctx_attn",
        grid=(BATCH,),
        in_specs=[blk, blk, blk],
        out_specs=blk,
        out_shape=jax.ShapeDtypeStruct((P_ROWS, NA_DIM), F32),
        compiler_params=_cparams(("parallel",)),
    )(q, k, v)


NA_ROWS = DEC_SEQ // GRID_W
NA_WIN = NA_MAX_ROWS * GRID_W


def _na_bias_table(rel_bias):
    cols = np.arange(GRID_W)
    c0 = np.clip(cols - NA_COLS // 2, 0, GRID_W - NA_COLS)
    col_ok = (cols[None, :] >= c0[:, None]) & (cols[None, :] < c0[:, None] + NA_COLS)
    dc = np.clip(cols[None, :] - cols[:, None] + NA_COLS - 1, 0, 2 * NA_COLS - 2)
    onehot = (dc[None] == np.arange(2 * NA_COLS - 1)[:, None, None]).astype(np.float32)
    toe = jnp.einsum("hrj,jqk->hrqk", rel_bias, jnp.asarray(onehot), precision=lax.Precision.HIGHEST)
    toe = jnp.where(col_ok[None, None], toe, NEG_INF)
    tabs = [jnp.transpose(toe[:, NA_MAX_ROWS - 1 - o: 2 * NA_MAX_ROWS - 1 - o], (0, 2, 1, 3))
            for o in range(NA_MAX_ROWS)]
    return jnp.stack(tabs).reshape(NA_MAX_ROWS, NA_HEADS, GRID_W, NA_WIN)


def _na_row_start(i):
    return jnp.clip(i - NA_MAX_ROWS // 2, 0, NA_ROWS - NA_MAX_ROWS)


def _na_kernel(q_ref, k_ref, v_ref, ck_ref, cv_ref, bias_ref, o_ref):
    scale = HEAD_DIM ** -0.5
    i = pl.program_id(1)
    start = pl.multiple_of(_na_row_start(i) * GRID_W, GRID_W)
    kw = k_ref[pl.ds(start, NA_WIN), :]
    vw = v_ref[pl.ds(start, NA_WIN), :]
    outs = []
    for h in range(NA_HEADS):
        sl = slice(h * HEAD_DIM, (h + 1) * HEAD_DIM)
        q = q_ref[:, sl].astype(BF16)
        s_loc = _dot_nt(q, kw[:, sl].astype(BF16)) * scale + bias_ref[0, h]
        s_ctx = _dot_nt(q, ck_ref[:, sl].astype(BF16)) * scale
        m = jnp.maximum(jnp.max(s_loc, axis=-1, keepdims=True), jnp.max(s_ctx, axis=-1, keepdims=True))
        p_loc = jnp.exp(s_loc - m)
        p_ctx = jnp.exp(s_ctx - m)
        den = jnp.sum(p_loc, axis=-1, keepdims=True) + jnp.sum(p_ctx, axis=-1, keepdims=True)
        outs.append(_dot((p_loc / den).astype(BF16), vw[:, sl].astype(BF16))
                    + _dot((p_ctx / den).astype(BF16), cv_ref[:, sl].astype(BF16)))
    o_ref[...] = jnp.concatenate(outs, axis=-1)


def neighbourhood_attention(q, k, v, ck, cv, bias_tab):
    seq = pl.BlockSpec((DEC_SEQ, NA_DIM), lambda b, i: (P_ROWS // DEC_SEQ + b, 0))
    ctx = pl.BlockSpec((PAST_LEN, NA_DIM), lambda b, i: (b, 0))
    row = pl.BlockSpec((GRID_W, NA_DIM), lambda b, i: (b * NA_ROWS + i, 0))
    qrow = pl.BlockSpec((GRID_W, NA_DIM), lambda b, i: (P_ROWS // GRID_W + b * NA_ROWS + i, 0))
    return pl.pallas_call(
        _na_kernel, name="na_attn",
        grid=(DEC_BATCH, NA_ROWS),
        in_specs=[qrow, seq, seq, ctx, ctx,
                  pl.BlockSpec((1, NA_HEADS, GRID_W, NA_WIN), lambda b, i: (i - _na_row_start(i), 0, 0, 0))],
        out_specs=row,
        out_shape=jax.ShapeDtypeStruct((S_ROWS, NA_DIM), F32),
        compiler_params=_cparams(("parallel", "arbitrary")),
    )(q, k, v, ck, cv, bias_tab)


def _proj_res_kernel(ap_ref, as_ref, bp_ref, bs_ref, x_ref, gate_ref, w_ref, o_ref, *, na, p_tiles):
    is_p = pl.program_id(0) < p_tiles
    a = jnp.where(is_p, ap_ref[...], as_ref[...]).astype(BF16)
    b = jnp.where(is_p, bp_ref[...], bs_ref[...]).astype(BF16)
    y = _dot(a, w_ref[:na, :]) + _dot(b, w_ref[na:, :])
    o_ref[...] = x_ref[...] + gate_ref[0] * y


def proj_residual(a_p, a_s, b_p, b_s, x, gate, w, tm=512):
    na, nb = a_p.shape[1], b_p.shape[1]
    si = _set_index(tm)
    pt = P_ROWS // tm
    p_spec = lambda n: pl.BlockSpec((tm, n), lambda i: (jnp.minimum(i, pt - 1), 0))
    s_spec = lambda n: pl.BlockSpec((tm, n), lambda i: (jnp.maximum(i - pt, 0), 0))
    return pl.pallas_call(
        functools.partial(_proj_res_kernel, na=na, p_tiles=pt), name="proj_residual",
        grid=(N_ROWS // tm,),
        in_specs=[p_spec(na), s_spec(na), p_spec(nb), s_spec(nb),
                  pl.BlockSpec((tm, D_MODEL), lambda i: (i, 0)),
                  pl.BlockSpec((1, 1, D_MODEL), lambda i: (si(i), 0, 0)),
                  pl.BlockSpec((na + nb, D_MODEL), lambda i: (0, 0))],
        out_specs=pl.BlockSpec((tm, D_MODEL), lambda i: (i, 0)),
        out_shape=jax.ShapeDtypeStruct((N_ROWS, D_MODEL), F32),
        compiler_params=_cparams(("parallel",)),
    )(a_p, a_s, b_p, b_s, x, gate, w)


def _ffn_kernel(x_ref, g_ref, sh_ref, sc_ref, gate_ref, wg_ref, wu_ref, wd_ref, o_ref, h_scr, acc_scr):
    j = pl.program_id(1)

    @pl.when(j == 0)
    def _():
        h_scr[...] = _modulated(x_ref[...], g_ref[...], sh_ref[0], sc_ref[0]).astype(BF16)
        acc_scr[...] = jnp.zeros_like(acc_scr)

    h = h_scr[...]
    gt = _dot(h, wg_ref[...])
    act = (gt * _sigmoid(gt)) * _dot(h, wu_ref[...])
    acc_scr[...] += _dot(act.astype(BF16), wd_ref[...])

    @pl.when(j == pl.num_programs(1) - 1)
    def _():
        o_ref[...] = x_ref[...] + gate_ref[0] * acc_scr[...]


def ffn_residual(x, g, sh, sc, gate, wg, wu, wd, tm=1024, tf=256):
    si = _set_index(tm)
    vec = pl.BlockSpec((1, 1, D_MODEL), lambda i, j: (si(i), 0, 0))
    return pl.pallas_call(
        _ffn_kernel, name="ffn",
        grid=(N_ROWS // tm, D_FF // tf),
        in_specs=[pl.BlockSpec((tm, D_MODEL), lambda i, j: (i, 0)),
                  pl.BlockSpec((1, D_MODEL), lambda i, j: (0, 0)),
                  vec, vec, vec,
                  pl.BlockSpec((D_MODEL, tf), lambda i, j: (0, j)),
                  pl.BlockSpec((D_MODEL, tf), lambda i, j: (0, j)),
                  pl.BlockSpec((tf, D_MODEL), lambda i, j: (j, 0))],
        out_specs=pl.BlockSpec((tm, D_MODEL), lambda i, j: (i, 0)),
        out_shape=jax.ShapeDtypeStruct((N_ROWS, D_MODEL), F32),
        scratch_shapes=[pltpu.VMEM((tm, D_MODEL), BF16), pltpu.VMEM((tm, D_MODEL), F32)],
        compiler_params=_cparams(("parallel", "arbitrary")),
    )(x, g.reshape(1, D_MODEL), sh, sc, gate, wg, wu, wd)


def _pool_consts(n):
    t = np.arange(n)
    mats, cnts = [], []
    for win in POOL_WINDOWS:
        lo = np.clip(t - win // 2, 0, n)
        hi = np.clip(t + win - win // 2, 0, n)
        mats.append(((t[None, :] >= lo[:, None]) & (t[None, :] < hi[:, None])).astype(np.float32))
        cnts.append(np.repeat((hi - lo).astype(np.float32)[:, None], POOL_GW, axis=1))
    return jnp.asarray(np.stack(mats)).astype(BF16), jnp.asarray(np.concatenate(cnts, axis=1))


def _pool_kernel(x_ref, pm_ref, cnt_ref, w_ref, scale_ref, o_ref):
    x = x_ref[...]
    hi = x.astype(BF16)
    lo = (x - hi.astype(F32)).astype(BF16)
    sums = []
    for g in range(len(POOL_WINDOWS)):
        sl = slice(g * POOL_GW, (g + 1) * POOL_GW)
        sums.append(_dot(pm_ref[g], hi[:, sl]) + _dot(pm_ref[g], lo[:, sl]))
    y = jnp.concatenate(sums, axis=-1) / cnt_ref[...] - x
    o_ref[...] = _dot(y.astype(BF16), w_ref[...]) * scale_ref[...]


def pool_mix(x, row0, nb, n, w_bd, scale):
    rows = nb * n
    b0 = row0 // n
    pm, cnt = _pool_consts(n)
    return pl.pallas_call(
        _pool_kernel, name="pool",
        grid=(rows // n,),
        in_specs=[pl.BlockSpec((n, POOL_CH), lambda b: (b + b0, 0)),
                  pl.BlockSpec((len(POOL_WINDOWS), n, n), lambda b: (0, 0, 0)),
                  pl.BlockSpec((n, POOL_CH), lambda b: (0, 0)),
                  pl.BlockSpec((POOL_CH, POOL_CH), lambda b: (0, 0)),
                  pl.BlockSpec((1, POOL_CH), lambda b: (0, 0))],
        out_specs=pl.BlockSpec((n, POOL_CH), lambda b: (b, 0)),
        out_shape=jax.ShapeDtypeStruct((rows, POOL_CH), F32),
        compiler_params=_cparams(("parallel",)),
    )(x, pm, cnt, w_bd, scale.reshape(1, POOL_CH))


RW_TILE = 256
HALO = 8
PACK_W = RWKV_HEADS * LANES


def _head_ones():
    h = np.arange(RWKV_DIM) // HEAD_DIM
    return jnp.asarray((h[:, None] == h[None, :]).astype(np.float32)).astype(BF16)


def _head_sum(x, ones):
    hi = x.astype(BF16)
    lo = (x - hi.astype(F32)).astype(BF16)
    return _dot(hi, ones) + _dot(lo, ones)


def _pack_heads(a, b, o_ref, lead):
    lane = lax.broadcasted_iota(jnp.int32, (a.shape[0], LANES), 1)
    low = lane < HEAD_DIM
    for c in range(RWKV_DIM // LANES):
        ac = a[:, c * LANES:(c + 1) * LANES]
        bc = b[:, c * LANES:(c + 1) * LANES]
        even = jnp.where(low, ac, pltpu.roll(bc, HEAD_DIM, 1))
        odd = jnp.where(low, pltpu.roll(ac, HEAD_DIM, 1), bc)
        o_ref[lead + (slice(None), slice((2 * c) * LANES, (2 * c + 1) * LANES))] = even
        o_ref[lead + (slice(None), slice((2 * c + 1) * LANES, (2 * c + 2) * LANES))] = odd


def _rwkv_prep_kernel(z_ref, zp_ref, zn_ref, mu_ref, kk_w_ref, ka_ref, rk_ref, w0_ref, a0_ref,
                      dup_ref, iup_ref, gup_ref, ones_ref,
                      g1_ref, g2_ref, g3_ref, g_ref, bonus_ref, *, tiles_per_seq):
    i = pl.program_id(0)
    pos = i % tiles_per_seq
    z = z_ref[...]
    row = lax.broadcasted_iota(jnp.int32, (RW_TILE, 1), 0)
    prev_edge = jnp.where(pos == 0, 0.0, zp_ref[HALO - 1:HALO, :])
    next_edge = jnp.where(pos == tiles_per_seq - 1, 0.0, zn_ref[0:1, :])
    prev = jnp.where(row == 0, prev_edge, pltpu.roll(z, 1, 0))
    nxt = jnp.where(row == RW_TILE - 1, next_edge, pltpu.roll(z, RW_TILE - 1, 0))
    zr = z + mu_ref[0:1, :] * (prev - z) + mu_ref[1:2, :] * (nxt - z)

    d = RWKV_DIM
    r, k, v = zr[:, :d], zr[:, d:2 * d], zr[:, 2 * d:3 * d]
    lora = 3 * d
    ones = ones_ref[...]
    kk = k * kk_w_ref[...]
    kk = kk * lax.rsqrt(_head_sum(kk * kk, ones) + L2_EPS)
    _pack_heads(r, v, g3_ref, ())
    for dr in range(2):
        wl = zr[:, lora + dr * DECAY_LORA: lora + (dr + 1) * DECAY_LORA]
        al = zr[:, lora + 2 * DECAY_LORA + dr * ICLR_LORA: lora + 2 * DECAY_LORA + (dr + 1) * ICLR_LORA]
        lw = w0_ref[dr:dr + 1, :] + _dot(jnp.tanh(wl).astype(BF16), dup_ref[dr])
        w = jnp.exp(-DECAY_SCALE * _sigmoid(lw))
        a = _sigmoid(a0_ref[dr:dr + 1, :] + _dot(al.astype(BF16), iup_ref[dr]))
        _pack_heads(w, kk * a, g1_ref, (dr,))
        _pack_heads(k * (1.0 + (a - 1.0) * ka_ref[...]), kk, g2_ref, (dr,))
    gl = zr[:, lora + 2 * DECAY_LORA + 2 * ICLR_LORA:]
    g_ref[...] = _dot(_sigmoid(gl).astype(BF16), gup_ref[...])
    bonus_ref[...] = _head_sum(r * k * rk_ref[...], ones) * v


def rwkv_prep(z, row0, nb, n, mu, k_k, k_a, r_k, w0, a0, dup, iup, gup):
    rows = nb * n
    tps = n // RW_TILE
    t0 = row0 // RW_TILE
    hb = RW_TILE // HALO
    last = z.shape[0] // HALO - 1
    d = RWKV_DIM
    full2 = lambda shape: pl.BlockSpec(shape, lambda i: (0, 0))
    full3 = lambda shape: pl.BlockSpec(shape, lambda i: (0, 0, 0))
    tile = pl.BlockSpec((RW_TILE, d), lambda i: (i, 0))
    pk2 = pl.BlockSpec((2, RW_TILE, PACK_W), lambda i: (0, i % tps, i // tps))
    pk1 = pl.BlockSpec((RW_TILE, PACK_W), lambda i: (i % tps, i // tps))
    return pl.pallas_call(
        functools.partial(_rwkv_prep_kernel, tiles_per_seq=tps), name="rwkv_prep",
        grid=(rows // RW_TILE,),
        in_specs=[pl.BlockSpec((RW_TILE, RWKV_IN), lambda i: (i + t0, 0)),
                  pl.BlockSpec((HALO, RWKV_IN), lambda i: (jnp.maximum((i + t0) * hb - 1, 0), 0)),
                  pl.BlockSpec((HALO, RWKV_IN), lambda i: (jnp.minimum((i + t0 + 1) * hb, last), 0)),
                  full2((2, RWKV_IN)), full2((1, d)), full2((1, d)), full2((1, d)),
                  full2((2, d)), full2((2, d)),
                  full3((2, DECAY_LORA, d)), full3((2, ICLR_LORA, d)), full2((GATE_LORA, d)),
                  full2((d, d))],
        out_specs=[pk2, pk2, pk1, tile, tile],
        out_shape=[jax.ShapeDtypeStruct((2, n, nb * PACK_W), F32),
                   jax.ShapeDtypeStruct((2, n, nb * PACK_W), F32),
                   jax.ShapeDtypeStruct((n, nb * PACK_W), F32),
                   jax.ShapeDtypeStruct((rows, d), F32),
                   jax.ShapeDtypeStruct((rows, d), F32)],
        compiler_params=_cparams(("parallel",)),
    )(z, z, z, mu, k_k.reshape(1, d), k_a.reshape(1, d), r_k.reshape(1, d), w0, a0,
      dup.astype(BF16), iup.astype(BF16), gup.astype(BF16), _head_ones())


SCAN_TC = 32
SLOTS = 4
V_BLOCK = 32


K_CHUNK = 16
N_KC = HEAD_DIM // K_CHUNK


def _wkv_first_sa(s_scr, t2, vs):
    sas = []
    for vb in range(vs // V_BLOCK):
        rows = slice(vb * V_BLOCK, (vb + 1) * V_BLOCK)

        def chunk(kc, sa):
            for j in range(K_CHUNK):
                sa = sa + s_scr[kc * K_CHUNK + j, rows, :] * t2[N_KC + kc, j:j + 1, :]
            return sa

        sas.append(lax.fori_loop(0, N_KC, chunk, jnp.zeros((V_BLOCK, LANES), F32)))
    return tuple(sas)


def _wkv_step(s_scr, t1, t2, t3, t2_next, v_blocks, o_ref, sas):
    nxt = []
    for vb, v_blk in enumerate(v_blocks):
        rows = slice(vb * V_BLOCK, (vb + 1) * V_BLOCK)
        sa = sas[vb]

        def chunk(kc, carry):
            o, sa_n = carry
            for j in range(K_CHUNK):
                k = kc * K_CHUNK + j
                s_new = (s_scr[k, rows, :] * t1[kc, j:j + 1, :] - sa * t1[N_KC + kc, j:j + 1, :]
                         + v_blk * t2[kc, j:j + 1, :])
                s_scr[k, rows, :] = s_new
                o = o + s_new * t3[kc, j:j + 1, :]
                sa_n = sa_n + s_new * t2_next[N_KC + kc, j:j + 1, :]
            return o, sa_n

        zero = jnp.zeros((V_BLOCK, LANES), F32)
        o, sa_n = lax.fori_loop(1, N_KC, chunk, chunk(0, (zero, zero)))
        o_ref[rows, :] = o
        nxt.append(sa_n)
    return tuple(nxt)


def _store_tile(ref, idx, x):
    ref[idx] = x.reshape(2 * N_KC, K_CHUNK, LANES)


def _scan_prompt_kernel(g1_ref, g2_ref, g3_ref, o_ref, st_ref, s_scr, ta, tb, tc, td, oa, ob):
    d = pl.program_id(0)
    c = pl.program_id(2)
    tiles = (ta, tb, tc, td)
    outs = (oa, ob)

    @pl.when(c == 0)
    def _():
        s_scr[...] = jnp.zeros_like(s_scr)

    for o_scr in outs:
        o_scr[...] = jnp.zeros_like(o_scr)

    def tl(t):
        return jnp.where(d == 0, t, SCAN_TC - 1 - t)

    def load_tiles(t, slot):
        i = tl(t)
        _store_tile(tiles[slot], 0, g1_ref[0, i].T)
        _store_tile(tiles[slot], 1, g2_ref[0, i].T)
        _store_tile(tiles[slot], 2, g3_ref[i].T)

    load_tiles(0, 0)
    load_tiles(1, 1)
    sas0 = _wkv_first_sa(s_scr, tiles[0].at[1], HEAD_DIM)

    def steps(i, sas):
        for j in range(SLOTS):
            t = i * SLOTS + j
            cur, nxt = tiles[j], tiles[(j + 1) % SLOTS]
            load_tiles(jnp.minimum(t + 2, SCAN_TC - 1), (j + 2) % SLOTS)
            o_ref[0, tl(jnp.maximum(t - 1, 0))] = outs[(j + 1) % 2][...].T
            per_blk = V_BLOCK // K_CHUNK
            v_blocks = [cur[2, N_KC + vb * per_blk: N_KC + (vb + 1) * per_blk].reshape(V_BLOCK, LANES)
                        for vb in range(HEAD_DIM // V_BLOCK)]
            sas = _wkv_step(s_scr, cur.at[0], cur.at[1], cur.at[2], nxt.at[1], v_blocks, outs[j % 2], sas)
        return sas

    lax.fori_loop(0, SCAN_TC // SLOTS, steps, sas0)
    o_ref[0, tl(SCAN_TC - 1)] = outs[(SCAN_TC - 1) % 2][...].T

    @pl.when(c == pl.num_programs(2) - 1)
    def _():
        st_ref[0, 0] = s_scr[...]


def wkv_scan_prompt(g1, g2, g3):
    t_len = g3.shape[0]
    chains = g3.shape[1] // LANES
    g1 = g1.reshape(2, t_len, chains, LANES)
    g2 = g2.reshape(2, t_len, chains, LANES)
    g3 = g3.reshape(t_len, chains, LANES)
    nblk = t_len // SCAN_TC
    tb = lambda d, s: jnp.where(d == 0, s, nblk - 1 - s)
    dir_blk = pl.BlockSpec((1, SCAN_TC, LANES, LANES), lambda d, g, s: (d, tb(d, s), g, 0))
    return pl.pallas_call(
        _scan_prompt_kernel, name="wkv_scan_prompt",
        grid=(2, chains // LANES, nblk),
        in_specs=[dir_blk, dir_blk,
                  pl.BlockSpec((SCAN_TC, LANES, LANES), lambda d, g, s: (tb(d, s), g, 0))],
        out_specs=[dir_blk,
                   pl.BlockSpec((1, 1, HEAD_DIM, HEAD_DIM, LANES), lambda d, g, s: (d, g, 0, 0, 0))],
        out_shape=[jax.ShapeDtypeStruct((2, t_len, chains, LANES), F32),
                   jax.ShapeDtypeStruct((2, chains // LANES, HEAD_DIM, HEAD_DIM, LANES), F32)],
        scratch_shapes=([pltpu.VMEM((HEAD_DIM, HEAD_DIM, LANES), F32)]
                        + [pltpu.VMEM((3, 2 * N_KC, K_CHUNK, LANES), F32)] * SLOTS
                        + [pltpu.VMEM((LANES, LANES), F32)] * 2),
        compiler_params=_cparams(("parallel", "parallel", "arbitrary")),
    )(g1, g2, g3)


S_CHAINS = DEC_BATCH * RWKV_HEADS
S_VS = HEAD_DIM // 2


def _scan_sample_kernel(g1f_ref, g1b_ref, g2f_ref, g2b_ref, g3f_ref, g3b_ref, s0_ref,
                        of_ref, ob_ref, s_scr, ta, tb, tc, td, va, vb, vc, vd, oa, ob):
    c = pl.program_id(0)
    tiles = (ta, tb, tc, td)
    vals = (va, vb, vc, vd)
    outs = (oa, ob)

    @pl.when(c == 0)
    def _():
        s_scr[...] = s0_ref[...]

    for o_scr in outs:
        o_scr[...] = jnp.zeros_like(o_scr)
    nc = 2 * S_CHAINS
    zpad = jnp.zeros((LANES - 2 * nc, LANES), F32)
    lane = lax.broadcasted_iota(jnp.int32, (S_VS, LANES), 1)

    def stacked_t(f_ref, b_ref, t):
        f = f_ref[0, t]
        b = b_ref[0, SCAN_TC - 1 - t]
        return jnp.concatenate([f, b, f, b, zpad], axis=0).T

    def load_tiles(t, slot):
        _store_tile(tiles[slot], 0, stacked_t(g1f_ref, g1b_ref, t))
        _store_tile(tiles[slot], 1, stacked_t(g2f_ref, g2b_ref, t))
        t3 = stacked_t(g3f_ref, g3b_ref, t)
        _store_tile(tiles[slot], 2, t3)
        vals[slot][...] = jnp.where(lane < nc, t3[HEAD_DIM:HEAD_DIM + S_VS], t3[HEAD_DIM + S_VS:])

    def flush(t, o_scr):
        o = o_scr[...]
        full = jnp.concatenate([o, pltpu.roll(o, LANES - nc, 1), jnp.zeros((LANES - HEAD_DIM, LANES), F32)], axis=0)
        ot = full.T
        of_ref[0, t] = ot[0:S_CHAINS]
        ob_ref[0, SCAN_TC - 1 - t] = ot[S_CHAINS:nc]

    load_tiles(0, 0)
    load_tiles(1, 1)
    sas0 = _wkv_first_sa(s_scr, tiles[0].at[1], S_VS)

    def steps(i, sas):
        for j in range(SLOTS):
            t = i * SLOTS + j
            cur, nxt = tiles[j], tiles[(j + 1) % SLOTS]
            load_tiles(jnp.minimum(t + 2, SCAN_TC - 1), (j + 2) % SLOTS)
            flush(jnp.maximum(t - 1, 0), outs[(j + 1) % 2])
            sas = _wkv_step(s_scr, cur.at[0], cur.at[1], cur.at[2], nxt.at[1], [vals[j][...]], outs[j % 2], sas)
        return sas

    lax.fori_loop(0, SCAN_TC // SLOTS, steps, sas0)
    flush(SCAN_TC - 1, outs[(SCAN_TC - 1) % 2])


def wkv_scan_sample(g1, g2, g3, s0):
    t_len = g3.shape[0]
    g1 = g1.reshape(2, t_len, S_CHAINS, LANES)
    g2 = g2.reshape(2, t_len, S_CHAINS, LANES)
    g3 = g3.reshape(1, t_len, S_CHAINS, LANES)
    nblk = t_len // SCAN_TC
    blk = (1, SCAN_TC, S_CHAINS, LANES)
    fwd = lambda d: pl.BlockSpec(blk, lambda s: (d, s, 0, 0))
    bwd = lambda d: pl.BlockSpec(blk, lambda s: (d, nblk - 1 - s, 0, 0))
    return pl.pallas_call(
        _scan_sample_kernel, name="wkv_scan_sample",
        grid=(nblk,),
        in_specs=[fwd(0), bwd(1), fwd(0), bwd(1), fwd(0), bwd(0),
                  pl.BlockSpec((HEAD_DIM, S_VS, LANES), lambda s: (0, 0, 0))],
        out_specs=[fwd(0), bwd(0)],
        out_shape=[jax.ShapeDtypeStruct((1, t_len, S_CHAINS, LANES), F32)] * 2,
        scratch_shapes=([pltpu.VMEM((HEAD_DIM, S_VS, LANES), F32)]
                        + [pltpu.VMEM((3, 2 * N_KC, K_CHUNK, LANES), F32)] * SLOTS
                        + [pltpu.VMEM((S_VS, LANES), F32)] * SLOTS
                        + [pltpu.VMEM((S_VS, LANES), F32)] * 2),
        compiler_params=_cparams(("arbitrary",)),
    )(g1, g1, g2, g2, g3, g3, s0)


def _sample_state_lanes(s0):
    nc = 2 * S_CHAINS
    st = jnp.transpose(s0, (4, 3, 1, 0, 2)).reshape(HEAD_DIM, HEAD_DIM, nc)
    st = jnp.concatenate([st[:, :S_VS], st[:, S_VS:]], axis=-1)
    return jnp.pad(st, ((0, 0), (0, 0), (0, LANES - 2 * nc)))


def _rwkv_post_kernel(of_ref, ob_ref, g_ref, bonus_ref, gw_ref, gb_ref, ones_ref, y_ref):
    ones = ones_ref[...]
    lane = lax.broadcasted_iota(jnp.int32, (RW_TILE, LANES), 1)
    low = lane < HEAD_DIM
    cols = []
    for c in range(RWKV_DIM // LANES):
        even = slice((2 * c) * LANES, (2 * c + 1) * LANES)
        odd = slice((2 * c + 1) * LANES, (2 * c + 2) * LANES)
        oe = of_ref[0, :, even] + ob_ref[0, :, even]
        oo = of_ref[0, :, odd] + ob_ref[0, :, odd]
        cols.append(jnp.where(low, oe, pltpu.roll(oo, HEAD_DIM, 1)))
    o = jnp.concatenate(cols, axis=-1)
    mu = _head_sum(o, ones) / HEAD_DIM
    oc = o - mu
    var = _head_sum(oc * oc, ones) / HEAD_DIM
    on = (oc * lax.rsqrt(var + GN_EPS)) * gw_ref[...] + gb_ref[...]
    y_ref[...] = (on + bonus_ref[...]) * g_ref[...]


def rwkv_post(o_f, o_b, df, db, nb, n, g, bonus, gn_w, gn_b):
    rows, d = g.shape
    tps = n // RW_TILE
    tile = pl.BlockSpec((RW_TILE, d), lambda i: (i, 0))
    vec = pl.BlockSpec((1, d), lambda i: (0, 0))
    pk = lambda dd: pl.BlockSpec((1, RW_TILE, PACK_W), lambda i: (dd, i % tps, i // tps))
    return pl.pallas_call(
        _rwkv_post_kernel, name="rwkv_post",
        grid=(rows // RW_TILE,),
        in_specs=[pk(df), pk(db), tile, tile, vec, vec, pl.BlockSpec((d, d), lambda i: (0, 0))],
        out_specs=tile,
        out_shape=jax.ShapeDtypeStruct((rows, d), F32),
        compiler_params=_cparams(("parallel",)),
    )(o_f, o_b, g, bonus, gn_w.reshape(1, d), gn_b.reshape(1, d), _head_ones())


def _split3(x):
    a = x.astype(BF16)
    r1 = x - a.astype(F32)
    b = r1.astype(BF16)
    c = (r1 - b.astype(F32)).astype(BF16)
    return a, b, c


def _router_kernel(x_ref, g_ref, sh_ref, sc_ref, w_ref, b_ref, h_ref, comb_ref):
    h = _modulated(x_ref[...], g_ref[...], sh_ref[0], sc_ref[0])
    h_ref[...] = h.astype(BF16)
    h1, h2, h3 = _split3(h)
    w1, w2, w3 = _split3(w_ref[...])
    logits = (_dot(h1, w1) + (_dot(h1, w2) + _dot(h2, w1))
              + (_dot(h1, w3) + _dot(h2, w2) + _dot(h3, w1))) + b_ref[...]
    col = lax.broadcasted_iota(jnp.int32, logits.shape, 1)
    logits = jnp.where(col < N_EXPERTS, logits, -jnp.inf)
    m1 = jnp.max(logits, axis=-1, keepdims=True)
    i1 = jnp.min(jnp.where(logits == m1, col, LANES), axis=-1, keepdims=True)
    rest = jnp.where(col == i1, -jnp.inf, logits)
    m2 = jnp.max(rest, axis=-1, keepdims=True)
    i2 = jnp.min(jnp.where(rest == m2, col, LANES), axis=-1, keepdims=True)
    e2 = jnp.exp(m2 - m1)
    den = 1.0 + e2
    comb_ref[...] = jnp.where(col == i1, 1.0 / den, 0.0) + jnp.where(col == i2, e2 / den, 0.0)


def moe_router(x, g, sh, sc, router_w, router_b, tm=512):
    si = _set_index(tm)
    vec = pl.BlockSpec((1, 1, D_MODEL), lambda i: (si(i), 0, 0))
    w = jnp.pad(router_w, ((0, 0), (0, LANES - N_EXPERTS)))
    b = jnp.pad(router_b, (0, LANES - N_EXPERTS)).reshape(1, LANES)
    return pl.pallas_call(
        _router_kernel, name="moe_router",
        grid=(N_ROWS // tm,),
        in_specs=[pl.BlockSpec((tm, D_MODEL), lambda i: (i, 0)),
                  pl.BlockSpec((1, D_MODEL), lambda i: (0, 0)),
                  vec, vec,
                  pl.BlockSpec((D_MODEL, LANES), lambda i: (0, 0)),
                  pl.BlockSpec((1, LANES), lambda i: (0, 0))],
        out_specs=[pl.BlockSpec((tm, D_MODEL), lambda i: (i, 0)),
                   pl.BlockSpec((tm, LANES), lambda i: (i, 0))],
        out_shape=[jax.ShapeDtypeStruct((N_ROWS, D_MODEL), BF16),
                   jax.ShapeDtypeStruct((N_ROWS, LANES), F32)],
        compiler_params=_cparams(("parallel",)),
    )(x, g.reshape(1, D_MODEL), sh, sc, w, b)


def _moe_kernel(x_ref, h_ref, comb_ref, gate_ref, gfin_ref, wg_ref, wu_ref, wd_ref, o_ref, acc_scr):
    e = pl.program_id(1)

    @pl.when(e == 0)
    def _():
        acc_scr[...] = jnp.zeros_like(acc_scr)

    comb = comb_ref[...]
    col = lax.broadcasted_iota(jnp.int32, comb.shape, 1)
    ce = jnp.sum(jnp.where(col == e, comb, 0.0), axis=-1, keepdims=True)
    h = h_ref[...]
    gt = _dot(h, wg_ref[0])
    act = (gt * _sigmoid(gt)) * _dot(h, wu_ref[0])
    acc_scr[...] += ce * _dot(act.astype(BF16), wd_ref[0])

    @pl.when(e == pl.num_programs(1) - 1)
    def _():
        y = x_ref[...] + gate_ref[0] * acc_scr[...]
        ms = jnp.mean(y * y, axis=-1, keepdims=True)
        o_ref[...] = y * lax.rsqrt(ms + RMS_EPS) * gfin_ref[...]


def moe_residual_norm(x, h, comb, gate, g_final, wg, wu, wd, tm=512):
    si = _set_index(tm)
    return pl.pallas_call(
        _moe_kernel, name="moe_experts",
        grid=(N_ROWS // tm, N_EXPERTS),
        in_specs=[pl.BlockSpec((tm, D_MODEL), lambda i, e: (i, 0)),
                  pl.BlockSpec((tm, D_MODEL), lambda i, e: (i, 0)),
                  pl.BlockSpec((tm, LANES), lambda i, e: (i, 0)),
                  pl.BlockSpec((1, 1, D_MODEL), lambda i, e: (si(i), 0, 0)),
                  pl.BlockSpec((1, D_MODEL), lambda i, e: (0, 0)),
                  pl.BlockSpec((1, D_MODEL, D_FF_EXPERT), lambda i, e: (e, 0, 0)),
                  pl.BlockSpec((1, D_MODEL, D_FF_EXPERT), lambda i, e: (e, 0, 0)),
                  pl.BlockSpec((1, D_FF_EXPERT, D_MODEL), lambda i, e: (e, 0, 0))],
        out_specs=pl.BlockSpec((tm, D_MODEL), lambda i, e: (i, 0)),
        out_shape=jax.ShapeDtypeStruct((N_ROWS, D_MODEL), F32),
        scratch_shapes=[pltpu.VMEM((tm, D_MODEL), F32)],
        compiler_params=_cparams(("parallel", "arbitrary")),
    )(x, h, comb, gate, g_final.reshape(1, D_MODEL), wg, wu, wd)


def kernel(x_prompt, x_sample, cache_na_k, cache_na_v, state_wkv, c, c_ctx, mod_w, mod_b, norm_mix, norm_ffn, norm_final, na_w_in, fourier_w, na_rel_bias, na_w_out, ffn_w_gate, ffn_w_up, ffn_w_down, rw_w_in, pool_w, pool_scale, shift_mu, decay_w0, decay_up, iclr_a0, iclr_up, gate_up, k_k, k_a, r_k, gn_w, gn_b, rw_w_out, router_w, router_b, moe_w_gate, moe_w_up, moe_w_down):
    x = jnp.concatenate([x_prompt.reshape(P_ROWS, D_MODEL), x_sample.reshape(S_ROWS, D_MODEL)], axis=0)
    cond = jnp.concatenate([c_ctx[None, :], c, jnp.zeros((8 - N_SETS, D_MODEL), F32)], axis=0)
    mods = adaln_all(cond, mod_w, mod_b)[:, :N_SETS].reshape(DEPTH, N_SETS, 6, 1, D_MODEL)

    sh1, sc1, g1, sh2, sc2, g2 = [mods[0, :, m] for m in range(6)]
    splits = ((0, FOURIER_CH), (FOURIER_CH, FOURIER_CH + NA_DIM),
              (FOURIER_CH + NA_DIM, FOURIER_CH + 2 * NA_DIM), (FOURIER_CH + 2 * NA_DIM, FOURIER_CH + 3 * NA_DIM))
    f, q, k, v = modulated_matmul(x, norm_mix[0], sh1, sc1, na_w_in[0].astype(BF16), splits)
    f_bd = _block_diag(fourier_w[0]).astype(BF16)
    ck = cache_na_k[:, 0].reshape(DEC_BATCH * PAST_LEN, NA_DIM)
    cv = cache_na_v[:, 0].reshape(DEC_BATCH * PAST_LEN, NA_DIM)
    x = proj_residual(fourier_mix(f, 0, BATCH, SEQ, f_bd), fourier_mix(f, P_ROWS, DEC_BATCH, DEC_SEQ, f_bd),
                      context_attention(q, k, v),
                      neighbourhood_attention(q, k, v, ck, cv, _na_bias_table(na_rel_bias[0])),
                      x, g1, na_w_out[0].astype(BF16))
    new_k = k[:P_ROWS].reshape(BATCH, 1, SEQ, NA_HEADS, HEAD_DIM)
    new_v = v[:P_ROWS].reshape(BATCH, 1, SEQ, NA_HEADS, HEAD_DIM)
    x = ffn_residual(x, norm_ffn[0], sh2, sc2, g2, ffn_w_gate[0].astype(BF16), ffn_w_up[0].astype(BF16),
                     ffn_w_down[0].astype(BF16))

    sh1, sc1, g1, sh2, sc2, g2 = [mods[1, :, m] for m in range(6)]
    pc, z = modulated_matmul(x, norm_mix[1], sh1, sc1, rw_w_in[0].astype(BF16),
                             ((0, POOL_CH), (POOL_CH, POOL_CH + RWKV_IN)))
    p_bd = _block_diag(pool_w[0]).astype(BF16)
    rw = (shift_mu[0], k_k[0], k_a[0], r_k[0], decay_w0[0], iclr_a0[0], decay_up[0], iclr_up[0], gate_up[0])
    g1p, g2p, g3p, gate_p, bonus_p = rwkv_prep(z, 0, BATCH, SEQ, *rw)
    g1s, g2s, g3s, gate_s, bonus_s = rwkv_prep(z, P_ROWS, DEC_BATCH, DEC_SEQ, *rw)
    o_p, st_p = wkv_scan_prompt(g1p, g2p, g3p)
    of_s, ob_s = wkv_scan_sample(g1s, g2s, g3s, _sample_state_lanes(state_wkv[:, 0]))
    o_p = o_p.reshape(2, SEQ, BATCH * PACK_W)
    d_p = rwkv_post(o_p, o_p, 0, 1, BATCH, SEQ, gate_p, bonus_p, gn_w[0], gn_b[0])
    d_s = rwkv_post(of_s.reshape(1, DEC_SEQ, DEC_BATCH * PACK_W), ob_s.reshape(1, DEC_SEQ, DEC_BATCH * PACK_W),
                    0, 0, DEC_BATCH, DEC_SEQ, gate_s, bonus_s, gn_w[0], gn_b[0])
    x = proj_residual(pool_mix(pc, 0, BATCH, SEQ, p_bd, pool_scale[0]),
                      pool_mix(pc, P_ROWS, DEC_BATCH, DEC_SEQ, p_bd, pool_scale[0]),
                      d_p, d_s, x, g1, rw_w_out[0].astype(BF16))
    st_p = jnp.transpose(st_p, (0, 1, 4, 3, 2)).reshape(2, BATCH, RWKV_HEADS, HEAD_DIM, HEAD_DIM)
    st_p = jnp.transpose(st_p, (1, 0, 2, 3, 4))
    h, comb = moe_router(x, norm_ffn[1], sh2, sc2, router_w[0], router_b[0])
    y = moe_residual_norm(x, h, comb, g2, norm_final, moe_w_gate[0].astype(BF16), moe_w_up[0].astype(BF16),
                          moe_w_down[0].astype(BF16))

    y_prompt = y[:P_ROWS].reshape(BATCH, SEQ, D_MODEL)
    y_sample = y[P_ROWS:].reshape(DEC_BATCH, DEC_SEQ, D_MODEL)
    return (y_prompt, y_sample, new_k, new_v, st_p[:, None])
```

```python
import functools
import math

import numpy as np
import jax
import jax.numpy as jnp
from jax import lax
from jax.experimental import pallas as pl
from jax.experimental.pallas import tpu as pltpu

F32 = jnp.float32
BF16 = jnp.bfloat16

D_MODEL = 1024
BATCH = 32
SEQ = 256
DEPTH = 2
DEC_BATCH = 2
DEC_SEQ = 1024
PAST_LEN = 512
GRID_W = 64
HEAD_DIM = 64
FOURIER_CH = D_MODEL // 4
FOURIER_GROUPS = 4
FOURIER_GW = FOURIER_CH // FOURIER_GROUPS
NA_DIM = D_MODEL - FOURIER_CH
NA_HEADS = NA_DIM // HEAD_DIM
NA_MAX_ROWS = 8
NA_COLS = 16
POOL_WINDOWS = (2, 4, 8, 16)
POOL_CH = D_MODEL // 4
POOL_GW = POOL_CH // len(POOL_WINDOWS)
RWKV_DIM = D_MODEL - POOL_CH
RWKV_HEADS = RWKV_DIM // HEAD_DIM
DECAY_LORA = 64
ICLR_LORA = 64
GATE_LORA = 128
RWKV_IN = 3 * RWKV_DIM + 2 * DECAY_LORA + 2 * ICLR_LORA + GATE_LORA
D_FF = 2816
N_EXPERTS = 8
D_FF_EXPERT = 1408
RMS_EPS = 1e-6
GN_EPS = 64e-5
L2_EPS = 1e-12
DECAY_SCALE = math.exp(-0.5)
NEG_INF = -1e30

P_ROWS = BATCH * SEQ
S_ROWS = DEC_BATCH * DEC_SEQ
N_ROWS = P_ROWS + S_ROWS
N_SETS = 1 + DEC_BATCH
LANES = 128
VMEM_LIMIT = 56 * 1024 * 1024


def _cparams(sem):
    return pltpu.CompilerParams(dimension_semantics=sem, vmem_limit_bytes=VMEM_LIMIT)


def _sigmoid(x):
    return 1.0 / (1.0 + jnp.exp(-x))


def _dot(a, b):
    return jnp.dot(a, b, preferred_element_type=F32)


def _dot_nt(a, b):
    return lax.dot_general(a, b, (((1,), (1,)), ((), ())), preferred_element_type=F32)


def _set_index(tm, rows_per_set):
    q = rows_per_set // tm
    return lambda i: i // q


def _modulated(x, g, sh, sc):
    ms = jnp.mean(x * x, axis=-1, keepdims=True)
    return (x * lax.rsqrt(ms + RMS_EPS) * g) * (1.0 + sc) + sh


def _adaln_kernel(c_ref, w_ref, b_ref, o_ref):
    c = c_ref[...]
    s = (c * _sigmoid(c)).astype(BF16)
    o_ref[0] = _dot(s, w_ref[0].astype(BF16)) + b_ref[0]


def adaln_all(cond, mod_w, mod_b):
    tn = 1536
    n = 6 * D_MODEL
    return pl.pallas_call(
        _adaln_kernel, name="adaln",
        grid=(DEPTH, n // tn),
        in_specs=[pl.BlockSpec((8, D_MODEL), lambda l, j: (0, 0)),
                  pl.BlockSpec((1, D_MODEL, tn), lambda l, j: (l, 0, j)),
                  pl.BlockSpec((1, 1, tn), lambda l, j: (l, 0, j))],
        out_specs=pl.BlockSpec((1, 8, tn), lambda l, j: (l, 0, j)),
        out_shape=jax.ShapeDtypeStruct((DEPTH, 8, n), F32),
        compiler_params=_cparams(("parallel", "parallel")),
    )(cond, mod_w, mod_b.reshape(DEPTH, 1, n))


def _modmm_kernel(x_ref, g_ref, sh_ref, sc_ref, w_ref, *o_refs, splits):
    h = _modulated(x_ref[...], g_ref[...], sh_ref[0], sc_ref[0]).astype(BF16)
    for o_ref, (a, b) in zip(o_refs, splits):
        o_ref[...] = _dot(h, w_ref[:, a:b]).astype(o_ref.dtype)


def modulated_matmul(x, g, sh, sc, w, splits, rows_per_set, tm=512):
    rows = x.shape[0]
    n_out = w.shape[1]
    si = _set_index(tm, rows_per_set)
    vec = pl.BlockSpec((1, 1, D_MODEL), lambda i: (si(i), 0, 0))
    return pl.pallas_call(
        functools.partial(_modmm_kernel, splits=splits), name="modulated_matmul",
        grid=(rows // tm,),
        in_specs=[pl.BlockSpec((tm, D_MODEL), lambda i: (i, 0)),
                  pl.BlockSpec((1, D_MODEL), lambda i: (0, 0)),
                  vec, vec,
                  pl.BlockSpec((D_MODEL, n_out), lambda i: (0, 0))],
        out_specs=[pl.BlockSpec((tm, b - a), lambda i: (i, 0)) for a, b in splits],
        out_shape=[jax.ShapeDtypeStruct((rows, b - a), F32) for a, b in splits],
        compiler_params=_cparams(("parallel",)),
    )(x, g.reshape(1, D_MODEL), sh, sc, w)


def _dft_mats(n):
    t = np.arange(n)
    ang = 2.0 * np.pi * ((t[:, None] * t[None, :]) % n) / n
    cn, sn = np.cos(ang) / np.sqrt(n), np.sin(ang) / np.sqrt(n)
    c = np.arange(FOURIER_GW)
    angc = 2.0 * np.pi * ((c[:, None] * c[None, :]) % FOURIER_GW) / FOURIER_GW
    eye = np.eye(FOURIER_GROUPS)
    cc = np.kron(eye, np.cos(angc) / np.sqrt(FOURIER_GW))
    sc = np.kron(eye, np.sin(angc) / np.sqrt(FOURIER_GW))
    as_bf = lambda a: jnp.asarray(a, dtype=F32).astype(BF16)
    return as_bf(cn), as_bf(sn), as_bf(cc), as_bf(sc)


def _fourier_kernel(f_ref, cn_ref, sn_ref, cc_ref, sc_ref, w_ref, o_ref):
    x = f_ref[...].astype(BF16)
    a = _dot(x, cc_ref[...]).astype(BF16)
    b = _dot(x, sc_ref[...]).astype(BF16)
    re = _dot(cn_ref[...], a) - _dot(sn_ref[...], b)
    o_ref[...] = _dot(re.astype(BF16), w_ref[...])


def _block_diag(w):
    g, c, _ = w.shape
    eye = jnp.eye(g, dtype=w.dtype)
    return (eye[:, None, :, None] * w[:, :, None, :]).reshape(g * c, g * c)


def fourier_mix(f, n, w_bd):
    rows = f.shape[0]
    cn, sn, cc, sc = _dft_mats(n)
    full = lambda shape: pl.BlockSpec(shape, lambda b: (0, 0))
    return pl.pallas_call(
        _fourier_kernel, name="fourier",
        grid=(rows // n,),
        in_specs=[pl.BlockSpec((n, FOURIER_CH), lambda b: (b, 0)),
                  full((n, n)), full((n, n)),
                  full((FOURIER_CH, FOURIER_CH)), full((FOURIER_CH, FOURIER_CH)),
                  full((FOURIER_CH, FOURIER_CH))],
        out_specs=pl.BlockSpec((n, FOURIER_CH), lambda b: (b, 0)),
        out_shape=jax.ShapeDtypeStruct((rows, FOURIER_CH), F32),
        compiler_params=_cparams(("parallel",)),
    )(f, cn, sn, cc, sc, w_bd)


def _ctx_attn_kernel(q_ref, k_ref, v_ref, o_ref):
    scale = HEAD_DIM ** -0.5
    outs = []
    for h in range(NA_HEADS):
        sl = slice(h * HEAD_DIM, (h + 1) * HEAD_DIM)
        q = q_ref[:, sl].astype(BF16)
        k = k_ref[:, sl].astype(BF16)
        v = v_ref[:, sl].astype(BF16)
        s = _dot_nt(q, k) * scale
        p = jnp.exp(s - jnp.max(s, axis=-1, keepdims=True))
        p = p / jnp.sum(p, axis=-1, keepdims=True)
        outs.append(_dot(p.astype(BF16), v))
    o_ref[...] = jnp.concatenate(outs, axis=-1)


def context_attention(q, k, v):
    blk = pl.BlockSpec((SEQ, NA_DIM), lambda b: (b, 0))
    return pl.pallas_call(
        _ctx_attn_kernel, name="ctx_attn",
        grid=(BATCH,),
        in_specs=[blk, blk, blk],
        out_specs=blk,
        out_shape=jax.ShapeDtypeStruct((P_ROWS, NA_DIM), F32),
        compiler_params=_cparams(("parallel",)),
    )(q, k, v)


NA_ROWS = DEC_SEQ // GRID_W
NA_WIN = NA_MAX_ROWS * GRID_W


def _na_bias_table(rel_bias):
    cols = np.arange(GRID_W)
    c0 = np.clip(cols - NA_COLS // 2, 0, GRID_W - NA_COLS)
    col_ok = (cols[None, :] >= c0[:, None]) & (cols[None, :] < c0[:, None] + NA_COLS)
    dc = np.clip(cols[None, :] - cols[:, None] + NA_COLS - 1, 0, 2 * NA_COLS - 2)
    onehot = (dc[None] == np.arange(2 * NA_COLS - 1)[:, None, None]).astype(np.float32)
    toe = jnp.einsum("hrj,jqk->hrqk", rel_bias, jnp.asarray(onehot), precision=lax.Precision.HIGHEST)
    toe = jnp.where(col_ok[None, None], toe, NEG_INF)
    tabs = [jnp.transpose(toe[:, NA_MAX_ROWS - 1 - o: 2 * NA_MAX_ROWS - 1 - o], (0, 2, 1, 3))
            for o in range(NA_MAX_ROWS)]
    return jnp.stack(tabs).reshape(NA_MAX_ROWS, NA_HEADS, GRID_W, NA_WIN)


def _na_row_start(i):
    return jnp.clip(i - NA_MAX_ROWS // 2, 0, NA_ROWS - NA_MAX_ROWS)


def _na_kernel(q_ref, k_ref, v_ref, ck_ref, cv_ref, bias_ref, o_ref):
    scale = HEAD_DIM ** -0.5
    i = pl.program_id(1)
    start = pl.multiple_of(_na_row_start(i) * GRID_W, GRID_W)
    kw = k_ref[pl.ds(start, NA_WIN), :]
    vw = v_ref[pl.ds(start, NA_WIN), :]
    outs = []
    for h in range(NA_HEADS):
        sl = slice(h * HEAD_DIM, (h + 1) * HEAD_DIM)
        q = q_ref[:, sl].astype(BF16)
        s_loc = _dot_nt(q, kw[:, sl].astype(BF16)) * scale + bias_ref[0, h]
        s_ctx = _dot_nt(q, ck_ref[:, sl].astype(BF16)) * scale
        m = jnp.maximum(jnp.max(s_loc, axis=-1, keepdims=True), jnp.max(s_ctx, axis=-1, keepdims=True))
        p_loc = jnp.exp(s_loc - m)
        p_ctx = jnp.exp(s_ctx - m)
        den = jnp.sum(p_loc, axis=-1, keepdims=True) + jnp.sum(p_ctx, axis=-1, keepdims=True)
        outs.append(_dot((p_loc / den).astype(BF16), vw[:, sl].astype(BF16))
                    + _dot((p_ctx / den).astype(BF16), cv_ref[:, sl].astype(BF16)))
    o_ref[...] = jnp.concatenate(outs, axis=-1)


def neighbourhood_attention(q, k, v, ck, cv, bias_tab):
    seq = pl.BlockSpec((DEC_SEQ, NA_DIM), lambda b, i: (b, 0))
    ctx = pl.BlockSpec((PAST_LEN, NA_DIM), lambda b, i: (b, 0))
    row = pl.BlockSpec((GRID_W, NA_DIM), lambda b, i: (b * NA_ROWS + i, 0))
    return pl.pallas_call(
        _na_kernel, name="na_attn",
        grid=(DEC_BATCH, NA_ROWS),
        in_specs=[row, seq, seq, ctx, ctx,
                  pl.BlockSpec((1, NA_HEADS, GRID_W, NA_WIN), lambda b, i: (i - _na_row_start(i), 0, 0, 0))],
        out_specs=row,
        out_shape=jax.ShapeDtypeStruct((S_ROWS, NA_DIM), F32),
        compiler_params=_cparams(("parallel", "arbitrary")),
    )(q, k, v, ck, cv, bias_tab)


def _proj_res_kernel(a_ref, b_ref, x_ref, gate_ref, w_ref, o_ref, *, na):
    y = _dot(a_ref[...].astype(BF16), w_ref[:na, :]) + _dot(b_ref[...].astype(BF16), w_ref[na:, :])
    o_ref[...] = x_ref[...] + gate_ref[0] * y


def proj_residual(a, b, x, gate, w, rows_per_set, tm=512):
    rows = x.shape[0]
    na, nb = a.shape[1], b.shape[1]
    si = _set_index(tm, rows_per_set)
    return pl.pallas_call(
        functools.partial(_proj_res_kernel, na=na), name="proj_residual",
        grid=(rows // tm,),
        in_specs=[pl.BlockSpec((tm, na), lambda i: (i, 0)),
                  pl.BlockSpec((tm, nb), lambda i: (i, 0)),
                  pl.BlockSpec((tm, D_MODEL), lambda i: (i, 0)),
                  pl.BlockSpec((1, 1, D_MODEL), lambda i: (si(i), 0, 0)),
                  pl.BlockSpec((na + nb, D_MODEL), lambda i: (0, 0))],
        out_specs=pl.BlockSpec((tm, D_MODEL), lambda i: (i, 0)),
        out_shape=jax.ShapeDtypeStruct((rows, D_MODEL), F32),
        compiler_params=_cparams(("parallel",)),
    )(a, b, x, gate, w)


def _ffn_kernel(x_ref, g_ref, sh_ref, sc_ref, gate_ref, wg_ref, wu_ref, wd_ref, o_ref, h_scr, acc_scr):
    j = pl.program_id(1)

    @pl.when(j == 0)
    def _():
        h_scr[...] = _modulated(x_ref[...], g_ref[...], sh_ref[0], sc_ref[0]).astype(BF16)
        acc_scr[...] = jnp.zeros_like(acc_scr)

    h = h_scr[...]
    gt = _dot(h, wg_ref[...])
    act = (gt * _sigmoid(gt)) * _dot(h, wu_ref[...])
    acc_scr[...] += _dot(act.astype(BF16), wd_ref[...])

    @pl.when(j == pl.num_programs(1) - 1)
    def _():
        o_ref[...] = x_ref[...] + gate_ref[0] * acc_scr[...]


def ffn_residual(x, g, sh, sc, gate, wg, wu, wd, rows_per_set, tm=1024, tf=256):
    rows = x.shape[0]
    si = _set_index(tm, rows_per_set)
    vec = pl.BlockSpec((1, 1, D_MODEL), lambda i, j: (si(i), 0, 0))
    return pl.pallas_call(
        _ffn_kernel, name="ffn",
        grid=(rows // tm, D_FF // tf),
        in_specs=[pl.BlockSpec((tm, D_MODEL), lambda i, j: (i, 0)),
                  pl.BlockSpec((1, D_MODEL), lambda i, j: (0, 0)),
                  vec, vec, vec,
                  pl.BlockSpec((D_MODEL, tf), lambda i, j: (0, j)),
                  pl.BlockSpec((D_MODEL, tf), lambda i, j: (0, j)),
                  pl.BlockSpec((tf, D_MODEL), lambda i, j: (j, 0))],
        out_specs=pl.BlockSpec((tm, D_MODEL), lambda i, j: (i, 0)),
        out_shape=jax.ShapeDtypeStruct((rows, D_MODEL), F32),
        scratch_shapes=[pltpu.VMEM((tm, D_MODEL), BF16), pltpu.VMEM((tm, D_MODEL), F32)],
        compiler_params=_cparams(("parallel", "arbitrary")),
    )(x, g.reshape(1, D_MODEL), sh, sc, gate, wg, wu, wd)


def _pool_consts(n):
    t = np.arange(n)
    mats, cnts = [], []
    for win in POOL_WINDOWS:
        lo = np.clip(t - win // 2, 0, n)
        hi = np.clip(t + win - win // 2, 0, n)
        mats.append(((t[None, :] >= lo[:, None]) & (t[None, :] < hi[:, None])).astype(np.float32))
        cnts.append(np.repeat((hi - lo).astype(np.float32)[:, None], POOL_GW, axis=1))
    return jnp.asarray(np.stack(mats)).astype(BF16), jnp.asarray(np.concatenate(cnts, axis=1))


def _pool_kernel(x_ref, pm_ref, cnt_ref, w_ref, scale_ref, o_ref):
    x = x_ref[...]
    hi = x.astype(BF16)
    lo = (x - hi.astype(F32)).astype(BF16)
    sums = []
    for g in range(len(POOL_WINDOWS)):
        sl = slice(g * POOL_GW, (g + 1) * POOL_GW)
        sums.append(_dot(pm_ref[g], hi[:, sl]) + _dot(pm_ref[g], lo[:, sl]))
    y = jnp.concatenate(sums, axis=-1) / cnt_ref[...] - x
    o_ref[...] = _dot(y.astype(BF16), w_ref[...]) * scale_ref[...]


def pool_mix(x, n, w_bd, scale):
    rows = x.shape[0]
    pm, cnt = _pool_consts(n)
    return pl.pallas_call(
        _pool_kernel, name="pool",
        grid=(rows // n,),
        in_specs=[pl.BlockSpec((n, POOL_CH), lambda b: (b, 0)),
                  pl.BlockSpec((len(POOL_WINDOWS), n, n), lambda b: (0, 0, 0)),
                  pl.BlockSpec((n, POOL_CH), lambda b: (0, 0)),
                  pl.BlockSpec((POOL_CH, POOL_CH), lambda b: (0, 0)),
                  pl.BlockSpec((1, POOL_CH), lambda b: (0, 0))],
        out_specs=pl.BlockSpec((n, POOL_CH), lambda b: (b, 0)),
        out_shape=jax.ShapeDtypeStruct((rows, POOL_CH), F32),
        compiler_params=_cparams(("parallel",)),
    )(x, pm, cnt, w_bd, scale.reshape(1, POOL_CH))


RW_TILE = 256
HALO = 8
SUB = 8
PACK_R = RWKV_HEADS * SUB


def _head_ones():
    h = np.arange(RWKV_DIM) // HEAD_DIM
    return jnp.asarray((h[:, None] == h[None, :]).astype(np.float32)).astype(BF16)


def _head_sum(x, ones):
    hi = x.astype(BF16)
    lo = (x - hi.astype(F32)).astype(BF16)
    return _dot(hi, ones) + _dot(lo, ones)


def _pack_heads(a, b, o_ref, lead):
    n = a.shape[0]
    lane = lax.broadcasted_iota(jnp.int32, (n, LANES), 1)
    low = lane < HEAD_DIM
    for c in range(RWKV_DIM // LANES):
        ac = a[:, c * LANES:(c + 1) * LANES]
        bc = b[:, c * LANES:(c + 1) * LANES]
        even = jnp.where(low, ac, pltpu.roll(bc, HEAD_DIM, 1))
        odd = jnp.where(low, pltpu.roll(ac, HEAD_DIM, 1), bc)
        for h, val in ((2 * c, even), (2 * c + 1, odd)):
            o_ref[lead + (slice(None), slice(h * SUB, (h + 1) * SUB), slice(None))] = val.reshape(n // SUB, SUB, LANES)


def _rwkv_prep_kernel(z_ref, zp_ref, zn_ref, mu_ref, kk_w_ref, ka_ref, rk_ref, w0_ref, a0_ref,
                      dup_ref, iup_ref, gup_ref, ones_ref,
                      g1_ref, g2_ref, g3_ref, g_ref, bonus_ref, *, tiles_per_seq):
    i = pl.program_id(0)
    pos = i % tiles_per_seq
    z = z_ref[...]
    row = lax.broadcasted_iota(jnp.int32, (RW_TILE, 1), 0)
    prev_edge = jnp.where(pos == 0, 0.0, zp_ref[HALO - 1:HALO, :])
    next_edge = jnp.where(pos == tiles_per_seq - 1, 0.0, zn_ref[0:1, :])
    prev = jnp.where(row == 0, prev_edge, pltpu.roll(z, 1, 0))
    nxt = jnp.where(row == RW_TILE - 1, next_edge, pltpu.roll(z, RW_TILE - 1, 0))
    zr = z + mu_ref[0:1, :] * (prev - z) + mu_ref[1:2, :] * (nxt - z)

    d = RWKV_DIM
    r, k, v = zr[:, :d], zr[:, d:2 * d], zr[:, 2 * d:3 * d]
    lora = 3 * d
    ones = ones_ref[...]
    kk = k * kk_w_ref[...]
    kk = kk * lax.rsqrt(_head_sum(kk * kk, ones) + L2_EPS)
    _pack_heads(r, v, g3_ref, ())
    for dr in range(2):
        wl = zr[:, lora + dr * DECAY_LORA: lora + (dr + 1) * DECAY_LORA]
        al = zr[:, lora + 2 * DECAY_LORA + dr * ICLR_LORA: lora + 2 * DECAY_LORA + (dr + 1) * ICLR_LORA]
        lw = w0_ref[dr:dr + 1, :] + _dot(jnp.tanh(wl).astype(BF16), dup_ref[dr])
        w = jnp.exp(-DECAY_SCALE * _sigmoid(lw))
        a = _sigmoid(a0_ref[dr:dr + 1, :] + _dot(al.astype(BF16), iup_ref[dr]))
        _pack_heads(w, kk * a, g1_ref, (dr,))
        _pack_heads(k * (1.0 + (a - 1.0) * ka_ref[...]), kk, g2_ref, (dr,))
    gl = zr[:, lora + 2 * DECAY_LORA + 2 * ICLR_LORA:]
    g_ref[...] = _dot(_sigmoid(gl).astype(BF16), gup_ref[...])
    bonus_ref[...] = _head_sum(r * k * rk_ref[...], ones) * v


def rwkv_prep(z, n, mu, k_k, k_a, r_k, w0, a0, dup, iup, gup):
    rows = z.shape[0]
    nb = rows // n
    tps = n // RW_TILE
    hb = RW_TILE // HALO
    last = rows // HALO - 1
    d = RWKV_DIM
    full2 = lambda shape: pl.BlockSpec(shape, lambda i: (0, 0))
    full3 = lambda shape: pl.BlockSpec(shape, lambda i: (0, 0, 0))
    tile = pl.BlockSpec((RW_TILE, d), lambda i: (i, 0))
    pk2 = pl.BlockSpec((2, RW_TILE // SUB, PACK_R, LANES), lambda i: (0, i % tps, i // tps, 0))
    pk1 = pl.BlockSpec((RW_TILE // SUB, PACK_R, LANES), lambda i: (i % tps, i // tps, 0))
    return pl.pallas_call(
        functools.partial(_rwkv_prep_kernel, tiles_per_seq=tps), name="rwkv_prep",
        grid=(rows // RW_TILE,),
        in_specs=[pl.BlockSpec((RW_TILE, RWKV_IN), lambda i: (i, 0)),
                  pl.BlockSpec((HALO, RWKV_IN), lambda i: (jnp.maximum(i * hb - 1, 0), 0)),
                  pl.BlockSpec((HALO, RWKV_IN), lambda i: (jnp.minimum((i + 1) * hb, last), 0)),
                  full2((2, RWKV_IN)), full2((1, d)), full2((1, d)), full2((1, d)),
                  full2((2, d)), full2((2, d)),
                  full3((2, DECAY_LORA, d)), full3((2, ICLR_LORA, d)), full2((GATE_LORA, d)),
                  full2((d, d))],
        out_specs=[pk2, pk2, pk1, tile, tile],
        out_shape=[jax.ShapeDtypeStruct((2, n // SUB, nb * PACK_R, LANES), F32),
                   jax.ShapeDtypeStruct((2, n // SUB, nb * PACK_R, LANES), F32),
                   jax.ShapeDtypeStruct((n // SUB, nb * PACK_R, LANES), F32),
                   jax.ShapeDtypeStruct((rows, d), F32),
                   jax.ShapeDtypeStruct((rows, d), F32)],
        compiler_params=_cparams(("parallel",)),
    )(z, z, z, mu, k_k.reshape(1, d), k_a.reshape(1, d), r_k.reshape(1, d), w0, a0,
      dup.astype(BF16), iup.astype(BF16), gup.astype(BF16), _head_ones())


SCAN_TC = 32
SCAN_G = SCAN_TC // SUB
SLOTS = 4
V_BLOCK = 32
K_CHUNK = 16
N_KC = HEAD_DIM // K_CHUNK


def _wkv_first_sa(s_scr, t2, vs):
    sas = []
    for vb in range(vs // V_BLOCK):
        rows = slice(vb * V_BLOCK, (vb + 1) * V_BLOCK)

        def chunk(kc, sa):
            for j in range(K_CHUNK):
                sa = sa + s_scr[kc * K_CHUNK + j, rows, :] * t2[N_KC + kc, j:j + 1, :]
            return sa

        sas.append(lax.fori_loop(0, N_KC, chunk, jnp.zeros((V_BLOCK, LANES), F32)))
    return tuple(sas)


def _wkv_step(s_scr, t1, t2, t3, t2_next, v_blocks, o_ref, sas):
    nxt = []
    for vb, v_blk in enumerate(v_blocks):
        rows = slice(vb * V_BLOCK, (vb + 1) * V_BLOCK)
        sa = sas[vb]

        def chunk(kc, carry):
            o, sa_n = carry
            for j in range(K_CHUNK):
                k = kc * K_CHUNK + j
                s_new = (s_scr[k, rows, :] * t1[kc, j:j + 1, :] - sa * t1[N_KC + kc, j:j + 1, :]
                         + v_blk * t2[kc, j:j + 1, :])
                s_scr[k, rows, :] = s_new
                o = o + s_new * t3[kc, j:j + 1, :]
                sa_n = sa_n + s_new * t2_next[N_KC + kc, j:j + 1, :]
            return o, sa_n

        zero = jnp.zeros((V_BLOCK, LANES), F32)
        o, sa_n = lax.fori_loop(1, N_KC, chunk, chunk(0, (zero, zero)))
        o_ref[rows, :] = o
        nxt.append(sa_n)
    return tuple(nxt)


def _store_tile(ref, idx, x):
    ref[idx] = x.reshape(2 * N_KC, K_CHUNK, LANES)


def _step_rows(ref, lead, grp, sub, n):
    return ref.at[lead + (grp,)][pl.ds(sub, n, stride=SUB), :]


def _scan_prompt_kernel(g1_ref, g2_ref, g3_ref, o_ref, st_ref, s_scr, ta, tb, tc, td, oa, ob, *, reverse):
    c = pl.program_id(1)
    tiles = (ta, tb, tc, td)
    outs = (oa, ob)

    @pl.when(c == 0)
    def _():
        s_scr[...] = jnp.zeros_like(s_scr)

    for o_scr in outs:
        o_scr[...] = jnp.zeros_like(o_scr)

    def where(grp, sub):
        return (SCAN_G - 1 - grp, SUB - 1 - sub) if reverse else (grp, sub)

    def load_tiles(grp, sub, slot):
        g, s = where(grp, sub)
        _store_tile(tiles[slot], 0, _step_rows(g1_ref, (0,), g, s, LANES).T)
        _store_tile(tiles[slot], 1, _step_rows(g2_ref, (0,), g, s, LANES).T)
        _store_tile(tiles[slot], 2, _step_rows(g3_ref, (), g, s, LANES).T)

    def flush(grp, sub, o_scr):
        g, s = where(grp, sub)
        o_ref.at[0, g][pl.ds(s, LANES, stride=SUB), :] = o_scr[...].T

    load_tiles(0, 0, 0)
    load_tiles(0, 1, 1)
    sas0 = _wkv_first_sa(s_scr, tiles[0].at[1], HEAD_DIM)
    per_blk = V_BLOCK // K_CHUNK

    def group(grp, sas):
        for j in range(SUB):
            cur, nxt = tiles[j % SLOTS], tiles[(j + 1) % SLOTS]
            ahead = j + 2
            load_tiles(grp if ahead < SUB else jnp.minimum(grp + 1, SCAN_G - 1), ahead % SUB, ahead % SLOTS)
            if j > 0:
                flush(grp, j - 1, outs[(j - 1) % 2])
            v_blocks = [cur[2, N_KC + vb * per_blk: N_KC + (vb + 1) * per_blk].reshape(V_BLOCK, LANES)
                        for vb in range(HEAD_DIM // V_BLOCK)]
            sas = _wkv_step(s_scr, cur.at[0], cur.at[1], cur.at[2], nxt.at[1], v_blocks, outs[j % 2], sas)
        flush(grp, SUB - 1, outs[(SUB - 1) % 2])
        return sas

    lax.fori_loop(0, SCAN_G, group, sas0)

    @pl.when(c == pl.num_programs(1) - 1)
    def _():
        st_ref[0] = s_scr[...]


def wkv_scan_prompt(g1, g2, g3, direction):
    n_grp, rows = g3.shape[0], g3.shape[1]
    groups = rows // (LANES * SUB)
    nblk = n_grp // SCAN_G
    tb = (lambda s: nblk - 1 - s) if direction else (lambda s: s)
    blk = (SCAN_G, LANES * SUB, LANES)
    dir_blk = pl.BlockSpec((1,) + blk, lambda g, s: (direction, tb(s), g, 0))
    return pl.pallas_call(
        functools.partial(_scan_prompt_kernel, reverse=bool(direction)), name="wkv_scan_prompt",
        grid=(groups, nblk),
        in_specs=[dir_blk, dir_blk, pl.BlockSpec(blk, lambda g, s: (tb(s), g, 0))],
        out_specs=[pl.BlockSpec((1,) + blk, lambda g, s: (0, tb(s), g, 0)),
                   pl.BlockSpec((1, HEAD_DIM, HEAD_DIM, LANES), lambda g, s: (g, 0, 0, 0))],
        out_shape=[jax.ShapeDtypeStruct((1, n_grp, rows, LANES), F32),
                   jax.ShapeDtypeStruct((groups, HEAD_DIM, HEAD_DIM, LANES), F32)],
        scratch_shapes=([pltpu.VMEM((HEAD_DIM, HEAD_DIM, LANES), F32)]
                        + [pltpu.VMEM((3, 2 * N_KC, K_CHUNK, LANES), F32)] * SLOTS
                        + [pltpu.VMEM((LANES, LANES), F32)] * 2),
        compiler_params=_cparams(("parallel", "arbitrary")),
    )(g1, g2, g3)


S_CHAINS = DEC_BATCH * RWKV_HEADS
S_VS = HEAD_DIM // 2


def _scan_sample_kernel(g1f_ref, g1b_ref, g2f_ref, g2b_ref, g3f_ref, g3b_ref, s0_ref,
                        of_ref, ob_ref, s_scr, ta, tb, tc, td, va, vb, vc, vd, oa, ob):
    c = pl.program_id(0)
    tiles = (ta, tb, tc, td)
    vals = (va, vb, vc, vd)
    outs = (oa, ob)

    @pl.when(c == 0)
    def _():
        s_scr[...] = s0_ref[...]

    nc = 2 * S_CHAINS
    zpad = jnp.zeros((LANES - 2 * nc, LANES), F32)
    lane = lax.broadcasted_iota(jnp.int32, (S_VS, LANES), 1)

    def stacked_t(f_ref, b_ref, grp, sub):
        f = _step_rows(f_ref, (0,), grp, sub, S_CHAINS)
        b = _step_rows(b_ref, (0,), SCAN_G - 1 - grp, SUB - 1 - sub, S_CHAINS)
        return jnp.concatenate([f, b, f, b, zpad], axis=0).T

    def load_tiles(grp, sub, slot):
        _store_tile(tiles[slot], 0, stacked_t(g1f_ref, g1b_ref, grp, sub))
        _store_tile(tiles[slot], 1, stacked_t(g2f_ref, g2b_ref, grp, sub))
        t3 = stacked_t(g3f_ref, g3b_ref, grp, sub)
        _store_tile(tiles[slot], 2, t3)
        vals[slot][...] = jnp.where(lane < nc, t3[HEAD_DIM:HEAD_DIM + S_VS], t3[HEAD_DIM + S_VS:])

    def flush(grp, sub, o_scr):
        o = o_scr[...]
        full = jnp.concatenate([o, pltpu.roll(o, LANES - nc, 1), jnp.zeros((LANES - HEAD_DIM, LANES), F32)], axis=0)
        ot = full.T
        of_ref.at[0, grp][pl.ds(sub, S_CHAINS, stride=SUB), :] = ot[0:S_CHAINS]
        ob_ref.at[0, SCAN_G - 1 - grp][pl.ds(SUB - 1 - sub, S_CHAINS, stride=SUB), :] = ot[S_CHAINS:nc]

    load_tiles(0, 0, 0)
    load_tiles(0, 1, 1)
    sas0 = _wkv_first_sa(s_scr, tiles[0].at[1], S_VS)

    def group(grp, sas):
        for j in range(SUB):
            cur, nxt = tiles[j % SLOTS], tiles[(j + 1) % SLOTS]
            ahead = j + 2
            load_tiles(grp if ahead < SUB else jnp.minimum(grp + 1, SCAN_G - 1), ahead % SUB, ahead % SLOTS)
            if j > 0:
                flush(grp, j - 1, outs[(j - 1) % 2])
            sas = _wkv_step(s_scr, cur.at[0], cur.at[1], cur.at[2], nxt.at[1], [vals[j % SLOTS][...]],
                            outs[j % 2], sas)
        flush(grp, SUB - 1, outs[(SUB - 1) % 2])
        return sas

    lax.fori_loop(0, SCAN_G, group, sas0)


def wkv_scan_sample(g1, g2, g3, s0):
    n_grp, rows = g3.shape[0], g3.shape[1]
    g3 = g3.reshape(1, n_grp, rows, LANES)
    nblk = n_grp // SCAN_G
    blk = (1, SCAN_G, rows, LANES)
    fwd = lambda d: pl.BlockSpec(blk, lambda s: (d, s, 0, 0))
    bwd = lambda d: pl.BlockSpec(blk, lambda s: (d, nblk - 1 - s, 0, 0))
    return pl.pallas_call(
        _scan_sample_kernel, name="wkv_scan_sample",
        grid=(nblk,),
        in_specs=[fwd(0), bwd(1), fwd(0), bwd(1), fwd(0), bwd(0),
                  pl.BlockSpec((HEAD_DIM, S_VS, LANES), lambda s: (0, 0, 0))],
        out_specs=[fwd(0), bwd(0)],
        out_shape=[jax.ShapeDtypeStruct((1, n_grp, rows, LANES), F32)] * 2,
        scratch_shapes=([pltpu.VMEM((HEAD_DIM, S_VS, LANES), F32)]
                        + [pltpu.VMEM((3, 2 * N_KC, K_CHUNK, LANES), F32)] * SLOTS
                        + [pltpu.VMEM((S_VS, LANES), F32)] * SLOTS
                        + [pltpu.VMEM((S_VS, LANES), F32)] * 2),
        compiler_params=_cparams(("arbitrary",)),
    )(g1, g1, g2, g2, g3, g3, s0)


def _sample_state_lanes(s0):
    nc = 2 * S_CHAINS
    st = jnp.transpose(s0, (4, 3, 1, 0, 2)).reshape(HEAD_DIM, HEAD_DIM, nc)
    st = jnp.concatenate([st[:, :S_VS], st[:, S_VS:]], axis=-1)
    return jnp.pad(st, ((0, 0), (0, 0), (0, LANES - 2 * nc)))


def _rwkv_post_kernel(of_ref, ob_ref, g_ref, bonus_ref, gw_ref, gb_ref, ones_ref, y_ref):
    ones = ones_ref[...]
    lane = lax.broadcasted_iota(jnp.int32, (RW_TILE, LANES), 1)
    low = lane < HEAD_DIM

    def head(h):
        rows = slice(h * SUB, (h + 1) * SUB)
        return (of_ref[0, :, rows, :] + ob_ref[0, :, rows, :]).reshape(RW_TILE, LANES)

    cols = [jnp.where(low, head(2 * c), pltpu.roll(head(2 * c + 1), HEAD_DIM, 1))
            for c in range(RWKV_DIM // LANES)]
    o = jnp.concatenate(cols, axis=-1)
    mu = _head_sum(o, ones) / HEAD_DIM
    oc = o - mu
    var = _head_sum(oc * oc, ones) / HEAD_DIM
    on = (oc * lax.rsqrt(var + GN_EPS)) * gw_ref[...] + gb_ref[...]
    y_ref[...] = (on + bonus_ref[...]) * g_ref[...]


def rwkv_post(o_f, o_b, n, g, bonus, gn_w, gn_b):
    rows, d = g.shape
    tps = n // RW_TILE
    tile = pl.BlockSpec((RW_TILE, d), lambda i: (i, 0))
    vec = pl.BlockSpec((1, d), lambda i: (0, 0))
    pk = pl.BlockSpec((1, RW_TILE // SUB, PACK_R, LANES), lambda i: (0, i % tps, i // tps, 0))
    return pl.pallas_call(
        _rwkv_post_kernel, name="rwkv_post",
        grid=(rows // RW_TILE,),
        in_specs=[pk, pk, tile, tile, vec, vec, pl.BlockSpec((d, d), lambda i: (0, 0))],
        out_specs=tile,
        out_shape=jax.ShapeDtypeStruct((rows, d), F32),
        compiler_params=_cparams(("parallel",)),
    )(o_f, o_b, g, bonus, gn_w.reshape(1, d), gn_b.reshape(1, d), _head_ones())


def _split3(x):
    a = x.astype(BF16)
    r1 = x - a.astype(F32)
    b = r1.astype(BF16)
    c = (r1 - b.astype(F32)).astype(BF16)
    return a, b, c


def _router_kernel(x_ref, g_ref, sh_ref, sc_ref, w_ref, b_ref, h_ref, comb_ref):
    h = _modulated(x_ref[...], g_ref[...], sh_ref[0], sc_ref[0])
    h_ref[...] = h.astype(BF16)
    h1, h2, h3 = _split3(h)
    w1, w2, w3 = _split3(w_ref[...])
    logits = (_dot(h1, w1) + (_dot(h1, w2) + _dot(h2, w1))
              + (_dot(h1, w3) + _dot(h2, w2) + _dot(h3, w1))) + b_ref[...]
    col = lax.broadcasted_iota(jnp.int32, logits.shape, 1)
    logits = jnp.where(col < N_EXPERTS, logits, -jnp.inf)
    m1 = jnp.max(logits, axis=-1, keepdims=True)
    i1 = jnp.min(jnp.where(logits == m1, col, LANES), axis=-1, keepdims=True)
    rest = jnp.where(col == i1, -jnp.inf, logits)
    m2 = jnp.max(rest, axis=-1, keepdims=True)
    i2 = jnp.min(jnp.where(rest == m2, col, LANES), axis=-1, keepdims=True)
    e2 = jnp.exp(m2 - m1)
    den = 1.0 + e2
    comb_ref[...] = jnp.where(col == i1, 1.0 / den, 0.0) + jnp.where(col == i2, e2 / den, 0.0)


def moe_router(x, g, sh, sc, router_w, router_b, rows_per_set, tm=512):
    rows = x.shape[0]
    si = _set_index(tm, rows_per_set)
    vec = pl.BlockSpec((1, 1, D_MODEL), lambda i: (si(i), 0, 0))
    w = jnp.pad(router_w, ((0, 0), (0, LANES - N_EXPERTS)))
    b = jnp.pad(router_b, (0, LANES - N_EXPERTS)).reshape(1, LANES)
    return pl.pallas_call(
        _router_kernel, name="moe_router",
        grid=(rows // tm,),
        in_specs=[pl.BlockSpec((tm, D_MODEL), lambda i: (i, 0)),
                  pl.BlockSpec((1, D_MODEL), lambda i: (0, 0)),
                  vec, vec,
                  pl.BlockSpec((D_MODEL, LANES), lambda i: (0, 0)),
                  pl.BlockSpec((1, LANES), lambda i: (0, 0))],
        out_specs=[pl.BlockSpec((tm, D_MODEL), lambda i: (i, 0)),
                   pl.BlockSpec((tm, LANES), lambda i: (i, 0))],
        out_shape=[jax.ShapeDtypeStruct((rows, D_MODEL), BF16),
                   jax.ShapeDtypeStruct((rows, LANES), F32)],
        compiler_params=_cparams(("parallel",)),
    )(x, g.reshape(1, D_MODEL), sh, sc, w, b)


def _moe_kernel(x_ref, h_ref, comb_ref, gate_ref, gfin_ref, wg_ref, wu_ref, wd_ref, o_ref, acc_scr):
    e = pl.program_id(1)

    @pl.when(e == 0)
    def _():
        acc_scr[...] = jnp.zeros_like(acc_scr)

    comb = comb_ref[...]
    col = lax.broadcasted_iota(jnp.int32, comb.shape, 1)
    ce = jnp.sum(jnp.where(col == e, comb, 0.0), axis=-1, keepdims=True)
    h = h_ref[...]
    gt = _dot(h, wg_ref[0])
    act = (gt * _sigmoid(gt)) * _dot(h, wu_ref[0])
    acc_scr[...] += ce * _dot(act.astype(BF16), wd_ref[0])

    @pl.when(e == pl.num_programs(1) - 1)
    def _():
        y = x_ref[...] + gate_ref[0] * acc_scr[...]
        ms = jnp.mean(y * y, axis=-1, keepdims=True)
        o_ref[...] = y * lax.rsqrt(ms + RMS_EPS) * gfin_ref[...]


def moe_residual_norm(x, h, comb, gate, g_final, wg, wu, wd, rows_per_set, tm=512):
    rows = x.shape[0]
    si = _set_index(tm, rows_per_set)
    return pl.pallas_call(
        _moe_kernel, name="moe_experts",
        grid=(rows // tm, N_EXPERTS),
        in_specs=[pl.BlockSpec((tm, D_MODEL), lambda i, e: (i, 0)),
                  pl.BlockSpec((tm, D_MODEL), lambda i, e: (i, 0)),
                  pl.BlockSpec((tm, LANES), lambda i, e: (i, 0)),
                  pl.BlockSpec((1, 1, D_MODEL), lambda i, e: (si(i), 0, 0)),
                  pl.BlockSpec((1, D_MODEL), lambda i, e: (0, 0)),
                  pl.BlockSpec((1, D_MODEL, D_FF_EXPERT), lambda i, e: (e, 0, 0)),
                  pl.BlockSpec((1, D_MODEL, D_FF_EXPERT), lambda i, e: (e, 0, 0)),
                  pl.BlockSpec((1, D_FF_EXPERT, D_MODEL), lambda i, e: (e, 0, 0))],
        out_specs=pl.BlockSpec((tm, D_MODEL), lambda i, e: (i, 0)),
        out_shape=jax.ShapeDtypeStruct((rows, D_MODEL), F32),
        scratch_shapes=[pltpu.VMEM((tm, D_MODEL), F32)],
        compiler_params=_cparams(("parallel", "arbitrary")),
    )(x, h, comb, gate, g_final.reshape(1, D_MODEL), wg, wu, wd)


def kernel(x_prompt, x_sample, cache_na_k, cache_na_v, state_wkv, c, c_ctx, mod_w, mod_b, norm_mix, norm_ffn, norm_final, na_w_in, fourier_w, na_rel_bias, na_w_out, ffn_w_gate, ffn_w_up, ffn_w_down, rw_w_in, pool_w, pool_scale, shift_mu, decay_w0, decay_up, iclr_a0, iclr_up, gate_up, k_k, k_a, r_k, gn_w, gn_b, rw_w_out, router_w, router_b, moe_w_gate, moe_w_up, moe_w_down):
    cond = jnp.concatenate([c_ctx[None, :], c, jnp.zeros((8 - N_SETS, D_MODEL), F32)], axis=0)
    mods = adaln_all(cond, mod_w, mod_b)[:, :N_SETS].reshape(DEPTH, N_SETS, 6, 1, D_MODEL)
    bf = lambda w: w.astype(BF16)

    xp = x_prompt.reshape(P_ROWS, D_MODEL)
    xs = x_sample.reshape(S_ROWS, D_MODEL)
    streams = {"p": (SEQ, P_ROWS, slice(0, 1)), "s": (DEC_SEQ, DEC_SEQ, slice(1, N_SETS))}
    x = {"p": xp, "s": xs}

    splits = ((0, FOURIER_CH), (FOURIER_CH, FOURIER_CH + NA_DIM),
              (FOURIER_CH + NA_DIM, FOURIER_CH + 2 * NA_DIM), (FOURIER_CH + 2 * NA_DIM, FOURIER_CH + 3 * NA_DIM))
    w_in, w_out = bf(na_w_in[0]), bf(na_w_out[0])
    f_bd = bf(_block_diag(fourier_w[0]))
    ffn_w = (bf(ffn_w_gate[0]), bf(ffn_w_up[0]), bf(ffn_w_down[0]))
    ck = cache_na_k[:, 0].reshape(DEC_BATCH * PAST_LEN, NA_DIM)
    cv = cache_na_v[:, 0].reshape(DEC_BATCH * PAST_LEN, NA_DIM)
    for name, (n, rps, sets) in streams.items():
        sh1, sc1, g1, sh2, sc2, g2 = [mods[0, sets, m] for m in range(6)]
        f, q, k, v = modulated_matmul(x[name], norm_mix[0], sh1, sc1, w_in, splits, rps)
        if name == "p":
            attn = context_attention(q, k, v)
            new_k = k.reshape(BATCH, 1, SEQ, NA_HEADS, HEAD_DIM)
            new_v = v.reshape(BATCH, 1, SEQ, NA_HEADS, HEAD_DIM)
        else:
            attn = neighbourhood_attention(q, k, v, ck, cv, _na_bias_table(na_rel_bias[0]))
        y = proj_residual(fourier_mix(f, n, f_bd), attn, x[name], g1, w_out, rps)
        x[name] = ffn_residual(y, norm_ffn[0], sh2, sc2, g2, *ffn_w, rps)

    w_in, w_out = bf(rw_w_in[0]), bf(rw_w_out[0])
    p_bd = bf(_block_diag(pool_w[0]))
    moe_w = (bf(moe_w_gate[0]), bf(moe_w_up[0]), bf(moe_w_down[0]))
    rw = (shift_mu[0], k_k[0], k_a[0], r_k[0], decay_w0[0], iclr_a0[0], decay_up[0], iclr_up[0], gate_up[0])
    out = {}
    for name, (n, rps, sets) in streams.items():
        sh1, sc1, g1, sh2, sc2, g2 = [mods[1, sets, m] for m in range(6)]
        pc, z = modulated_matmul(x[name], norm_mix[1], sh1, sc1, w_in, ((0, POOL_CH), (POOL_CH, POOL_CH + RWKV_IN)), rps)
        t1, t2, t3, gate, bonus = rwkv_prep(z, n, *rw)
        if name == "p":
            o_f, st_f = wkv_scan_prompt(t1, t2, t3, 0)
            o_b, st_b = wkv_scan_prompt(t1, t2, t3, 1)
            st = jnp.transpose(jnp.stack([st_f, st_b]), (0, 1, 4, 3, 2))
            st = jnp.transpose(st.reshape(2, BATCH, RWKV_HEADS, HEAD_DIM, HEAD_DIM), (1, 0, 2, 3, 4))
        else:
            o_f, o_b = wkv_scan_sample(t1, t2, t3, _sample_state_lanes(state_wkv[:, 0]))
        mixed = rwkv_post(o_f, o_b, n, gate, bonus, gn_w[0], gn_b[0])
        y = proj_residual(pool_mix(pc, n, p_bd, pool_scale[0]), mixed, x[name], g1, w_out, rps)
        h, comb = moe_router(y, norm_ffn[1], sh2, sc2, router_w[0], router_b[0], rps)
        out[name] = moe_residual_norm(y, h, comb, g2, norm_final, *moe_w, rps)

    return (out["p"].reshape(BATCH, SEQ, D_MODEL), out["s"].reshape(DEC_BATCH, DEC_SEQ, D_MODEL),
            new_k, new_v, st[:, None])
```

```python
import functools
import math

import numpy as np
import jax
import jax.numpy as jnp
from jax import lax
from jax.experimental import pallas as pl
from jax.experimental.pallas import tpu as pltpu

F32 = jnp.float32
BF16 = jnp.bfloat16

D_MODEL = 1024
BATCH = 32
SEQ = 256
DEPTH = 2
DEC_BATCH = 2
DEC_SEQ = 1024
PAST_LEN = 512
GRID_W = 64
HEAD_DIM = 64
FOURIER_CH = D_MODEL // 4
FOURIER_GROUPS = 4
FOURIER_GW = FOURIER_CH // FOURIER_GROUPS
NA_DIM = D_MODEL - FOURIER_CH
NA_HEADS = NA_DIM // HEAD_DIM
NA_MAX_ROWS = 8
NA_COLS = 16
POOL_WINDOWS = (2, 4, 8, 16)
POOL_CH = D_MODEL // 4
POOL_GW = POOL_CH // len(POOL_WINDOWS)
RWKV_DIM = D_MODEL - POOL_CH
RWKV_HEADS = RWKV_DIM // HEAD_DIM
DECAY_LORA = 64
ICLR_LORA = 64
GATE_LORA = 128
RWKV_IN = 3 * RWKV_DIM + 2 * DECAY_LORA + 2 * ICLR_LORA + GATE_LORA
D_FF = 2816
N_EXPERTS = 8
D_FF_EXPERT = 1408
RMS_EPS = 1e-6
GN_EPS = 64e-5
L2_EPS = 1e-12
DECAY_SCALE = math.exp(-0.5)
NEG_INF = -1e30

P_ROWS = BATCH * SEQ
S_ROWS = DEC_BATCH * DEC_SEQ
N_ROWS = P_ROWS + S_ROWS
N_SETS = 1 + DEC_BATCH
LANES = 128
VMEM_LIMIT = 56 * 1024 * 1024


def _cparams(sem):
    return pltpu.CompilerParams(dimension_semantics=sem, vmem_limit_bytes=VMEM_LIMIT)


def _sigmoid(x):
    return 1.0 / (1.0 + jnp.exp(-x))


def _dot(a, b):
    return jnp.dot(a, b, preferred_element_type=F32)


def _dot_nt(a, b):
    return lax.dot_general(a, b, (((1,), (1,)), ((), ())), preferred_element_type=F32)


def _set_index(tm, rows_per_set):
    q = rows_per_set // tm
    return lambda i: i // q


def _modulated(x, g, sh, sc):
    ms = jnp.mean(x * x, axis=-1, keepdims=True)
    return (x * lax.rsqrt(ms + RMS_EPS) * g) * (1.0 + sc) + sh


def _adaln_kernel(c_ref, w_ref, b_ref, o_ref):
    c = c_ref[...]
    s = (c * _sigmoid(c)).astype(BF16)
    o_ref[0] = _dot(s, w_ref[0].astype(BF16)) + b_ref[0]


def adaln_all(cond, mod_w, mod_b):
    tn = 1536
    n = 6 * D_MODEL
    return pl.pallas_call(
        _adaln_kernel, name="adaln",
        grid=(DEPTH, n // tn),
        in_specs=[pl.BlockSpec((8, D_MODEL), lambda l, j: (0, 0)),
                  pl.BlockSpec((1, D_MODEL, tn), lambda l, j: (l, 0, j)),
                  pl.BlockSpec((1, 1, tn), lambda l, j: (l, 0, j))],
        out_specs=pl.BlockSpec((1, 8, tn), lambda l, j: (l, 0, j)),
        out_shape=jax.ShapeDtypeStruct((DEPTH, 8, n), F32),
        compiler_params=_cparams(("parallel", "parallel")),
    )(cond, mod_w, mod_b.reshape(DEPTH, 1, n))


def _modmm_kernel(x_ref, g_ref, sh_ref, sc_ref, w_ref, *o_refs, splits):
    h = _modulated(x_ref[...], g_ref[...], sh_ref[0], sc_ref[0]).astype(BF16)
    for o_ref, (a, b) in zip(o_refs, splits):
        o_ref[...] = _dot(h, w_ref[:, a:b]).astype(o_ref.dtype)


def modulated_matmul(x, g, sh, sc, w, splits, rows_per_set, tm=512):
    rows = x.shape[0]
    n_out = w.shape[1]
    si = _set_index(tm, rows_per_set)
    vec = pl.BlockSpec((1, 1, D_MODEL), lambda i: (si(i), 0, 0))
    return pl.pallas_call(
        functools.partial(_modmm_kernel, splits=splits), name="modulated_matmul",
        grid=(rows // tm,),
        in_specs=[pl.BlockSpec((tm, D_MODEL), lambda i: (i, 0)),
                  pl.BlockSpec((1, D_MODEL), lambda i: (0, 0)),
                  vec, vec,
                  pl.BlockSpec((D_MODEL, n_out), lambda i: (0, 0))],
        out_specs=[pl.BlockSpec((tm, b - a), lambda i: (i, 0)) for a, b in splits],
        out_shape=[jax.ShapeDtypeStruct((rows, b - a), F32) for a, b in splits],
        compiler_params=_cparams(("parallel",)),
    )(x, g.reshape(1, D_MODEL), sh, sc, w)


def _dft_mats(n):
    t = np.arange(n)
    ang = 2.0 * np.pi * ((t[:, None] * t[None, :]) % n) / n
    cn, sn = np.cos(ang) / np.sqrt(n), np.sin(ang) / np.sqrt(n)
    c = np.arange(FOURIER_GW)
    angc = 2.0 * np.pi * ((c[:, None] * c[None, :]) % FOURIER_GW) / FOURIER_GW
    eye = np.eye(FOURIER_GROUPS)
    cc = np.kron(eye, np.cos(angc) / np.sqrt(FOURIER_GW))
    sc = np.kron(eye, np.sin(angc) / np.sqrt(FOURIER_GW))
    as_bf = lambda a: jnp.asarray(a, dtype=F32).astype(BF16)
    return as_bf(cn), as_bf(sn), as_bf(cc), as_bf(sc)


def _fourier_kernel(f_ref, cn_ref, sn_ref, cc_ref, sc_ref, w_ref, o_ref):
    x = f_ref[...].astype(BF16)
    a = _dot(x, cc_ref[...]).astype(BF16)
    b = _dot(x, sc_ref[...]).astype(BF16)
    re = _dot(cn_ref[...], a) - _dot(sn_ref[...], b)
    o_ref[...] = _dot(re.astype(BF16), w_ref[...])


def _block_diag(w):
    g, c, _ = w.shape
    eye = jnp.eye(g, dtype=w.dtype)
    return (eye[:, None, :, None] * w[:, :, None, :]).reshape(g * c, g * c)


def fourier_mix(f, n, w_bd):
    rows = f.shape[0]
    cn, sn, cc, sc = _dft_mats(n)
    full = lambda shape: pl.BlockSpec(shape, lambda b: (0, 0))
    return pl.pallas_call(
        _fourier_kernel, name="fourier",
        grid=(rows // n,),
        in_specs=[pl.BlockSpec((n, FOURIER_CH), lambda b: (b, 0)),
                  full((n, n)), full((n, n)),
                  full((FOURIER_CH, FOURIER_CH)), full((FOURIER_CH, FOURIER_CH)),
                  full((FOURIER_CH, FOURIER_CH))],
        out_specs=pl.BlockSpec((n, FOURIER_CH), lambda b: (b, 0)),
        out_shape=jax.ShapeDtypeStruct((rows, FOURIER_CH), F32),
        compiler_params=_cparams(("parallel",)),
    )(f, cn, sn, cc, sc, w_bd)


def _ctx_attn_kernel(q_ref, k_ref, v_ref, o_ref):
    scale = HEAD_DIM ** -0.5
    outs = []
    for h in range(NA_HEADS):
        sl = slice(h * HEAD_DIM, (h + 1) * HEAD_DIM)
        q = q_ref[:, sl].astype(BF16)
        k = k_ref[:, sl].astype(BF16)
        v = v_ref[:, sl].astype(BF16)
        s = _dot_nt(q, k) * scale
        p = jnp.exp(s - jnp.max(s, axis=-1, keepdims=True))
        p = p / jnp.sum(p, axis=-1, keepdims=True)
        outs.append(_dot(p.astype(BF16), v))
    o_ref[...] = jnp.concatenate(outs, axis=-1)


def context_attention(q, k, v):
    blk = pl.BlockSpec((SEQ, NA_DIM), lambda b: (b, 0))
    return pl.pallas_call(
        _ctx_attn_kernel, name="ctx_attn",
        grid=(BATCH,),
        in_specs=[blk, blk, blk],
        out_specs=blk,
        out_shape=jax.ShapeDtypeStruct((P_ROWS, NA_DIM), F32),
        compiler_params=_cparams(("parallel",)),
    )(q, k, v)


NA_ROWS = DEC_SEQ // GRID_W
NA_WIN = NA_MAX_ROWS * GRID_W


def _na_bias_table(rel_bias):
    cols = np.arange(GRID_W)
    c0 = np.clip(cols - NA_COLS // 2, 0, GRID_W - NA_COLS)
    col_ok = (cols[None, :] >= c0[:, None]) & (cols[None, :] < c0[:, None] + NA_COLS)
    dc = np.clip(cols[None, :] - cols[:, None] + NA_COLS - 1, 0, 2 * NA_COLS - 2)
    onehot = (dc[None] == np.arange(2 * NA_COLS - 1)[:, None, None]).astype(np.float32)
    toe = jnp.einsum("hrj,jqk->hrqk", rel_bias, jnp.asarray(onehot), precision=lax.Precision.HIGHEST)
    toe = jnp.where(col_ok[None, None], toe, NEG_INF)
    tabs = [jnp.transpose(toe[:, NA_MAX_ROWS - 1 - o: 2 * NA_MAX_ROWS - 1 - o], (0, 2, 1, 3))
            for o in range(NA_MAX_ROWS)]
    return jnp.stack(tabs).reshape(NA_MAX_ROWS, NA_HEADS, GRID_W, NA_WIN)


def _na_row_start(i):
    return jnp.clip(i - NA_MAX_ROWS // 2, 0, NA_ROWS - NA_MAX_ROWS)


def _na_kernel(q_ref, k_ref, v_ref, ck_ref, cv_ref, bias_ref, o_ref):
    scale = HEAD_DIM ** -0.5
    i = pl.program_id(1)
    start = pl.multiple_of(_na_row_start(i) * GRID_W, GRID_W)
    kw = k_ref[pl.ds(start, NA_WIN), :]
    vw = v_ref[pl.ds(start, NA_WIN), :]
    outs = []
    for h in range(NA_HEADS):
        sl = slice(h * HEAD_DIM, (h + 1) * HEAD_DIM)
        q = q_ref[:, sl].astype(BF16)
        s_loc = _dot_nt(q, kw[:, sl].astype(BF16)) * scale + bias_ref[0, h]
        s_ctx = _dot_nt(q, ck_ref[:, sl].astype(BF16)) * scale
        m = jnp.maximum(jnp.max(s_loc, axis=-1, keepdims=True), jnp.max(s_ctx, axis=-1, keepdims=True))
        p_loc = jnp.exp(s_loc - m)
        p_ctx = jnp.exp(s_ctx - m)
        den = jnp.sum(p_loc, axis=-1, keepdims=True) + jnp.sum(p_ctx, axis=-1, keepdims=True)
        outs.append(_dot((p_loc / den).astype(BF16), vw[:, sl].astype(BF16))
                    + _dot((p_ctx / den).astype(BF16), cv_ref[:, sl].astype(BF16)))
    o_ref[...] = jnp.concatenate(outs, axis=-1)


def neighbourhood_attention(q, k, v, ck, cv, bias_tab):
    seq = pl.BlockSpec((DEC_SEQ, NA_DIM), lambda b, i: (b, 0))
    ctx = pl.BlockSpec((PAST_LEN, NA_DIM), lambda b, i: (b, 0))
    row = pl.BlockSpec((GRID_W, NA_DIM), lambda b, i: (b * NA_ROWS + i, 0))
    return pl.pallas_call(
        _na_kernel, name="na_attn",
        grid=(DEC_BATCH, NA_ROWS),
        in_specs=[row, seq, seq, ctx, ctx,
                  pl.BlockSpec((1, NA_HEADS, GRID_W, NA_WIN), lambda b, i: (i - _na_row_start(i), 0, 0, 0))],
        out_specs=row,
        out_shape=jax.ShapeDtypeStruct((S_ROWS, NA_DIM), F32),
        compiler_params=_cparams(("parallel", "arbitrary")),
    )(q, k, v, ck, cv, bias_tab)


def _proj_res_kernel(a_ref, b_ref, x_ref, gate_ref, w_ref, o_ref, *, na):
    y = _dot(a_ref[...].astype(BF16), w_ref[:na, :]) + _dot(b_ref[...].astype(BF16), w_ref[na:, :])
    o_ref[...] = x_ref[...] + gate_ref[0] * y


def proj_residual(a, b, x, gate, w, rows_per_set, tm=512):
    rows = x.shape[0]
    na, nb = a.shape[1], b.shape[1]
    si = _set_index(tm, rows_per_set)
    return pl.pallas_call(
        functools.partial(_proj_res_kernel, na=na), name="proj_residual",
        grid=(rows // tm,),
        in_specs=[pl.BlockSpec((tm, na), lambda i: (i, 0)),
                  pl.BlockSpec((tm, nb), lambda i: (i, 0)),
                  pl.BlockSpec((tm, D_MODEL), lambda i: (i, 0)),
                  pl.BlockSpec((1, 1, D_MODEL), lambda i: (si(i), 0, 0)),
                  pl.BlockSpec((na + nb, D_MODEL), lambda i: (0, 0))],
        out_specs=pl.BlockSpec((tm, D_MODEL), lambda i: (i, 0)),
        out_shape=jax.ShapeDtypeStruct((rows, D_MODEL), F32),
        compiler_params=_cparams(("parallel",)),
    )(a, b, x, gate, w)


def _ffn_kernel(x_ref, g_ref, sh_ref, sc_ref, gate_ref, wg_ref, wu_ref, wd_ref, o_ref, h_scr, acc_scr):
    j = pl.program_id(1)

    @pl.when(j == 0)
    def _():
        h_scr[...] = _modulated(x_ref[...], g_ref[...], sh_ref[0], sc_ref[0]).astype(BF16)
        acc_scr[...] = jnp.zeros_like(acc_scr)

    h = h_scr[...]
    gt = _dot(h, wg_ref[...])
    act = (gt * _sigmoid(gt)) * _dot(h, wu_ref[...])
    acc_scr[...] += _dot(act.astype(BF16), wd_ref[...])

    @pl.when(j == pl.num_programs(1) - 1)
    def _():
        o_ref[...] = x_ref[...] + gate_ref[0] * acc_scr[...]


def ffn_residual(x, g, sh, sc, gate, wg, wu, wd, rows_per_set, tm=1024, tf=256):
    rows = x.shape[0]
    si = _set_index(tm, rows_per_set)
    vec = pl.BlockSpec((1, 1, D_MODEL), lambda i, j: (si(i), 0, 0))
    return pl.pallas_call(
        _ffn_kernel, name="ffn",
        grid=(rows // tm, D_FF // tf),
        in_specs=[pl.BlockSpec((tm, D_MODEL), lambda i, j: (i, 0)),
                  pl.BlockSpec((1, D_MODEL), lambda i, j: (0, 0)),
                  vec, vec, vec,
                  pl.BlockSpec((D_MODEL, tf), lambda i, j: (0, j)),
                  pl.BlockSpec((D_MODEL, tf), lambda i, j: (0, j)),
                  pl.BlockSpec((tf, D_MODEL), lambda i, j: (j, 0))],
        out_specs=pl.BlockSpec((tm, D_MODEL), lambda i, j: (i, 0)),
        out_shape=jax.ShapeDtypeStruct((rows, D_MODEL), F32),
        scratch_shapes=[pltpu.VMEM((tm, D_MODEL), BF16), pltpu.VMEM((tm, D_MODEL), F32)],
        compiler_params=_cparams(("parallel", "arbitrary")),
    )(x, g.reshape(1, D_MODEL), sh, sc, gate, wg, wu, wd)


def _pool_consts(n):
    t = np.arange(n)
    mats, cnts = [], []
    for win in POOL_WINDOWS:
        lo = np.clip(t - win // 2, 0, n)
        hi = np.clip(t + win - win // 2, 0, n)
        mats.append(((t[None, :] >= lo[:, None]) & (t[None, :] < hi[:, None])).astype(np.float32))
        cnts.append(np.repeat((hi - lo).astype(np.float32)[:, None], POOL_GW, axis=1))
    return jnp.asarray(np.stack(mats)).astype(BF16), jnp.asarray(np.concatenate(cnts, axis=1))


def _pool_kernel(x_ref, pm_ref, cnt_ref, w_ref, scale_ref, o_ref):
    x = x_ref[...]
    hi = x.astype(BF16)
    lo = (x - hi.astype(F32)).astype(BF16)
    sums = []
    for g in range(len(POOL_WINDOWS)):
        sl = slice(g * POOL_GW, (g + 1) * POOL_GW)
        sums.append(_dot(pm_ref[g], hi[:, sl]) + _dot(pm_ref[g], lo[:, sl]))
    y = jnp.concatenate(sums, axis=-1) / cnt_ref[...] - x
    o_ref[...] = _dot(y.astype(BF16), w_ref[...]) * scale_ref[...]


def pool_mix(x, n, w_bd, scale):
    rows = x.shape[0]
    pm, cnt = _pool_consts(n)
    return pl.pallas_call(
        _pool_kernel, name="pool",
        grid=(rows // n,),
        in_specs=[pl.BlockSpec((n, POOL_CH), lambda b: (b, 0)),
                  pl.BlockSpec((len(POOL_WINDOWS), n, n), lambda b: (0, 0, 0)),
                  pl.BlockSpec((n, POOL_CH), lambda b: (0, 0)),
                  pl.BlockSpec((POOL_CH, POOL_CH), lambda b: (0, 0)),
                  pl.BlockSpec((1, POOL_CH), lambda b: (0, 0))],
        out_specs=pl.BlockSpec((n, POOL_CH), lambda b: (b, 0)),
        out_shape=jax.ShapeDtypeStruct((rows, POOL_CH), F32),
        compiler_params=_cparams(("parallel",)),
    )(x, pm, cnt, w_bd, scale.reshape(1, POOL_CH))


RW_TILE = 256
HALO = 8
SUB = 8
PACK_R = RWKV_HEADS * SUB


def _head_ones():
    h = np.arange(RWKV_DIM) // HEAD_DIM
    return jnp.asarray((h[:, None] == h[None, :]).astype(np.float32)).astype(BF16)


def _head_sum(x, ones):
    hi = x.astype(BF16)
    lo = (x - hi.astype(F32)).astype(BF16)
    return _dot(hi, ones) + _dot(lo, ones)


def _pack_heads(a, b, o_ref, lead):
    n = a.shape[0]
    lane = lax.broadcasted_iota(jnp.int32, (n, LANES), 1)
    low = lane < HEAD_DIM
    for c in range(RWKV_DIM // LANES):
        ac = a[:, c * LANES:(c + 1) * LANES]
        bc = b[:, c * LANES:(c + 1) * LANES]
        even = jnp.where(low, ac, pltpu.roll(bc, HEAD_DIM, 1))
        odd = jnp.where(low, pltpu.roll(ac, HEAD_DIM, 1), bc)
        for h, val in ((2 * c, even), (2 * c + 1, odd)):
            o_ref[lead + (slice(None), slice(h * SUB, (h + 1) * SUB), slice(None))] = val.reshape(n // SUB, SUB, LANES)


def _rwkv_prep_kernel(z_ref, zp_ref, zn_ref, mu_ref, kk_w_ref, ka_ref, rk_ref, w0_ref, a0_ref,
                      dup_ref, iup_ref, gup_ref, ones_ref,
                      g1_ref, g2_ref, g3_ref, g_ref, bonus_ref, *, tiles_per_seq):
    i = pl.program_id(0)
    pos = i % tiles_per_seq
    z = z_ref[...]
    row = lax.broadcasted_iota(jnp.int32, (RW_TILE, 1), 0)
    prev_edge = jnp.where(pos == 0, 0.0, zp_ref[HALO - 1:HALO, :])
    next_edge = jnp.where(pos == tiles_per_seq - 1, 0.0, zn_ref[0:1, :])
    prev = jnp.where(row == 0, prev_edge, pltpu.roll(z, 1, 0))
    nxt = jnp.where(row == RW_TILE - 1, next_edge, pltpu.roll(z, RW_TILE - 1, 0))
    zr = z + mu_ref[0:1, :] * (prev - z) + mu_ref[1:2, :] * (nxt - z)

    d = RWKV_DIM
    r, k, v = zr[:, :d], zr[:, d:2 * d], zr[:, 2 * d:3 * d]
    lora = 3 * d
    ones = ones_ref[...]
    kk = k * kk_w_ref[...]
    kk = kk * lax.rsqrt(_head_sum(kk * kk, ones) + L2_EPS)
    _pack_heads(r, v, g3_ref, ())
    for dr in range(2):
        wl = zr[:, lora + dr * DECAY_LORA: lora + (dr + 1) * DECAY_LORA]
        al = zr[:, lora + 2 * DECAY_LORA + dr * ICLR_LORA: lora + 2 * DECAY_LORA + (dr + 1) * ICLR_LORA]
        lw = w0_ref[dr:dr + 1, :] + _dot(jnp.tanh(wl).astype(BF16), dup_ref[dr])
        w = jnp.exp(-DECAY_SCALE * _sigmoid(lw))
        a = _sigmoid(a0_ref[dr:dr + 1, :] + _dot(al.astype(BF16), iup_ref[dr]))
        _pack_heads(w, kk * a, g1_ref, (dr,))
        _pack_heads(k * (1.0 + (a - 1.0) * ka_ref[...]), kk, g2_ref, (dr,))
    gl = zr[:, lora + 2 * DECAY_LORA + 2 * ICLR_LORA:]
    g_ref[...] = _dot(_sigmoid(gl).astype(BF16), gup_ref[...])
    bonus_ref[...] = _head_sum(r * k * rk_ref[...], ones) * v


def rwkv_prep(z, n, mu, k_k, k_a, r_k, w0, a0, dup, iup, gup):
    rows = z.shape[0]
    nb = rows // n
    tps = n // RW_TILE
    hb = RW_TILE // HALO
    last = rows // HALO - 1
    d = RWKV_DIM
    full2 = lambda shape: pl.BlockSpec(shape, lambda i: (0, 0))
    full3 = lambda shape: pl.BlockSpec(shape, lambda i: (0, 0, 0))
    tile = pl.BlockSpec((RW_TILE, d), lambda i: (i, 0))
    pk2 = pl.BlockSpec((2, RW_TILE // SUB, PACK_R, LANES), lambda i: (0, i % tps, i // tps, 0))
    pk1 = pl.BlockSpec((RW_TILE // SUB, PACK_R, LANES), lambda i: (i % tps, i // tps, 0))
    return pl.pallas_call(
        functools.partial(_rwkv_prep_kernel, tiles_per_seq=tps), name="rwkv_prep",
        grid=(rows // RW_TILE,),
        in_specs=[pl.BlockSpec((RW_TILE, RWKV_IN), lambda i: (i, 0)),
                  pl.BlockSpec((HALO, RWKV_IN), lambda i: (jnp.maximum(i * hb - 1, 0), 0)),
                  pl.BlockSpec((HALO, RWKV_IN), lambda i: (jnp.minimum((i + 1) * hb, last), 0)),
                  full2((2, RWKV_IN)), full2((1, d)), full2((1, d)), full2((1, d)),
                  full2((2, d)), full2((2, d)),
                  full3((2, DECAY_LORA, d)), full3((2, ICLR_LORA, d)), full2((GATE_LORA, d)),
                  full2((d, d))],
        out_specs=[pk2, pk2, pk1, tile, tile],
        out_shape=[jax.ShapeDtypeStruct((2, n // SUB, nb * PACK_R, LANES), F32),
                   jax.ShapeDtypeStruct((2, n // SUB, nb * PACK_R, LANES), F32),
                   jax.ShapeDtypeStruct((n // SUB, nb * PACK_R, LANES), F32),
                   jax.ShapeDtypeStruct((rows, d), F32),
                   jax.ShapeDtypeStruct((rows, d), F32)],
        compiler_params=_cparams(("parallel",)),
    )(z, z, z, mu, k_k.reshape(1, d), k_a.reshape(1, d), r_k.reshape(1, d), w0, a0,
      dup.astype(BF16), iup.astype(BF16), gup.astype(BF16), _head_ones())


SCAN_TC = 32
SCAN_G = SCAN_TC // SUB
SLOTS = 4
V_BLOCK = 32
K_CHUNK = 16
N_KC = HEAD_DIM // K_CHUNK
PEEL = 2


def _wkv_first_sa(s_scr, t2, vs):
    sas = []
    for vb in range(vs // V_BLOCK):
        rows = slice(vb * V_BLOCK, (vb + 1) * V_BLOCK)

        def chunk(kc, sa):
            for j in range(K_CHUNK):
                sa = sa + s_scr[kc * K_CHUNK + j, rows, :] * t2[N_KC + kc, j:j + 1, :]
            return sa

        sas.append(lax.fori_loop(0, N_KC, chunk, jnp.zeros((V_BLOCK, LANES), F32)))
    return tuple(sas)


def _wkv_step(s_scr, t1, t2, t3, t2_next, v_blocks, o_ref, sas):
    nxt = []
    for vb, v_blk in enumerate(v_blocks):
        rows = slice(vb * V_BLOCK, (vb + 1) * V_BLOCK)
        sa = sas[vb]

        def chunk(kc, carry):
            o, sa_n = carry
            for j in range(K_CHUNK):
                k = kc * K_CHUNK + j
                s_new = (s_scr[k, rows, :] * t1[kc, j:j + 1, :] - sa * t1[N_KC + kc, j:j + 1, :]
                         + v_blk * t2[kc, j:j + 1, :])
                s_scr[k, rows, :] = s_new
                o = o + s_new * t3[kc, j:j + 1, :]
                sa_n = sa_n + s_new * t2_next[N_KC + kc, j:j + 1, :]
            return o, sa_n

        zero = jnp.zeros((V_BLOCK, LANES), F32)
        carry = (zero, zero)
        for kc in range(PEEL):
            carry = chunk(kc, carry)
        o, sa_n = lax.fori_loop(PEEL, N_KC, chunk, carry)
        o_ref[rows, :] = o
        nxt.append(sa_n)
    return tuple(nxt)


def _store_tile(ref, idx, x):
    ref[idx] = x.reshape(2 * N_KC, K_CHUNK, LANES)


def _step_rows(ref, lead, grp, sub, n):
    return ref.at[lead + (grp,)][pl.ds(sub, n, stride=SUB), :]


def _scan_prompt_kernel(g1_ref, g2_ref, g3_ref, o_ref, st_ref, s_scr, ta, tb, tc, td, oa, ob, *, reverse):
    c = pl.program_id(1)
    tiles = (ta, tb, tc, td)
    outs = (oa, ob)

    @pl.when(c == 0)
    def _():
        s_scr[...] = jnp.zeros_like(s_scr)

    for o_scr in outs:
        o_scr[...] = jnp.zeros_like(o_scr)

    def where(grp, sub):
        return (SCAN_G - 1 - grp, SUB - 1 - sub) if reverse else (grp, sub)

    def load_tiles(grp, sub, slot):
        g, s = where(grp, sub)
        _store_tile(tiles[slot], 0, _step_rows(g1_ref, (0,), g, s, LANES).T)
        _store_tile(tiles[slot], 1, _step_rows(g2_ref, (0,), g, s, LANES).T)
        _store_tile(tiles[slot], 2, _step_rows(g3_ref, (), g, s, LANES).T)

    def flush(grp, sub, o_scr):
        g, s = where(grp, sub)
        o_ref.at[0, g][pl.ds(s, LANES, stride=SUB), :] = o_scr[...].T

    load_tiles(0, 0, 0)
    load_tiles(0, 1, 1)
    sas0 = _wkv_first_sa(s_scr, tiles[0].at[1], HEAD_DIM)
    per_blk = V_BLOCK // K_CHUNK

    def group(grp, sas):
        for j in range(SUB):
            cur, nxt = tiles[j % SLOTS], tiles[(j + 1) % SLOTS]
            ahead = j + 2
            load_tiles(grp if ahead < SUB else jnp.minimum(grp + 1, SCAN_G - 1), ahead % SUB, ahead % SLOTS)
            if j > 0:
                flush(grp, j - 1, outs[(j - 1) % 2])
            v_blocks = [cur[2, N_KC + vb * per_blk: N_KC + (vb + 1) * per_blk].reshape(V_BLOCK, LANES)
                        for vb in range(HEAD_DIM // V_BLOCK)]
            sas = _wkv_step(s_scr, cur.at[0], cur.at[1], cur.at[2], nxt.at[1], v_blocks, outs[j % 2], sas)
        flush(grp, SUB - 1, outs[(SUB - 1) % 2])
        return sas

    lax.fori_loop(0, SCAN_G, group, sas0)

    @pl.when(c == pl.num_programs(1) - 1)
    def _():
        st_ref[0] = s_scr[...]


def wkv_scan_prompt(g1, g2, g3, direction):
    n_grp, rows = g3.shape[0], g3.shape[1]
    groups = rows // (LANES * SUB)
    nblk = n_grp // SCAN_G
    tb = (lambda s: nblk - 1 - s) if direction else (lambda s: s)
    blk = (SCAN_G, LANES * SUB, LANES)
    dir_blk = pl.BlockSpec((1,) + blk, lambda g, s: (direction, tb(s), g, 0))
    return pl.pallas_call(
        functools.partial(_scan_prompt_kernel, reverse=bool(direction)), name="wkv_scan_prompt",
        grid=(groups, nblk),
        in_specs=[dir_blk, dir_blk, pl.BlockSpec(blk, lambda g, s: (tb(s), g, 0))],
        out_specs=[pl.BlockSpec((1,) + blk, lambda g, s: (0, tb(s), g, 0)),
                   pl.BlockSpec((1, HEAD_DIM, HEAD_DIM, LANES), lambda g, s: (g, 0, 0, 0))],
        out_shape=[jax.ShapeDtypeStruct((1, n_grp, rows, LANES), F32),
                   jax.ShapeDtypeStruct((groups, HEAD_DIM, HEAD_DIM, LANES), F32)],
        scratch_shapes=([pltpu.VMEM((HEAD_DIM, HEAD_DIM, LANES), F32)]
                        + [pltpu.VMEM((3, 2 * N_KC, K_CHUNK, LANES), F32)] * SLOTS
                        + [pltpu.VMEM((LANES, LANES), F32)] * 2),
        compiler_params=_cparams(("parallel", "arbitrary")),
    )(g1, g2, g3)


S_CHAINS = DEC_BATCH * RWKV_HEADS
S_VS = HEAD_DIM // 2


def _scan_sample_kernel(g1f_ref, g1b_ref, g2f_ref, g2b_ref, g3f_ref, g3b_ref, s0_ref,
                        of_ref, ob_ref, s_scr, ta, tb, tc, td, va, vb, vc, vd, oa, ob):
    c = pl.program_id(0)
    tiles = (ta, tb, tc, td)
    vals = (va, vb, vc, vd)
    outs = (oa, ob)

    @pl.when(c == 0)
    def _():
        s_scr[...] = s0_ref[...]

    nc = 2 * S_CHAINS
    zpad = jnp.zeros((LANES - 2 * nc, LANES), F32)
    lane = lax.broadcasted_iota(jnp.int32, (S_VS, LANES), 1)

    def stacked_t(f_ref, b_ref, grp, sub):
        f = _step_rows(f_ref, (0,), grp, sub, S_CHAINS)
        b = _step_rows(b_ref, (0,), SCAN_G - 1 - grp, SUB - 1 - sub, S_CHAINS)
        return jnp.concatenate([f, b, f, b, zpad], axis=0).T

    def load_tiles(grp, sub, slot):
        _store_tile(tiles[slot], 0, stacked_t(g1f_ref, g1b_ref, grp, sub))
        _store_tile(tiles[slot], 1, stacked_t(g2f_ref, g2b_ref, grp, sub))
        t3 = stacked_t(g3f_ref, g3b_ref, grp, sub)
        _store_tile(tiles[slot], 2, t3)
        vals[slot][...] = jnp.where(lane < nc, t3[HEAD_DIM:HEAD_DIM + S_VS], t3[HEAD_DIM + S_VS:])

    def flush(grp, sub, o_scr):
        o = o_scr[...]
        full = jnp.concatenate([o, pltpu.roll(o, LANES - nc, 1), jnp.zeros((LANES - HEAD_DIM, LANES), F32)], axis=0)
        ot = full.T
        of_ref.at[0, grp][pl.ds(sub, S_CHAINS, stride=SUB), :] = ot[0:S_CHAINS]
        ob_ref.at[0, SCAN_G - 1 - grp][pl.ds(SUB - 1 - sub, S_CHAINS, stride=SUB), :] = ot[S_CHAINS:nc]

    load_tiles(0, 0, 0)
    load_tiles(0, 1, 1)
    sas0 = _wkv_first_sa(s_scr, tiles[0].at[1], S_VS)

    def group(grp, sas):
        for j in range(SUB):
            cur, nxt = tiles[j % SLOTS], tiles[(j + 1) % SLOTS]
            ahead = j + 2
            load_tiles(grp if ahead < SUB else jnp.minimum(grp + 1, SCAN_G - 1), ahead % SUB, ahead % SLOTS)
            if j > 0:
                flush(grp, j - 1, outs[(j - 1) % 2])
            sas = _wkv_step(s_scr, cur.at[0], cur.at[1], cur.at[2], nxt.at[1], [vals[j % SLOTS][...]],
                            outs[j % 2], sas)
        flush(grp, SUB - 1, outs[(SUB - 1) % 2])
        return sas

    lax.fori_loop(0, SCAN_G, group, sas0)


def wkv_scan_sample(g1, g2, g3, s0):
    n_grp, rows = g3.shape[0], g3.shape[1]
    g3 = g3.reshape(1, n_grp, rows, LANES)
    nblk = n_grp // SCAN_G
    blk = (1, SCAN_G, rows, LANES)
    fwd = lambda d: pl.BlockSpec(blk, lambda s: (d, s, 0, 0))
    bwd = lambda d: pl.BlockSpec(blk, lambda s: (d, nblk - 1 - s, 0, 0))
    return pl.pallas_call(
        _scan_sample_kernel, name="wkv_scan_sample",
        grid=(nblk,),
        in_specs=[fwd(0), bwd(1), fwd(0), bwd(1), fwd(0), bwd(0),
                  pl.BlockSpec((HEAD_DIM, S_VS, LANES), lambda s: (0, 0, 0))],
        out_specs=[fwd(0), bwd(0)],
        out_shape=[jax.ShapeDtypeStruct((1, n_grp, rows, LANES), F32)] * 2,
        scratch_shapes=([pltpu.VMEM((HEAD_DIM, S_VS, LANES), F32)]
                        + [pltpu.VMEM((3, 2 * N_KC, K_CHUNK, LANES), F32)] * SLOTS
                        + [pltpu.VMEM((S_VS, LANES), F32)] * SLOTS
                        + [pltpu.VMEM((S_VS, LANES), F32)] * 2),
        compiler_params=_cparams(("arbitrary",)),
    )(g1, g1, g2, g2, g3, g3, s0)


def _sample_state_lanes(s0):
    nc = 2 * S_CHAINS
    st = jnp.transpose(s0, (4, 3, 1, 0, 2)).reshape(HEAD_DIM, HEAD_DIM, nc)
    st = jnp.concatenate([st[:, :S_VS], st[:, S_VS:]], axis=-1)
    return jnp.pad(st, ((0, 0), (0, 0), (0, LANES - 2 * nc)))


def _rwkv_post_kernel(of_ref, ob_ref, g_ref, bonus_ref, gw_ref, gb_ref, ones_ref, y_ref):
    ones = ones_ref[...]
    lane = lax.broadcasted_iota(jnp.int32, (RW_TILE, LANES), 1)
    low = lane < HEAD_DIM

    def head(h):
        rows = slice(h * SUB, (h + 1) * SUB)
        return (of_ref[0, :, rows, :] + ob_ref[0, :, rows, :]).reshape(RW_TILE, LANES)

    cols = [jnp.where(low, head(2 * c), pltpu.roll(head(2 * c + 1), HEAD_DIM, 1))
            for c in range(RWKV_DIM // LANES)]
    o = jnp.concatenate(cols, axis=-1)
    mu = _head_sum(o, ones) / HEAD_DIM
    oc = o - mu
    var = _head_sum(oc * oc, ones) / HEAD_DIM
    on = (oc * lax.rsqrt(var + GN_EPS)) * gw_ref[...] + gb_ref[...]
    y_ref[...] = (on + bonus_ref[...]) * g_ref[...]


def rwkv_post(o_f, o_b, n, g, bonus, gn_w, gn_b):
    rows, d = g.shape
    tps = n // RW_TILE
    tile = pl.BlockSpec((RW_TILE, d), lambda i: (i, 0))
    vec = pl.BlockSpec((1, d), lambda i: (0, 0))
    pk = pl.BlockSpec((1, RW_TILE // SUB, PACK_R, LANES), lambda i: (0, i % tps, i // tps, 0))
    return pl.pallas_call(
        _rwkv_post_kernel, name="rwkv_post",
        grid=(rows // RW_TILE,),
        in_specs=[pk, pk, tile, tile, vec, vec, pl.BlockSpec((d, d), lambda i: (0, 0))],
        out_specs=tile,
        out_shape=jax.ShapeDtypeStruct((rows, d), F32),
        compiler_params=_cparams(("parallel",)),
    )(o_f, o_b, g, bonus, gn_w.reshape(1, d), gn_b.reshape(1, d), _head_ones())


MOE_R = 512
MOE_M = 256


def _split3(x):
    a = x.astype(BF16)
    r1 = x - a.astype(F32)
    b = r1.astype(BF16)
    c = (r1 - b.astype(F32)).astype(BF16)
    return a, b, c


def _router_kernel(x_ref, g_ref, sh_ref, sc_ref, w_ref, b_ref, tri_ref, h_ref, comb_ref, rank_ref, rank_t_ref, cnt_ref):
    h = _modulated(x_ref[...], g_ref[...], sh_ref[0], sc_ref[0])
    h_ref[...] = h.astype(BF16)
    h1, h2, h3 = _split3(h)
    w1, w2, w3 = _split3(w_ref[...])
    logits = (_dot(h1, w1) + (_dot(h1, w2) + _dot(h2, w1))
              + (_dot(h1, w3) + _dot(h2, w2) + _dot(h3, w1))) + b_ref[...]
    col = lax.broadcasted_iota(jnp.int32, logits.shape, 1)
    logits = jnp.where(col < N_EXPERTS, logits, -jnp.inf)
    m1 = jnp.max(logits, axis=-1, keepdims=True)
    i1 = jnp.min(jnp.where(logits == m1, col, LANES), axis=-1, keepdims=True)
    rest = jnp.where(col == i1, -jnp.inf, logits)
    m2 = jnp.max(rest, axis=-1, keepdims=True)
    i2 = jnp.min(jnp.where(rest == m2, col, LANES), axis=-1, keepdims=True)
    e2 = jnp.exp(m2 - m1)
    den = 1.0 + e2
    comb_ref[...] = jnp.where(col == i1, 1.0 / den, 0.0) + jnp.where(col == i2, e2 / den, 0.0)
    chosen = (col == i1) | (col == i2)
    upto = _dot(tri_ref[...], jnp.where(chosen, 1.0, 0.0).astype(BF16))
    rank = jnp.where(chosen, upto - 1.0, -1.0)
    rank_ref[...] = rank
    rank_t_ref[0] = rank.T[0:N_EXPERTS, :]
    cnt_ref[0] = jnp.broadcast_to(upto[MOE_R - 1:MOE_R, :], (8, LANES))


def moe_router(x, g, sh, sc, router_w, router_b, rows_per_set):
    rows = x.shape[0]
    tm = MOE_R
    nblk = rows // tm
    si = _set_index(tm, rows_per_set)
    vec = pl.BlockSpec((1, 1, D_MODEL), lambda i: (si(i), 0, 0))
    w = jnp.pad(router_w, ((0, 0), (0, LANES - N_EXPERTS)))
    b = jnp.pad(router_b, (0, LANES - N_EXPERTS)).reshape(1, LANES)
    tri = jnp.asarray(np.tril(np.ones((tm, tm), np.float32))).astype(BF16)
    return pl.pallas_call(
        _router_kernel, name="moe_router",
        grid=(nblk,),
        in_specs=[pl.BlockSpec((tm, D_MODEL), lambda i: (i, 0)),
                  pl.BlockSpec((1, D_MODEL), lambda i: (0, 0)),
                  vec, vec,
                  pl.BlockSpec((D_MODEL, LANES), lambda i: (0, 0)),
                  pl.BlockSpec((1, LANES), lambda i: (0, 0)),
                  pl.BlockSpec((tm, tm), lambda i: (0, 0))],
        out_specs=[pl.BlockSpec((tm, D_MODEL), lambda i: (i, 0)),
                   pl.BlockSpec((tm, LANES), lambda i: (i, 0)),
                   pl.BlockSpec((tm, LANES), lambda i: (i, 0)),
                   pl.BlockSpec((1, N_EXPERTS, tm), lambda i: (i, 0, 0)),
                   pl.BlockSpec((1, 8, LANES), lambda i: (i, 0, 0))],
        out_shape=[jax.ShapeDtypeStruct((rows, D_MODEL), BF16),
                   jax.ShapeDtypeStruct((rows, LANES), F32),
                   jax.ShapeDtypeStruct((rows, LANES), F32),
                   jax.ShapeDtypeStruct((nblk, N_EXPERTS, tm), F32),
                   jax.ShapeDtypeStruct((nblk, 8, LANES), F32)],
        compiler_params=_cparams(("parallel",)),
    )(x, g.reshape(1, D_MODEL), sh, sc, w, b, tri)


MOE_TM = 1024


def _moe_kernel(cnt_ref, x_ref, h_ref, comb_ref, rank_ref, rank_t_ref, gate_ref, gfin_ref, wg_ref, wu_ref, wd_ref,
                o_ref, acc_scr):
    i = pl.program_id(0)
    e = pl.program_id(1)

    @pl.when(e == 0)
    def _():
        acc_scr[...] = jnp.zeros_like(acc_scr)

    col = lax.broadcasted_iota(jnp.int32, (MOE_R, LANES), 1)
    slot_rows = lax.broadcasted_iota(jnp.int32, (MOE_M, MOE_R), 0).astype(F32)
    slot_cols = lax.broadcasted_iota(jnp.int32, (MOE_R, MOE_M), 1).astype(F32)
    for s in range(MOE_TM // MOE_R):
        blk = slice(s * MOE_R, (s + 1) * MOE_R)
        count = cnt_ref[(i * (MOE_TM // MOE_R) + s) * N_EXPERTS + e]
        for m in range(MOE_R // MOE_M):
            @pl.when(count > m * MOE_M)
            def _():
                take = (rank_t_ref[s, pl.ds(e, 1), :] == slot_rows + float(m * MOE_M))
                hc = _dot(jnp.where(take, 1.0, 0.0).astype(BF16), h_ref[blk, :]).astype(BF16)
                gt = _dot(hc, wg_ref[0])
                act = (gt * _sigmoid(gt)) * _dot(hc, wu_ref[0])
                y = _dot(act.astype(BF16), wd_ref[0]).astype(BF16)
                mine = col == e
                rank_e = jnp.sum(jnp.where(mine, rank_ref[blk, :], 0.0), axis=-1, keepdims=True)
                ce = jnp.sum(jnp.where(mine, comb_ref[blk, :], 0.0), axis=-1, keepdims=True)
                put = rank_e == slot_cols + float(m * MOE_M)
                acc_scr[blk, :] += ce * _dot(jnp.where(put, 1.0, 0.0).astype(BF16), y)

    @pl.when(e == pl.num_programs(1) - 1)
    def _():
        y = x_ref[...] + gate_ref[0] * acc_scr[...]
        ms = jnp.mean(y * y, axis=-1, keepdims=True)
        o_ref[...] = y * lax.rsqrt(ms + RMS_EPS) * gfin_ref[...]


def moe_residual_norm(x, h, comb, rank, rank_t, counts, gate, g_final, wg, wu, wd, rows_per_set):
    rows = x.shape[0]
    tm = MOE_TM
    sub = tm // MOE_R
    si = _set_index(tm, rows_per_set)
    cnt = counts[:, 0, :N_EXPERTS].astype(jnp.int32).reshape(-1)
    grid_spec = pltpu.PrefetchScalarGridSpec(
        num_scalar_prefetch=1,
        grid=(rows // tm, N_EXPERTS),
        in_specs=[pl.BlockSpec((tm, D_MODEL), lambda i, e, c: (i, 0)),
                  pl.BlockSpec((tm, D_MODEL), lambda i, e, c: (i, 0)),
                  pl.BlockSpec((tm, LANES), lambda i, e, c: (i, 0)),
                  pl.BlockSpec((tm, LANES), lambda i, e, c: (i, 0)),
                  pl.BlockSpec((sub, N_EXPERTS, MOE_R), lambda i, e, c: (i, 0, 0)),
                  pl.BlockSpec((1, 1, D_MODEL), lambda i, e, c: (si(i), 0, 0)),
                  pl.BlockSpec((1, D_MODEL), lambda i, e, c: (0, 0)),
                  pl.BlockSpec((1, D_MODEL, D_FF_EXPERT), lambda i, e, c: (e, 0, 0)),
                  pl.BlockSpec((1, D_MODEL, D_FF_EXPERT), lambda i, e, c: (e, 0, 0)),
                  pl.BlockSpec((1, D_FF_EXPERT, D_MODEL), lambda i, e, c: (e, 0, 0))],
        out_specs=pl.BlockSpec((tm, D_MODEL), lambda i, e, c: (i, 0)),
        scratch_shapes=[pltpu.VMEM((tm, D_MODEL), F32)])
    return pl.pallas_call(
        _moe_kernel, name="moe_experts",
        grid_spec=grid_spec,
        out_shape=jax.ShapeDtypeStruct((rows, D_MODEL), F32),
        compiler_params=_cparams(("parallel", "arbitrary")),
    )(cnt, x, h, comb, rank, rank_t, gate, g_final.reshape(1, D_MODEL), wg, wu, wd)


def kernel(x_prompt, x_sample, cache_na_k, cache_na_v, state_wkv, c, c_ctx, mod_w, mod_b, norm_mix, norm_ffn, norm_final, na_w_in, fourier_w, na_rel_bias, na_w_out, ffn_w_gate, ffn_w_up, ffn_w_down, rw_w_in, pool_w, pool_scale, shift_mu, decay_w0, decay_up, iclr_a0, iclr_up, gate_up, k_k, k_a, r_k, gn_w, gn_b, rw_w_out, router_w, router_b, moe_w_gate, moe_w_up, moe_w_down):
    cond = jnp.concatenate([c_ctx[None, :], c, jnp.zeros((8 - N_SETS, D_MODEL), F32)], axis=0)
    mods = adaln_all(cond, mod_w, mod_b)[:, :N_SETS].reshape(DEPTH, N_SETS, 6, 1, D_MODEL)
    bf = lambda w: w.astype(BF16)

    xp = x_prompt.reshape(P_ROWS, D_MODEL)
    xs = x_sample.reshape(S_ROWS, D_MODEL)
    streams = {"p": (SEQ, P_ROWS, slice(0, 1)), "s": (DEC_SEQ, DEC_SEQ, slice(1, N_SETS))}
    x = {"p": xp, "s": xs}

    splits = ((0, FOURIER_CH), (FOURIER_CH, FOURIER_CH + NA_DIM),
              (FOURIER_CH + NA_DIM, FOURIER_CH + 2 * NA_DIM), (FOURIER_CH + 2 * NA_DIM, FOURIER_CH + 3 * NA_DIM))
    w_in, w_out = bf(na_w_in[0]), bf(na_w_out[0])
    f_bd = bf(_block_diag(fourier_w[0]))
    ffn_w = (bf(ffn_w_gate[0]), bf(ffn_w_up[0]), bf(ffn_w_down[0]))
    ck = cache_na_k[:, 0].reshape(DEC_BATCH * PAST_LEN, NA_DIM)
    cv = cache_na_v[:, 0].reshape(DEC_BATCH * PAST_LEN, NA_DIM)
    for name, (n, rps, sets) in streams.items():
        sh1, sc1, g1, sh2, sc2, g2 = [mods[0, sets, m] for m in range(6)]
        f, q, k, v = modulated_matmul(x[name], norm_mix[0], sh1, sc1, w_in, splits, rps)
        if name == "p":
            attn = context_attention(q, k, v)
            new_k = k.reshape(BATCH, 1, SEQ, NA_HEADS, HEAD_DIM)
            new_v = v.reshape(BATCH, 1, SEQ, NA_HEADS, HEAD_DIM)
        else:
            attn = neighbourhood_attention(q, k, v, ck, cv, _na_bias_table(na_rel_bias[0]))
        y = proj_residual(fourier_mix(f, n, f_bd), attn, x[name], g1, w_out, rps)
        x[name] = ffn_residual(y, norm_ffn[0], sh2, sc2, g2, *ffn_w, rps)

    w_in, w_out = bf(rw_w_in[0]), bf(rw_w_out[0])
    p_bd = bf(_block_diag(pool_w[0]))
    moe_w = (bf(moe_w_gate[0]), bf(moe_w_up[0]), bf(moe_w_down[0]))
    rw = (shift_mu[0], k_k[0], k_a[0], r_k[0], decay_w0[0], iclr_a0[0], decay_up[0], iclr_up[0], gate_up[0])
    out = {}
    for name, (n, rps, sets) in streams.items():
        sh1, sc1, g1, sh2, sc2, g2 = [mods[1, sets, m] for m in range(6)]
        pc, z = modulated_matmul(x[name], norm_mix[1], sh1, sc1, w_in, ((0, POOL_CH), (POOL_CH, POOL_CH + RWKV_IN)), rps)
        t1, t2, t3, gate, bonus = rwkv_prep(z, n, *rw)
        if name == "p":
            o_f, st_f = wkv_scan_prompt(t1, t2, t3, 0)
            o_b, st_b = wkv_scan_prompt(t1, t2, t3, 1)
            st = jnp.transpose(jnp.stack([st_f, st_b]), (0, 1, 4, 3, 2))
            st = jnp.transpose(st.reshape(2, BATCH, RWKV_HEADS, HEAD_DIM, HEAD_DIM), (1, 0, 2, 3, 4))
        else:
            o_f, o_b = wkv_scan_sample(t1, t2, t3, _sample_state_lanes(state_wkv[:, 0]))
        mixed = rwkv_post(o_f, o_b, n, gate, bonus, gn_w[0], gn_b[0])
        y = proj_residual(pool_mix(pc, n, p_bd, pool_scale[0]), mixed, x[name], g1, w_out, rps)
        routed = moe_router(y, norm_ffn[1], sh2, sc2, router_w[0], router_b[0], rps)
        out[name] = moe_residual_norm(y, *routed, g2, norm_final, *moe_w, rps)

    return (out["p"].reshape(BATCH, SEQ, D_MODEL), out["s"].reshape(DEC_BATCH, DEC_SEQ, D_MODEL),
            new_k, new_v, st[:, None])
```

```python
import functools
import math

import numpy as np
import jax
import jax.numpy as jnp
from jax import lax
from jax.experimental import pallas as pl
from jax.experimental.pallas import tpu as pltpu

F32 = jnp.float32
BF16 = jnp.bfloat16

D_MODEL = 1024
BATCH = 32
SEQ = 256
DEPTH = 2
DEC_BATCH = 2
DEC_SEQ = 1024
PAST_LEN = 512
GRID_W = 64
HEAD_DIM = 64
FOURIER_CH = D_MODEL // 4
FOURIER_GROUPS = 4
FOURIER_GW = FOURIER_CH // FOURIER_GROUPS
NA_DIM = D_MODEL - FOURIER_CH
NA_HEADS = NA_DIM // HEAD_DIM
NA_MAX_ROWS = 8
NA_COLS = 16
POOL_WINDOWS = (2, 4, 8, 16)
POOL_CH = D_MODEL // 4
POOL_GW = POOL_CH // len(POOL_WINDOWS)
RWKV_DIM = D_MODEL - POOL_CH
RWKV_HEADS = RWKV_DIM // HEAD_DIM
DECAY_LORA = 64
ICLR_LORA = 64
GATE_LORA = 128
RWKV_IN = 3 * RWKV_DIM + 2 * DECAY_LORA + 2 * ICLR_LORA + GATE_LORA
D_FF = 2816
N_EXPERTS = 8
D_FF_EXPERT = 1408
RMS_EPS = 1e-6
GN_EPS = 64e-5
L2_EPS = 1e-12
DECAY_SCALE = math.exp(-0.5)
NEG_INF = -1e30

P_ROWS = BATCH * SEQ
S_ROWS = DEC_BATCH * DEC_SEQ
N_ROWS = P_ROWS + S_ROWS
N_SETS = 1 + DEC_BATCH
LANES = 128
VMEM_LIMIT = 56 * 1024 * 1024


def _cparams(sem):
    return pltpu.CompilerParams(dimension_semantics=sem, vmem_limit_bytes=VMEM_LIMIT)


def _sigmoid(x):
    return 0.5 * jnp.tanh(0.5 * x) + 0.5


def _dot(a, b):
    return jnp.dot(a, b, preferred_element_type=F32)


def _dot_nt(a, b):
    return lax.dot_general(a, b, (((1,), (1,)), ((), ())), preferred_element_type=F32)


def _set_index(tm, rows_per_set):
    q = rows_per_set // tm
    return lambda i: i // q


def _modulated(x, g, sh, sc):
    ms = jnp.mean(x * x, axis=-1, keepdims=True)
    return (x * lax.rsqrt(ms + RMS_EPS) * g) * (1.0 + sc) + sh


def _adaln_kernel(c_ref, w_ref, b_ref, o_ref):
    c = c_ref[...]
    s = (c * _sigmoid(c)).astype(BF16)
    o_ref[0] = _dot(s, w_ref[0].astype(BF16)) + b_ref[0]


def adaln_all(cond, mod_w, mod_b):
    tn = 1536
    n = 6 * D_MODEL
    return pl.pallas_call(
        _adaln_kernel, name="adaln",
        grid=(DEPTH, n // tn),
        in_specs=[pl.BlockSpec((8, D_MODEL), lambda l, j: (0, 0)),
                  pl.BlockSpec((1, D_MODEL, tn), lambda l, j: (l, 0, j)),
                  pl.BlockSpec((1, 1, tn), lambda l, j: (l, 0, j))],
        out_specs=pl.BlockSpec((1, 8, tn), lambda l, j: (l, 0, j)),
        out_shape=jax.ShapeDtypeStruct((DEPTH, 8, n), F32),
        compiler_params=_cparams(("parallel", "parallel")),
    )(cond, mod_w, mod_b.reshape(DEPTH, 1, n))


def _modmm_kernel(x_ref, g_ref, sh_ref, sc_ref, w_ref, *o_refs, splits):
    h = _modulated(x_ref[...], g_ref[...], sh_ref[0], sc_ref[0]).astype(BF16)
    for o_ref, (a, b) in zip(o_refs, splits):
        o_ref[...] = _dot(h, w_ref[:, a:b]).astype(o_ref.dtype)


def modulated_matmul(x, g, sh, sc, w, splits, rows_per_set, tm=512):
    rows = x.shape[0]
    n_out = w.shape[1]
    si = _set_index(tm, rows_per_set)
    vec = pl.BlockSpec((1, 1, D_MODEL), lambda i: (si(i), 0, 0))
    return pl.pallas_call(
        functools.partial(_modmm_kernel, splits=splits), name="modulated_matmul",
        grid=(rows // tm,),
        in_specs=[pl.BlockSpec((tm, D_MODEL), lambda i: (i, 0)),
                  pl.BlockSpec((1, D_MODEL), lambda i: (0, 0)),
                  vec, vec,
                  pl.BlockSpec((D_MODEL, n_out), lambda i: (0, 0))],
        out_specs=[pl.BlockSpec((tm, b - a), lambda i: (i, 0)) for a, b in splits],
        out_shape=[jax.ShapeDtypeStruct((rows, b - a), F32) for a, b in splits],
        compiler_params=_cparams(("parallel",)),
    )(x, g.reshape(1, D_MODEL), sh, sc, w)


def _dft_mats(n):
    t = np.arange(n)
    ang = 2.0 * np.pi * ((t[:, None] * t[None, :]) % n) / n
    cn, sn = np.cos(ang) / np.sqrt(n), np.sin(ang) / np.sqrt(n)
    c = np.arange(FOURIER_GW)
    angc = 2.0 * np.pi * ((c[:, None] * c[None, :]) % FOURIER_GW) / FOURIER_GW
    eye = np.eye(FOURIER_GROUPS)
    cc = np.kron(eye, np.cos(angc) / np.sqrt(FOURIER_GW))
    sc = np.kron(eye, np.sin(angc) / np.sqrt(FOURIER_GW))
    as_bf = lambda a: jnp.asarray(a, dtype=F32).astype(BF16)
    return as_bf(cn), as_bf(sn), as_bf(cc), as_bf(sc)


def _fourier_kernel(f_ref, cn_ref, sn_ref, cc_ref, sc_ref, w_ref, o_ref):
    x = f_ref[...].astype(BF16)
    a = _dot(x, cc_ref[...]).astype(BF16)
    b = _dot(x, sc_ref[...]).astype(BF16)
    re = _dot(cn_ref[...], a) - _dot(sn_ref[...], b)
    o_ref[...] = _dot(re.astype(BF16), w_ref[...])


def _block_diag(w):
    g, c, _ = w.shape
    eye = jnp.eye(g, dtype=w.dtype)
    return (eye[:, None, :, None] * w[:, :, None, :]).reshape(g * c, g * c)


def fourier_mix(f, n, w_bd):
    rows = f.shape[0]
    cn, sn, cc, sc = _dft_mats(n)
    full = lambda shape: pl.BlockSpec(shape, lambda b: (0, 0))
    return pl.pallas_call(
        _fourier_kernel, name="fourier",
        grid=(rows // n,),
        in_specs=[pl.BlockSpec((n, FOURIER_CH), lambda b: (b, 0)),
                  full((n, n)), full((n, n)),
                  full((FOURIER_CH, FOURIER_CH)), full((FOURIER_CH, FOURIER_CH)),
                  full((FOURIER_CH, FOURIER_CH))],
        out_specs=pl.BlockSpec((n, FOURIER_CH), lambda b: (b, 0)),
        out_shape=jax.ShapeDtypeStruct((rows, FOURIER_CH), F32),
        compiler_params=_cparams(("parallel",)),
    )(f, cn, sn, cc, sc, w_bd)


def _ctx_attn_kernel(q_ref, k_ref, v_ref, o_ref):
    scale = HEAD_DIM ** -0.5
    ones = jnp.ones((SEQ, LANES), BF16)
    def scores(h):
        sl = slice(h * HEAD_DIM, (h + 1) * HEAD_DIM)
        return _dot_nt(q_ref[:, sl].astype(BF16), k_ref[:, sl].astype(BF16)) * scale

    outs = []
    ahead = 1
    pending = [scores(h) for h in range(ahead)]
    for h in range(NA_HEADS):
        if h + ahead < NA_HEADS:
            pending.append(scores(h + ahead))
        s = pending.pop(0)
        v = v_ref[:, h * HEAD_DIM:(h + 1) * HEAD_DIM].astype(BF16)
        p = jnp.exp(s - jnp.max(s, axis=-1, keepdims=True)).astype(BF16)
        den = _dot(p, ones)
        outs.append(_dot(p, v) / den[:, :HEAD_DIM])
    o_ref[...] = jnp.concatenate(outs, axis=-1)


def context_attention(q, k, v):
    blk = pl.BlockSpec((SEQ, NA_DIM), lambda b: (b, 0))
    return pl.pallas_call(
        _ctx_attn_kernel, name="ctx_attn",
        grid=(BATCH,),
        in_specs=[blk, blk, blk],
        out_specs=blk,
        out_shape=jax.ShapeDtypeStruct((P_ROWS, NA_DIM), F32),
        compiler_params=_cparams(("parallel",)),
    )(q, k, v)


NA_ROWS = DEC_SEQ // GRID_W
NA_WIN = NA_MAX_ROWS * GRID_W


def _na_bias_table(rel_bias):
    cols = np.arange(GRID_W)
    c0 = np.clip(cols - NA_COLS // 2, 0, GRID_W - NA_COLS)
    col_ok = (cols[None, :] >= c0[:, None]) & (cols[None, :] < c0[:, None] + NA_COLS)
    dc = np.clip(cols[None, :] - cols[:, None] + NA_COLS - 1, 0, 2 * NA_COLS - 2)
    onehot = (dc[None] == np.arange(2 * NA_COLS - 1)[:, None, None]).astype(np.float32)
    toe = jnp.einsum("hrj,jqk->hrqk", rel_bias, jnp.asarray(onehot), precision=lax.Precision.HIGHEST)
    toe = jnp.where(col_ok[None, None], toe, NEG_INF)
    tabs = [jnp.transpose(toe[:, NA_MAX_ROWS - 1 - o: 2 * NA_MAX_ROWS - 1 - o], (0, 2, 1, 3))
            for o in range(NA_MAX_ROWS)]
    return jnp.stack(tabs).reshape(NA_MAX_ROWS, NA_HEADS, GRID_W, NA_WIN)


def _na_row_start(i):
    return jnp.clip(i - NA_MAX_ROWS // 2, 0, NA_ROWS - NA_MAX_ROWS)


def _na_kernel(q_ref, k_ref, v_ref, ck_ref, cv_ref, bias_ref, o_ref):
    scale = HEAD_DIM ** -0.5
    i = pl.program_id(1)
    start = pl.multiple_of(_na_row_start(i) * GRID_W, GRID_W)
    kw = k_ref[pl.ds(start, NA_WIN), :]
    vw = v_ref[pl.ds(start, NA_WIN), :]
    ones = jnp.ones((NA_WIN, LANES), BF16)
    ones_ctx = jnp.ones((PAST_LEN, LANES), BF16)
    def scores(h):
        sl = slice(h * HEAD_DIM, (h + 1) * HEAD_DIM)
        q = q_ref[:, sl].astype(BF16)
        return (_dot_nt(q, kw[:, sl].astype(BF16)) * scale + bias_ref[0, h],
                _dot_nt(q, ck_ref[:, sl].astype(BF16)) * scale)

    outs = []
    ahead = 2
    pending = [scores(h) for h in range(ahead)]
    for h in range(NA_HEADS):
        if h + ahead < NA_HEADS:
            pending.append(scores(h + ahead))
        s_loc, s_ctx = pending.pop(0)
        sl = slice(h * HEAD_DIM, (h + 1) * HEAD_DIM)
        m = jnp.maximum(jnp.max(s_loc, axis=-1, keepdims=True), jnp.max(s_ctx, axis=-1, keepdims=True))
        p_loc = jnp.exp(s_loc - m).astype(BF16)
        p_ctx = jnp.exp(s_ctx - m).astype(BF16)
        den = _dot(p_loc, ones) + _dot(p_ctx, ones_ctx)
        num = _dot(p_loc, vw[:, sl].astype(BF16)) + _dot(p_ctx, cv_ref[:, sl].astype(BF16))
        outs.append(num / den[:, :HEAD_DIM])
    o_ref[...] = jnp.concatenate(outs, axis=-1)


def neighbourhood_attention(q, k, v, ck, cv, bias_tab):
    seq = pl.BlockSpec((DEC_SEQ, NA_DIM), lambda b, i: (b, 0))
    ctx = pl.BlockSpec((PAST_LEN, NA_DIM), lambda b, i: (b, 0))
    row = pl.BlockSpec((GRID_W, NA_DIM), lambda b, i: (b * NA_ROWS + i, 0))
    return pl.pallas_call(
        _na_kernel, name="na_attn",
        grid=(DEC_BATCH, NA_ROWS),
        in_specs=[row, seq, seq, ctx, ctx,
                  pl.BlockSpec((1, NA_HEADS, GRID_W, NA_WIN), lambda b, i: (i - _na_row_start(i), 0, 0, 0))],
        out_specs=row,
        out_shape=jax.ShapeDtypeStruct((S_ROWS, NA_DIM), F32),
        compiler_params=_cparams(("parallel", "arbitrary")),
    )(q, k, v, ck, cv, bias_tab)


def _proj_res_kernel(a_ref, b_ref, x_ref, gate_ref, w_ref, o_ref, *, na):
    y = _dot(a_ref[...].astype(BF16), w_ref[:na, :]) + _dot(b_ref[...].astype(BF16), w_ref[na:, :])
    o_ref[...] = x_ref[...] + gate_ref[0] * y


def proj_residual(a, b, x, gate, w, rows_per_set, tm=512):
    rows = x.shape[0]
    na, nb = a.shape[1], b.shape[1]
    si = _set_index(tm, rows_per_set)
    return pl.pallas_call(
        functools.partial(_proj_res_kernel, na=na), name="proj_residual",
        grid=(rows // tm,),
        in_specs=[pl.BlockSpec((tm, na), lambda i: (i, 0)),
                  pl.BlockSpec((tm, nb), lambda i: (i, 0)),
                  pl.BlockSpec((tm, D_MODEL), lambda i: (i, 0)),
                  pl.BlockSpec((1, 1, D_MODEL), lambda i: (si(i), 0, 0)),
                  pl.BlockSpec((na + nb, D_MODEL), lambda i: (0, 0))],
        out_specs=pl.BlockSpec((tm, D_MODEL), lambda i: (i, 0)),
        out_shape=jax.ShapeDtypeStruct((rows, D_MODEL), F32),
        compiler_params=_cparams(("parallel",)),
    )(a, b, x, gate, w)


def _ffn_kernel(x_ref, g_ref, sh_ref, sc_ref, gate_ref, wg_ref, wu_ref, wd_ref, o_ref, h_scr, acc_scr):
    j = pl.program_id(1)

    @pl.when(j == 0)
    def _():
        h_scr[...] = _modulated(x_ref[...], g_ref[...], sh_ref[0], sc_ref[0]).astype(BF16)
        acc_scr[...] = jnp.zeros_like(acc_scr)

    h = h_scr[...]
    gt = _dot(h, wg_ref[...])
    act = (gt * _sigmoid(gt)) * _dot(h, wu_ref[...])
    acc_scr[...] += _dot(act.astype(BF16), wd_ref[...])

    @pl.when(j == pl.num_programs(1) - 1)
    def _():
        o_ref[...] = x_ref[...] + gate_ref[0] * acc_scr[...]


def ffn_residual(x, g, sh, sc, gate, wg, wu, wd, rows_per_set, tm=512, tf=D_FF // 2):
    rows = x.shape[0]
    si = _set_index(tm, rows_per_set)
    vec = pl.BlockSpec((1, 1, D_MODEL), lambda i, j: (si(i), 0, 0))
    return pl.pallas_call(
        _ffn_kernel, name="ffn",
        grid=(rows // tm, D_FF // tf),
        in_specs=[pl.BlockSpec((tm, D_MODEL), lambda i, j: (i, 0)),
                  pl.BlockSpec((1, D_MODEL), lambda i, j: (0, 0)),
                  vec, vec, vec,
                  pl.BlockSpec((D_MODEL, tf), lambda i, j: (0, j)),
                  pl.BlockSpec((D_MODEL, tf), lambda i, j: (0, j)),
                  pl.BlockSpec((tf, D_MODEL), lambda i, j: (j, 0))],
        out_specs=pl.BlockSpec((tm, D_MODEL), lambda i, j: (i, 0)),
        out_shape=jax.ShapeDtypeStruct((rows, D_MODEL), F32),
        scratch_shapes=[pltpu.VMEM((tm, D_MODEL), BF16), pltpu.VMEM((tm, D_MODEL), F32)],
        compiler_params=_cparams(("parallel", "arbitrary")),
    )(x, g.reshape(1, D_MODEL), sh, sc, gate, wg, wu, wd)


def _pool_consts(n):
    t = np.arange(n)
    mats, cnts = [], []
    for win in POOL_WINDOWS:
        lo = np.clip(t - win // 2, 0, n)
        hi = np.clip(t + win - win // 2, 0, n)
        mats.append(((t[None, :] >= lo[:, None]) & (t[None, :] < hi[:, None])).astype(np.float32))
        cnts.append(np.repeat((hi - lo).astype(np.float32)[:, None], POOL_GW, axis=1))
    return jnp.asarray(np.stack(mats)).astype(BF16), jnp.asarray(np.concatenate(cnts, axis=1))


def _pool_kernel(x_ref, pm_ref, cnt_ref, w_ref, scale_ref, o_ref):
    x = x_ref[...]
    hi = x.astype(BF16)
    lo = (x - hi.astype(F32)).astype(BF16)
    sums = []
    for g in range(len(POOL_WINDOWS)):
        sl = slice(g * POOL_GW, (g + 1) * POOL_GW)
        sums.append(_dot(pm_ref[g], hi[:, sl]) + _dot(pm_ref[g], lo[:, sl]))
    y = jnp.concatenate(sums, axis=-1) / cnt_ref[...] - x
    o_ref[...] = _dot(y.astype(BF16), w_ref[...]) * scale_ref[...]


def pool_mix(x, n, w_bd, scale):
    rows = x.shape[0]
    pm, cnt = _pool_consts(n)
    return pl.pallas_call(
        _pool_kernel, name="pool",
        grid=(rows // n,),
        in_specs=[pl.BlockSpec((n, POOL_CH), lambda b: (b, 0)),
                  pl.BlockSpec((len(POOL_WINDOWS), n, n), lambda b: (0, 0, 0)),
                  pl.BlockSpec((n, POOL_CH), lambda b: (0, 0)),
                  pl.BlockSpec((POOL_CH, POOL_CH), lambda b: (0, 0)),
                  pl.BlockSpec((1, POOL_CH), lambda b: (0, 0))],
        out_specs=pl.BlockSpec((n, POOL_CH), lambda b: (b, 0)),
        out_shape=jax.ShapeDtypeStruct((rows, POOL_CH), F32),
        compiler_params=_cparams(("parallel",)),
    )(x, pm, cnt, w_bd, scale.reshape(1, POOL_CH))


RW_TILE = 256
HALO = 8
SUB = 8
PACK_R = RWKV_HEADS * SUB


def _head_ones():
    h = np.arange(RWKV_DIM) // HEAD_DIM
    return jnp.asarray((h[:, None] == h[None, :]).astype(np.float32)).astype(BF16)


def _head_sum(x, ones):
    hi = x.astype(BF16)
    lo = (x - hi.astype(F32)).astype(BF16)
    return _dot(hi, ones) + _dot(lo, ones)


def _pack_heads(a, b, o_ref, lead):
    n = a.shape[0]
    lane = lax.broadcasted_iota(jnp.int32, (n, LANES), 1)
    low = lane < HEAD_DIM
    for c in range(RWKV_DIM // LANES):
        ac = a[:, c * LANES:(c + 1) * LANES]
        bc = b[:, c * LANES:(c + 1) * LANES]
        even = jnp.where(low, ac, pltpu.roll(bc, HEAD_DIM, 1))
        odd = jnp.where(low, pltpu.roll(ac, HEAD_DIM, 1), bc)
        for h, val in ((2 * c, even), (2 * c + 1, odd)):
            o_ref[lead + (slice(None), slice(h * SUB, (h + 1) * SUB), slice(None))] = val.reshape(n // SUB, SUB, LANES)


def _rwkv_prep_kernel(z_ref, zp_ref, zn_ref, mu_ref, kk_w_ref, ka_ref, rk_ref, w0_ref, a0_ref,
                      dup_ref, iup_ref, gup_ref, ones_ref,
                      g1_ref, g2_ref, g3_ref, g_ref, bonus_ref, *, tiles_per_seq):
    i = pl.program_id(0)
    pos = i % tiles_per_seq
    z = z_ref[...]
    row = lax.broadcasted_iota(jnp.int32, (RW_TILE, 1), 0)
    prev_edge = jnp.where(pos == 0, 0.0, zp_ref[HALO - 1:HALO, :])
    next_edge = jnp.where(pos == tiles_per_seq - 1, 0.0, zn_ref[0:1, :])
    prev = jnp.where(row == 0, prev_edge, pltpu.roll(z, 1, 0))
    nxt = jnp.where(row == RW_TILE - 1, next_edge, pltpu.roll(z, RW_TILE - 1, 0))
    zr = z + mu_ref[0:1, :] * (prev - z) + mu_ref[1:2, :] * (nxt - z)

    d = RWKV_DIM
    r, k, v = zr[:, :d], zr[:, d:2 * d], zr[:, 2 * d:3 * d]
    lora = 3 * d
    ones = ones_ref[...]
    kk = k * kk_w_ref[...]
    kk = kk * lax.rsqrt(_head_sum(kk * kk, ones) + L2_EPS)
    _pack_heads(r, v, g3_ref, ())
    for dr in range(2):
        wl = zr[:, lora + dr * DECAY_LORA: lora + (dr + 1) * DECAY_LORA]
        al = zr[:, lora + 2 * DECAY_LORA + dr * ICLR_LORA: lora + 2 * DECAY_LORA + (dr + 1) * ICLR_LORA]
        lw = w0_ref[dr:dr + 1, :] + _dot(jnp.tanh(wl).astype(BF16), dup_ref[dr])
        w = jnp.exp(-DECAY_SCALE * _sigmoid(lw))
        a = _sigmoid(a0_ref[dr:dr + 1, :] + _dot(al.astype(BF16), iup_ref[dr]))
        _pack_heads(w, kk * a, g1_ref, (dr,))
        _pack_heads(k * (1.0 + (a - 1.0) * ka_ref[...]), kk, g2_ref, (dr,))
    gl = zr[:, lora + 2 * DECAY_LORA + 2 * ICLR_LORA:]
    g_ref[...] = _dot(_sigmoid(gl).astype(BF16), gup_ref[...])
    bonus_ref[...] = _head_sum(r * k * rk_ref[...], ones) * v


def rwkv_prep(z, n, mu, k_k, k_a, r_k, w0, a0, dup, iup, gup):
    rows = z.shape[0]
    nb = rows // n
    tps = n // RW_TILE
    hb = RW_TILE // HALO
    last = rows // HALO - 1
    d = RWKV_DIM
    full2 = lambda shape: pl.BlockSpec(shape, lambda i: (0, 0))
    full3 = lambda shape: pl.BlockSpec(shape, lambda i: (0, 0, 0))
    tile = pl.BlockSpec((RW_TILE, d), lambda i: (i, 0))
    pk2 = pl.BlockSpec((2, RW_TILE // SUB, PACK_R, LANES), lambda i: (0, i % tps, i // tps, 0))
    pk1 = pl.BlockSpec((RW_TILE // SUB, PACK_R, LANES), lambda i: (i % tps, i // tps, 0))
    return pl.pallas_call(
        functools.partial(_rwkv_prep_kernel, tiles_per_seq=tps), name="rwkv_prep",
        grid=(rows // RW_TILE,),
        in_specs=[pl.BlockSpec((RW_TILE, RWKV_IN), lambda i: (i, 0)),
                  pl.BlockSpec((HALO, RWKV_IN), lambda i: (jnp.maximum(i * hb - 1, 0), 0)),
                  pl.BlockSpec((HALO, RWKV_IN), lambda i: (jnp.minimum((i + 1) * hb, last), 0)),
                  full2((2, RWKV_IN)), full2((1, d)), full2((1, d)), full2((1, d)),
                  full2((2, d)), full2((2, d)),
                  full3((2, DECAY_LORA, d)), full3((2, ICLR_LORA, d)), full2((GATE_LORA, d)),
                  full2((d, d))],
        out_specs=[pk2, pk2, pk1, tile, tile],
        out_shape=[jax.ShapeDtypeStruct((2, n // SUB, nb * PACK_R, LANES), F32),
                   jax.ShapeDtypeStruct((2, n // SUB, nb * PACK_R, LANES), F32),
                   jax.ShapeDtypeStruct((n // SUB, nb * PACK_R, LANES), F32),
                   jax.ShapeDtypeStruct((rows, d), F32),
                   jax.ShapeDtypeStruct((rows, d), F32)],
        compiler_params=_cparams(("parallel",)),
    )(z, z, z, mu, k_k.reshape(1, d), k_a.reshape(1, d), r_k.reshape(1, d), w0, a0,
      dup.astype(BF16), iup.astype(BF16), gup.astype(BF16), _head_ones())


SCAN_TC = 32
SCAN_G = SCAN_TC // SUB
SLOTS = 4
V_BLOCK = 32
K_CHUNK = 16
N_KC = HEAD_DIM // K_CHUNK
PEEL = 2


def _wkv_first_sa(s_scr, t2, vs):
    sas = []
    for vb in range(vs // V_BLOCK):
        rows = slice(vb * V_BLOCK, (vb + 1) * V_BLOCK)

        def chunk(kc, sa):
            for j in range(K_CHUNK):
                sa = sa + s_scr[kc * K_CHUNK + j, rows, :] * t2[N_KC + kc, j:j + 1, :]
            return sa

        sas.append(lax.fori_loop(0, N_KC, chunk, jnp.zeros((V_BLOCK, LANES), F32)))
    return tuple(sas)


def _wkv_step(s_scr, t1, t2, t3, t2_next, v_blocks, o_ref, sas):
    nxt = []
    for vb, v_blk in enumerate(v_blocks):
        rows = slice(vb * V_BLOCK, (vb + 1) * V_BLOCK)
        sa = sas[vb]

        def chunk(kc, carry):
            o, sa_n = carry
            for j in range(K_CHUNK):
                k = kc * K_CHUNK + j
                s_new = (s_scr[k, rows, :] * t1[kc, j:j + 1, :] - sa * t1[N_KC + kc, j:j + 1, :]
                         + v_blk * t2[kc, j:j + 1, :])
                s_scr[k, rows, :] = s_new
                o = o + s_new * t3[kc, j:j + 1, :]
                sa_n = sa_n + s_new * t2_next[N_KC + kc, j:j + 1, :]
            return o, sa_n

        zero = jnp.zeros((V_BLOCK, LANES), F32)
        carry = (zero, zero)
        for kc in range(PEEL):
            carry = chunk(kc, carry)
        o, sa_n = lax.fori_loop(PEEL, N_KC, chunk, carry)
        o_ref[rows, :] = o
        nxt.append(sa_n)
    return tuple(nxt)


def _store_tile(ref, idx, x):
    ref[idx] = x.reshape(2 * N_KC, K_CHUNK, LANES)


def _step_rows(ref, lead, grp, sub, n):
    return ref.at[lead + (grp,)][pl.ds(sub, n, stride=SUB), :]


def _scan_prompt_kernel(g1_ref, g2_ref, g3_ref, o_ref, st_ref, s_scr, ta, tb, tc, td, oa, ob, *, reverse):
    c = pl.program_id(1)
    tiles = (ta, tb, tc, td)
    outs = (oa, ob)

    @pl.when(c == 0)
    def _():
        s_scr[...] = jnp.zeros_like(s_scr)

    for o_scr in outs:
        o_scr[...] = jnp.zeros_like(o_scr)

    def where(grp, sub):
        return (SCAN_G - 1 - grp, SUB - 1 - sub) if reverse else (grp, sub)

    def load_tiles(grp, sub, slot):
        g, s = where(grp, sub)
        _store_tile(tiles[slot], 0, _step_rows(g1_ref, (0,), g, s, LANES).T)
        _store_tile(tiles[slot], 1, _step_rows(g2_ref, (0,), g, s, LANES).T)
        _store_tile(tiles[slot], 2, _step_rows(g3_ref, (), g, s, LANES).T)

    def flush(grp, sub, o_scr):
        g, s = where(grp, sub)
        o_ref.at[0, g][pl.ds(s, LANES, stride=SUB), :] = o_scr[...].T

    load_tiles(0, 0, 0)
    load_tiles(0, 1, 1)
    sas0 = _wkv_first_sa(s_scr, tiles[0].at[1], HEAD_DIM)
    per_blk = V_BLOCK // K_CHUNK

    def group(grp, sas):
        for j in range(SUB):
            cur, nxt = tiles[j % SLOTS], tiles[(j + 1) % SLOTS]
            ahead = j + 2
            load_tiles(grp if ahead < SUB else jnp.minimum(grp + 1, SCAN_G - 1), ahead % SUB, ahead % SLOTS)
            if j > 0:
                flush(grp, j - 1, outs[(j - 1) % 2])
            v_blocks = [cur[2, N_KC + vb * per_blk: N_KC + (vb + 1) * per_blk].reshape(V_BLOCK, LANES)
                        for vb in range(HEAD_DIM // V_BLOCK)]
            sas = _wkv_step(s_scr, cur.at[0], cur.at[1], cur.at[2], nxt.at[1], v_blocks, outs[j % 2], sas)
        flush(grp, SUB - 1, outs[(SUB - 1) % 2])
        return sas

    lax.fori_loop(0, SCAN_G, group, sas0)

    @pl.when(c == pl.num_programs(1) - 1)
    def _():
        st_ref[0] = s_scr[...]


def wkv_scan_prompt(g1, g2, g3, direction):
    n_grp, rows = g3.shape[0], g3.shape[1]
    groups = rows // (LANES * SUB)
    nblk = n_grp // SCAN_G
    tb = (lambda s: nblk - 1 - s) if direction else (lambda s: s)
    blk = (SCAN_G, LANES * SUB, LANES)
    dir_blk = pl.BlockSpec((1,) + blk, lambda g, s: (direction, tb(s), g, 0))
    return pl.pallas_call(
        functools.partial(_scan_prompt_kernel, reverse=bool(direction)), name="wkv_scan_prompt",
        grid=(groups, nblk),
        in_specs=[dir_blk, dir_blk, pl.BlockSpec(blk, lambda g, s: (tb(s), g, 0))],
        out_specs=[pl.BlockSpec((1,) + blk, lambda g, s: (0, tb(s), g, 0)),
                   pl.BlockSpec((1, HEAD_DIM, HEAD_DIM, LANES), lambda g, s: (g, 0, 0, 0))],
        out_shape=[jax.ShapeDtypeStruct((1, n_grp, rows, LANES), F32),
                   jax.ShapeDtypeStruct((groups, HEAD_DIM, HEAD_DIM, LANES), F32)],
        scratch_shapes=([pltpu.VMEM((HEAD_DIM, HEAD_DIM, LANES), F32)]
                        + [pltpu.VMEM((3, 2 * N_KC, K_CHUNK, LANES), F32)] * SLOTS
                        + [pltpu.VMEM((LANES, LANES), F32)] * 2),
        compiler_params=_cparams(("parallel", "arbitrary")),
    )(g1, g2, g3)


S_CHAINS = DEC_BATCH * RWKV_HEADS
S_VS = HEAD_DIM // 2


def _scan_sample_kernel(g1f_ref, g1b_ref, g2f_ref, g2b_ref, g3f_ref, g3b_ref, s0_ref,
                        of_ref, ob_ref, s_scr, ta, tb, tc, td, va, vb, vc, vd, oa, ob):
    c = pl.program_id(0)
    tiles = (ta, tb, tc, td)
    vals = (va, vb, vc, vd)
    outs = (oa, ob)

    @pl.when(c == 0)
    def _():
        s_scr[...] = s0_ref[...]

    nc = 2 * S_CHAINS
    zpad = jnp.zeros((LANES - 2 * nc, LANES), F32)
    lane = lax.broadcasted_iota(jnp.int32, (S_VS, LANES), 1)

    def stacked_t(f_ref, b_ref, grp, sub):
        f = _step_rows(f_ref, (0,), grp, sub, S_CHAINS)
        b = _step_rows(b_ref, (0,), SCAN_G - 1 - grp, SUB - 1 - sub, S_CHAINS)
        return jnp.concatenate([f, b, f, b, zpad], axis=0).T

    def load_tiles(grp, sub, slot):
        _store_tile(tiles[slot], 0, stacked_t(g1f_ref, g1b_ref, grp, sub))
        _store_tile(tiles[slot], 1, stacked_t(g2f_ref, g2b_ref, grp, sub))
        t3 = stacked_t(g3f_ref, g3b_ref, grp, sub)
        _store_tile(tiles[slot], 2, t3)
        vals[slot][...] = jnp.where(lane < nc, t3[HEAD_DIM:HEAD_DIM + S_VS], t3[HEAD_DIM + S_VS:])

    def flush(grp, sub, o_scr):
        o = o_scr[...]
        full = jnp.concatenate([o, pltpu.roll(o, LANES - nc, 1), jnp.zeros((LANES - HEAD_DIM, LANES), F32)], axis=0)
        ot = full.T
        of_ref.at[0, grp][pl.ds(sub, S_CHAINS, stride=SUB), :] = ot[0:S_CHAINS]
        ob_ref.at[0, SCAN_G - 1 - grp][pl.ds(SUB - 1 - sub, S_CHAINS, stride=SUB), :] = ot[S_CHAINS:nc]

    load_tiles(0, 0, 0)
    load_tiles(0, 1, 1)
    sas0 = _wkv_first_sa(s_scr, tiles[0].at[1], S_VS)

    def group(grp, sas):
        for j in range(SUB):
            cur, nxt = tiles[j % SLOTS], tiles[(j + 1) % SLOTS]
            ahead = j + 2
            load_tiles(grp if ahead < SUB else jnp.minimum(grp + 1, SCAN_G - 1), ahead % SUB, ahead % SLOTS)
            if j > 0:
                flush(grp, j - 1, outs[(j - 1) % 2])
            sas = _wkv_step(s_scr, cur.at[0], cur.at[1], cur.at[2], nxt.at[1], [vals[j % SLOTS][...]],
                            outs[j % 2], sas)
        flush(grp, SUB - 1, outs[(SUB - 1) % 2])
        return sas

    lax.fori_loop(0, SCAN_G, group, sas0)


def wkv_scan_sample(g1, g2, g3, s0):
    n_grp, rows = g3.shape[0], g3.shape[1]
    g3 = g3.reshape(1, n_grp, rows, LANES)
    nblk = n_grp // SCAN_G
    blk = (1, SCAN_G, rows, LANES)
    fwd = lambda d: pl.BlockSpec(blk, lambda s: (d, s, 0, 0))
    bwd = lambda d: pl.BlockSpec(blk, lambda s: (d, nblk - 1 - s, 0, 0))
    return pl.pallas_call(
        _scan_sample_kernel, name="wkv_scan_sample",
        grid=(nblk,),
        in_specs=[fwd(0), bwd(1), fwd(0), bwd(1), fwd(0), bwd(0),
                  pl.BlockSpec((HEAD_DIM, S_VS, LANES), lambda s: (0, 0, 0))],
        out_specs=[fwd(0), bwd(0)],
        out_shape=[jax.ShapeDtypeStruct((1, n_grp, rows, LANES), F32)] * 2,
        scratch_shapes=([pltpu.VMEM((HEAD_DIM, S_VS, LANES), F32)]
                        + [pltpu.VMEM((3, 2 * N_KC, K_CHUNK, LANES), F32)] * SLOTS
                        + [pltpu.VMEM((S_VS, LANES), F32)] * SLOTS
                        + [pltpu.VMEM((S_VS, LANES), F32)] * 2),
        compiler_params=_cparams(("arbitrary",)),
    )(g1, g1, g2, g2, g3, g3, s0)


def _sample_state_lanes(s0):
    nc = 2 * S_CHAINS
    st = jnp.transpose(s0, (4, 3, 1, 0, 2)).reshape(HEAD_DIM, HEAD_DIM, nc)
    st = jnp.concatenate([st[:, :S_VS], st[:, S_VS:]], axis=-1)
    return jnp.pad(st, ((0, 0), (0, 0), (0, LANES - 2 * nc)))


def _rwkv_post_kernel(of_ref, ob_ref, g_ref, bonus_ref, gw_ref, gb_ref, ones_ref, y_ref):
    ones = ones_ref[...]
    lane = lax.broadcasted_iota(jnp.int32, (RW_TILE, LANES), 1)
    low = lane < HEAD_DIM

    def head(h):
        rows = slice(h * SUB, (h + 1) * SUB)
        return (of_ref[0, :, rows, :] + ob_ref[0, :, rows, :]).reshape(RW_TILE, LANES)

    cols = [jnp.where(low, head(2 * c), pltpu.roll(head(2 * c + 1), HEAD_DIM, 1))
            for c in range(RWKV_DIM // LANES)]
    o = jnp.concatenate(cols, axis=-1)
    mu = _head_sum(o, ones) / HEAD_DIM
    oc = o - mu
    var = _head_sum(oc * oc, ones) / HEAD_DIM
    on = (oc * lax.rsqrt(var + GN_EPS)) * gw_ref[...] + gb_ref[...]
    y_ref[...] = (on + bonus_ref[...]) * g_ref[...]


def rwkv_post(o_f, o_b, n, g, bonus, gn_w, gn_b):
    rows, d = g.shape
    tps = n // RW_TILE
    tile = pl.BlockSpec((RW_TILE, d), lambda i: (i, 0))
    vec = pl.BlockSpec((1, d), lambda i: (0, 0))
    pk = pl.BlockSpec((1, RW_TILE // SUB, PACK_R, LANES), lambda i: (0, i % tps, i // tps, 0))
    return pl.pallas_call(
        _rwkv_post_kernel, name="rwkv_post",
        grid=(rows // RW_TILE,),
        in_specs=[pk, pk, tile, tile, vec, vec, pl.BlockSpec((d, d), lambda i: (0, 0))],
        out_specs=tile,
        out_shape=jax.ShapeDtypeStruct((rows, d), F32),
        compiler_params=_cparams(("parallel",)),
    )(o_f, o_b, g, bonus, gn_w.reshape(1, d), gn_b.reshape(1, d), _head_ones())


MOE_R = 512
MOE_M = 128


def _split3(x):
    a = x.astype(BF16)
    r1 = x - a.astype(F32)
    b = r1.astype(BF16)
    c = (r1 - b.astype(F32)).astype(BF16)
    return a, b, c


def _router_kernel(x_ref, g_ref, sh_ref, sc_ref, w_ref, b_ref, tri_ref, h_ref, comb_ref, rank_ref, rank_t_ref, cnt_ref):
    h = _modulated(x_ref[...], g_ref[...], sh_ref[0], sc_ref[0])
    h_ref[...] = h.astype(BF16)
    h1, h2, h3 = _split3(h)
    w1, w2, w3 = _split3(w_ref[...])
    logits = (_dot(h1, w1) + (_dot(h1, w2) + _dot(h2, w1))
              + (_dot(h1, w3) + _dot(h2, w2) + _dot(h3, w1))) + b_ref[...]
    col = lax.broadcasted_iota(jnp.int32, logits.shape, 1)
    logits = jnp.where(col < N_EXPERTS, logits, -jnp.inf)
    m1 = jnp.max(logits, axis=-1, keepdims=True)
    i1 = jnp.min(jnp.where(logits == m1, col, LANES), axis=-1, keepdims=True)
    rest = jnp.where(col == i1, -jnp.inf, logits)
    m2 = jnp.max(rest, axis=-1, keepdims=True)
    i2 = jnp.min(jnp.where(rest == m2, col, LANES), axis=-1, keepdims=True)
    e2 = jnp.exp(m2 - m1)
    den = 1.0 + e2
    comb_ref[...] = jnp.where(col == i1, 1.0 / den, 0.0) + jnp.where(col == i2, e2 / den, 0.0)
    chosen = (col == i1) | (col == i2)
    upto = _dot(tri_ref[...], jnp.where(chosen, 1.0, 0.0).astype(BF16))
    rank = jnp.where(chosen, upto - 1.0, -1.0)
    rank_ref[...] = rank
    rank_t_ref[0] = rank.T[0:N_EXPERTS, :]
    cnt_ref[0] = jnp.broadcast_to(upto[MOE_R - 1:MOE_R, :], (8, LANES))


def moe_router(x, g, sh, sc, router_w, router_b, rows_per_set):
    rows = x.shape[0]
    tm = MOE_R
    nblk = rows // tm
    si = _set_index(tm, rows_per_set)
    vec = pl.BlockSpec((1, 1, D_MODEL), lambda i: (si(i), 0, 0))
    w = jnp.pad(router_w, ((0, 0), (0, LANES - N_EXPERTS)))
    b = jnp.pad(router_b, (0, LANES - N_EXPERTS)).reshape(1, LANES)
    tri = jnp.asarray(np.tril(np.ones((tm, tm), np.float32))).astype(BF16)
    return pl.pallas_call(
        _router_kernel, name="moe_router",
        grid=(nblk,),
        in_specs=[pl.BlockSpec((tm, D_MODEL), lambda i: (i, 0)),
                  pl.BlockSpec((1, D_MODEL), lambda i: (0, 0)),
                  vec, vec,
                  pl.BlockSpec((D_MODEL, LANES), lambda i: (0, 0)),
                  pl.BlockSpec((1, LANES), lambda i: (0, 0)),
                  pl.BlockSpec((tm, tm), lambda i: (0, 0))],
        out_specs=[pl.BlockSpec((tm, D_MODEL), lambda i: (i, 0)),
                   pl.BlockSpec((tm, LANES), lambda i: (i, 0)),
                   pl.BlockSpec((tm, LANES), lambda i: (i, 0)),
                   pl.BlockSpec((1, N_EXPERTS, tm), lambda i: (i, 0, 0)),
                   pl.BlockSpec((1, 8, LANES), lambda i: (i, 0, 0))],
        out_shape=[jax.ShapeDtypeStruct((rows, D_MODEL), BF16),
                   jax.ShapeDtypeStruct((rows, LANES), F32),
                   jax.ShapeDtypeStruct((rows, LANES), F32),
                   jax.ShapeDtypeStruct((nblk, N_EXPERTS, tm), F32),
                   jax.ShapeDtypeStruct((nblk, 8, LANES), F32)],
        compiler_params=_cparams(("parallel",)),
    )(x, g.reshape(1, D_MODEL), sh, sc, w, b, tri)


MOE_TM = 1024


def _moe_kernel(cnt_ref, x_ref, h_ref, comb_ref, rank_ref, rank_t_ref, gate_ref, gfin_ref, wg_ref, wu_ref, wd_ref,
                o_ref, acc_scr):
    i = pl.program_id(0)
    e = pl.program_id(1)

    @pl.when(e == 0)
    def _():
        acc_scr[...] = jnp.zeros_like(acc_scr)

    col = lax.broadcasted_iota(jnp.int32, (MOE_R, LANES), 1)
    slot_rows = lax.broadcasted_iota(jnp.int32, (MOE_M, MOE_R), 0).astype(F32)
    slot_cols = lax.broadcasted_iota(jnp.int32, (MOE_R, MOE_M), 1).astype(F32)
    for s in range(MOE_TM // MOE_R):
        blk = slice(s * MOE_R, (s + 1) * MOE_R)
        count = cnt_ref[(i * (MOE_TM // MOE_R) + s) * N_EXPERTS + e]
        for m in range(MOE_R // MOE_M):
            @pl.when(count > m * MOE_M)
            def _():
                take = (rank_t_ref[s, pl.ds(e, 1), :] == slot_rows + float(m * MOE_M))
                hc = _dot(jnp.where(take, 1.0, 0.0).astype(BF16), h_ref[blk, :]).astype(BF16)
                gt = _dot(hc, wg_ref[0])
                act = (gt * _sigmoid(gt)) * _dot(hc, wu_ref[0])
                y = _dot(act.astype(BF16), wd_ref[0]).astype(BF16)
                mine = col == e
                rank_e = jnp.sum(jnp.where(mine, rank_ref[blk, :], 0.0), axis=-1, keepdims=True)
                ce = jnp.sum(jnp.where(mine, comb_ref[blk, :], 0.0), axis=-1, keepdims=True)
                put = rank_e == slot_cols + float(m * MOE_M)
                acc_scr[blk, :] += ce * _dot(jnp.where(put, 1.0, 0.0).astype(BF16), y)

    @pl.when(e == pl.num_programs(1) - 1)
    def _():
        y = x_ref[...] + gate_ref[0] * acc_scr[...]
        ms = jnp.mean(y * y, axis=-1, keepdims=True)
        o_ref[...] = y * lax.rsqrt(ms + RMS_EPS) * gfin_ref[...]


def moe_residual_norm(x, h, comb, rank, rank_t, counts, gate, g_final, wg, wu, wd, rows_per_set):
    rows = x.shape[0]
    tm = MOE_TM
    sub = tm // MOE_R
    si = _set_index(tm, rows_per_set)
    cnt = counts[:, 0, :N_EXPERTS].astype(jnp.int32).reshape(-1)
    grid_spec = pltpu.PrefetchScalarGridSpec(
        num_scalar_prefetch=1,
        grid=(rows // tm, N_EXPERTS),
        in_specs=[pl.BlockSpec((tm, D_MODEL), lambda i, e, c: (i, 0)),
                  pl.BlockSpec((tm, D_MODEL), lambda i, e, c: (i, 0)),
                  pl.BlockSpec((tm, LANES), lambda i, e, c: (i, 0)),
                  pl.BlockSpec((tm, LANES), lambda i, e, c: (i, 0)),
                  pl.BlockSpec((sub, N_EXPERTS, MOE_R), lambda i, e, c: (i, 0, 0)),
                  pl.BlockSpec((1, 1, D_MODEL), lambda i, e, c: (si(i), 0, 0)),
                  pl.BlockSpec((1, D_MODEL), lambda i, e, c: (0, 0)),
                  pl.BlockSpec((1, D_MODEL, D_FF_EXPERT), lambda i, e, c: (e, 0, 0)),
                  pl.BlockSpec((1, D_MODEL, D_FF_EXPERT), lambda i, e, c: (e, 0, 0)),
                  pl.BlockSpec((1, D_FF_EXPERT, D_MODEL), lambda i, e, c: (e, 0, 0))],
        out_specs=pl.BlockSpec((tm, D_MODEL), lambda i, e, c: (i, 0)),
        scratch_shapes=[pltpu.VMEM((tm, D_MODEL), F32)])
    return pl.pallas_call(
        _moe_kernel, name="moe_experts",
        grid_spec=grid_spec,
        out_shape=jax.ShapeDtypeStruct((rows, D_MODEL), F32),
        compiler_params=_cparams(("parallel", "arbitrary")),
    )(cnt, x, h, comb, rank, rank_t, gate, g_final.reshape(1, D_MODEL), wg, wu, wd)


def kernel(x_prompt, x_sample, cache_na_k, cache_na_v, state_wkv, c, c_ctx, mod_w, mod_b, norm_mix, norm_ffn, norm_final, na_w_in, fourier_w, na_rel_bias, na_w_out, ffn_w_gate, ffn_w_up, ffn_w_down, rw_w_in, pool_w, pool_scale, shift_mu, decay_w0, decay_up, iclr_a0, iclr_up, gate_up, k_k, k_a, r_k, gn_w, gn_b, rw_w_out, router_w, router_b, moe_w_gate, moe_w_up, moe_w_down):
    cond = jnp.concatenate([c_ctx[None, :], c, jnp.zeros((8 - N_SETS, D_MODEL), F32)], axis=0)
    mods = adaln_all(cond, mod_w, mod_b)[:, :N_SETS].reshape(DEPTH, N_SETS, 6, 1, D_MODEL)
    bf = lambda w: w.astype(BF16)

    xp = x_prompt.reshape(P_ROWS, D_MODEL)
    xs = x_sample.reshape(S_ROWS, D_MODEL)
    streams = {"p": (SEQ, P_ROWS, slice(0, 1)), "s": (DEC_SEQ, DEC_SEQ, slice(1, N_SETS))}
    x = {"p": xp, "s": xs}

    splits = ((0, FOURIER_CH), (FOURIER_CH, FOURIER_CH + NA_DIM),
              (FOURIER_CH + NA_DIM, FOURIER_CH + 2 * NA_DIM), (FOURIER_CH + 2 * NA_DIM, FOURIER_CH + 3 * NA_DIM))
    w_in, w_out = bf(na_w_in[0]), bf(na_w_out[0])
    f_bd = bf(_block_diag(fourier_w[0]))
    ffn_w = (bf(ffn_w_gate[0]), bf(ffn_w_up[0]), bf(ffn_w_down[0]))
    ck = cache_na_k[:, 0].reshape(DEC_BATCH * PAST_LEN, NA_DIM)
    cv = cache_na_v[:, 0].reshape(DEC_BATCH * PAST_LEN, NA_DIM)
    for name, (n, rps, sets) in streams.items():
        sh1, sc1, g1, sh2, sc2, g2 = [mods[0, sets, m] for m in range(6)]
        f, q, k, v = modulated_matmul(x[name], norm_mix[0], sh1, sc1, w_in, splits, rps)
        if name == "p":
            attn = context_attention(q, k, v)
            new_k = k.reshape(BATCH, 1, SEQ, NA_HEADS, HEAD_DIM)
            new_v = v.reshape(BATCH, 1, SEQ, NA_HEADS, HEAD_DIM)
        else:
            attn = neighbourhood_attention(q, k, v, ck, cv, _na_bias_table(na_rel_bias[0]))
        y = proj_residual(fourier_mix(f, n, f_bd), attn, x[name], g1, w_out, rps)
        x[name] = ffn_residual(y, norm_ffn[0], sh2, sc2, g2, *ffn_w, rps)

    w_in, w_out = bf(rw_w_in[0]), bf(rw_w_out[0])
    p_bd = bf(_block_diag(pool_w[0]))
    moe_w = (bf(moe_w_gate[0]), bf(moe_w_up[0]), bf(moe_w_down[0]))
    rw = (shift_mu[0], k_k[0], k_a[0], r_k[0], decay_w0[0], iclr_a0[0], decay_up[0], iclr_up[0], gate_up[0])
    out = {}
    for name, (n, rps, sets) in streams.items():
        sh1, sc1, g1, sh2, sc2, g2 = [mods[1, sets, m] for m in range(6)]
        pc, z = modulated_matmul(x[name], norm_mix[1], sh1, sc1, w_in, ((0, POOL_CH), (POOL_CH, POOL_CH + RWKV_IN)), rps)
        t1, t2, t3, gate, bonus = rwkv_prep(z, n, *rw)
        if name == "p":
            o_f, st_f = wkv_scan_prompt(t1, t2, t3, 0)
            o_b, st_b = wkv_scan_prompt(t1, t2, t3, 1)
            st = jnp.transpose(jnp.stack([st_f, st_b]), (0, 1, 4, 3, 2))
            st = jnp.transpose(st.reshape(2, BATCH, RWKV_HEADS, HEAD_DIM, HEAD_DIM), (1, 0, 2, 3, 4))
        else:
            o_f, o_b = wkv_scan_sample(t1, t2, t3, _sample_state_lanes(state_wkv[:, 0]))
        mixed = rwkv_post(o_f, o_b, n, gate, bonus, gn_w[0], gn_b[0])
        y = proj_residual(pool_mix(pc, n, p_bd, pool_scale[0]), mixed, x[name], g1, w_out, rps)
        routed = moe_router(y, norm_ffn[1], sh2, sc2, router_w[0], router_b[0], rps)
        out[name] = moe_residual_norm(y, *routed, g2, norm_final, *moe_w, rps)

    return (out["p"].reshape(BATCH, SEQ, D_MODEL), out["s"].reshape(DEC_BATCH, DEC_SEQ, D_MODEL),
            new_k, new_v, st[:, None])
```

```python
import functools
import math

import numpy as np
import jax
import jax.numpy as jnp
from jax import lax
from jax.experimental import pallas as pl
from jax.experimental.pallas import tpu as pltpu

F32 = jnp.float32
BF16 = jnp.bfloat16

D_MODEL = 1024
BATCH = 32
SEQ = 256
DEPTH = 2
DEC_BATCH = 2
DEC_SEQ = 1024
PAST_LEN = 512
GRID_W = 64
HEAD_DIM = 64
FOURIER_CH = D_MODEL // 4
FOURIER_GROUPS = 4
FOURIER_GW = FOURIER_CH // FOURIER_GROUPS
NA_DIM = D_MODEL - FOURIER_CH
NA_HEADS = NA_DIM // HEAD_DIM
NA_MAX_ROWS = 8
NA_COLS = 16
POOL_WINDOWS = (2, 4, 8, 16)
POOL_CH = D_MODEL // 4
POOL_GW = POOL_CH // len(POOL_WINDOWS)
RWKV_DIM = D_MODEL - POOL_CH
RWKV_HEADS = RWKV_DIM // HEAD_DIM
DECAY_LORA = 64
ICLR_LORA = 64
GATE_LORA = 128
RWKV_IN = 3 * RWKV_DIM + 2 * DECAY_LORA + 2 * ICLR_LORA + GATE_LORA
D_FF = 2816
N_EXPERTS = 8
D_FF_EXPERT = 1408
RMS_EPS = 1e-6
GN_EPS = 64e-5
L2_EPS = 1e-12
DECAY_SCALE = math.exp(-0.5)
NEG_INF = -1e30

P_ROWS = BATCH * SEQ
S_ROWS = DEC_BATCH * DEC_SEQ
N_ROWS = P_ROWS + S_ROWS
N_SETS = 1 + DEC_BATCH
LANES = 128
VMEM_LIMIT = 56 * 1024 * 1024


def _cparams(sem):
    return pltpu.CompilerParams(dimension_semantics=sem, vmem_limit_bytes=VMEM_LIMIT)


def _sigmoid(x):
    return 0.5 * jnp.tanh(0.5 * x) + 0.5


def _dot(a, b):
    return jnp.dot(a, b, preferred_element_type=F32)


def _dot_nt(a, b):
    return lax.dot_general(a, b, (((1,), (1,)), ((), ())), preferred_element_type=F32)


def _set_index(tm, rows_per_set):
    q = rows_per_set // tm
    return lambda i: i // q


def _modulated(x, g, sh, sc):
    ms = jnp.mean(x * x, axis=-1, keepdims=True)
    return (x * lax.rsqrt(ms + RMS_EPS) * g) * (1.0 + sc) + sh


def _adaln_kernel(c_ref, w_ref, b_ref, o_ref):
    c = c_ref[...]
    s = (c * _sigmoid(c)).astype(BF16)
    o_ref[0] = _dot(s, w_ref[0].astype(BF16)) + b_ref[0]


def adaln_all(cond, mod_w, mod_b):
    tn = 1536
    n = 6 * D_MODEL
    return pl.pallas_call(
        _adaln_kernel, name="adaln",
        grid=(DEPTH, n // tn),
        in_specs=[pl.BlockSpec((8, D_MODEL), lambda l, j: (0, 0)),
                  pl.BlockSpec((1, D_MODEL, tn), lambda l, j: (l, 0, j)),
                  pl.BlockSpec((1, 1, tn), lambda l, j: (l, 0, j))],
        out_specs=pl.BlockSpec((1, 8, tn), lambda l, j: (l, 0, j)),
        out_shape=jax.ShapeDtypeStruct((DEPTH, 8, n), F32),
        compiler_params=_cparams(("parallel", "parallel")),
    )(cond, mod_w, mod_b.reshape(DEPTH, 1, n))


def _modmm_kernel(x_ref, g_ref, sh_ref, sc_ref, w_ref, *o_refs, splits):
    h = _modulated(x_ref[...], g_ref[...], sh_ref[0], sc_ref[0]).astype(BF16)
    for o_ref, (a, b) in zip(o_refs, splits):
        o_ref[...] = _dot(h, w_ref[:, a:b]).astype(o_ref.dtype)


def modulated_matmul(x, g, sh, sc, w, splits, rows_per_set, tm=512):
    rows = x.shape[0]
    n_out = w.shape[1]
    si = _set_index(tm, rows_per_set)
    vec = pl.BlockSpec((1, 1, D_MODEL), lambda i: (si(i), 0, 0))
    return pl.pallas_call(
        functools.partial(_modmm_kernel, splits=splits), name="modulated_matmul",
        grid=(rows // tm,),
        in_specs=[pl.BlockSpec((tm, D_MODEL), lambda i: (i, 0)),
                  pl.BlockSpec((1, D_MODEL), lambda i: (0, 0)),
                  vec, vec,
                  pl.BlockSpec((D_MODEL, n_out), lambda i: (0, 0))],
        out_specs=[pl.BlockSpec((tm, b - a), lambda i: (i, 0)) for a, b in splits],
        out_shape=[jax.ShapeDtypeStruct((rows, b - a), F32) for a, b in splits],
        compiler_params=_cparams(("parallel",)),
    )(x, g.reshape(1, D_MODEL), sh, sc, w)


def _dft_mats(n):
    t = np.arange(n)
    ang = 2.0 * np.pi * ((t[:, None] * t[None, :]) % n) / n
    cn, sn = np.cos(ang) / np.sqrt(n), np.sin(ang) / np.sqrt(n)
    c = np.arange(FOURIER_GW)
    angc = 2.0 * np.pi * ((c[:, None] * c[None, :]) % FOURIER_GW) / FOURIER_GW
    eye = np.eye(FOURIER_GROUPS)
    cc = np.kron(eye, np.cos(angc) / np.sqrt(FOURIER_GW))
    sc = np.kron(eye, np.sin(angc) / np.sqrt(FOURIER_GW))
    as_bf = lambda a: jnp.asarray(a, dtype=F32).astype(BF16)
    return as_bf(cn), as_bf(sn), as_bf(cc), as_bf(sc)


def _fourier_kernel(f_ref, cn_ref, sn_ref, cc_ref, sc_ref, w_ref, o_ref):
    x = f_ref[...].astype(BF16)
    a = _dot(x, cc_ref[...]).astype(BF16)
    b = _dot(x, sc_ref[...]).astype(BF16)
    re = _dot(cn_ref[...], a) - _dot(sn_ref[...], b)
    o_ref[...] = _dot(re.astype(BF16), w_ref[...])


def _block_diag(w):
    g, c, _ = w.shape
    eye = jnp.eye(g, dtype=w.dtype)
    return (eye[:, None, :, None] * w[:, :, None, :]).reshape(g * c, g * c)


def fourier_mix(f, n, w_bd):
    rows = f.shape[0]
    cn, sn, cc, sc = _dft_mats(n)
    full = lambda shape: pl.BlockSpec(shape, lambda b: (0, 0))
    return pl.pallas_call(
        _fourier_kernel, name="fourier",
        grid=(rows // n,),
        in_specs=[pl.BlockSpec((n, FOURIER_CH), lambda b: (b, 0)),
                  full((n, n)), full((n, n)),
                  full((FOURIER_CH, FOURIER_CH)), full((FOURIER_CH, FOURIER_CH)),
                  full((FOURIER_CH, FOURIER_CH))],
        out_specs=pl.BlockSpec((n, FOURIER_CH), lambda b: (b, 0)),
        out_shape=jax.ShapeDtypeStruct((rows, FOURIER_CH), F32),
        compiler_params=_cparams(("parallel",)),
    )(f, cn, sn, cc, sc, w_bd)


def _ctx_attn_kernel(q_ref, k_ref, v_ref, o_ref):
    scale = HEAD_DIM ** -0.5
    ones = jnp.ones((SEQ, LANES), BF16)
    def scores(h):
        sl = slice(h * HEAD_DIM, (h + 1) * HEAD_DIM)
        return _dot_nt(q_ref[:, sl].astype(BF16), k_ref[:, sl].astype(BF16)) * scale

    outs = []
    ahead = 1
    pending = [scores(h) for h in range(ahead)]
    for h in range(NA_HEADS):
        if h + ahead < NA_HEADS:
            pending.append(scores(h + ahead))
        s = pending.pop(0)
        v = v_ref[:, h * HEAD_DIM:(h + 1) * HEAD_DIM].astype(BF16)
        p = jnp.exp(s - jnp.max(s, axis=-1, keepdims=True)).astype(BF16)
        den = _dot(p, ones)
        outs.append(_dot(p, v) / den[:, :HEAD_DIM])
    o_ref[...] = jnp.concatenate(outs, axis=-1)


def context_attention(q, k, v):
    blk = pl.BlockSpec((SEQ, NA_DIM), lambda b: (b, 0))
    return pl.pallas_call(
        _ctx_attn_kernel, name="ctx_attn",
        grid=(BATCH,),
        in_specs=[blk, blk, blk],
        out_specs=blk,
        out_shape=jax.ShapeDtypeStruct((P_ROWS, NA_DIM), F32),
        compiler_params=_cparams(("parallel",)),
    )(q, k, v)


NA_ROWS = DEC_SEQ // GRID_W
NA_WIN = NA_MAX_ROWS * GRID_W


def _na_bias_table(rel_bias):
    cols = np.arange(GRID_W)
    c0 = np.clip(cols - NA_COLS // 2, 0, GRID_W - NA_COLS)
    col_ok = (cols[None, :] >= c0[:, None]) & (cols[None, :] < c0[:, None] + NA_COLS)
    dc = np.clip(cols[None, :] - cols[:, None] + NA_COLS - 1, 0, 2 * NA_COLS - 2)
    onehot = (dc[None] == np.arange(2 * NA_COLS - 1)[:, None, None]).astype(np.float32)
    toe = jnp.einsum("hrj,jqk->hrqk", rel_bias, jnp.asarray(onehot), precision=lax.Precision.HIGHEST)
    toe = jnp.where(col_ok[None, None], toe, NEG_INF)
    tabs = [jnp.transpose(toe[:, NA_MAX_ROWS - 1 - o: 2 * NA_MAX_ROWS - 1 - o], (0, 2, 1, 3))
            for o in range(NA_MAX_ROWS)]
    return jnp.stack(tabs).reshape(NA_MAX_ROWS, NA_HEADS, GRID_W, NA_WIN)


def _na_row_start(i):
    return jnp.clip(i - NA_MAX_ROWS // 2, 0, NA_ROWS - NA_MAX_ROWS)


def _na_kernel(q_ref, k_ref, v_ref, ck_ref, cv_ref, bias_ref, o_ref):
    scale = HEAD_DIM ** -0.5
    i = pl.program_id(1)
    start = pl.multiple_of(_na_row_start(i) * GRID_W, GRID_W)
    kw = k_ref[pl.ds(start, NA_WIN), :]
    vw = v_ref[pl.ds(start, NA_WIN), :]
    ones = jnp.ones((NA_WIN, LANES), BF16)
    ones_ctx = jnp.ones((PAST_LEN, LANES), BF16)
    def scores(h):
        sl = slice(h * HEAD_DIM, (h + 1) * HEAD_DIM)
        q = q_ref[:, sl].astype(BF16)
        return (_dot_nt(q, kw[:, sl].astype(BF16)) * scale + bias_ref[0, h],
                _dot_nt(q, ck_ref[:, sl].astype(BF16)) * scale)

    outs = []
    ahead = 2
    pending = [scores(h) for h in range(ahead)]
    for h in range(NA_HEADS):
        if h + ahead < NA_HEADS:
            pending.append(scores(h + ahead))
        s_loc, s_ctx = pending.pop(0)
        sl = slice(h * HEAD_DIM, (h + 1) * HEAD_DIM)
        m = jnp.maximum(jnp.max(s_loc, axis=-1, keepdims=True), jnp.max(s_ctx, axis=-1, keepdims=True))
        p_loc = jnp.exp(s_loc - m).astype(BF16)
        p_ctx = jnp.exp(s_ctx - m).astype(BF16)
        den = _dot(p_loc, ones) + _dot(p_ctx, ones_ctx)
        num = _dot(p_loc, vw[:, sl].astype(BF16)) + _dot(p_ctx, cv_ref[:, sl].astype(BF16))
        outs.append(num / den[:, :HEAD_DIM])
    o_ref[...] = jnp.concatenate(outs, axis=-1)


def neighbourhood_attention(q, k, v, ck, cv, bias_tab):
    seq = pl.BlockSpec((DEC_SEQ, NA_DIM), lambda b, i: (b, 0))
    ctx = pl.BlockSpec((PAST_LEN, NA_DIM), lambda b, i: (b, 0))
    row = pl.BlockSpec((GRID_W, NA_DIM), lambda b, i: (b * NA_ROWS + i, 0))
    return pl.pallas_call(
        _na_kernel, name="na_attn",
        grid=(DEC_BATCH, NA_ROWS),
        in_specs=[row, seq, seq, ctx, ctx,
                  pl.BlockSpec((1, NA_HEADS, GRID_W, NA_WIN), lambda b, i: (i - _na_row_start(i), 0, 0, 0))],
        out_specs=row,
        out_shape=jax.ShapeDtypeStruct((S_ROWS, NA_DIM), F32),
        compiler_params=_cparams(("parallel", "arbitrary")),
    )(q, k, v, ck, cv, bias_tab)


def _proj_res_kernel(a_ref, b_ref, x_ref, gate_ref, w_ref, o_ref, *, na):
    y = _dot(a_ref[...].astype(BF16), w_ref[:na, :]) + _dot(b_ref[...].astype(BF16), w_ref[na:, :])
    o_ref[...] = x_ref[...] + gate_ref[0] * y


def proj_residual(a, b, x, gate, w, rows_per_set, tm=512):
    rows = x.shape[0]
    na, nb = a.shape[1], b.shape[1]
    si = _set_index(tm, rows_per_set)
    return pl.pallas_call(
        functools.partial(_proj_res_kernel, na=na), name="proj_residual",
        grid=(rows // tm,),
        in_specs=[pl.BlockSpec((tm, na), lambda i: (i, 0)),
                  pl.BlockSpec((tm, nb), lambda i: (i, 0)),
                  pl.BlockSpec((tm, D_MODEL), lambda i: (i, 0)),
                  pl.BlockSpec((1, 1, D_MODEL), lambda i: (si(i), 0, 0)),
                  pl.BlockSpec((na + nb, D_MODEL), lambda i: (0, 0))],
        out_specs=pl.BlockSpec((tm, D_MODEL), lambda i: (i, 0)),
        out_shape=jax.ShapeDtypeStruct((rows, D_MODEL), F32),
        compiler_params=_cparams(("parallel",)),
    )(a, b, x, gate, w)


def _ffn_kernel(x_ref, g_ref, sh_ref, sc_ref, gate_ref, wg_ref, wu_ref, wd_ref, o_ref, h_scr, acc_scr):
    j = pl.program_id(1)

    @pl.when(j == 0)
    def _():
        h_scr[...] = _modulated(x_ref[...], g_ref[...], sh_ref[0], sc_ref[0]).astype(BF16)
        acc_scr[...] = jnp.zeros_like(acc_scr)

    h = h_scr[...]
    gt = _dot(h, wg_ref[...])
    act = (gt * _sigmoid(gt)) * _dot(h, wu_ref[...])
    acc_scr[...] += _dot(act.astype(BF16), wd_ref[...])

    @pl.when(j == pl.num_programs(1) - 1)
    def _():
        o_ref[...] = x_ref[...] + gate_ref[0] * acc_scr[...]


def ffn_residual(x, g, sh, sc, gate, wg, wu, wd, rows_per_set, tm=512, tf=D_FF // 2):
    rows = x.shape[0]
    si = _set_index(tm, rows_per_set)
    vec = pl.BlockSpec((1, 1, D_MODEL), lambda i, j: (si(i), 0, 0))
    return pl.pallas_call(
        _ffn_kernel, name="ffn",
        grid=(rows // tm, D_FF // tf),
        in_specs=[pl.BlockSpec((tm, D_MODEL), lambda i, j: (i, 0)),
                  pl.BlockSpec((1, D_MODEL), lambda i, j: (0, 0)),
                  vec, vec, vec,
                  pl.BlockSpec((D_MODEL, tf), lambda i, j: (0, j)),
                  pl.BlockSpec((D_MODEL, tf), lambda i, j: (0, j)),
                  pl.BlockSpec((tf, D_MODEL), lambda i, j: (j, 0))],
        out_specs=pl.BlockSpec((tm, D_MODEL), lambda i, j: (i, 0)),
        out_shape=jax.ShapeDtypeStruct((rows, D_MODEL), F32),
        scratch_shapes=[pltpu.VMEM((tm, D_MODEL), BF16), pltpu.VMEM((tm, D_MODEL), F32)],
        compiler_params=_cparams(("parallel", "arbitrary")),
    )(x, g.reshape(1, D_MODEL), sh, sc, gate, wg, wu, wd)


def _pool_consts(n):
    t = np.arange(n)
    mats, cnts = [], []
    for win in POOL_WINDOWS:
        lo = np.clip(t - win // 2, 0, n)
        hi = np.clip(t + win - win // 2, 0, n)
        mats.append(((t[None, :] >= lo[:, None]) & (t[None, :] < hi[:, None])).astype(np.float32))
        cnts.append(np.repeat((hi - lo).astype(np.float32)[:, None], POOL_GW, axis=1))
    return jnp.asarray(np.stack(mats)).astype(BF16), jnp.asarray(np.concatenate(cnts, axis=1))


def _pool_kernel(x_ref, pm_ref, cnt_ref, w_ref, scale_ref, o_ref):
    x = x_ref[...]
    hi = x.astype(BF16)
    lo = (x - hi.astype(F32)).astype(BF16)
    sums = []
    for g in range(len(POOL_WINDOWS)):
        sl = slice(g * POOL_GW, (g + 1) * POOL_GW)
        sums.append(_dot(pm_ref[g], hi[:, sl]) + _dot(pm_ref[g], lo[:, sl]))
    y = jnp.concatenate(sums, axis=-1) / cnt_ref[...] - x
    o_ref[...] = _dot(y.astype(BF16), w_ref[...]) * scale_ref[...]


def pool_mix(x, n, w_bd, scale):
    rows = x.shape[0]
    pm, cnt = _pool_consts(n)
    return pl.pallas_call(
        _pool_kernel, name="pool",
        grid=(rows // n,),
        in_specs=[pl.BlockSpec((n, POOL_CH), lambda b: (b, 0)),
                  pl.BlockSpec((len(POOL_WINDOWS), n, n), lambda b: (0, 0, 0)),
                  pl.BlockSpec((n, POOL_CH), lambda b: (0, 0)),
                  pl.BlockSpec((POOL_CH, POOL_CH), lambda b: (0, 0)),
                  pl.BlockSpec((1, POOL_CH), lambda b: (0, 0))],
        out_specs=pl.BlockSpec((n, POOL_CH), lambda b: (b, 0)),
        out_shape=jax.ShapeDtypeStruct((rows, POOL_CH), F32),
        compiler_params=_cparams(("parallel",)),
    )(x, pm, cnt, w_bd, scale.reshape(1, POOL_CH))


RW_TILE = 256
HALO = 8
SUB = 8
PACK_R = RWKV_HEADS * SUB


def _head_ones():
    h = np.arange(RWKV_DIM) // HEAD_DIM
    return jnp.asarray((h[:, None] == h[None, :]).astype(np.float32)).astype(BF16)


def _head_sum(x, ones):
    hi = x.astype(BF16)
    lo = (x - hi.astype(F32)).astype(BF16)
    return _dot(hi, ones) + _dot(lo, ones)


def _pack_heads(a, b, o_ref, lead):
    n = a.shape[0]
    lane = lax.broadcasted_iota(jnp.int32, (n, LANES), 1)
    low = lane < HEAD_DIM
    for c in range(RWKV_DIM // LANES):
        ac = a[:, c * LANES:(c + 1) * LANES]
        bc = b[:, c * LANES:(c + 1) * LANES]
        even = jnp.where(low, ac, pltpu.roll(bc, HEAD_DIM, 1))
        odd = jnp.where(low, pltpu.roll(ac, HEAD_DIM, 1), bc)
        for h, val in ((2 * c, even), (2 * c + 1, odd)):
            o_ref[lead + (slice(None), slice(h * SUB, (h + 1) * SUB), slice(None))] = val.reshape(n // SUB, SUB, LANES)


def _rwkv_prep_kernel(z_ref, zp_ref, zn_ref, mu_ref, kk_w_ref, ka_ref, rk_ref, w0_ref, a0_ref,
                      dup_ref, iup_ref, gup_ref, ones_ref,
                      g1_ref, g2_ref, g3_ref, g_ref, bonus_ref, *, tiles_per_seq):
    i = pl.program_id(0)
    pos = i % tiles_per_seq
    z = z_ref[...]
    row = lax.broadcasted_iota(jnp.int32, (RW_TILE, 1), 0)
    prev_edge = jnp.where(pos == 0, 0.0, zp_ref[HALO - 1:HALO, :])
    next_edge = jnp.where(pos == tiles_per_seq - 1, 0.0, zn_ref[0:1, :])
    prev = jnp.where(row == 0, prev_edge, pltpu.roll(z, 1, 0))
    nxt = jnp.where(row == RW_TILE - 1, next_edge, pltpu.roll(z, RW_TILE - 1, 0))
    zr = z + mu_ref[0:1, :] * (prev - z) + mu_ref[1:2, :] * (nxt - z)

    d = RWKV_DIM
    r, k, v = zr[:, :d], zr[:, d:2 * d], zr[:, 2 * d:3 * d]
    lora = 3 * d
    ones = ones_ref[...]
    kk = k * kk_w_ref[...]
    kk = kk * lax.rsqrt(_head_sum(kk * kk, ones) + L2_EPS)
    _pack_heads(r, v, g3_ref, ())
    for dr in range(2):
        wl = zr[:, lora + dr * DECAY_LORA: lora + (dr + 1) * DECAY_LORA]
        al = zr[:, lora + 2 * DECAY_LORA + dr * ICLR_LORA: lora + 2 * DECAY_LORA + (dr + 1) * ICLR_LORA]
        lw = w0_ref[dr:dr + 1, :] + _dot(jnp.tanh(wl).astype(BF16), dup_ref[dr])
        w = jnp.exp(-DECAY_SCALE * _sigmoid(lw))
        a = _sigmoid(a0_ref[dr:dr + 1, :] + _dot(al.astype(BF16), iup_ref[dr]))
        _pack_heads(w, kk * a, g1_ref, (dr,))
        _pack_heads(k * (1.0 + (a - 1.0) * ka_ref[...]), kk, g2_ref, (dr,))
    gl = zr[:, lora + 2 * DECAY_LORA + 2 * ICLR_LORA:]
    g_ref[...] = _dot(_sigmoid(gl).astype(BF16), gup_ref[...])
    bonus_ref[...] = _head_sum(r * k * rk_ref[...], ones) * v


def rwkv_prep(z, n, mu, k_k, k_a, r_k, w0, a0, dup, iup, gup):
    rows = z.shape[0]
    nb = rows // n
    tps = n // RW_TILE
    hb = RW_TILE // HALO
    last = rows // HALO - 1
    d = RWKV_DIM
    full2 = lambda shape: pl.BlockSpec(shape, lambda i: (0, 0))
    full3 = lambda shape: pl.BlockSpec(shape, lambda i: (0, 0, 0))
    tile = pl.BlockSpec((RW_TILE, d), lambda i: (i, 0))
    pk2 = pl.BlockSpec((2, RW_TILE // SUB, PACK_R, LANES), lambda i: (0, i % tps, i // tps, 0))
    pk1 = pl.BlockSpec((RW_TILE // SUB, PACK_R, LANES), lambda i: (i % tps, i // tps, 0))
    return pl.pallas_call(
        functools.partial(_rwkv_prep_kernel, tiles_per_seq=tps), name="rwkv_prep",
        grid=(rows // RW_TILE,),
        in_specs=[pl.BlockSpec((RW_TILE, RWKV_IN), lambda i: (i, 0)),
                  pl.BlockSpec((HALO, RWKV_IN), lambda i: (jnp.maximum(i * hb - 1, 0), 0)),
                  pl.BlockSpec((HALO, RWKV_IN), lambda i: (jnp.minimum((i + 1) * hb, last), 0)),
                  full2((2, RWKV_IN)), full2((1, d)), full2((1, d)), full2((1, d)),
                  full2((2, d)), full2((2, d)),
                  full3((2, DECAY_LORA, d)), full3((2, ICLR_LORA, d)), full2((GATE_LORA, d)),
                  full2((d, d))],
        out_specs=[pk2, pk2, pk1, tile, tile],
        out_shape=[jax.ShapeDtypeStruct((2, n // SUB, nb * PACK_R, LANES), F32),
                   jax.ShapeDtypeStruct((2, n // SUB, nb * PACK_R, LANES), F32),
                   jax.ShapeDtypeStruct((n // SUB, nb * PACK_R, LANES), F32),
                   jax.ShapeDtypeStruct((rows, d), F32),
                   jax.ShapeDtypeStruct((rows, d), F32)],
        compiler_params=_cparams(("parallel",)),
    )(z, z, z, mu, k_k.reshape(1, d), k_a.reshape(1, d), r_k.reshape(1, d), w0, a0,
      dup.astype(BF16), iup.astype(BF16), gup.astype(BF16), _head_ones())


SCAN_TC = 32
SCAN_G = SCAN_TC // SUB
SLOTS = 4
V_BLOCK = 32
K_CHUNK = 16
N_KC = HEAD_DIM // K_CHUNK
PEEL = 2


def _wkv_first_sa(s_scr, t2, vs):
    sas = []
    for vb in range(vs // V_BLOCK):
        rows = slice(vb * V_BLOCK, (vb + 1) * V_BLOCK)

        def chunk(kc, sa):
            for j in range(K_CHUNK):
                sa = sa + s_scr[kc * K_CHUNK + j, rows, :] * t2[N_KC + kc, j:j + 1, :]
            return sa

        sas.append(lax.fori_loop(0, N_KC, chunk, jnp.zeros((V_BLOCK, LANES), F32)))
    return tuple(sas)


def _wkv_step(s_scr, t1, t2, t3, t2_next, v_blocks, o_ref, sas):
    nxt = []
    for vb, v_blk in enumerate(v_blocks):
        rows = slice(vb * V_BLOCK, (vb + 1) * V_BLOCK)
        sa = sas[vb]

        def chunk(kc, carry):
            o, sa_n = carry
            for j in range(K_CHUNK):
                k = kc * K_CHUNK + j
                s_new = s_scr[k, rows, :] - sa * t1[N_KC + kc, j:j + 1, :] + v_blk * t2[kc, j:j + 1, :]
                s_scr[k, rows, :] = s_new
                o = o + s_new * t3[kc, j:j + 1, :]
                sa_n = sa_n + s_new * t2_next[N_KC + kc, j:j + 1, :]
            return o, sa_n

        zero = jnp.zeros((V_BLOCK, LANES), F32)
        carry = (zero, zero)
        for kc in range(PEEL):
            carry = chunk(kc, carry)
        o, sa_n = lax.fori_loop(PEEL, N_KC, chunk, carry)
        o_ref[rows, :] = o
        nxt.append(sa_n)
    return tuple(nxt)


def _store_tile(ref, idx, x):
    ref[idx] = x.reshape(2 * N_KC, K_CHUNK, LANES)


def _store_scaled(tile_ref, t1, t2, t3, g, advance):
    d = HEAD_DIM
    g_new = g * t1[:d]
    inv = 1.0 / g_new
    _store_tile(tile_ref, 0, jnp.concatenate([t1[:d], t1[d:] * inv], axis=0))
    _store_tile(tile_ref, 1, jnp.concatenate([t2[:d] * inv, t2[d:] * g], axis=0))
    _store_tile(tile_ref, 2, jnp.concatenate([t3[:d] * g_new, t3[d:]], axis=0))
    return g_new if advance is True else jnp.where(advance, g_new, g)


def _unscale_state(s_scr, g_scr, g):
    g_scr[...] = g.reshape(N_KC, K_CHUNK, LANES)

    def chunk(kc, carry):
        for j in range(K_CHUNK):
            k = kc * K_CHUNK + j
            s_scr[k] = s_scr[k] * g_scr[kc, j:j + 1, :]
        return carry

    lax.fori_loop(0, N_KC, chunk, 0)


def _step_rows(ref, lead, grp, sub, n):
    return ref.at[lead + (grp,)][pl.ds(sub, n, stride=SUB), :]


def _scan_prompt_kernel(g1_ref, g2_ref, g3_ref, o_ref, st_ref, s_scr, g_scr, ta, tb, tc, td, oa, ob, *, reverse):
    c = pl.program_id(1)
    tiles = (ta, tb, tc, td)
    outs = (oa, ob)

    @pl.when(c == 0)
    def _():
        s_scr[...] = jnp.zeros_like(s_scr)

    for o_scr in outs:
        o_scr[...] = jnp.zeros_like(o_scr)

    def where(grp, sub):
        return (SCAN_G - 1 - grp, SUB - 1 - sub) if reverse else (grp, sub)

    def load_tiles(grp, sub, slot, dec, advance=True):
        g, s = where(grp, sub)
        return _store_scaled(tiles[slot], _step_rows(g1_ref, (0,), g, s, LANES).T,
                             _step_rows(g2_ref, (0,), g, s, LANES).T,
                             _step_rows(g3_ref, (), g, s, LANES).T, dec, advance)

    def flush(grp, sub, o_scr):
        g, s = where(grp, sub)
        o_ref.at[0, g][pl.ds(s, LANES, stride=SUB), :] = o_scr[...].T

    dec0 = load_tiles(0, 0, 0, jnp.ones((HEAD_DIM, LANES), F32))
    dec0 = load_tiles(0, 1, 1, dec0)
    sas0 = _wkv_first_sa(s_scr, tiles[0].at[1], HEAD_DIM)
    per_blk = V_BLOCK // K_CHUNK

    def group(grp, carry):
        sas, dec = carry
        for j in range(SUB):
            cur, nxt = tiles[j % SLOTS], tiles[(j + 1) % SLOTS]
            ahead = j + 2
            if ahead < SUB:
                dec = load_tiles(grp, ahead, ahead % SLOTS, dec)
            else:
                dec = load_tiles(jnp.minimum(grp + 1, SCAN_G - 1), ahead % SUB, ahead % SLOTS, dec,
                                 advance=grp + 1 < SCAN_G)
            if j > 0:
                flush(grp, j - 1, outs[(j - 1) % 2])
            v_blocks = [cur[2, N_KC + vb * per_blk: N_KC + (vb + 1) * per_blk].reshape(V_BLOCK, LANES)
                        for vb in range(HEAD_DIM // V_BLOCK)]
            sas = _wkv_step(s_scr, cur.at[0], cur.at[1], cur.at[2], nxt.at[1], v_blocks, outs[j % 2], sas)
        flush(grp, SUB - 1, outs[(SUB - 1) % 2])
        return sas, dec

    _, dec = lax.fori_loop(0, SCAN_G, group, (sas0, dec0))
    _unscale_state(s_scr, g_scr, dec)

    @pl.when(c == pl.num_programs(1) - 1)
    def _():
        st_ref[0] = s_scr[...]


def wkv_scan_prompt(g1, g2, g3, direction):
    n_grp, rows = g3.shape[0], g3.shape[1]
    groups = rows // (LANES * SUB)
    nblk = n_grp // SCAN_G
    tb = (lambda s: nblk - 1 - s) if direction else (lambda s: s)
    blk = (SCAN_G, LANES * SUB, LANES)
    dir_blk = pl.BlockSpec((1,) + blk, lambda g, s: (direction, tb(s), g, 0))
    return pl.pallas_call(
        functools.partial(_scan_prompt_kernel, reverse=bool(direction)), name="wkv_scan_prompt",
        grid=(groups, nblk),
        in_specs=[dir_blk, dir_blk, pl.BlockSpec(blk, lambda g, s: (tb(s), g, 0))],
        out_specs=[pl.BlockSpec((1,) + blk, lambda g, s: (0, tb(s), g, 0)),
                   pl.BlockSpec((1, HEAD_DIM, HEAD_DIM, LANES), lambda g, s: (g, 0, 0, 0))],
        out_shape=[jax.ShapeDtypeStruct((1, n_grp, rows, LANES), F32),
                   jax.ShapeDtypeStruct((groups, HEAD_DIM, HEAD_DIM, LANES), F32)],
        scratch_shapes=([pltpu.VMEM((HEAD_DIM, HEAD_DIM, LANES), F32),
                         pltpu.VMEM((N_KC, K_CHUNK, LANES), F32)]
                        + [pltpu.VMEM((3, 2 * N_KC, K_CHUNK, LANES), F32)] * SLOTS
                        + [pltpu.VMEM((LANES, LANES), F32)] * 2),
        compiler_params=_cparams(("parallel", "arbitrary")),
    )(g1, g2, g3)


S_CHAINS = DEC_BATCH * RWKV_HEADS
S_VS = HEAD_DIM // 2


def _scan_sample_kernel(g1f_ref, g1b_ref, g2f_ref, g2b_ref, g3f_ref, g3b_ref, s0_ref,
                        of_ref, ob_ref, s_scr, g_scr, ta, tb, tc, td, va, vb, vc, vd, oa, ob):
    c = pl.program_id(0)
    tiles = (ta, tb, tc, td)
    vals = (va, vb, vc, vd)
    outs = (oa, ob)

    @pl.when(c == 0)
    def _():
        s_scr[...] = s0_ref[...]

    nc = 2 * S_CHAINS
    n_pad = LANES - 2 * nc
    zpad = jnp.zeros((n_pad, LANES), F32)
    wpad = jnp.where(lax.broadcasted_iota(jnp.int32, (n_pad, LANES), 1) < HEAD_DIM, 1.0, 0.0)
    lane = lax.broadcasted_iota(jnp.int32, (S_VS, LANES), 1)

    def stacked_t(f_ref, b_ref, grp, sub, pad):
        f = _step_rows(f_ref, (0,), grp, sub, S_CHAINS)
        b = _step_rows(b_ref, (0,), SCAN_G - 1 - grp, SUB - 1 - sub, S_CHAINS)
        return jnp.concatenate([f, b, f, b, pad], axis=0).T

    def load_tiles(grp, sub, slot, dec, advance=True):
        t3 = stacked_t(g3f_ref, g3b_ref, grp, sub, zpad)
        vals[slot][...] = jnp.where(lane < nc, t3[HEAD_DIM:HEAD_DIM + S_VS], t3[HEAD_DIM + S_VS:])
        return _store_scaled(tiles[slot], stacked_t(g1f_ref, g1b_ref, grp, sub, wpad),
                             stacked_t(g2f_ref, g2b_ref, grp, sub, zpad), t3, dec, advance)

    def flush(grp, sub, o_scr):
        o = o_scr[...]
        full = jnp.concatenate([o, pltpu.roll(o, LANES - nc, 1), jnp.zeros((LANES - HEAD_DIM, LANES), F32)], axis=0)
        ot = full.T
        of_ref.at[0, grp][pl.ds(sub, S_CHAINS, stride=SUB), :] = ot[0:S_CHAINS]
        ob_ref.at[0, SCAN_G - 1 - grp][pl.ds(SUB - 1 - sub, S_CHAINS, stride=SUB), :] = ot[S_CHAINS:nc]

    dec0 = load_tiles(0, 0, 0, jnp.ones((HEAD_DIM, LANES), F32))
    dec0 = load_tiles(0, 1, 1, dec0)
    sas0 = _wkv_first_sa(s_scr, tiles[0].at[1], S_VS)

    def group(grp, carry):
        sas, dec = carry
        for j in range(SUB):
            cur, nxt = tiles[j % SLOTS], tiles[(j + 1) % SLOTS]
            ahead = j + 2
            if ahead < SUB:
                dec = load_tiles(grp, ahead, ahead % SLOTS, dec)
            else:
                dec = load_tiles(jnp.minimum(grp + 1, SCAN_G - 1), ahead % SUB, ahead % SLOTS, dec,
                                 advance=grp + 1 < SCAN_G)
            if j > 0:
                flush(grp, j - 1, outs[(j - 1) % 2])
            sas = _wkv_step(s_scr, cur.at[0], cur.at[1], cur.at[2], nxt.at[1], [vals[j % SLOTS][...]],
                            outs[j % 2], sas)
        flush(grp, SUB - 1, outs[(SUB - 1) % 2])
        return sas, dec

    _, dec = lax.fori_loop(0, SCAN_G, group, (sas0, dec0))
    _unscale_state(s_scr, g_scr, dec)


def wkv_scan_sample(g1, g2, g3, s0):
    n_grp, rows = g3.shape[0], g3.shape[1]
    g3 = g3.reshape(1, n_grp, rows, LANES)
    nblk = n_grp // SCAN_G
    blk = (1, SCAN_G, rows, LANES)
    fwd = lambda d: pl.BlockSpec(blk, lambda s: (d, s, 0, 0))
    bwd = lambda d: pl.BlockSpec(blk, lambda s: (d, nblk - 1 - s, 0, 0))
    return pl.pallas_call(
        _scan_sample_kernel, name="wkv_scan_sample",
        grid=(nblk,),
        in_specs=[fwd(0), bwd(1), fwd(0), bwd(1), fwd(0), bwd(0),
                  pl.BlockSpec((HEAD_DIM, S_VS, LANES), lambda s: (0, 0, 0))],
        out_specs=[fwd(0), bwd(0)],
        out_shape=[jax.ShapeDtypeStruct((1, n_grp, rows, LANES), F32)] * 2,
        scratch_shapes=([pltpu.VMEM((HEAD_DIM, S_VS, LANES), F32),
                         pltpu.VMEM((N_KC, K_CHUNK, LANES), F32)]
                        + [pltpu.VMEM((3, 2 * N_KC, K_CHUNK, LANES), F32)] * SLOTS
                        + [pltpu.VMEM((S_VS, LANES), F32)] * SLOTS
                        + [pltpu.VMEM((S_VS, LANES), F32)] * 2),
        compiler_params=_cparams(("arbitrary",)),
    )(g1, g1, g2, g2, g3, g3, s0)


def _sample_state_lanes(s0):
    nc = 2 * S_CHAINS
    st = jnp.transpose(s0, (4, 3, 1, 0, 2)).reshape(HEAD_DIM, HEAD_DIM, nc)
    st = jnp.concatenate([st[:, :S_VS], st[:, S_VS:]], axis=-1)
    return jnp.pad(st, ((0, 0), (0, 0), (0, LANES - 2 * nc)))


def _rwkv_post_kernel(of_ref, ob_ref, g_ref, bonus_ref, gw_ref, gb_ref, ones_ref, y_ref):
    ones = ones_ref[...]
    lane = lax.broadcasted_iota(jnp.int32, (RW_TILE, LANES), 1)
    low = lane < HEAD_DIM

    def head(h):
        rows = slice(h * SUB, (h + 1) * SUB)
        return (of_ref[0, :, rows, :] + ob_ref[0, :, rows, :]).reshape(RW_TILE, LANES)

    cols = [jnp.where(low, head(2 * c), pltpu.roll(head(2 * c + 1), HEAD_DIM, 1))
            for c in range(RWKV_DIM // LANES)]
    o = jnp.concatenate(cols, axis=-1)
    mu = _head_sum(o, ones) / HEAD_DIM
    oc = o - mu
    var = _head_sum(oc * oc, ones) / HEAD_DIM
    on = (oc * lax.rsqrt(var + GN_EPS)) * gw_ref[...] + gb_ref[...]
    y_ref[...] = (on + bonus_ref[...]) * g_ref[...]


def rwkv_post(o_f, o_b, n, g, bonus, gn_w, gn_b):
    rows, d = g.shape
    tps = n // RW_TILE
    tile = pl.BlockSpec((RW_TILE, d), lambda i: (i, 0))
    vec = pl.BlockSpec((1, d), lambda i: (0, 0))
    pk = pl.BlockSpec((1, RW_TILE // SUB, PACK_R, LANES), lambda i: (0, i % tps, i // tps, 0))
    return pl.pallas_call(
        _rwkv_post_kernel, name="rwkv_post",
        grid=(rows // RW_TILE,),
        in_specs=[pk, pk, tile, tile, vec, vec, pl.BlockSpec((d, d), lambda i: (0, 0))],
        out_specs=tile,
        out_shape=jax.ShapeDtypeStruct((rows, d), F32),
        compiler_params=_cparams(("parallel",)),
    )(o_f, o_b, g, bonus, gn_w.reshape(1, d), gn_b.reshape(1, d), _head_ones())


MOE_R = 512
MOE_M = 128


def _split3(x):
    a = x.astype(BF16)
    r1 = x - a.astype(F32)
    b = r1.astype(BF16)
    c = (r1 - b.astype(F32)).astype(BF16)
    return a, b, c


def _router_kernel(x_ref, g_ref, sh_ref, sc_ref, w_ref, b_ref, tri_ref, h_ref, comb_ref, rank_ref, rank_t_ref, cnt_ref):
    h = _modulated(x_ref[...], g_ref[...], sh_ref[0], sc_ref[0])
    h_ref[...] = h.astype(BF16)
    h1, h2, h3 = _split3(h)
    w1, w2, w3 = _split3(w_ref[...])
    logits = (_dot(h1, w1) + (_dot(h1, w2) + _dot(h2, w1))
              + (_dot(h1, w3) + _dot(h2, w2) + _dot(h3, w1))) + b_ref[...]
    col = lax.broadcasted_iota(jnp.int32, logits.shape, 1)
    logits = jnp.where(col < N_EXPERTS, logits, -jnp.inf)
    m1 = jnp.max(logits, axis=-1, keepdims=True)
    i1 = jnp.min(jnp.where(logits == m1, col, LANES), axis=-1, keepdims=True)
    rest = jnp.where(col == i1, -jnp.inf, logits)
    m2 = jnp.max(rest, axis=-1, keepdims=True)
    i2 = jnp.min(jnp.where(rest == m2, col, LANES), axis=-1, keepdims=True)
    e2 = jnp.exp(m2 - m1)
    den = 1.0 + e2
    comb_ref[...] = jnp.where(col == i1, 1.0 / den, 0.0) + jnp.where(col == i2, e2 / den, 0.0)
    chosen = (col == i1) | (col == i2)
    upto = _dot(tri_ref[...], jnp.where(chosen, 1.0, 0.0).astype(BF16))
    rank = jnp.where(chosen, upto - 1.0, -1.0)
    rank_ref[...] = rank
    rank_t_ref[0] = rank.T[0:N_EXPERTS, :]
    cnt_ref[0] = jnp.broadcast_to(upto[MOE_R - 1:MOE_R, :], (8, LANES))


def moe_router(x, g, sh, sc, router_w, router_b, rows_per_set):
    rows = x.shape[0]
    tm = MOE_R
    nblk = rows // tm
    si = _set_index(tm, rows_per_set)
    vec = pl.BlockSpec((1, 1, D_MODEL), lambda i: (si(i), 0, 0))
    w = jnp.pad(router_w, ((0, 0), (0, LANES - N_EXPERTS)))
    b = jnp.pad(router_b, (0, LANES - N_EXPERTS)).reshape(1, LANES)
    tri = jnp.asarray(np.tril(np.ones((tm, tm), np.float32))).astype(BF16)
    return pl.pallas_call(
        _router_kernel, name="moe_router",
        grid=(nblk,),
        in_specs=[pl.BlockSpec((tm, D_MODEL), lambda i: (i, 0)),
                  pl.BlockSpec((1, D_MODEL), lambda i: (0, 0)),
                  vec, vec,
                  pl.BlockSpec((D_MODEL, LANES), lambda i: (0, 0)),
                  pl.BlockSpec((1, LANES), lambda i: (0, 0)),
                  pl.BlockSpec((tm, tm), lambda i: (0, 0))],
        out_specs=[pl.BlockSpec((tm, D_MODEL), lambda i: (i, 0)),
                   pl.BlockSpec((tm, LANES), lambda i: (i, 0)),
                   pl.BlockSpec((tm, LANES), lambda i: (i, 0)),
                   pl.BlockSpec((1, N_EXPERTS, tm), lambda i: (i, 0, 0)),
                   pl.BlockSpec((1, 8, LANES), lambda i: (i, 0, 0))],
        out_shape=[jax.ShapeDtypeStruct((rows, D_MODEL), BF16),
                   jax.ShapeDtypeStruct((rows, LANES), F32),
                   jax.ShapeDtypeStruct((rows, LANES), F32),
                   jax.ShapeDtypeStruct((nblk, N_EXPERTS, tm), F32),
                   jax.ShapeDtypeStruct((nblk, 8, LANES), F32)],
        compiler_params=_cparams(("parallel",)),
    )(x, g.reshape(1, D_MODEL), sh, sc, w, b, tri)


MOE_TM = 1024


def _moe_kernel(cnt_ref, x_ref, h_ref, comb_ref, rank_ref, rank_t_ref, gate_ref, gfin_ref, wg_ref, wu_ref, wd_ref,
                o_ref, acc_scr):
    i = pl.program_id(0)
    e = pl.program_id(1)

    @pl.when(e == 0)
    def _():
        acc_scr[...] = jnp.zeros_like(acc_scr)

    col = lax.broadcasted_iota(jnp.int32, (MOE_R, LANES), 1)
    slot_rows = lax.broadcasted_iota(jnp.int32, (MOE_M, MOE_R), 0).astype(F32)
    slot_cols = lax.broadcasted_iota(jnp.int32, (MOE_R, MOE_M), 1).astype(F32)
    for s in range(MOE_TM // MOE_R):
        blk = slice(s * MOE_R, (s + 1) * MOE_R)
        count = cnt_ref[(i * (MOE_TM // MOE_R) + s) * N_EXPERTS + e]
        for m in range(MOE_R // MOE_M):
            @pl.when(count > m * MOE_M)
            def _():
                take = (rank_t_ref[s, pl.ds(e, 1), :] == slot_rows + float(m * MOE_M))
                hc = _dot(jnp.where(take, 1.0, 0.0).astype(BF16), h_ref[blk, :]).astype(BF16)
                gt = _dot(hc, wg_ref[0])
                act = (gt * _sigmoid(gt)) * _dot(hc, wu_ref[0])
                y = _dot(act.astype(BF16), wd_ref[0]).astype(BF16)
                mine = col == e
                rank_e = jnp.sum(jnp.where(mine, rank_ref[blk, :], 0.0), axis=-1, keepdims=True)
                ce = jnp.sum(jnp.where(mine, comb_ref[blk, :], 0.0), axis=-1, keepdims=True)
                put = rank_e == slot_cols + float(m * MOE_M)
                acc_scr[blk, :] += ce * _dot(jnp.where(put, 1.0, 0.0).astype(BF16), y)

    @pl.when(e == pl.num_programs(1) - 1)
    def _():
        y = x_ref[...] + gate_ref[0] * acc_scr[...]
        ms = jnp.mean(y * y, axis=-1, keepdims=True)
        o_ref[...] = y * lax.rsqrt(ms + RMS_EPS) * gfin_ref[...]


def moe_residual_norm(x, h, comb, rank, rank_t, counts, gate, g_final, wg, wu, wd, rows_per_set):
    rows = x.shape[0]
    tm = MOE_TM
    sub = tm // MOE_R
    si = _set_index(tm, rows_per_set)
    cnt = counts[:, 0, :N_EXPERTS].astype(jnp.int32).reshape(-1)
    grid_spec = pltpu.PrefetchScalarGridSpec(
        num_scalar_prefetch=1,
        grid=(rows // tm, N_EXPERTS),
        in_specs=[pl.BlockSpec((tm, D_MODEL), lambda i, e, c: (i, 0)),
                  pl.BlockSpec((tm, D_MODEL), lambda i, e, c: (i, 0)),
                  pl.BlockSpec((tm, LANES), lambda i, e, c: (i, 0)),
                  pl.BlockSpec((tm, LANES), lambda i, e, c: (i, 0)),
                  pl.BlockSpec((sub, N_EXPERTS, MOE_R), lambda i, e, c: (i, 0, 0)),
                  pl.BlockSpec((1, 1, D_MODEL), lambda i, e, c: (si(i), 0, 0)),
                  pl.BlockSpec((1, D_MODEL), lambda i, e, c: (0, 0)),
                  pl.BlockSpec((1, D_MODEL, D_FF_EXPERT), lambda i, e, c: (e, 0, 0)),
                  pl.BlockSpec((1, D_MODEL, D_FF_EXPERT), lambda i, e, c: (e, 0, 0)),
                  pl.BlockSpec((1, D_FF_EXPERT, D_MODEL), lambda i, e, c: (e, 0, 0))],
        out_specs=pl.BlockSpec((tm, D_MODEL), lambda i, e, c: (i, 0)),
        scratch_shapes=[pltpu.VMEM((tm, D_MODEL), F32)])
    return pl.pallas_call(
        _moe_kernel, name="moe_experts",
        grid_spec=grid_spec,
        out_shape=jax.ShapeDtypeStruct((rows, D_MODEL), F32),
        compiler_params=_cparams(("parallel", "arbitrary")),
    )(cnt, x, h, comb, rank, rank_t, gate, g_final.reshape(1, D_MODEL), wg, wu, wd)


def kernel(x_prompt, x_sample, cache_na_k, cache_na_v, state_wkv, c, c_ctx, mod_w, mod_b, norm_mix, norm_ffn, norm_final, na_w_in, fourier_w, na_rel_bias, na_w_out, ffn_w_gate, ffn_w_up, ffn_w_down, rw_w_in, pool_w, pool_scale, shift_mu, decay_w0, decay_up, iclr_a0, iclr_up, gate_up, k_k, k_a, r_k, gn_w, gn_b, rw_w_out, router_w, router_b, moe_w_gate, moe_w_up, moe_w_down):
    cond = jnp.concatenate([c_ctx[None, :], c, jnp.zeros((8 - N_SETS, D_MODEL), F32)], axis=0)
    mods = adaln_all(cond, mod_w, mod_b)[:, :N_SETS].reshape(DEPTH, N_SETS, 6, 1, D_MODEL)
    bf = lambda w: w.astype(BF16)

    xp = x_prompt.reshape(P_ROWS, D_MODEL)
    xs = x_sample.reshape(S_ROWS, D_MODEL)
    streams = {"p": (SEQ, P_ROWS, slice(0, 1)), "s": (DEC_SEQ, DEC_SEQ, slice(1, N_SETS))}
    x = {"p": xp, "s": xs}

    splits = ((0, FOURIER_CH), (FOURIER_CH, FOURIER_CH + NA_DIM),
              (FOURIER_CH + NA_DIM, FOURIER_CH + 2 * NA_DIM), (FOURIER_CH + 2 * NA_DIM, FOURIER_CH + 3 * NA_DIM))
    w_in, w_out = bf(na_w_in[0]), bf(na_w_out[0])
    f_bd = bf(_block_diag(fourier_w[0]))
    ffn_w = (bf(ffn_w_gate[0]), bf(ffn_w_up[0]), bf(ffn_w_down[0]))
    ck = cache_na_k[:, 0].reshape(DEC_BATCH * PAST_LEN, NA_DIM)
    cv = cache_na_v[:, 0].reshape(DEC_BATCH * PAST_LEN, NA_DIM)
    for name, (n, rps, sets) in streams.items():
        sh1, sc1, g1, sh2, sc2, g2 = [mods[0, sets, m] for m in range(6)]
        f, q, k, v = modulated_matmul(x[name], norm_mix[0], sh1, sc1, w_in, splits, rps)
        if name == "p":
            attn = context_attention(q, k, v)
            new_k = k.reshape(BATCH, 1, SEQ, NA_HEADS, HEAD_DIM)
            new_v = v.reshape(BATCH, 1, SEQ, NA_HEADS, HEAD_DIM)
        else:
            attn = neighbourhood_attention(q, k, v, ck, cv, _na_bias_table(na_rel_bias[0]))
        y = proj_residual(fourier_mix(f, n, f_bd), attn, x[name], g1, w_out, rps)
        x[name] = ffn_residual(y, norm_ffn[0], sh2, sc2, g2, *ffn_w, rps)

    w_in, w_out = bf(rw_w_in[0]), bf(rw_w_out[0])
    p_bd = bf(_block_diag(pool_w[0]))
    moe_w = (bf(moe_w_gate[0]), bf(moe_w_up[0]), bf(moe_w_down[0]))
    rw = (shift_mu[0], k_k[0], k_a[0], r_k[0], decay_w0[0], iclr_a0[0], decay_up[0], iclr_up[0], gate_up[0])
    out = {}
    for name, (n, rps, sets) in streams.items():
        sh1, sc1, g1, sh2, sc2, g2 = [mods[1, sets, m] for m in range(6)]
        pc, z = modulated_matmul(x[name], norm_mix[1], sh1, sc1, w_in, ((0, POOL_CH), (POOL_CH, POOL_CH + RWKV_IN)), rps)
        t1, t2, t3, gate, bonus = rwkv_prep(z, n, *rw)
        if name == "p":
            o_f, st_f = wkv_scan_prompt(t1, t2, t3, 0)
            o_b, st_b = wkv_scan_prompt(t1, t2, t3, 1)
            st = jnp.transpose(jnp.stack([st_f, st_b]), (0, 1, 4, 3, 2))
            st = jnp.transpose(st.reshape(2, BATCH, RWKV_HEADS, HEAD_DIM, HEAD_DIM), (1, 0, 2, 3, 4))
        else:
            o_f, o_b = wkv_scan_sample(t1, t2, t3, _sample_state_lanes(state_wkv[:, 0]))
        mixed = rwkv_post(o_f, o_b, n, gate, bonus, gn_w[0], gn_b[0])
        y = proj_residual(pool_mix(pc, n, p_bd, pool_scale[0]), mixed, x[name], g1, w_out, rps)
        routed = moe_router(y, norm_ffn[1], sh2, sc2, router_w[0], router_b[0], rps)
        out[name] = moe_residual_norm(y, *routed, g2, norm_final, *moe_w, rps)

    return (out["p"].reshape(BATCH, SEQ, D_MODEL), out["s"].reshape(DEC_BATCH, DEC_SEQ, D_MODEL),
            new_k, new_v, st[:, None])
```

```python
import functools
import math

import numpy as np
import jax
import jax.numpy as jnp
from jax import lax
from jax.experimental import pallas as pl
from jax.experimental.pallas import tpu as pltpu

F32 = jnp.float32
BF16 = jnp.bfloat16

D_MODEL = 1024
BATCH = 32
SEQ = 256
DEPTH = 2
DEC_BATCH = 2
DEC_SEQ = 1024
PAST_LEN = 512
GRID_W = 64
HEAD_DIM = 64
FOURIER_CH = D_MODEL // 4
FOURIER_GROUPS = 4
FOURIER_GW = FOURIER_CH // FOURIER_GROUPS
NA_DIM = D_MODEL - FOURIER_CH
NA_HEADS = NA_DIM // HEAD_DIM
NA_MAX_ROWS = 8
NA_COLS = 16
POOL_WINDOWS = (2, 4, 8, 16)
POOL_CH = D_MODEL // 4
POOL_GW = POOL_CH // len(POOL_WINDOWS)
RWKV_DIM = D_MODEL - POOL_CH
RWKV_HEADS = RWKV_DIM // HEAD_DIM
DECAY_LORA = 64
ICLR_LORA = 64
GATE_LORA = 128
RWKV_IN = 3 * RWKV_DIM + 2 * DECAY_LORA + 2 * ICLR_LORA + GATE_LORA
D_FF = 2816
N_EXPERTS = 8
D_FF_EXPERT = 1408
RMS_EPS = 1e-6
GN_EPS = 64e-5
L2_EPS = 1e-12
DECAY_SCALE = math.exp(-0.5)
NEG_INF = -1e30

P_ROWS = BATCH * SEQ
S_ROWS = DEC_BATCH * DEC_SEQ
N_ROWS = P_ROWS + S_ROWS
N_SETS = 1 + DEC_BATCH
LANES = 128
VMEM_LIMIT = 56 * 1024 * 1024


def _cparams(sem):
    return pltpu.CompilerParams(dimension_semantics=sem, vmem_limit_bytes=VMEM_LIMIT)


def _sigmoid(x):
    return 0.5 * jnp.tanh(0.5 * x) + 0.5


def _dot(a, b):
    return jnp.dot(a, b, preferred_element_type=F32)


def _dot_nt(a, b):
    return lax.dot_general(a, b, (((1,), (1,)), ((), ())), preferred_element_type=F32)


def _set_index(tm, rows_per_set):
    q = rows_per_set // tm
    return lambda i: i // q


def _modulated(x, g, sh, sc):
    ms = jnp.mean(x * x, axis=-1, keepdims=True)
    return (x * lax.rsqrt(ms + RMS_EPS) * g) * (1.0 + sc) + sh


def _adaln_kernel(c_ref, w_ref, b_ref, o_ref):
    c = c_ref[...]
    s = (c * _sigmoid(c)).astype(BF16)
    o_ref[0] = _dot(s, w_ref[0].astype(BF16)) + b_ref[0]


def adaln_all(cond, mod_w, mod_b):
    tn = 1536
    n = 6 * D_MODEL
    return pl.pallas_call(
        _adaln_kernel, name="adaln",
        grid=(DEPTH, n // tn),
        in_specs=[pl.BlockSpec((8, D_MODEL), lambda l, j: (0, 0)),
                  pl.BlockSpec((1, D_MODEL, tn), lambda l, j: (l, 0, j)),
                  pl.BlockSpec((1, 1, tn), lambda l, j: (l, 0, j))],
        out_specs=pl.BlockSpec((1, 8, tn), lambda l, j: (l, 0, j)),
        out_shape=jax.ShapeDtypeStruct((DEPTH, 8, n), F32),
        compiler_params=_cparams(("parallel", "parallel")),
    )(cond, mod_w, mod_b.reshape(DEPTH, 1, n))


def _modmm_kernel(x_ref, g_ref, sh_ref, sc_ref, w_ref, *o_refs, splits):
    h = _modulated(x_ref[...], g_ref[...], sh_ref[0], sc_ref[0]).astype(BF16)
    for o_ref, (a, b) in zip(o_refs, splits):
        o_ref[...] = _dot(h, w_ref[:, a:b]).astype(o_ref.dtype)


def modulated_matmul(x, g, sh, sc, w, splits, rows_per_set, tm=512):
    rows = x.shape[0]
    n_out = w.shape[1]
    si = _set_index(tm, rows_per_set)
    vec = pl.BlockSpec((1, 1, D_MODEL), lambda i: (si(i), 0, 0))
    return pl.pallas_call(
        functools.partial(_modmm_kernel, splits=splits), name="modulated_matmul",
        grid=(rows // tm,),
        in_specs=[pl.BlockSpec((tm, D_MODEL), lambda i: (i, 0)),
                  pl.BlockSpec((1, D_MODEL), lambda i: (0, 0)),
                  vec, vec,
                  pl.BlockSpec((D_MODEL, n_out), lambda i: (0, 0))],
        out_specs=[pl.BlockSpec((tm, b - a), lambda i: (i, 0)) for a, b in splits],
        out_shape=[jax.ShapeDtypeStruct((rows, b - a), F32) for a, b in splits],
        compiler_params=_cparams(("parallel",)),
    )(x, g.reshape(1, D_MODEL), sh, sc, w)


def _dft_mats(n):
    t = np.arange(n)
    ang = 2.0 * np.pi * ((t[:, None] * t[None, :]) % n) / n
    cn, sn = np.cos(ang) / np.sqrt(n), np.sin(ang) / np.sqrt(n)
    c = np.arange(FOURIER_GW)
    angc = 2.0 * np.pi * ((c[:, None] * c[None, :]) % FOURIER_GW) / FOURIER_GW
    eye = np.eye(FOURIER_GROUPS)
    cc = np.kron(eye, np.cos(angc) / np.sqrt(FOURIER_GW))
    sc = np.kron(eye, np.sin(angc) / np.sqrt(FOURIER_GW))
    as_bf = lambda a: jnp.asarray(a, dtype=F32).astype(BF16)
    return as_bf(cn), as_bf(sn), as_bf(cc), as_bf(sc)


def _fourier_kernel(f_ref, cn_ref, sn_ref, cc_ref, sc_ref, w_ref, o_ref):
    x = f_ref[...].astype(BF16)
    a = _dot(x, cc_ref[...]).astype(BF16)
    b = _dot(x, sc_ref[...]).astype(BF16)
    re = _dot(cn_ref[...], a) - _dot(sn_ref[...], b)
    o_ref[...] = _dot(re.astype(BF16), w_ref[...])


def _block_diag(w):
    g, c, _ = w.shape
    eye = jnp.eye(g, dtype=w.dtype)
    return (eye[:, None, :, None] * w[:, :, None, :]).reshape(g * c, g * c)


def fourier_mix(f, n, w_bd):
    rows = f.shape[0]
    cn, sn, cc, sc = _dft_mats(n)
    full = lambda shape: pl.BlockSpec(shape, lambda b: (0, 0))
    return pl.pallas_call(
        _fourier_kernel, name="fourier",
        grid=(rows // n,),
        in_specs=[pl.BlockSpec((n, FOURIER_CH), lambda b: (b, 0)),
                  full((n, n)), full((n, n)),
                  full((FOURIER_CH, FOURIER_CH)), full((FOURIER_CH, FOURIER_CH)),
                  full((FOURIER_CH, FOURIER_CH))],
        out_specs=pl.BlockSpec((n, FOURIER_CH), lambda b: (b, 0)),
        out_shape=jax.ShapeDtypeStruct((rows, FOURIER_CH), F32),
        compiler_params=_cparams(("parallel",)),
    )(f, cn, sn, cc, sc, w_bd)


def _head_pair(ref, c):
    x = ref[:, c * LANES:(c + 1) * LANES]
    low = lax.broadcasted_iota(jnp.int32, x.shape, 1) < HEAD_DIM
    return jnp.where(low, x, 0.0).astype(BF16), jnp.where(low, 0.0, x).astype(BF16), low


def _value_pair(ref, c):
    x = ref[:, c * LANES:(c + 1) * LANES]
    low = lax.broadcasted_iota(jnp.int32, x.shape, 1) < HEAD_DIM
    return jnp.where(low, x, 1.0).astype(BF16), jnp.where(low, 1.0, x).astype(BF16)


def _normalised_pair(res_even, res_odd, low):
    even = res_even / pltpu.roll(res_even, HEAD_DIM, 1)
    odd = res_odd / pltpu.roll(res_odd, HEAD_DIM, 1)
    return jnp.where(low, even, odd)


def _ctx_attn_kernel(q_ref, k_ref, v_ref, o_ref):
    scale = HEAD_DIM ** -0.5

    def scores(c):
        q_even, q_odd, _ = _head_pair(q_ref, c)
        k = k_ref[:, c * LANES:(c + 1) * LANES].astype(BF16)
        return _dot_nt(q_even, k) * scale, _dot_nt(q_odd, k) * scale

    def softmax_numerator(s):
        return jnp.exp(s - jnp.max(s, axis=-1, keepdims=True)).astype(BF16)

    n_pairs = NA_HEADS // 2
    pending = [scores(0)]
    for c in range(n_pairs):
        if c + 1 < n_pairs:
            pending.append(scores(c + 1))
        s_even, s_odd = pending.pop(0)
        v_even, v_odd = _value_pair(v_ref, c)
        low = lax.broadcasted_iota(jnp.int32, (SEQ, LANES), 1) < HEAD_DIM
        o_ref[:, c * LANES:(c + 1) * LANES] = _normalised_pair(
            _dot(softmax_numerator(s_even), v_even), _dot(softmax_numerator(s_odd), v_odd), low)


def context_attention(q, k, v):
    blk = pl.BlockSpec((SEQ, NA_DIM), lambda b: (b, 0))
    return pl.pallas_call(
        _ctx_attn_kernel, name="ctx_attn",
        grid=(BATCH,),
        in_specs=[blk, blk, blk],
        out_specs=blk,
        out_shape=jax.ShapeDtypeStruct((P_ROWS, NA_DIM), F32),
        compiler_params=_cparams(("parallel",)),
    )(q, k, v)


NA_ROWS = DEC_SEQ // GRID_W
NA_WIN = NA_MAX_ROWS * GRID_W


def _na_bias_table(rel_bias):
    cols = np.arange(GRID_W)
    c0 = np.clip(cols - NA_COLS // 2, 0, GRID_W - NA_COLS)
    col_ok = (cols[None, :] >= c0[:, None]) & (cols[None, :] < c0[:, None] + NA_COLS)
    dc = np.clip(cols[None, :] - cols[:, None] + NA_COLS - 1, 0, 2 * NA_COLS - 2)
    onehot = (dc[None] == np.arange(2 * NA_COLS - 1)[:, None, None]).astype(np.float32)
    toe = jnp.einsum("hrj,jqk->hrqk", rel_bias, jnp.asarray(onehot), precision=lax.Precision.HIGHEST)
    toe = jnp.where(col_ok[None, None], toe, NEG_INF)
    tabs = [jnp.transpose(toe[:, NA_MAX_ROWS - 1 - o: 2 * NA_MAX_ROWS - 1 - o], (0, 2, 1, 3))
            for o in range(NA_MAX_ROWS)]
    return jnp.stack(tabs).reshape(NA_MAX_ROWS, NA_HEADS, GRID_W, NA_WIN)


def _na_row_start(i):
    return jnp.clip(i - NA_MAX_ROWS // 2, 0, NA_ROWS - NA_MAX_ROWS)


def _na_kernel(q_ref, k_ref, v_ref, ck_ref, cv_ref, bias_ref, o_ref):
    scale = HEAD_DIM ** -0.5
    i = pl.program_id(1)
    start = pl.multiple_of(_na_row_start(i) * GRID_W, GRID_W)
    k_win = k_ref.at[pl.ds(start, NA_WIN)]
    v_win = v_ref.at[pl.ds(start, NA_WIN)]

    def scores(c):
        q_even, q_odd, _ = _head_pair(q_ref, c)
        kw = k_win[:, c * LANES:(c + 1) * LANES].astype(BF16)
        ck = ck_ref[:, c * LANES:(c + 1) * LANES].astype(BF16)
        return [(_dot_nt(q, kw) * scale + bias_ref[0, 2 * c + par], _dot_nt(q, ck) * scale)
                for par, q in enumerate((q_even, q_odd))]

    def weighted_values(s, vw, cv):
        s_loc, s_ctx = s
        m = jnp.maximum(jnp.max(s_loc, axis=-1, keepdims=True), jnp.max(s_ctx, axis=-1, keepdims=True))
        return _dot(jnp.exp(s_loc - m).astype(BF16), vw) + _dot(jnp.exp(s_ctx - m).astype(BF16), cv)

    n_pairs = NA_HEADS // 2
    pending = [scores(0)]
    for c in range(n_pairs):
        if c + 1 < n_pairs:
            pending.append(scores(c + 1))
        s_even, s_odd = pending.pop(0)
        vw_even, vw_odd = _value_pair(v_win, c)
        cv_even, cv_odd = _value_pair(cv_ref, c)
        low = lax.broadcasted_iota(jnp.int32, (GRID_W, LANES), 1) < HEAD_DIM
        o_ref[:, c * LANES:(c + 1) * LANES] = _normalised_pair(
            weighted_values(s_even, vw_even, cv_even), weighted_values(s_odd, vw_odd, cv_odd), low)


def neighbourhood_attention(q, k, v, ck, cv, bias_tab):
    seq = pl.BlockSpec((DEC_SEQ, NA_DIM), lambda b, i: (b, 0))
    ctx = pl.BlockSpec((PAST_LEN, NA_DIM), lambda b, i: (b, 0))
    row = pl.BlockSpec((GRID_W, NA_DIM), lambda b, i: (b * NA_ROWS + i, 0))
    return pl.pallas_call(
        _na_kernel, name="na_attn",
        grid=(DEC_BATCH, NA_ROWS),
        in_specs=[row, seq, seq, ctx, ctx,
                  pl.BlockSpec((1, NA_HEADS, GRID_W, NA_WIN), lambda b, i: (i - _na_row_start(i), 0, 0, 0))],
        out_specs=row,
        out_shape=jax.ShapeDtypeStruct((S_ROWS, NA_DIM), F32),
        compiler_params=_cparams(("parallel", "arbitrary")),
    )(q, k, v, ck, cv, bias_tab)


def _mixer_residual(a_ref, b_ref, x_ref, gate_ref, wo_ref):
    na = a_ref.shape[1]
    y = _dot(a_ref[...].astype(BF16), wo_ref[:na, :]) + _dot(b_ref[...].astype(BF16), wo_ref[na:, :])
    return x_ref[...] + gate_ref[0] * y


def _ffn_kernel(a_ref, b_ref, x_ref, gate1_ref, wo_ref, g_ref, sh_ref, sc_ref, gate2_ref, wg_ref, wu_ref, wd_ref,
                o_ref, y_scr, h_scr, acc_scr):
    j = pl.program_id(1)

    @pl.when(j == 0)
    def _():
        y = _mixer_residual(a_ref, b_ref, x_ref, gate1_ref, wo_ref)
        y_scr[...] = y
        h_scr[...] = _modulated(y, g_ref[...], sh_ref[0], sc_ref[0]).astype(BF16)
        acc_scr[...] = jnp.zeros_like(acc_scr)

    h = h_scr[...]
    gt = _dot(h, wg_ref[...])
    act = (gt * _sigmoid(gt)) * _dot(h, wu_ref[...])
    acc_scr[...] += _dot(act.astype(BF16), wd_ref[...])

    @pl.when(j == pl.num_programs(1) - 1)
    def _():
        o_ref[...] = y_scr[...] + gate2_ref[0] * acc_scr[...]


def mixer_ffn_residual(a, b, x, gate1, w_out, g, sh, sc, gate2, wg, wu, wd, rows_per_set, tm=512, tf=D_FF // 2):
    rows = x.shape[0]
    na, nb = a.shape[1], b.shape[1]
    si = _set_index(tm, rows_per_set)
    vec = pl.BlockSpec((1, 1, D_MODEL), lambda i, j: (si(i), 0, 0))
    row = lambda n: pl.BlockSpec((tm, n), lambda i, j: (i, 0))
    return pl.pallas_call(
        _ffn_kernel, name="ffn",
        grid=(rows // tm, D_FF // tf),
        in_specs=[row(na), row(nb), row(D_MODEL), vec,
                  pl.BlockSpec((na + nb, D_MODEL), lambda i, j: (0, 0)),
                  pl.BlockSpec((1, D_MODEL), lambda i, j: (0, 0)),
                  vec, vec, vec,
                  pl.BlockSpec((D_MODEL, tf), lambda i, j: (0, j)),
                  pl.BlockSpec((D_MODEL, tf), lambda i, j: (0, j)),
                  pl.BlockSpec((tf, D_MODEL), lambda i, j: (j, 0))],
        out_specs=row(D_MODEL),
        out_shape=jax.ShapeDtypeStruct((rows, D_MODEL), F32),
        scratch_shapes=[pltpu.VMEM((tm, D_MODEL), F32), pltpu.VMEM((tm, D_MODEL), BF16),
                        pltpu.VMEM((tm, D_MODEL), F32)],
        compiler_params=_cparams(("parallel", "arbitrary")),
    )(a, b, x, gate1, w_out, g.reshape(1, D_MODEL), sh, sc, gate2, wg, wu, wd)


def _pool_consts(n):
    t = np.arange(n)
    mats, cnts = [], []
    for win in POOL_WINDOWS:
        lo = np.clip(t - win // 2, 0, n)
        hi = np.clip(t + win - win // 2, 0, n)
        mats.append(((t[None, :] >= lo[:, None]) & (t[None, :] < hi[:, None])).astype(np.float32))
        cnts.append(np.repeat((hi - lo).astype(np.float32)[:, None], POOL_GW, axis=1))
    return jnp.asarray(np.stack(mats)).astype(BF16), jnp.asarray(np.concatenate(cnts, axis=1))


def _pool_kernel(x_ref, pm_ref, cnt_ref, w_ref, scale_ref, o_ref):
    x = x_ref[...]
    hi = x.astype(BF16)
    lo = (x - hi.astype(F32)).astype(BF16)
    sums = []
    for g in range(len(POOL_WINDOWS)):
        sl = slice(g * POOL_GW, (g + 1) * POOL_GW)
        sums.append(_dot(pm_ref[g], hi[:, sl]) + _dot(pm_ref[g], lo[:, sl]))
    y = jnp.concatenate(sums, axis=-1) / cnt_ref[...] - x
    o_ref[...] = _dot(y.astype(BF16), w_ref[...]) * scale_ref[...]


def pool_mix(x, n, w_bd, scale):
    rows = x.shape[0]
    pm, cnt = _pool_consts(n)
    return pl.pallas_call(
        _pool_kernel, name="pool",
        grid=(rows // n,),
        in_specs=[pl.BlockSpec((n, POOL_CH), lambda b: (b, 0)),
                  pl.BlockSpec((len(POOL_WINDOWS), n, n), lambda b: (0, 0, 0)),
                  pl.BlockSpec((n, POOL_CH), lambda b: (0, 0)),
                  pl.BlockSpec((POOL_CH, POOL_CH), lambda b: (0, 0)),
                  pl.BlockSpec((1, POOL_CH), lambda b: (0, 0))],
        out_specs=pl.BlockSpec((n, POOL_CH), lambda b: (b, 0)),
        out_shape=jax.ShapeDtypeStruct((rows, POOL_CH), F32),
        compiler_params=_cparams(("parallel",)),
    )(x, pm, cnt, w_bd, scale.reshape(1, POOL_CH))


RW_TILE = 256
HALO = 8
SUB = 8
PACK_R = RWKV_HEADS * SUB


def _head_ones():
    h = np.arange(RWKV_DIM) // HEAD_DIM
    return jnp.asarray((h[:, None] == h[None, :]).astype(np.float32)).astype(BF16)


def _head_sum(x, ones):
    hi = x.astype(BF16)
    lo = (x - hi.astype(F32)).astype(BF16)
    return _dot(hi, ones) + _dot(lo, ones)


def _pack_heads(a, b, o_ref, lead):
    n = a.shape[0]
    lane = lax.broadcasted_iota(jnp.int32, (n, LANES), 1)
    low = lane < HEAD_DIM
    for c in range(RWKV_DIM // LANES):
        ac = a[:, c * LANES:(c + 1) * LANES]
        bc = b[:, c * LANES:(c + 1) * LANES]
        even = jnp.where(low, ac, pltpu.roll(bc, HEAD_DIM, 1))
        odd = jnp.where(low, pltpu.roll(ac, HEAD_DIM, 1), bc)
        for h, val in ((2 * c, even), (2 * c + 1, odd)):
            o_ref[lead + (slice(None), slice(h * SUB, (h + 1) * SUB), slice(None))] = val.reshape(n // SUB, SUB, LANES)


def _rwkv_prep_kernel(z_ref, zp_ref, zn_ref, mu_ref, kk_w_ref, ka_ref, rk_ref, w0_ref, a0_ref,
                      dup_ref, iup_ref, gup_ref, ones_ref,
                      g1_ref, g2_ref, g3_ref, g_ref, bonus_ref, *, tiles_per_seq):
    i = pl.program_id(0)
    pos = i % tiles_per_seq
    z = z_ref[...]
    row = lax.broadcasted_iota(jnp.int32, (RW_TILE, 1), 0)
    prev_edge = jnp.where(pos == 0, 0.0, zp_ref[HALO - 1:HALO, :])
    next_edge = jnp.where(pos == tiles_per_seq - 1, 0.0, zn_ref[0:1, :])
    prev = jnp.where(row == 0, prev_edge, pltpu.roll(z, 1, 0))
    nxt = jnp.where(row == RW_TILE - 1, next_edge, pltpu.roll(z, RW_TILE - 1, 0))
    zr = z + mu_ref[0:1, :] * (prev - z) + mu_ref[1:2, :] * (nxt - z)

    d = RWKV_DIM
    r, k, v = zr[:, :d], zr[:, d:2 * d], zr[:, 2 * d:3 * d]
    lora = 3 * d
    ones = ones_ref[...]
    kk = k * kk_w_ref[...]
    kk = kk * lax.rsqrt(_head_sum(kk * kk, ones) + L2_EPS)
    _pack_heads(r, v, g3_ref, ())
    for dr in range(2):
        wl = zr[:, lora + dr * DECAY_LORA: lora + (dr + 1) * DECAY_LORA]
        al = zr[:, lora + 2 * DECAY_LORA + dr * ICLR_LORA: lora + 2 * DECAY_LORA + (dr + 1) * ICLR_LORA]
        lw = w0_ref[dr:dr + 1, :] + _dot(jnp.tanh(wl).astype(BF16), dup_ref[dr])
        w = jnp.exp(-DECAY_SCALE * _sigmoid(lw))
        a = _sigmoid(a0_ref[dr:dr + 1, :] + _dot(al.astype(BF16), iup_ref[dr]))
        _pack_heads(w, kk * a, g1_ref, (dr,))
        _pack_heads(k * (1.0 + (a - 1.0) * ka_ref[...]), kk, g2_ref, (dr,))
    gl = zr[:, lora + 2 * DECAY_LORA + 2 * ICLR_LORA:]
    g_ref[...] = _dot(_sigmoid(gl).astype(BF16), gup_ref[...])
    bonus_ref[...] = _head_sum(r * k * rk_ref[...], ones) * v


def rwkv_prep(z, n, mu, k_k, k_a, r_k, w0, a0, dup, iup, gup):
    rows = z.shape[0]
    nb = rows // n
    tps = n // RW_TILE
    hb = RW_TILE // HALO
    last = rows // HALO - 1
    d = RWKV_DIM
    full2 = lambda shape: pl.BlockSpec(shape, lambda i: (0, 0))
    full3 = lambda shape: pl.BlockSpec(shape, lambda i: (0, 0, 0))
    tile = pl.BlockSpec((RW_TILE, d), lambda i: (i, 0))
    pk2 = pl.BlockSpec((2, RW_TILE // SUB, PACK_R, LANES), lambda i: (0, i % tps, i // tps, 0))
    pk1 = pl.BlockSpec((RW_TILE // SUB, PACK_R, LANES), lambda i: (i % tps, i // tps, 0))
    return pl.pallas_call(
        functools.partial(_rwkv_prep_kernel, tiles_per_seq=tps), name="rwkv_prep",
        grid=(rows // RW_TILE,),
        in_specs=[pl.BlockSpec((RW_TILE, RWKV_IN), lambda i: (i, 0)),
                  pl.BlockSpec((HALO, RWKV_IN), lambda i: (jnp.maximum(i * hb - 1, 0), 0)),
                  pl.BlockSpec((HALO, RWKV_IN), lambda i: (jnp.minimum((i + 1) * hb, last), 0)),
                  full2((2, RWKV_IN)), full2((1, d)), full2((1, d)), full2((1, d)),
                  full2((2, d)), full2((2, d)),
                  full3((2, DECAY_LORA, d)), full3((2, ICLR_LORA, d)), full2((GATE_LORA, d)),
                  full2((d, d))],
        out_specs=[pk2, pk2, pk1, tile, tile],
        out_shape=[jax.ShapeDtypeStruct((2, n // SUB, nb * PACK_R, LANES), F32),
                   jax.ShapeDtypeStruct((2, n // SUB, nb * PACK_R, LANES), F32),
                   jax.ShapeDtypeStruct((n // SUB, nb * PACK_R, LANES), F32),
                   jax.ShapeDtypeStruct((rows, d), F32),
                   jax.ShapeDtypeStruct((rows, d), F32)],
        compiler_params=_cparams(("parallel",)),
    )(z, z, z, mu, k_k.reshape(1, d), k_a.reshape(1, d), r_k.reshape(1, d), w0, a0,
      dup.astype(BF16), iup.astype(BF16), gup.astype(BF16), _head_ones())


SCAN_TC = 32
SCAN_G = SCAN_TC // SUB
SLOTS = 4
V_BLOCK = 32
K_CHUNK = 16
N_KC = HEAD_DIM // K_CHUNK
PEEL = 2


def _wkv_first_sa(s_scr, t2, vs):
    sas = []
    for vb in range(vs // V_BLOCK):
        rows = slice(vb * V_BLOCK, (vb + 1) * V_BLOCK)

        def chunk(kc, sa):
            for j in range(K_CHUNK):
                sa = sa + s_scr[kc * K_CHUNK + j, rows, :] * t2[N_KC + kc, j:j + 1, :]
            return sa

        sas.append(lax.fori_loop(0, N_KC, chunk, jnp.zeros((V_BLOCK, LANES), F32)))
    return tuple(sas)


def _wkv_step(s_scr, t1, t2, t3, t2_next, v_blocks, o_ref, sas):
    nxt = []
    for vb, v_blk in enumerate(v_blocks):
        rows = slice(vb * V_BLOCK, (vb + 1) * V_BLOCK)
        sa = sas[vb]

        def chunk(kc, carry):
            o, sa_n = carry
            for j in range(K_CHUNK):
                k = kc * K_CHUNK + j
                s_new = s_scr[k, rows, :] - sa * t1[N_KC + kc, j:j + 1, :] + v_blk * t2[kc, j:j + 1, :]
                s_scr[k, rows, :] = s_new
                o = o + s_new * t3[kc, j:j + 1, :]
                sa_n = sa_n + s_new * t2_next[N_KC + kc, j:j + 1, :]
            return o, sa_n

        zero = jnp.zeros((V_BLOCK, LANES), F32)
        carry = (zero, zero)
        for kc in range(PEEL):
            carry = chunk(kc, carry)
        o, sa_n = lax.fori_loop(PEEL, N_KC, chunk, carry)
        o_ref[rows, :] = o
        nxt.append(sa_n)
    return tuple(nxt)


def _store_tile(ref, idx, x):
    ref[idx] = x.reshape(2 * N_KC, K_CHUNK, LANES)


def _store_scaled(tile_ref, t1, t2, t3, g, advance):
    d = HEAD_DIM
    g_new = g * t1[:d]
    inv = 1.0 / g_new
    _store_tile(tile_ref, 0, jnp.concatenate([t1[:d], t1[d:] * inv], axis=0))
    _store_tile(tile_ref, 1, jnp.concatenate([t2[:d] * inv, t2[d:] * g], axis=0))
    _store_tile(tile_ref, 2, jnp.concatenate([t3[:d] * g_new, t3[d:]], axis=0))
    return g_new if advance is True else jnp.where(advance, g_new, g)


def _unscale_state(s_scr, g_scr, g):
    g_scr[...] = g.reshape(N_KC, K_CHUNK, LANES)

    def chunk(kc, carry):
        for j in range(K_CHUNK):
            k = kc * K_CHUNK + j
            s_scr[k] = s_scr[k] * g_scr[kc, j:j + 1, :]
        return carry

    lax.fori_loop(0, N_KC, chunk, 0)


def _step_rows(ref, lead, grp, sub, n):
    return ref.at[lead + (grp,)][pl.ds(sub, n, stride=SUB), :]


def _scan_prompt_kernel(g1_ref, g2_ref, g3_ref, o_ref, st_ref, s_scr, g_scr, ta, tb, tc, td, oa, ob, *, reverse):
    c = pl.program_id(1)
    tiles = (ta, tb, tc, td)
    outs = (oa, ob)

    @pl.when(c == 0)
    def _():
        s_scr[...] = jnp.zeros_like(s_scr)

    for o_scr in outs:
        o_scr[...] = jnp.zeros_like(o_scr)

    def where(grp, sub):
        return (SCAN_G - 1 - grp, SUB - 1 - sub) if reverse else (grp, sub)

    def load_tiles(grp, sub, slot, dec, advance=True):
        g, s = where(grp, sub)
        return _store_scaled(tiles[slot], _step_rows(g1_ref, (0,), g, s, LANES).T,
                             _step_rows(g2_ref, (0,), g, s, LANES).T,
                             _step_rows(g3_ref, (), g, s, LANES).T, dec, advance)

    def flush(grp, sub, o_scr):
        g, s = where(grp, sub)
        o_ref.at[0, g][pl.ds(s, LANES, stride=SUB), :] = o_scr[...].T

    dec0 = load_tiles(0, 0, 0, jnp.ones((HEAD_DIM, LANES), F32))
    dec0 = load_tiles(0, 1, 1, dec0)
    sas0 = _wkv_first_sa(s_scr, tiles[0].at[1], HEAD_DIM)
    per_blk = V_BLOCK // K_CHUNK

    def group(grp, carry):
        sas, dec = carry
        for j in range(SUB):
            cur, nxt = tiles[j % SLOTS], tiles[(j + 1) % SLOTS]
            ahead = j + 2
            if ahead < SUB:
                dec = load_tiles(grp, ahead, ahead % SLOTS, dec)
            else:
                dec = load_tiles(jnp.minimum(grp + 1, SCAN_G - 1), ahead % SUB, ahead % SLOTS, dec,
                                 advance=grp + 1 < SCAN_G)
            if j > 0:
                flush(grp, j - 1, outs[(j - 1) % 2])
            v_blocks = [cur[2, N_KC + vb * per_blk: N_KC + (vb + 1) * per_blk].reshape(V_BLOCK, LANES)
                        for vb in range(HEAD_DIM // V_BLOCK)]
            sas = _wkv_step(s_scr, cur.at[0], cur.at[1], cur.at[2], nxt.at[1], v_blocks, outs[j % 2], sas)
        flush(grp, SUB - 1, outs[(SUB - 1) % 2])
        return sas, dec

    _, dec = lax.fori_loop(0, SCAN_G, group, (sas0, dec0))
    _unscale_state(s_scr, g_scr, dec)

    @pl.when(c == pl.num_programs(1) - 1)
    def _():
        st_ref[0] = s_scr[...]


def wkv_scan_prompt(g1, g2, g3, direction):
    n_grp, rows = g3.shape[0], g3.shape[1]
    groups = rows // (LANES * SUB)
    nblk = n_grp // SCAN_G
    tb = (lambda s: nblk - 1 - s) if direction else (lambda s: s)
    blk = (SCAN_G, LANES * SUB, LANES)
    dir_blk = pl.BlockSpec((1,) + blk, lambda g, s: (direction, tb(s), g, 0))
    return pl.pallas_call(
        functools.partial(_scan_prompt_kernel, reverse=bool(direction)), name="wkv_scan_prompt",
        grid=(groups, nblk),
        in_specs=[dir_blk, dir_blk, pl.BlockSpec(blk, lambda g, s: (tb(s), g, 0))],
        out_specs=[pl.BlockSpec((1,) + blk, lambda g, s: (0, tb(s), g, 0)),
                   pl.BlockSpec((1, HEAD_DIM, HEAD_DIM, LANES), lambda g, s: (g, 0, 0, 0))],
        out_shape=[jax.ShapeDtypeStruct((1, n_grp, rows, LANES), F32),
                   jax.ShapeDtypeStruct((groups, HEAD_DIM, HEAD_DIM, LANES), F32)],
        scratch_shapes=([pltpu.VMEM((HEAD_DIM, HEAD_DIM, LANES), F32),
                         pltpu.VMEM((N_KC, K_CHUNK, LANES), F32)]
                        + [pltpu.VMEM((3, 2 * N_KC, K_CHUNK, LANES), F32)] * SLOTS
                        + [pltpu.VMEM((LANES, LANES), F32)] * 2),
        compiler_params=_cparams(("parallel", "arbitrary")),
    )(g1, g2, g3)


S_CHAINS = DEC_BATCH * RWKV_HEADS
S_VS = HEAD_DIM // 2


def _scan_sample_kernel(g1f_ref, g1b_ref, g2f_ref, g2b_ref, g3f_ref, g3b_ref, s0_ref,
                        of_ref, ob_ref, s_scr, g_scr, ta, tb, tc, td, va, vb, vc, vd, oa, ob):
    c = pl.program_id(0)
    tiles = (ta, tb, tc, td)
    vals = (va, vb, vc, vd)
    outs = (oa, ob)

    @pl.when(c == 0)
    def _():
        s_scr[...] = s0_ref[...]

    nc = 2 * S_CHAINS
    n_pad = LANES - 2 * nc
    zpad = jnp.zeros((n_pad, LANES), F32)
    wpad = jnp.where(lax.broadcasted_iota(jnp.int32, (n_pad, LANES), 1) < HEAD_DIM, 1.0, 0.0)
    lane = lax.broadcasted_iota(jnp.int32, (S_VS, LANES), 1)

    def stacked_t(f_ref, b_ref, grp, sub, pad):
        f = _step_rows(f_ref, (0,), grp, sub, S_CHAINS)
        b = _step_rows(b_ref, (0,), SCAN_G - 1 - grp, SUB - 1 - sub, S_CHAINS)
        return jnp.concatenate([f, b, f, b, pad], axis=0).T

    def load_tiles(grp, sub, slot, dec, advance=True):
        t3 = stacked_t(g3f_ref, g3b_ref, grp, sub, zpad)
        vals[slot][...] = jnp.where(lane < nc, t3[HEAD_DIM:HEAD_DIM + S_VS], t3[HEAD_DIM + S_VS:])
        return _store_scaled(tiles[slot], stacked_t(g1f_ref, g1b_ref, grp, sub, wpad),
                             stacked_t(g2f_ref, g2b_ref, grp, sub, zpad), t3, dec, advance)

    def flush(grp, sub, o_scr):
        o = o_scr[...]
        full = jnp.concatenate([o, pltpu.roll(o, LANES - nc, 1), jnp.zeros((LANES - HEAD_DIM, LANES), F32)], axis=0)
        ot = full.T
        of_ref.at[0, grp][pl.ds(sub, S_CHAINS, stride=SUB), :] = ot[0:S_CHAINS]
        ob_ref.at[0, SCAN_G - 1 - grp][pl.ds(SUB - 1 - sub, S_CHAINS, stride=SUB), :] = ot[S_CHAINS:nc]

    dec0 = load_tiles(0, 0, 0, jnp.ones((HEAD_DIM, LANES), F32))
    dec0 = load_tiles(0, 1, 1, dec0)
    sas0 = _wkv_first_sa(s_scr, tiles[0].at[1], S_VS)

    def group(grp, carry):
        sas, dec = carry
        for j in range(SUB):
            cur, nxt = tiles[j % SLOTS], tiles[(j + 1) % SLOTS]
            ahead = j + 2
            if ahead < SUB:
                dec = load_tiles(grp, ahead, ahead % SLOTS, dec)
            else:
                dec = load_tiles(jnp.minimum(grp + 1, SCAN_G - 1), ahead % SUB, ahead % SLOTS, dec,
                                 advance=grp + 1 < SCAN_G)
            if j > 0:
                flush(grp, j - 1, outs[(j - 1) % 2])
            sas = _wkv_step(s_scr, cur.at[0], cur.at[1], cur.at[2], nxt.at[1], [vals[j % SLOTS][...]],
                            outs[j % 2], sas)
        flush(grp, SUB - 1, outs[(SUB - 1) % 2])
        return sas, dec

    _, dec = lax.fori_loop(0, SCAN_G, group, (sas0, dec0))
    _unscale_state(s_scr, g_scr, dec)


def wkv_scan_sample(g1, g2, g3, s0):
    n_grp, rows = g3.shape[0], g3.shape[1]
    g3 = g3.reshape(1, n_grp, rows, LANES)
    nblk = n_grp // SCAN_G
    blk = (1, SCAN_G, rows, LANES)
    fwd = lambda d: pl.BlockSpec(blk, lambda s: (d, s, 0, 0))
    bwd = lambda d: pl.BlockSpec(blk, lambda s: (d, nblk - 1 - s, 0, 0))
    return pl.pallas_call(
        _scan_sample_kernel, name="wkv_scan_sample",
        grid=(nblk,),
        in_specs=[fwd(0), bwd(1), fwd(0), bwd(1), fwd(0), bwd(0),
                  pl.BlockSpec((HEAD_DIM, S_VS, LANES), lambda s: (0, 0, 0))],
        out_specs=[fwd(0), bwd(0)],
        out_shape=[jax.ShapeDtypeStruct((1, n_grp, rows, LANES), F32)] * 2,
        scratch_shapes=([pltpu.VMEM((HEAD_DIM, S_VS, LANES), F32),
                         pltpu.VMEM((N_KC, K_CHUNK, LANES), F32)]
                        + [pltpu.VMEM((3, 2 * N_KC, K_CHUNK, LANES), F32)] * SLOTS
                        + [pltpu.VMEM((S_VS, LANES), F32)] * SLOTS
                        + [pltpu.VMEM((S_VS, LANES), F32)] * 2),
        compiler_params=_cparams(("arbitrary",)),
    )(g1, g1, g2, g2, g3, g3, s0)


def _sample_state_lanes(s0):
    nc = 2 * S_CHAINS
    st = jnp.transpose(s0, (4, 3, 1, 0, 2)).reshape(HEAD_DIM, HEAD_DIM, nc)
    st = jnp.concatenate([st[:, :S_VS], st[:, S_VS:]], axis=-1)
    return jnp.pad(st, ((0, 0), (0, 0), (0, LANES - 2 * nc)))


def _rwkv_post_kernel(of_ref, ob_ref, g_ref, bonus_ref, gw_ref, gb_ref, ones_ref, y_ref):
    ones = ones_ref[...]
    lane = lax.broadcasted_iota(jnp.int32, (RW_TILE, LANES), 1)
    low = lane < HEAD_DIM

    def head(h):
        rows = slice(h * SUB, (h + 1) * SUB)
        return (of_ref[0, :, rows, :] + ob_ref[0, :, rows, :]).reshape(RW_TILE, LANES)

    cols = [jnp.where(low, head(2 * c), pltpu.roll(head(2 * c + 1), HEAD_DIM, 1))
            for c in range(RWKV_DIM // LANES)]
    o = jnp.concatenate(cols, axis=-1)
    mu = _head_sum(o, ones) / HEAD_DIM
    oc = o - mu
    var = _head_sum(oc * oc, ones) / HEAD_DIM
    on = (oc * lax.rsqrt(var + GN_EPS)) * gw_ref[...] + gb_ref[...]
    y_ref[...] = (on + bonus_ref[...]) * g_ref[...]


def rwkv_post(o_f, o_b, n, g, bonus, gn_w, gn_b):
    rows, d = g.shape
    tps = n // RW_TILE
    tile = pl.BlockSpec((RW_TILE, d), lambda i: (i, 0))
    vec = pl.BlockSpec((1, d), lambda i: (0, 0))
    pk = pl.BlockSpec((1, RW_TILE // SUB, PACK_R, LANES), lambda i: (0, i % tps, i // tps, 0))
    return pl.pallas_call(
        _rwkv_post_kernel, name="rwkv_post",
        grid=(rows // RW_TILE,),
        in_specs=[pk, pk, tile, tile, vec, vec, pl.BlockSpec((d, d), lambda i: (0, 0))],
        out_specs=tile,
        out_shape=jax.ShapeDtypeStruct((rows, d), F32),
        compiler_params=_cparams(("parallel",)),
    )(o_f, o_b, g, bonus, gn_w.reshape(1, d), gn_b.reshape(1, d), _head_ones())


MOE_R = 512
MOE_M = 128


def _split3(x):
    a = x.astype(BF16)
    r1 = x - a.astype(F32)
    b = r1.astype(BF16)
    c = (r1 - b.astype(F32)).astype(BF16)
    return a, b, c


def _router_kernel(mix_a_ref, mix_b_ref, x_ref, gate1_ref, wo_ref, g_ref, sh_ref, sc_ref, w_ref, b_ref, tri_ref,
                   y_ref, h_ref, comb_ref, rank_ref, rank_t_ref, cnt_ref):
    y = _mixer_residual(mix_a_ref, mix_b_ref, x_ref, gate1_ref, wo_ref)
    y_ref[...] = y
    h = _modulated(y, g_ref[...], sh_ref[0], sc_ref[0])
    h_ref[...] = h.astype(BF16)
    h1, h2, h3 = _split3(h)
    w1, w2, w3 = _split3(w_ref[...])
    logits = (_dot(h1, w1) + (_dot(h1, w2) + _dot(h2, w1))
              + (_dot(h1, w3) + _dot(h2, w2) + _dot(h3, w1))) + b_ref[...]
    col = lax.broadcasted_iota(jnp.int32, logits.shape, 1)
    logits = jnp.where(col < N_EXPERTS, logits, -jnp.inf)
    m1 = jnp.max(logits, axis=-1, keepdims=True)
    i1 = jnp.min(jnp.where(logits == m1, col, LANES), axis=-1, keepdims=True)
    rest = jnp.where(col == i1, -jnp.inf, logits)
    m2 = jnp.max(rest, axis=-1, keepdims=True)
    i2 = jnp.min(jnp.where(rest == m2, col, LANES), axis=-1, keepdims=True)
    e2 = jnp.exp(m2 - m1)
    den = 1.0 + e2
    comb_ref[...] = jnp.where(col == i1, 1.0 / den, 0.0) + jnp.where(col == i2, e2 / den, 0.0)
    chosen = (col == i1) | (col == i2)
    upto = _dot(tri_ref[...], jnp.where(chosen, 1.0, 0.0).astype(BF16))
    rank = jnp.where(chosen, upto - 1.0, -1.0)
    rank_ref[...] = rank
    rank_t_ref[0] = rank.T[0:N_EXPERTS, :]
    cnt_ref[0] = jnp.broadcast_to(upto[MOE_R - 1:MOE_R, :], (8, LANES))


def mixer_residual_router(a, b, x, gate1, w_out, g, sh, sc, router_w, router_b, rows_per_set):
    rows = x.shape[0]
    na, nb = a.shape[1], b.shape[1]
    tm = MOE_R
    nblk = rows // tm
    si = _set_index(tm, rows_per_set)
    vec = pl.BlockSpec((1, 1, D_MODEL), lambda i: (si(i), 0, 0))
    row = lambda n: pl.BlockSpec((tm, n), lambda i: (i, 0))
    w = jnp.pad(router_w, ((0, 0), (0, LANES - N_EXPERTS)))
    bias = jnp.pad(router_b, (0, LANES - N_EXPERTS)).reshape(1, LANES)
    tri = jnp.asarray(np.tril(np.ones((tm, tm), np.float32))).astype(BF16)
    return pl.pallas_call(
        _router_kernel, name="moe_router",
        grid=(nblk,),
        in_specs=[row(na), row(nb), row(D_MODEL), vec,
                  pl.BlockSpec((na + nb, D_MODEL), lambda i: (0, 0)),
                  pl.BlockSpec((1, D_MODEL), lambda i: (0, 0)),
                  vec, vec,
                  pl.BlockSpec((D_MODEL, LANES), lambda i: (0, 0)),
                  pl.BlockSpec((1, LANES), lambda i: (0, 0)),
                  pl.BlockSpec((tm, tm), lambda i: (0, 0))],
        out_specs=[row(D_MODEL), row(D_MODEL), row(LANES), row(LANES),
                   pl.BlockSpec((1, N_EXPERTS, tm), lambda i: (i, 0, 0)),
                   pl.BlockSpec((1, 8, LANES), lambda i: (i, 0, 0))],
        out_shape=[jax.ShapeDtypeStruct((rows, D_MODEL), F32),
                   jax.ShapeDtypeStruct((rows, D_MODEL), BF16),
                   jax.ShapeDtypeStruct((rows, LANES), F32),
                   jax.ShapeDtypeStruct((rows, LANES), F32),
                   jax.ShapeDtypeStruct((nblk, N_EXPERTS, tm), F32),
                   jax.ShapeDtypeStruct((nblk, 8, LANES), F32)],
        compiler_params=_cparams(("parallel",)),
    )(a, b, x, gate1, w_out, g.reshape(1, D_MODEL), sh, sc, w, bias, tri)


MOE_TM = 1024


def _moe_kernel(cnt_ref, x_ref, h_ref, comb_ref, rank_ref, rank_t_ref, gate_ref, gfin_ref, wg_ref, wu_ref, wd_ref,
                o_ref, acc_scr):
    i = pl.program_id(0)
    e = pl.program_id(1)

    @pl.when(e == 0)
    def _():
        acc_scr[...] = jnp.zeros_like(acc_scr)

    col = lax.broadcasted_iota(jnp.int32, (MOE_R, LANES), 1)
    slot_rows = lax.broadcasted_iota(jnp.int32, (MOE_M, MOE_R), 0).astype(F32)
    slot_cols = lax.broadcasted_iota(jnp.int32, (MOE_R, MOE_M), 1).astype(F32)
    for s in range(MOE_TM // MOE_R):
        blk = slice(s * MOE_R, (s + 1) * MOE_R)
        count = cnt_ref[(i * (MOE_TM // MOE_R) + s) * N_EXPERTS + e]
        for m in range(MOE_R // MOE_M):
            @pl.when(count > m * MOE_M)
            def _():
                take = (rank_t_ref[s, pl.ds(e, 1), :] == slot_rows + float(m * MOE_M))
                hc = _dot(jnp.where(take, 1.0, 0.0).astype(BF16), h_ref[blk, :]).astype(BF16)
                gt = _dot(hc, wg_ref[0])
                act = (gt * _sigmoid(gt)) * _dot(hc, wu_ref[0])
                y = _dot(act.astype(BF16), wd_ref[0]).astype(BF16)
                mine = col == e
                rank_e = jnp.sum(jnp.where(mine, rank_ref[blk, :], 0.0), axis=-1, keepdims=True)
                ce = jnp.sum(jnp.where(mine, comb_ref[blk, :], 0.0), axis=-1, keepdims=True)
                put = rank_e == slot_cols + float(m * MOE_M)
                acc_scr[blk, :] += ce * _dot(jnp.where(put, 1.0, 0.0).astype(BF16), y)

    @pl.when(e == pl.num_programs(1) - 1)
    def _():
        y = x_ref[...] + gate_ref[0] * acc_scr[...]
        ms = jnp.mean(y * y, axis=-1, keepdims=True)
        o_ref[...] = y * lax.rsqrt(ms + RMS_EPS) * gfin_ref[...]


def moe_residual_norm(x, h, comb, rank, rank_t, counts, gate, g_final, wg, wu, wd, rows_per_set):
    rows = x.shape[0]
    tm = MOE_TM
    sub = tm // MOE_R
    si = _set_index(tm, rows_per_set)
    cnt = counts[:, 0, :N_EXPERTS].astype(jnp.int32).reshape(-1)
    grid_spec = pltpu.PrefetchScalarGridSpec(
        num_scalar_prefetch=1,
        grid=(rows // tm, N_EXPERTS),
        in_specs=[pl.BlockSpec((tm, D_MODEL), lambda i, e, c: (i, 0)),
                  pl.BlockSpec((tm, D_MODEL), lambda i, e, c: (i, 0)),
                  pl.BlockSpec((tm, LANES), lambda i, e, c: (i, 0)),
                  pl.BlockSpec((tm, LANES), lambda i, e, c: (i, 0)),
                  pl.BlockSpec((sub, N_EXPERTS, MOE_R), lambda i, e, c: (i, 0, 0)),
                  pl.BlockSpec((1, 1, D_MODEL), lambda i, e, c: (si(i), 0, 0)),
                  pl.BlockSpec((1, D_MODEL), lambda i, e, c: (0, 0)),
                  pl.BlockSpec((1, D_MODEL, D_FF_EXPERT), lambda i, e, c: (e, 0, 0)),
                  pl.BlockSpec((1, D_MODEL, D_FF_EXPERT), lambda i, e, c: (e, 0, 0)),
                  pl.BlockSpec((1, D_FF_EXPERT, D_MODEL), lambda i, e, c: (e, 0, 0))],
        out_specs=pl.BlockSpec((tm, D_MODEL), lambda i, e, c: (i, 0)),
        scratch_shapes=[pltpu.VMEM((tm, D_MODEL), F32)])
    return pl.pallas_call(
        _moe_kernel, name="moe_experts",
        grid_spec=grid_spec,
        out_shape=jax.ShapeDtypeStruct((rows, D_MODEL), F32),
        compiler_params=_cparams(("parallel", "arbitrary")),
    )(cnt, x, h, comb, rank, rank_t, gate, g_final.reshape(1, D_MODEL), wg, wu, wd)


def kernel(x_prompt, x_sample, cache_na_k, cache_na_v, state_wkv, c, c_ctx, mod_w, mod_b, norm_mix, norm_ffn, norm_final, na_w_in, fourier_w, na_rel_bias, na_w_out, ffn_w_gate, ffn_w_up, ffn_w_down, rw_w_in, pool_w, pool_scale, shift_mu, decay_w0, decay_up, iclr_a0, iclr_up, gate_up, k_k, k_a, r_k, gn_w, gn_b, rw_w_out, router_w, router_b, moe_w_gate, moe_w_up, moe_w_down):
    cond = jnp.concatenate([c_ctx[None, :], c, jnp.zeros((8 - N_SETS, D_MODEL), F32)], axis=0)
    mods = adaln_all(cond, mod_w, mod_b)[:, :N_SETS].reshape(DEPTH, N_SETS, 6, 1, D_MODEL)
    bf = lambda w: w.astype(BF16)

    xp = x_prompt.reshape(P_ROWS, D_MODEL)
    xs = x_sample.reshape(S_ROWS, D_MODEL)
    streams = {"p": (SEQ, P_ROWS, slice(0, 1)), "s": (DEC_SEQ, DEC_SEQ, slice(1, N_SETS))}
    x = {"p": xp, "s": xs}

    splits = ((0, FOURIER_CH), (FOURIER_CH, FOURIER_CH + NA_DIM),
              (FOURIER_CH + NA_DIM, FOURIER_CH + 2 * NA_DIM), (FOURIER_CH + 2 * NA_DIM, FOURIER_CH + 3 * NA_DIM))
    w_in, w_out = bf(na_w_in[0]), bf(na_w_out[0])
    f_bd = bf(_block_diag(fourier_w[0]))
    ffn_w = (bf(ffn_w_gate[0]), bf(ffn_w_up[0]), bf(ffn_w_down[0]))
    ck = cache_na_k[:, 0].reshape(DEC_BATCH * PAST_LEN, NA_DIM)
    cv = cache_na_v[:, 0].reshape(DEC_BATCH * PAST_LEN, NA_DIM)
    for name, (n, rps, sets) in streams.items():
        sh1, sc1, g1, sh2, sc2, g2 = [mods[0, sets, m] for m in range(6)]
        f, q, k, v = modulated_matmul(x[name], norm_mix[0], sh1, sc1, w_in, splits, rps)
        if name == "p":
            attn = context_attention(q, k, v)
            new_k = k.reshape(BATCH, 1, SEQ, NA_HEADS, HEAD_DIM)
            new_v = v.reshape(BATCH, 1, SEQ, NA_HEADS, HEAD_DIM)
        else:
            attn = neighbourhood_attention(q, k, v, ck, cv, _na_bias_table(na_rel_bias[0]))
        x[name] = mixer_ffn_residual(fourier_mix(f, n, f_bd), attn, x[name], g1, w_out,
                                     norm_ffn[0], sh2, sc2, g2, *ffn_w, rps)

    w_in, w_out = bf(rw_w_in[0]), bf(rw_w_out[0])
    p_bd = bf(_block_diag(pool_w[0]))
    moe_w = (bf(moe_w_gate[0]), bf(moe_w_up[0]), bf(moe_w_down[0]))
    rw = (shift_mu[0], k_k[0], k_a[0], r_k[0], decay_w0[0], iclr_a0[0], decay_up[0], iclr_up[0], gate_up[0])
    out = {}
    for name, (n, rps, sets) in streams.items():
        sh1, sc1, g1, sh2, sc2, g2 = [mods[1, sets, m] for m in range(6)]
        pc, z = modulated_matmul(x[name], norm_mix[1], sh1, sc1, w_in, ((0, POOL_CH), (POOL_CH, POOL_CH + RWKV_IN)), rps)
        t1, t2, t3, gate, bonus = rwkv_prep(z, n, *rw)
        if name == "p":
            o_f, st_f = wkv_scan_prompt(t1, t2, t3, 0)
            o_b, st_b = wkv_scan_prompt(t1, t2, t3, 1)
            st = jnp.transpose(jnp.stack([st_f, st_b]), (0, 1, 4, 3, 2))
            st = jnp.transpose(st.reshape(2, BATCH, RWKV_HEADS, HEAD_DIM, HEAD_DIM), (1, 0, 2, 3, 4))
        else:
            o_f, o_b = wkv_scan_sample(t1, t2, t3, _sample_state_lanes(state_wkv[:, 0]))
        mixed = rwkv_post(o_f, o_b, n, gate, bonus, gn_w[0], gn_b[0])
        y, *routed = mixer_residual_router(pool_mix(pc, n, p_bd, pool_scale[0]), mixed, x[name], g1, w_out,
                                           norm_ffn[1], sh2, sc2, router_w[0], router_b[0], rps)
        out[name] = moe_residual_norm(y, *routed, g2, norm_final, *moe_w, rps)

    return (out["p"].reshape(BATCH, SEQ, D_MODEL), out["s"].reshape(DEC_BATCH, DEC_SEQ, D_MODEL),
            new_k, new_v, st[:, None])
```

```python
import functools
import math

import numpy as np
import jax
import jax.numpy as jnp
from jax import lax
from jax.experimental import pallas as pl
from jax.experimental.pallas import tpu as pltpu

F32 = jnp.float32
BF16 = jnp.bfloat16

D_MODEL = 1024
BATCH = 32
SEQ = 256
DEPTH = 2
DEC_BATCH = 2
DEC_SEQ = 1024
PAST_LEN = 512
GRID_W = 64
HEAD_DIM = 64
FOURIER_CH = D_MODEL // 4
FOURIER_GROUPS = 4
FOURIER_GW = FOURIER_CH // FOURIER_GROUPS
NA_DIM = D_MODEL - FOURIER_CH
NA_HEADS = NA_DIM // HEAD_DIM
NA_MAX_ROWS = 8
NA_COLS = 16
POOL_WINDOWS = (2, 4, 8, 16)
POOL_CH = D_MODEL // 4
POOL_GW = POOL_CH // len(POOL_WINDOWS)
RWKV_DIM = D_MODEL - POOL_CH
RWKV_HEADS = RWKV_DIM // HEAD_DIM
DECAY_LORA = 64
ICLR_LORA = 64
GATE_LORA = 128
RWKV_IN = 3 * RWKV_DIM + 2 * DECAY_LORA + 2 * ICLR_LORA + GATE_LORA
D_FF = 2816
N_EXPERTS = 8
D_FF_EXPERT = 1408
RMS_EPS = 1e-6
GN_EPS = 64e-5
L2_EPS = 1e-12
DECAY_SCALE = math.exp(-0.5)
NEG_INF = -1e30

P_ROWS = BATCH * SEQ
S_ROWS = DEC_BATCH * DEC_SEQ
N_ROWS = P_ROWS + S_ROWS
N_SETS = 1 + DEC_BATCH
LANES = 128
VMEM_LIMIT = 56 * 1024 * 1024


def _cparams(sem):
    return pltpu.CompilerParams(dimension_semantics=sem, vmem_limit_bytes=VMEM_LIMIT)


def _sigmoid(x):
    return 0.5 * jnp.tanh(0.5 * x) + 0.5


def _dot(a, b):
    return jnp.dot(a, b, preferred_element_type=F32)


def _dot_nt(a, b):
    return lax.dot_general(a, b, (((1,), (1,)), ((), ())), preferred_element_type=F32)


def _set_index(tm, rows_per_set):
    q = rows_per_set // tm
    return lambda i: i // q


def _modulated(x, g, sh, sc):
    ms = jnp.mean(x * x, axis=-1, keepdims=True)
    return (x * lax.rsqrt(ms + RMS_EPS) * g) * (1.0 + sc) + sh


def _adaln_kernel(c_ref, w_ref, b_ref, o_ref):
    c = c_ref[...]
    s = (c * _sigmoid(c)).astype(BF16)
    o_ref[0] = _dot(s, w_ref[0].astype(BF16)) + b_ref[0]


def adaln_all(cond, mod_w, mod_b):
    tn = 1536
    n = 6 * D_MODEL
    return pl.pallas_call(
        _adaln_kernel, name="adaln",
        grid=(DEPTH, n // tn),
        in_specs=[pl.BlockSpec((8, D_MODEL), lambda l, j: (0, 0)),
                  pl.BlockSpec((1, D_MODEL, tn), lambda l, j: (l, 0, j)),
                  pl.BlockSpec((1, 1, tn), lambda l, j: (l, 0, j))],
        out_specs=pl.BlockSpec((1, 8, tn), lambda l, j: (l, 0, j)),
        out_shape=jax.ShapeDtypeStruct((DEPTH, 8, n), F32),
        compiler_params=_cparams(("parallel", "parallel")),
    )(cond, mod_w, mod_b.reshape(DEPTH, 1, n))


def _modmm_kernel(x_ref, g_ref, sh_ref, sc_ref, w_ref, *o_refs, splits):
    h = _modulated(x_ref[...], g_ref[...], sh_ref[0], sc_ref[0]).astype(BF16)
    for o_ref, (a, b) in zip(o_refs, splits):
        o_ref[...] = _dot(h, w_ref[:, a:b]).astype(o_ref.dtype)


def modulated_matmul(x, g, sh, sc, w, splits, rows_per_set, tm=512):
    rows = x.shape[0]
    n_out = w.shape[1]
    si = _set_index(tm, rows_per_set)
    vec = pl.BlockSpec((1, 1, D_MODEL), lambda i: (si(i), 0, 0))
    return pl.pallas_call(
        functools.partial(_modmm_kernel, splits=splits), name="modulated_matmul",
        grid=(rows // tm,),
        in_specs=[pl.BlockSpec((tm, D_MODEL), lambda i: (i, 0)),
                  pl.BlockSpec((1, D_MODEL), lambda i: (0, 0)),
                  vec, vec,
                  pl.BlockSpec((D_MODEL, n_out), lambda i: (0, 0))],
        out_specs=[pl.BlockSpec((tm, b - a), lambda i: (i, 0)) for a, b in splits],
        out_shape=[jax.ShapeDtypeStruct((rows, b - a), F32) for a, b in splits],
        compiler_params=_cparams(("parallel",)),
    )(x, g.reshape(1, D_MODEL), sh, sc, w)


def _dft_mats(n):
    t = np.arange(n)
    ang = 2.0 * np.pi * ((t[:, None] * t[None, :]) % n) / n
    cn, sn = np.cos(ang) / np.sqrt(n), np.sin(ang) / np.sqrt(n)
    c = np.arange(FOURIER_GW)
    angc = 2.0 * np.pi * ((c[:, None] * c[None, :]) % FOURIER_GW) / FOURIER_GW
    eye = np.eye(FOURIER_GROUPS)
    cc = np.kron(eye, np.cos(angc) / np.sqrt(FOURIER_GW))
    sc = np.kron(eye, np.sin(angc) / np.sqrt(FOURIER_GW))
    as_bf = lambda a: jnp.asarray(a, dtype=F32).astype(BF16)
    return as_bf(cn), as_bf(sn), as_bf(cc), as_bf(sc)


def _fourier_kernel(f_ref, cn_ref, sn_ref, cc_ref, sc_ref, w_ref, o_ref):
    x = f_ref[...].astype(BF16)
    a = _dot(x, cc_ref[...]).astype(BF16)
    b = _dot(x, sc_ref[...]).astype(BF16)
    re = _dot(cn_ref[...], a) - _dot(sn_ref[...], b)
    o_ref[...] = _dot(re.astype(BF16), w_ref[...])


def _block_diag(w):
    g, c, _ = w.shape
    eye = jnp.eye(g, dtype=w.dtype)
    return (eye[:, None, :, None] * w[:, :, None, :]).reshape(g * c, g * c)


def fourier_mix(f, n, w_bd):
    rows = f.shape[0]
    cn, sn, cc, sc = _dft_mats(n)
    full = lambda shape: pl.BlockSpec(shape, lambda b: (0, 0))
    return pl.pallas_call(
        _fourier_kernel, name="fourier",
        grid=(rows // n,),
        in_specs=[pl.BlockSpec((n, FOURIER_CH), lambda b: (b, 0)),
                  full((n, n)), full((n, n)),
                  full((FOURIER_CH, FOURIER_CH)), full((FOURIER_CH, FOURIER_CH)),
                  full((FOURIER_CH, FOURIER_CH))],
        out_specs=pl.BlockSpec((n, FOURIER_CH), lambda b: (b, 0)),
        out_shape=jax.ShapeDtypeStruct((rows, FOURIER_CH), F32),
        compiler_params=_cparams(("parallel",)),
    )(f, cn, sn, cc, sc, w_bd)


def _head_pair(ref, c):
    x = ref[:, c * LANES:(c + 1) * LANES]
    low = lax.broadcasted_iota(jnp.int32, x.shape, 1) < HEAD_DIM
    return jnp.where(low, x, 0.0).astype(BF16), jnp.where(low, 0.0, x).astype(BF16), low


def _value_pair(ref, c):
    x = ref[:, c * LANES:(c + 1) * LANES]
    low = lax.broadcasted_iota(jnp.int32, x.shape, 1) < HEAD_DIM
    return jnp.where(low, x, 1.0).astype(BF16), jnp.where(low, 1.0, x).astype(BF16)


def _normalised_pair(res_even, res_odd, low):
    even = res_even / pltpu.roll(res_even, HEAD_DIM, 1)
    odd = res_odd / pltpu.roll(res_odd, HEAD_DIM, 1)
    return jnp.where(low, even, odd)


def _ctx_attn_kernel(q_ref, k_ref, v_ref, o_ref):
    scale = HEAD_DIM ** -0.5

    def scores(c):
        q_even, q_odd, _ = _head_pair(q_ref, c)
        k = k_ref[:, c * LANES:(c + 1) * LANES].astype(BF16)
        return _dot_nt(q_even, k) * scale, _dot_nt(q_odd, k) * scale

    def softmax_numerator(s):
        return jnp.exp(s - jnp.max(s, axis=-1, keepdims=True)).astype(BF16)

    n_pairs = NA_HEADS // 2
    pending = [scores(0)]
    for c in range(n_pairs):
        if c + 1 < n_pairs:
            pending.append(scores(c + 1))
        s_even, s_odd = pending.pop(0)
        v_even, v_odd = _value_pair(v_ref, c)
        low = lax.broadcasted_iota(jnp.int32, (SEQ, LANES), 1) < HEAD_DIM
        o_ref[:, c * LANES:(c + 1) * LANES] = _normalised_pair(
            _dot(softmax_numerator(s_even), v_even), _dot(softmax_numerator(s_odd), v_odd), low)


def context_attention(q, k, v):
    blk = pl.BlockSpec((SEQ, NA_DIM), lambda b: (b, 0))
    return pl.pallas_call(
        _ctx_attn_kernel, name="ctx_attn",
        grid=(BATCH,),
        in_specs=[blk, blk, blk],
        out_specs=blk,
        out_shape=jax.ShapeDtypeStruct((P_ROWS, NA_DIM), F32),
        compiler_params=_cparams(("parallel",)),
    )(q, k, v)


NA_ROWS = DEC_SEQ // GRID_W
NA_WIN = NA_MAX_ROWS * GRID_W


def _na_bias_table(rel_bias):
    cols = np.arange(GRID_W)
    c0 = np.clip(cols - NA_COLS // 2, 0, GRID_W - NA_COLS)
    col_ok = (cols[None, :] >= c0[:, None]) & (cols[None, :] < c0[:, None] + NA_COLS)
    dc = np.clip(cols[None, :] - cols[:, None] + NA_COLS - 1, 0, 2 * NA_COLS - 2)
    onehot = (dc[None] == np.arange(2 * NA_COLS - 1)[:, None, None]).astype(np.float32)
    toe = jnp.einsum("hrj,jqk->hrqk", rel_bias, jnp.asarray(onehot), precision=lax.Precision.HIGHEST)
    toe = jnp.where(col_ok[None, None], toe, NEG_INF)
    tabs = [jnp.transpose(toe[:, NA_MAX_ROWS - 1 - o: 2 * NA_MAX_ROWS - 1 - o], (0, 2, 1, 3))
            for o in range(NA_MAX_ROWS)]
    return jnp.stack(tabs).reshape(NA_MAX_ROWS, NA_HEADS, GRID_W, NA_WIN)


def _na_row_start(i):
    return jnp.clip(i - NA_MAX_ROWS // 2, 0, NA_ROWS - NA_MAX_ROWS)


def _na_kernel(q_ref, k_ref, v_ref, ck_ref, cv_ref, bias_ref, o_ref):
    scale = HEAD_DIM ** -0.5
    i = pl.program_id(1)
    start = pl.multiple_of(_na_row_start(i) * GRID_W, GRID_W)
    k_win = k_ref.at[pl.ds(start, NA_WIN)]
    v_win = v_ref.at[pl.ds(start, NA_WIN)]

    def scores(c):
        q_even, q_odd, _ = _head_pair(q_ref, c)
        kw = k_win[:, c * LANES:(c + 1) * LANES].astype(BF16)
        ck = ck_ref[:, c * LANES:(c + 1) * LANES].astype(BF16)
        return [(_dot_nt(q, kw) * scale + bias_ref[0, 2 * c + par], _dot_nt(q, ck) * scale)
                for par, q in enumerate((q_even, q_odd))]

    def weighted_values(s, vw, cv):
        s_loc, s_ctx = s
        m = jnp.maximum(jnp.max(s_loc, axis=-1, keepdims=True), jnp.max(s_ctx, axis=-1, keepdims=True))
        return _dot(jnp.exp(s_loc - m).astype(BF16), vw) + _dot(jnp.exp(s_ctx - m).astype(BF16), cv)

    n_pairs = NA_HEADS // 2
    pending = [scores(0)]
    for c in range(n_pairs):
        if c + 1 < n_pairs:
            pending.append(scores(c + 1))
        s_even, s_odd = pending.pop(0)
        vw_even, vw_odd = _value_pair(v_win, c)
        cv_even, cv_odd = _value_pair(cv_ref, c)
        low = lax.broadcasted_iota(jnp.int32, (GRID_W, LANES), 1) < HEAD_DIM
        o_ref[:, c * LANES:(c + 1) * LANES] = _normalised_pair(
            weighted_values(s_even, vw_even, cv_even), weighted_values(s_odd, vw_odd, cv_odd), low)


def neighbourhood_attention(q, k, v, ck, cv, bias_tab):
    seq = pl.BlockSpec((DEC_SEQ, NA_DIM), lambda b, i: (b, 0))
    ctx = pl.BlockSpec((PAST_LEN, NA_DIM), lambda b, i: (b, 0))
    row = pl.BlockSpec((GRID_W, NA_DIM), lambda b, i: (b * NA_ROWS + i, 0))
    return pl.pallas_call(
        _na_kernel, name="na_attn",
        grid=(DEC_BATCH, NA_ROWS),
        in_specs=[row, seq, seq, ctx, ctx,
                  pl.BlockSpec((1, NA_HEADS, GRID_W, NA_WIN), lambda b, i: (i - _na_row_start(i), 0, 0, 0))],
        out_specs=row,
        out_shape=jax.ShapeDtypeStruct((S_ROWS, NA_DIM), F32),
        compiler_params=_cparams(("parallel", "arbitrary")),
    )(q, k, v, ck, cv, bias_tab)


def _mixer_residual(a_ref, b_ref, x_ref, gate_ref, wo_ref):
    na = a_ref.shape[1]
    y = _dot(a_ref[...].astype(BF16), wo_ref[:na, :]) + _dot(b_ref[...].astype(BF16), wo_ref[na:, :])
    return x_ref[...] + gate_ref[0] * y


def _ffn_kernel(a_ref, b_ref, x_ref, gate1_ref, wo_ref, g_ref, sh_ref, sc_ref, gate2_ref, wg_ref, wu_ref, wd_ref,
                o_ref, y_scr, h_scr, acc_scr):
    j = pl.program_id(1)

    @pl.when(j == 0)
    def _():
        y = _mixer_residual(a_ref, b_ref, x_ref, gate1_ref, wo_ref)
        y_scr[...] = y
        h_scr[...] = _modulated(y, g_ref[...], sh_ref[0], sc_ref[0]).astype(BF16)
        acc_scr[...] = jnp.zeros_like(acc_scr)

    h = h_scr[...]
    gt = _dot(h, wg_ref[...])
    act = (gt * _sigmoid(gt)) * _dot(h, wu_ref[...])
    acc_scr[...] += _dot(act.astype(BF16), wd_ref[...])

    @pl.when(j == pl.num_programs(1) - 1)
    def _():
        o_ref[...] = y_scr[...] + gate2_ref[0] * acc_scr[...]


def mixer_ffn_residual(a, b, x, gate1, w_out, g, sh, sc, gate2, wg, wu, wd, rows_per_set, tm=512, tf=D_FF // 2):
    rows = x.shape[0]
    na, nb = a.shape[1], b.shape[1]
    si = _set_index(tm, rows_per_set)
    vec = pl.BlockSpec((1, 1, D_MODEL), lambda i, j: (si(i), 0, 0))
    row = lambda n: pl.BlockSpec((tm, n), lambda i, j: (i, 0))
    return pl.pallas_call(
        _ffn_kernel, name="ffn",
        grid=(rows // tm, D_FF // tf),
        in_specs=[row(na), row(nb), row(D_MODEL), vec,
                  pl.BlockSpec((na + nb, D_MODEL), lambda i, j: (0, 0)),
                  pl.BlockSpec((1, D_MODEL), lambda i, j: (0, 0)),
                  vec, vec, vec,
                  pl.BlockSpec((D_MODEL, tf), lambda i, j: (0, j)),
                  pl.BlockSpec((D_MODEL, tf), lambda i, j: (0, j)),
                  pl.BlockSpec((tf, D_MODEL), lambda i, j: (j, 0))],
        out_specs=row(D_MODEL),
        out_shape=jax.ShapeDtypeStruct((rows, D_MODEL), F32),
        scratch_shapes=[pltpu.VMEM((tm, D_MODEL), F32), pltpu.VMEM((tm, D_MODEL), BF16),
                        pltpu.VMEM((tm, D_MODEL), F32)],
        compiler_params=_cparams(("parallel", "arbitrary")),
    )(a, b, x, gate1, w_out, g.reshape(1, D_MODEL), sh, sc, gate2, wg, wu, wd)


def _pool_consts(n):
    t = np.arange(n)
    mats, cnts = [], []
    for win in POOL_WINDOWS:
        lo = np.clip(t - win // 2, 0, n)
        hi = np.clip(t + win - win // 2, 0, n)
        mats.append(((t[None, :] >= lo[:, None]) & (t[None, :] < hi[:, None])).astype(np.float32))
        cnts.append(np.repeat((hi - lo).astype(np.float32)[:, None], POOL_GW, axis=1))
    return jnp.asarray(np.stack(mats)).astype(BF16), jnp.asarray(np.concatenate(cnts, axis=1))


def _pool_kernel(x_ref, pm_ref, cnt_ref, w_ref, scale_ref, o_ref):
    x = x_ref[...]
    hi = x.astype(BF16)
    lo = (x - hi.astype(F32)).astype(BF16)
    sums = []
    for g in range(len(POOL_WINDOWS)):
        sl = slice(g * POOL_GW, (g + 1) * POOL_GW)
        sums.append(_dot(pm_ref[g], hi[:, sl]) + _dot(pm_ref[g], lo[:, sl]))
    y = jnp.concatenate(sums, axis=-1) / cnt_ref[...] - x
    o_ref[...] = _dot(y.astype(BF16), w_ref[...]) * scale_ref[...]


def pool_mix(x, n, w_bd, scale):
    rows = x.shape[0]
    pm, cnt = _pool_consts(n)
    return pl.pallas_call(
        _pool_kernel, name="pool",
        grid=(rows // n,),
        in_specs=[pl.BlockSpec((n, POOL_CH), lambda b: (b, 0)),
                  pl.BlockSpec((len(POOL_WINDOWS), n, n), lambda b: (0, 0, 0)),
                  pl.BlockSpec((n, POOL_CH), lambda b: (0, 0)),
                  pl.BlockSpec((POOL_CH, POOL_CH), lambda b: (0, 0)),
                  pl.BlockSpec((1, POOL_CH), lambda b: (0, 0))],
        out_specs=pl.BlockSpec((n, POOL_CH), lambda b: (b, 0)),
        out_shape=jax.ShapeDtypeStruct((rows, POOL_CH), F32),
        compiler_params=_cparams(("parallel",)),
    )(x, pm, cnt, w_bd, scale.reshape(1, POOL_CH))


RW_TILE = 256
HALO = 8
SUB = 8
PACK_R = RWKV_HEADS * SUB


def _head_ones():
    h = np.arange(RWKV_DIM) // HEAD_DIM
    return jnp.asarray((h[:, None] == h[None, :]).astype(np.float32)).astype(BF16)


def _head_sum(x, ones):
    hi = x.astype(BF16)
    lo = (x - hi.astype(F32)).astype(BF16)
    return _dot(hi, ones) + _dot(lo, ones)


def _pack_heads(a, b, o_ref, lead):
    n = a.shape[0]
    lane = lax.broadcasted_iota(jnp.int32, (n, LANES), 1)
    low = lane < HEAD_DIM
    for c in range(RWKV_DIM // LANES):
        ac = a[:, c * LANES:(c + 1) * LANES]
        bc = b[:, c * LANES:(c + 1) * LANES]
        even = jnp.where(low, ac, pltpu.roll(bc, HEAD_DIM, 1))
        odd = jnp.where(low, pltpu.roll(ac, HEAD_DIM, 1), bc)
        for h, val in ((2 * c, even), (2 * c + 1, odd)):
            o_ref[lead + (slice(None), slice(h * SUB, (h + 1) * SUB), slice(None))] = val.reshape(n // SUB, SUB, LANES)


def _rwkv_prep_kernel(z_ref, zp_ref, zn_ref, mu_ref, kk_w_ref, ka_ref, rk_ref, w0_ref, a0_ref,
                      dup_ref, iup_ref, gup_ref, ones_ref,
                      g1_ref, g2_ref, g3_ref, g_ref, bonus_ref, *, tiles_per_seq):
    i = pl.program_id(0)
    pos = i % tiles_per_seq
    z = z_ref[...]
    row = lax.broadcasted_iota(jnp.int32, (HALO, 1), 0)
    prev_edge = jnp.where(pos == 0, 0.0, zp_ref[HALO - 1:HALO, :])
    next_edge = jnp.where(pos == tiles_per_seq - 1, 0.0, zn_ref[0:1, :])
    prev = pltpu.roll(z, 1, 0)
    prev = jnp.concatenate([jnp.where(row == 0, prev_edge, prev[:HALO]), prev[HALO:]], axis=0)
    nxt = pltpu.roll(z, RW_TILE - 1, 0)
    nxt = jnp.concatenate([nxt[:-HALO], jnp.where(row == HALO - 1, next_edge, nxt[-HALO:])], axis=0)
    mu_prev, mu_next = mu_ref[0:1, :], mu_ref[1:2, :]
    zr = z * (1.0 - mu_prev - mu_next) + mu_prev * prev + mu_next * nxt

    d = RWKV_DIM
    r, k, v = zr[:, :d], zr[:, d:2 * d], zr[:, 2 * d:3 * d]
    lora = 3 * d
    ones = ones_ref[...]
    kk = k * kk_w_ref[...]
    kk = kk * lax.rsqrt(_head_sum(kk * kk, ones) + L2_EPS)
    _pack_heads(r, v, g3_ref, ())
    for dr in range(2):
        wl = zr[:, lora + dr * DECAY_LORA: lora + (dr + 1) * DECAY_LORA]
        al = zr[:, lora + 2 * DECAY_LORA + dr * ICLR_LORA: lora + 2 * DECAY_LORA + (dr + 1) * ICLR_LORA]
        lw = w0_ref[dr:dr + 1, :] + _dot(jnp.tanh(wl).astype(BF16), dup_ref[dr])
        w = jnp.exp(-DECAY_SCALE * _sigmoid(lw))
        a = _sigmoid(a0_ref[dr:dr + 1, :] + _dot(al.astype(BF16), iup_ref[dr]))
        _pack_heads(w, kk * a, g1_ref, (dr,))
        _pack_heads(k * (1.0 + (a - 1.0) * ka_ref[...]), kk, g2_ref, (dr,))
    gl = zr[:, lora + 2 * DECAY_LORA + 2 * ICLR_LORA:]
    g_ref[...] = _dot(_sigmoid(gl).astype(BF16), gup_ref[...])
    bonus_ref[...] = _head_sum(r * k * rk_ref[...], ones) * v


def rwkv_prep(z, n, mu, k_k, k_a, r_k, w0, a0, dup, iup, gup):
    rows = z.shape[0]
    nb = rows // n
    tps = n // RW_TILE
    hb = RW_TILE // HALO
    last = rows // HALO - 1
    d = RWKV_DIM
    full2 = lambda shape: pl.BlockSpec(shape, lambda i: (0, 0))
    full3 = lambda shape: pl.BlockSpec(shape, lambda i: (0, 0, 0))
    tile = pl.BlockSpec((RW_TILE, d), lambda i: (i, 0))
    pk2 = pl.BlockSpec((2, RW_TILE // SUB, PACK_R, LANES), lambda i: (0, i % tps, i // tps, 0))
    pk1 = pl.BlockSpec((RW_TILE // SUB, PACK_R, LANES), lambda i: (i % tps, i // tps, 0))
    return pl.pallas_call(
        functools.partial(_rwkv_prep_kernel, tiles_per_seq=tps), name="rwkv_prep",
        grid=(rows // RW_TILE,),
        in_specs=[pl.BlockSpec((RW_TILE, RWKV_IN), lambda i: (i, 0)),
                  pl.BlockSpec((HALO, RWKV_IN), lambda i: (jnp.maximum(i * hb - 1, 0), 0)),
                  pl.BlockSpec((HALO, RWKV_IN), lambda i: (jnp.minimum((i + 1) * hb, last), 0)),
                  full2((2, RWKV_IN)), full2((1, d)), full2((1, d)), full2((1, d)),
                  full2((2, d)), full2((2, d)),
                  full3((2, DECAY_LORA, d)), full3((2, ICLR_LORA, d)), full2((GATE_LORA, d)),
                  full2((d, d))],
        out_specs=[pk2, pk2, pk1, tile, tile],
        out_shape=[jax.ShapeDtypeStruct((2, n // SUB, nb * PACK_R, LANES), F32),
                   jax.ShapeDtypeStruct((2, n // SUB, nb * PACK_R, LANES), F32),
                   jax.ShapeDtypeStruct((n // SUB, nb * PACK_R, LANES), F32),
                   jax.ShapeDtypeStruct((rows, d), F32),
                   jax.ShapeDtypeStruct((rows, d), F32)],
        compiler_params=_cparams(("parallel",)),
    )(z, z, z, mu, k_k.reshape(1, d), k_a.reshape(1, d), r_k.reshape(1, d), w0, a0,
      dup.astype(BF16), iup.astype(BF16), gup.astype(BF16), _head_ones())


SCAN_TC = 64
SCAN_G = SCAN_TC // SUB
SLOTS = 4
V_BLOCK = 32
K_CHUNK = 16
N_KC = HEAD_DIM // K_CHUNK
PEEL = 2


def _wkv_first_sa(s_scr, t2, vs):
    sas = []
    for vb in range(vs // V_BLOCK):
        rows = slice(vb * V_BLOCK, (vb + 1) * V_BLOCK)

        def chunk(kc, sa):
            for j in range(K_CHUNK):
                sa = sa + s_scr[kc * K_CHUNK + j, rows, :] * t2[N_KC + kc, j:j + 1, :]
            return sa

        sas.append(lax.fori_loop(0, N_KC, chunk, jnp.zeros((V_BLOCK, LANES), F32)))
    return tuple(sas)


def _wkv_step(s_scr, t1, t2, t3, t2_next, v_blocks, o_ref, sas):
    nxt = []
    for vb, v_blk in enumerate(v_blocks):
        rows = slice(vb * V_BLOCK, (vb + 1) * V_BLOCK)
        sa = sas[vb]

        def chunk(kc, carry):
            o, sa_n = carry
            for j in range(K_CHUNK):
                k = kc * K_CHUNK + j
                s_new = s_scr[k, rows, :] - sa * t1[N_KC + kc, j:j + 1, :] + v_blk * t2[kc, j:j + 1, :]
                s_scr[k, rows, :] = s_new
                o = o + s_new * t3[kc, j:j + 1, :]
                sa_n = sa_n + s_new * t2_next[N_KC + kc, j:j + 1, :]
            return o, sa_n

        zero = jnp.zeros((V_BLOCK, LANES), F32)
        carry = (zero, zero)
        for kc in range(PEEL):
            carry = chunk(kc, carry)
        o, sa_n = lax.fori_loop(PEEL, N_KC, chunk, carry)
        o_ref[rows, :] = o
        nxt.append(sa_n)
    return tuple(nxt)


def _store_tile(ref, idx, x):
    ref[idx] = x.reshape(2 * N_KC, K_CHUNK, LANES)


def _store_scaled(tile_ref, t1, t2, t3, g, advance):
    d = HEAD_DIM
    g_new = g * t1[:d]
    inv = 1.0 / g_new
    _store_tile(tile_ref, 0, jnp.concatenate([t1[:d], t1[d:] * inv], axis=0))
    _store_tile(tile_ref, 1, jnp.concatenate([t2[:d] * inv, t2[d:] * g], axis=0))
    _store_tile(tile_ref, 2, jnp.concatenate([t3[:d] * g_new, t3[d:]], axis=0))
    return g_new if advance is True else jnp.where(advance, g_new, g)


def _unscale_state(s_scr, g_scr, g):
    g_scr[...] = g.reshape(N_KC, K_CHUNK, LANES)

    def chunk(kc, carry):
        for j in range(K_CHUNK):
            k = kc * K_CHUNK + j
            s_scr[k] = s_scr[k] * g_scr[kc, j:j + 1, :]
        return carry

    lax.fori_loop(0, N_KC, chunk, 0)


def _step_rows(ref, lead, grp, sub, n):
    return ref.at[lead + (grp,)][pl.ds(sub, n, stride=SUB), :]


def _scan_prompt_kernel(g1_ref, g2_ref, g3_ref, o_ref, st_ref, s_scr, g_scr, ta, tb, tc, td, oa, ob, *, reverse):
    c = pl.program_id(1)
    tiles = (ta, tb, tc, td)
    outs = (oa, ob)

    @pl.when(c == 0)
    def _():
        s_scr[...] = jnp.zeros_like(s_scr)

    for o_scr in outs:
        o_scr[...] = jnp.zeros_like(o_scr)

    def where(grp, sub):
        return (SCAN_G - 1 - grp, SUB - 1 - sub) if reverse else (grp, sub)

    def load_tiles(grp, sub, slot, dec, advance=True):
        g, s = where(grp, sub)
        return _store_scaled(tiles[slot], _step_rows(g1_ref, (0,), g, s, LANES).T,
                             _step_rows(g2_ref, (0,), g, s, LANES).T,
                             _step_rows(g3_ref, (), g, s, LANES).T, dec, advance)

    def flush(grp, sub, o_scr):
        g, s = where(grp, sub)
        o_ref.at[0, g][pl.ds(s, LANES, stride=SUB), :] = o_scr[...].T

    dec0 = load_tiles(0, 0, 0, jnp.ones((HEAD_DIM, LANES), F32))
    dec0 = load_tiles(0, 1, 1, dec0)
    sas0 = _wkv_first_sa(s_scr, tiles[0].at[1], HEAD_DIM)
    per_blk = V_BLOCK // K_CHUNK

    def group(grp, carry):
        sas, dec = carry
        for j in range(SUB):
            cur, nxt = tiles[j % SLOTS], tiles[(j + 1) % SLOTS]
            ahead = j + 2
            if ahead < SUB:
                dec = load_tiles(grp, ahead, ahead % SLOTS, dec)
            else:
                dec = load_tiles(jnp.minimum(grp + 1, SCAN_G - 1), ahead % SUB, ahead % SLOTS, dec,
                                 advance=grp + 1 < SCAN_G)
            if j > 0:
                flush(grp, j - 1, outs[(j - 1) % 2])
            v_blocks = [cur[2, N_KC + vb * per_blk: N_KC + (vb + 1) * per_blk].reshape(V_BLOCK, LANES)
                        for vb in range(HEAD_DIM // V_BLOCK)]
            sas = _wkv_step(s_scr, cur.at[0], cur.at[1], cur.at[2], nxt.at[1], v_blocks, outs[j % 2], sas)
        flush(grp, SUB - 1, outs[(SUB - 1) % 2])
        return sas, dec

    _, dec = lax.fori_loop(0, SCAN_G, group, (sas0, dec0))
    _unscale_state(s_scr, g_scr, dec)

    @pl.when(c == pl.num_programs(1) - 1)
    def _():
        st_ref[0] = s_scr[...]


def wkv_scan_prompt(g1, g2, g3, direction):
    n_grp, rows = g3.shape[0], g3.shape[1]
    groups = rows // (LANES * SUB)
    nblk = n_grp // SCAN_G
    tb = (lambda s: nblk - 1 - s) if direction else (lambda s: s)
    blk = (SCAN_G, LANES * SUB, LANES)
    dir_blk = pl.BlockSpec((1,) + blk, lambda g, s: (direction, tb(s), g, 0))
    return pl.pallas_call(
        functools.partial(_scan_prompt_kernel, reverse=bool(direction)), name="wkv_scan_prompt",
        grid=(groups, nblk),
        in_specs=[dir_blk, dir_blk, pl.BlockSpec(blk, lambda g, s: (tb(s), g, 0))],
        out_specs=[pl.BlockSpec((1,) + blk, lambda g, s: (0, tb(s), g, 0)),
                   pl.BlockSpec((1, HEAD_DIM, HEAD_DIM, LANES), lambda g, s: (g, 0, 0, 0))],
        out_shape=[jax.ShapeDtypeStruct((1, n_grp, rows, LANES), F32),
                   jax.ShapeDtypeStruct((groups, HEAD_DIM, HEAD_DIM, LANES), F32)],
        scratch_shapes=([pltpu.VMEM((HEAD_DIM, HEAD_DIM, LANES), F32),
                         pltpu.VMEM((N_KC, K_CHUNK, LANES), F32)]
                        + [pltpu.VMEM((3, 2 * N_KC, K_CHUNK, LANES), F32)] * SLOTS
                        + [pltpu.VMEM((LANES, LANES), F32)] * 2),
        compiler_params=_cparams(("parallel", "arbitrary")),
    )(g1, g2, g3)


S_CHAINS = DEC_BATCH * RWKV_HEADS
S_VS = HEAD_DIM // 2


def _scan_sample_kernel(g1f_ref, g1b_ref, g2f_ref, g2b_ref, g3f_ref, g3b_ref, s0_ref,
                        of_ref, ob_ref, s_scr, g_scr, ta, tb, tc, td, va, vb, vc, vd, oa, ob):
    c = pl.program_id(0)
    tiles = (ta, tb, tc, td)
    vals = (va, vb, vc, vd)
    outs = (oa, ob)

    @pl.when(c == 0)
    def _():
        s_scr[...] = s0_ref[...]

    nc = 2 * S_CHAINS
    n_pad = LANES - 2 * nc
    zpad = jnp.zeros((n_pad, LANES), F32)
    wpad = jnp.where(lax.broadcasted_iota(jnp.int32, (n_pad, LANES), 1) < HEAD_DIM, 1.0, 0.0)
    lane = lax.broadcasted_iota(jnp.int32, (S_VS, LANES), 1)

    def stacked_t(f_ref, b_ref, grp, sub, pad):
        f = _step_rows(f_ref, (0,), grp, sub, S_CHAINS)
        b = _step_rows(b_ref, (0,), SCAN_G - 1 - grp, SUB - 1 - sub, S_CHAINS)
        return jnp.concatenate([f, b, f, b, pad], axis=0).T

    def load_tiles(grp, sub, slot, dec, advance=True):
        t3 = stacked_t(g3f_ref, g3b_ref, grp, sub, zpad)
        vals[slot][...] = jnp.where(lane < nc, t3[HEAD_DIM:HEAD_DIM + S_VS], t3[HEAD_DIM + S_VS:])
        return _store_scaled(tiles[slot], stacked_t(g1f_ref, g1b_ref, grp, sub, wpad),
                             stacked_t(g2f_ref, g2b_ref, grp, sub, zpad), t3, dec, advance)

    def flush(grp, sub, o_scr):
        o = o_scr[...]
        full = jnp.concatenate([o, pltpu.roll(o, LANES - nc, 1), jnp.zeros((LANES - HEAD_DIM, LANES), F32)], axis=0)
        ot = full.T
        of_ref.at[0, grp][pl.ds(sub, S_CHAINS, stride=SUB), :] = ot[0:S_CHAINS]
        ob_ref.at[0, SCAN_G - 1 - grp][pl.ds(SUB - 1 - sub, S_CHAINS, stride=SUB), :] = ot[S_CHAINS:nc]

    dec0 = load_tiles(0, 0, 0, jnp.ones((HEAD_DIM, LANES), F32))
    dec0 = load_tiles(0, 1, 1, dec0)
    sas0 = _wkv_first_sa(s_scr, tiles[0].at[1], S_VS)

    def group(grp, carry):
        sas, dec = carry
        for j in range(SUB):
            cur, nxt = tiles[j % SLOTS], tiles[(j + 1) % SLOTS]
            ahead = j + 2
            if ahead < SUB:
                dec = load_tiles(grp, ahead, ahead % SLOTS, dec)
            else:
                dec = load_tiles(jnp.minimum(grp + 1, SCAN_G - 1), ahead % SUB, ahead % SLOTS, dec,
                                 advance=grp + 1 < SCAN_G)
            if j > 0:
                flush(grp, j - 1, outs[(j - 1) % 2])
            sas = _wkv_step(s_scr, cur.at[0], cur.at[1], cur.at[2], nxt.at[1], [vals[j % SLOTS][...]],
                            outs[j % 2], sas)
        flush(grp, SUB - 1, outs[(SUB - 1) % 2])
        return sas, dec

    _, dec = lax.fori_loop(0, SCAN_G, group, (sas0, dec0))
    _unscale_state(s_scr, g_scr, dec)


def wkv_scan_sample(g1, g2, g3, s0):
    n_grp, rows = g3.shape[0], g3.shape[1]
    g3 = g3.reshape(1, n_grp, rows, LANES)
    nblk = n_grp // SCAN_G
    blk = (1, SCAN_G, rows, LANES)
    fwd = lambda d: pl.BlockSpec(blk, lambda s: (d, s, 0, 0))
    bwd = lambda d: pl.BlockSpec(blk, lambda s: (d, nblk - 1 - s, 0, 0))
    return pl.pallas_call(
        _scan_sample_kernel, name="wkv_scan_sample",
        grid=(nblk,),
        in_specs=[fwd(0), bwd(1), fwd(0), bwd(1), fwd(0), bwd(0),
                  pl.BlockSpec((HEAD_DIM, S_VS, LANES), lambda s: (0, 0, 0))],
        out_specs=[fwd(0), bwd(0)],
        out_shape=[jax.ShapeDtypeStruct((1, n_grp, rows, LANES), F32)] * 2,
        scratch_shapes=([pltpu.VMEM((HEAD_DIM, S_VS, LANES), F32),
                         pltpu.VMEM((N_KC, K_CHUNK, LANES), F32)]
                        + [pltpu.VMEM((3, 2 * N_KC, K_CHUNK, LANES), F32)] * SLOTS
                        + [pltpu.VMEM((S_VS, LANES), F32)] * SLOTS
                        + [pltpu.VMEM((S_VS, LANES), F32)] * 2),
        compiler_params=_cparams(("arbitrary",)),
    )(g1, g1, g2, g2, g3, g3, s0)


def _sample_state_lanes(s0):
    nc = 2 * S_CHAINS
    st = jnp.transpose(s0, (4, 3, 1, 0, 2)).reshape(HEAD_DIM, HEAD_DIM, nc)
    st = jnp.concatenate([st[:, :S_VS], st[:, S_VS:]], axis=-1)
    return jnp.pad(st, ((0, 0), (0, 0), (0, LANES - 2 * nc)))


def _rwkv_post_kernel(of_ref, ob_ref, g_ref, bonus_ref, gw_ref, gb_ref, ones_ref, y_ref):
    head_sum = functools.partial(_head_sum, ones=ones_ref[...])
    lane = lax.broadcasted_iota(jnp.int32, (RW_TILE, LANES), 1)
    low = lane < HEAD_DIM

    def head(h):
        rows = slice(h * SUB, (h + 1) * SUB)
        return (of_ref[0, :, rows, :] + ob_ref[0, :, rows, :]).reshape(RW_TILE, LANES)

    cols = [jnp.where(low, head(2 * c), pltpu.roll(head(2 * c + 1), HEAD_DIM, 1))
            for c in range(RWKV_DIM // LANES)]
    o = jnp.concatenate(cols, axis=-1)
    mu = head_sum(o) / HEAD_DIM
    oc = o - mu
    var = head_sum(oc * oc) / HEAD_DIM
    on = (oc * lax.rsqrt(var + GN_EPS)) * gw_ref[...] + gb_ref[...]
    y_ref[...] = (on + bonus_ref[...]) * g_ref[...]


def rwkv_post(o_f, o_b, n, g, bonus, gn_w, gn_b):
    rows, d = g.shape
    tps = n // RW_TILE
    tile = pl.BlockSpec((RW_TILE, d), lambda i: (i, 0))
    vec = pl.BlockSpec((1, d), lambda i: (0, 0))
    pk = pl.BlockSpec((1, RW_TILE // SUB, PACK_R, LANES), lambda i: (0, i % tps, i // tps, 0))
    return pl.pallas_call(
        _rwkv_post_kernel, name="rwkv_post",
        grid=(rows // RW_TILE,),
        in_specs=[pk, pk, tile, tile, vec, vec, pl.BlockSpec((d, d), lambda i: (0, 0))],
        out_specs=tile,
        out_shape=jax.ShapeDtypeStruct((rows, d), F32),
        compiler_params=_cparams(("parallel",)),
    )(o_f, o_b, g, bonus, gn_w.reshape(1, d), gn_b.reshape(1, d), _head_ones())


MOE_R = 512
MOE_M = 128


def _split3(x):
    a = x.astype(BF16)
    r1 = x - a.astype(F32)
    b = r1.astype(BF16)
    c = (r1 - b.astype(F32)).astype(BF16)
    return a, b, c


def _router_kernel(mix_a_ref, mix_b_ref, x_ref, gate1_ref, wo_ref, g_ref, sh_ref, sc_ref, w_ref, b_ref, tri_ref,
                   y_ref, h_ref, comb_ref, rank_ref, rank_t_ref, cnt_ref):
    y = _mixer_residual(mix_a_ref, mix_b_ref, x_ref, gate1_ref, wo_ref)
    y_ref[...] = y
    h = _modulated(y, g_ref[...], sh_ref[0], sc_ref[0])
    h_ref[...] = h.astype(BF16)
    h1, h2, h3 = _split3(h)
    w1, w2, w3 = _split3(w_ref[...])
    logits = (_dot(h1, w1) + (_dot(h1, w2) + _dot(h2, w1))
              + (_dot(h1, w3) + _dot(h2, w2) + _dot(h3, w1))) + b_ref[...]
    col = lax.broadcasted_iota(jnp.int32, logits.shape, 1)
    logits = jnp.where(col < N_EXPERTS, logits, -jnp.inf)
    m1 = jnp.max(logits, axis=-1, keepdims=True)
    i1 = jnp.min(jnp.where(logits == m1, col, LANES), axis=-1, keepdims=True)
    rest = jnp.where(col == i1, -jnp.inf, logits)
    m2 = jnp.max(rest, axis=-1, keepdims=True)
    i2 = jnp.min(jnp.where(rest == m2, col, LANES), axis=-1, keepdims=True)
    e2 = jnp.exp(m2 - m1)
    den = 1.0 + e2
    comb_ref[...] = jnp.where(col == i1, 1.0 / den, 0.0) + jnp.where(col == i2, e2 / den, 0.0)
    chosen = (col == i1) | (col == i2)
    upto = _dot(tri_ref[...], jnp.where(chosen, 1.0, 0.0).astype(BF16))
    rank = jnp.where(chosen, upto - 1.0, -1.0)
    rank_ref[...] = rank
    rank_t_ref[0] = rank.T[0:N_EXPERTS, :]
    cnt_ref[0] = jnp.broadcast_to(upto[MOE_R - 1:MOE_R, :], (8, LANES))


def mixer_residual_router(a, b, x, gate1, w_out, g, sh, sc, router_w, router_b, rows_per_set):
    rows = x.shape[0]
    na, nb = a.shape[1], b.shape[1]
    tm = MOE_R
    nblk = rows // tm
    si = _set_index(tm, rows_per_set)
    vec = pl.BlockSpec((1, 1, D_MODEL), lambda i: (si(i), 0, 0))
    row = lambda n: pl.BlockSpec((tm, n), lambda i: (i, 0))
    w = jnp.pad(router_w, ((0, 0), (0, LANES - N_EXPERTS)))
    bias = jnp.pad(router_b, (0, LANES - N_EXPERTS)).reshape(1, LANES)
    tri = jnp.asarray(np.tril(np.ones((tm, tm), np.float32))).astype(BF16)
    return pl.pallas_call(
        _router_kernel, name="moe_router",
        grid=(nblk,),
        in_specs=[row(na), row(nb), row(D_MODEL), vec,
                  pl.BlockSpec((na + nb, D_MODEL), lambda i: (0, 0)),
                  pl.BlockSpec((1, D_MODEL), lambda i: (0, 0)),
                  vec, vec,
                  pl.BlockSpec((D_MODEL, LANES), lambda i: (0, 0)),
                  pl.BlockSpec((1, LANES), lambda i: (0, 0)),
                  pl.BlockSpec((tm, tm), lambda i: (0, 0))],
        out_specs=[row(D_MODEL), row(D_MODEL), row(LANES), row(LANES),
                   pl.BlockSpec((1, N_EXPERTS, tm), lambda i: (i, 0, 0)),
                   pl.BlockSpec((1, 8, LANES), lambda i: (i, 0, 0))],
        out_shape=[jax.ShapeDtypeStruct((rows, D_MODEL), F32),
                   jax.ShapeDtypeStruct((rows, D_MODEL), BF16),
                   jax.ShapeDtypeStruct((rows, LANES), F32),
                   jax.ShapeDtypeStruct((rows, LANES), F32),
                   jax.ShapeDtypeStruct((nblk, N_EXPERTS, tm), F32),
                   jax.ShapeDtypeStruct((nblk, 8, LANES), F32)],
        compiler_params=_cparams(("parallel",)),
    )(a, b, x, gate1, w_out, g.reshape(1, D_MODEL), sh, sc, w, bias, tri)


MOE_TM = 1024


def _moe_kernel(cnt_ref, x_ref, h_ref, comb_ref, rank_ref, rank_t_ref, gate_ref, gfin_ref, wg_ref, wu_ref, wd_ref,
                o_ref, acc_scr):
    i = pl.program_id(0)
    e = pl.program_id(1)

    @pl.when(e == 0)
    def _():
        acc_scr[...] = jnp.zeros_like(acc_scr)

    col = lax.broadcasted_iota(jnp.int32, (MOE_R, LANES), 1)
    slot_rows = lax.broadcasted_iota(jnp.int32, (MOE_M, MOE_R), 0).astype(F32)
    slot_cols = lax.broadcasted_iota(jnp.int32, (MOE_R, MOE_M), 1).astype(F32)
    for s in range(MOE_TM // MOE_R):
        blk = slice(s * MOE_R, (s + 1) * MOE_R)
        count = cnt_ref[(i * (MOE_TM // MOE_R) + s) * N_EXPERTS + e]
        for m in range(MOE_R // MOE_M):
            @pl.when(count > m * MOE_M)
            def _():
                take = (rank_t_ref[s, pl.ds(e, 1), :] == slot_rows + float(m * MOE_M))
                hc = _dot(jnp.where(take, 1.0, 0.0).astype(BF16), h_ref[blk, :]).astype(BF16)
                gt = _dot(hc, wg_ref[0])
                act = (gt * _sigmoid(gt)) * _dot(hc, wu_ref[0])
                y = _dot(act.astype(BF16), wd_ref[0]).astype(BF16)
                mine = col == e
                rank_e = jnp.sum(jnp.where(mine, rank_ref[blk, :], 0.0), axis=-1, keepdims=True)
                ce = jnp.sum(jnp.where(mine, comb_ref[blk, :], 0.0), axis=-1, keepdims=True)
                put = rank_e == slot_cols + float(m * MOE_M)
                acc_scr[blk, :] += ce * _dot(jnp.where(put, 1.0, 0.0).astype(BF16), y)

    @pl.when(e == pl.num_programs(1) - 1)
    def _():
        y = x_ref[...] + gate_ref[0] * acc_scr[...]
        ms = jnp.mean(y * y, axis=-1, keepdims=True)
        o_ref[...] = y * lax.rsqrt(ms + RMS_EPS) * gfin_ref[...]


def moe_residual_norm(x, h, comb, rank, rank_t, counts, gate, g_final, wg, wu, wd, rows_per_set):
    rows = x.shape[0]
    tm = MOE_TM
    sub = tm // MOE_R
    si = _set_index(tm, rows_per_set)
    cnt = counts[:, 0, :N_EXPERTS].astype(jnp.int32).reshape(-1)
    grid_spec = pltpu.PrefetchScalarGridSpec(
        num_scalar_prefetch=1,
        grid=(rows // tm, N_EXPERTS),
        in_specs=[pl.BlockSpec((tm, D_MODEL), lambda i, e, c: (i, 0)),
                  pl.BlockSpec((tm, D_MODEL), lambda i, e, c: (i, 0)),
                  pl.BlockSpec((tm, LANES), lambda i, e, c: (i, 0)),
                  pl.BlockSpec((tm, LANES), lambda i, e, c: (i, 0)),
                  pl.BlockSpec((sub, N_EXPERTS, MOE_R), lambda i, e, c: (i, 0, 0)),
                  pl.BlockSpec((1, 1, D_MODEL), lambda i, e, c: (si(i), 0, 0)),
                  pl.BlockSpec((1, D_MODEL), lambda i, e, c: (0, 0)),
                  pl.BlockSpec((1, D_MODEL, D_FF_EXPERT), lambda i, e, c: (e, 0, 0)),
                  pl.BlockSpec((1, D_MODEL, D_FF_EXPERT), lambda i, e, c: (e, 0, 0)),
                  pl.BlockSpec((1, D_FF_EXPERT, D_MODEL), lambda i, e, c: (e, 0, 0))],
        out_specs=pl.BlockSpec((tm, D_MODEL), lambda i, e, c: (i, 0)),
        scratch_shapes=[pltpu.VMEM((tm, D_MODEL), F32)])
    return pl.pallas_call(
        _moe_kernel, name="moe_experts",
        grid_spec=grid_spec,
        out_shape=jax.ShapeDtypeStruct((rows, D_MODEL), F32),
        compiler_params=_cparams(("parallel", "arbitrary")),
    )(cnt, x, h, comb, rank, rank_t, gate, g_final.reshape(1, D_MODEL), wg, wu, wd)


def kernel(x_prompt, x_sample, cache_na_k, cache_na_v, state_wkv, c, c_ctx, mod_w, mod_b, norm_mix, norm_ffn, norm_final, na_w_in, fourier_w, na_rel_bias, na_w_out, ffn_w_gate, ffn_w_up, ffn_w_down, rw_w_in, pool_w, pool_scale, shift_mu, decay_w0, decay_up, iclr_a0, iclr_up, gate_up, k_k, k_a, r_k, gn_w, gn_b, rw_w_out, router_w, router_b, moe_w_gate, moe_w_up, moe_w_down):
    cond = jnp.concatenate([c_ctx[None, :], c, jnp.zeros((8 - N_SETS, D_MODEL), F32)], axis=0)
    mods = adaln_all(cond, mod_w, mod_b)[:, :N_SETS].reshape(DEPTH, N_SETS, 6, 1, D_MODEL)
    bf = lambda w: w.astype(BF16)

    xp = x_prompt.reshape(P_ROWS, D_MODEL)
    xs = x_sample.reshape(S_ROWS, D_MODEL)
    streams = {"p": (SEQ, P_ROWS, slice(0, 1)), "s": (DEC_SEQ, DEC_SEQ, slice(1, N_SETS))}
    x = {"p": xp, "s": xs}

    splits = ((0, FOURIER_CH), (FOURIER_CH, FOURIER_CH + NA_DIM),
              (FOURIER_CH + NA_DIM, FOURIER_CH + 2 * NA_DIM), (FOURIER_CH + 2 * NA_DIM, FOURIER_CH + 3 * NA_DIM))
    w_in, w_out = bf(na_w_in[0]), bf(na_w_out[0])
    f_bd = bf(_block_diag(fourier_w[0]))
    ffn_w = (bf(ffn_w_gate[0]), bf(ffn_w_up[0]), bf(ffn_w_down[0]))
    ck = cache_na_k[:, 0].reshape(DEC_BATCH * PAST_LEN, NA_DIM)
    cv = cache_na_v[:, 0].reshape(DEC_BATCH * PAST_LEN, NA_DIM)
    for name, (n, rps, sets) in streams.items():
        sh1, sc1, g1, sh2, sc2, g2 = [mods[0, sets, m] for m in range(6)]
        f, q, k, v = modulated_matmul(x[name], norm_mix[0], sh1, sc1, w_in, splits, rps)
        if name == "p":
            attn = context_attention(q, k, v)
            new_k = k.reshape(BATCH, 1, SEQ, NA_HEADS, HEAD_DIM)
            new_v = v.reshape(BATCH, 1, SEQ, NA_HEADS, HEAD_DIM)
        else:
            attn = neighbourhood_attention(q, k, v, ck, cv, _na_bias_table(na_rel_bias[0]))
        x[name] = mixer_ffn_residual(fourier_mix(f, n, f_bd), attn, x[name], g1, w_out,
                                     norm_ffn[0], sh2, sc2, g2, *ffn_w, rps)

    w_in, w_out = bf(rw_w_in[0]), bf(rw_w_out[0])
    p_bd = bf(_block_diag(pool_w[0]))
    moe_w = (bf(moe_w_gate[0]), bf(moe_w_up[0]), bf(moe_w_down[0]))
    rw = (shift_mu[0], k_k[0], k_a[0], r_k[0], decay_w0[0], iclr_a0[0], decay_up[0], iclr_up[0], gate_up[0])
    out = {}
    for name, (n, rps, sets) in streams.items():
        sh1, sc1, g1, sh2, sc2, g2 = [mods[1, sets, m] for m in range(6)]
        pc, z = modulated_matmul(x[name], norm_mix[1], sh1, sc1, w_in, ((0, POOL_CH), (POOL_CH, POOL_CH + RWKV_IN)), rps)
        t1, t2, t3, gate, bonus = rwkv_prep(z, n, *rw)
        if name == "p":
            o_f, st_f = wkv_scan_prompt(t1, t2, t3, 0)
            o_b, st_b = wkv_scan_prompt(t1, t2, t3, 1)
            st = jnp.transpose(jnp.stack([st_f, st_b]), (0, 1, 4, 3, 2))
            st = jnp.transpose(st.reshape(2, BATCH, RWKV_HEADS, HEAD_DIM, HEAD_DIM), (1, 0, 2, 3, 4))
        else:
            o_f, o_b = wkv_scan_sample(t1, t2, t3, _sample_state_lanes(state_wkv[:, 0]))
        mixed = rwkv_post(o_f, o_b, n, gate, bonus, gn_w[0], gn_b[0])
        y, *routed = mixer_residual_router(pool_mix(pc, n, p_bd, pool_scale[0]), mixed, x[name], g1, w_out,
                                           norm_ffn[1], sh2, sc2, router_w[0], router_b[0], rps)
        out[name] = moe_residual_norm(y, *routed, g2, norm_final, *moe_w, rps)

    return (out["p"].reshape(BATCH, SEQ, D_MODEL), out["s"].reshape(DEC_BATCH, DEC_SEQ, D_MODEL),
            new_k, new_v, st[:, None])
```

```python
import functools
import math

import numpy as np
import jax
import jax.numpy as jnp
from jax import lax
from jax.experimental import pallas as pl
from jax.experimental.pallas import tpu as pltpu

F32 = jnp.float32
BF16 = jnp.bfloat16

D_MODEL = 1024
BATCH = 32
SEQ = 256
DEPTH = 2
DEC_BATCH = 2
DEC_SEQ = 1024
PAST_LEN = 512
GRID_W = 64
HEAD_DIM = 64
FOURIER_CH = D_MODEL // 4
FOURIER_GROUPS = 4
FOURIER_GW = FOURIER_CH // FOURIER_GROUPS
NA_DIM = D_MODEL - FOURIER_CH
NA_HEADS = NA_DIM // HEAD_DIM
NA_MAX_ROWS = 8
NA_COLS = 16
POOL_WINDOWS = (2, 4, 8, 16)
POOL_CH = D_MODEL // 4
POOL_GW = POOL_CH // len(POOL_WINDOWS)
RWKV_DIM = D_MODEL - POOL_CH
RWKV_HEADS = RWKV_DIM // HEAD_DIM
DECAY_LORA = 64
ICLR_LORA = 64
GATE_LORA = 128
RWKV_IN = 3 * RWKV_DIM + 2 * DECAY_LORA + 2 * ICLR_LORA + GATE_LORA
D_FF = 2816
N_EXPERTS = 8
D_FF_EXPERT = 1408
RMS_EPS = 1e-6
GN_EPS = 64e-5
L2_EPS = 1e-12
DECAY_SCALE = math.exp(-0.5)
NEG_INF = -1e30

P_ROWS = BATCH * SEQ
S_ROWS = DEC_BATCH * DEC_SEQ
N_ROWS = P_ROWS + S_ROWS
N_SETS = 1 + DEC_BATCH
LANES = 128
VMEM_LIMIT = 56 * 1024 * 1024


def _cparams(sem):
    return pltpu.CompilerParams(dimension_semantics=sem, vmem_limit_bytes=VMEM_LIMIT)


def _sigmoid(x):
    return 0.5 * jnp.tanh(0.5 * x) + 0.5


def _dot(a, b):
    return jnp.dot(a, b, preferred_element_type=F32)


def _dot_nt(a, b):
    return lax.dot_general(a, b, (((1,), (1,)), ((), ())), preferred_element_type=F32)


def _set_index(tm, rows_per_set):
    q = rows_per_set // tm
    return lambda i: i // q


def _modulated(x, g, sh, sc):
    ms = jnp.mean(x * x, axis=-1, keepdims=True)
    return (x * lax.rsqrt(ms + RMS_EPS) * g) * (1.0 + sc) + sh


def _adaln_kernel(c_ref, w_ref, b_ref, o_ref):
    c = c_ref[...]
    s = (c * _sigmoid(c)).astype(BF16)
    o_ref[0] = _dot(s, w_ref[0].astype(BF16)) + b_ref[0]


def adaln_all(cond, mod_w, mod_b):
    tn = 1536
    n = 6 * D_MODEL
    return pl.pallas_call(
        _adaln_kernel, name="adaln",
        grid=(DEPTH, n // tn),
        in_specs=[pl.BlockSpec((8, D_MODEL), lambda l, j: (0, 0)),
                  pl.BlockSpec((1, D_MODEL, tn), lambda l, j: (l, 0, j)),
                  pl.BlockSpec((1, 1, tn), lambda l, j: (l, 0, j))],
        out_specs=pl.BlockSpec((1, 8, tn), lambda l, j: (l, 0, j)),
        out_shape=jax.ShapeDtypeStruct((DEPTH, 8, n), F32),
        compiler_params=_cparams(("parallel", "parallel")),
    )(cond, mod_w, mod_b.reshape(DEPTH, 1, n))


def _modmm_kernel(x_ref, g_ref, sh_ref, sc_ref, w_ref, *o_refs, splits):
    h = _modulated(x_ref[...], g_ref[...], sh_ref[0], sc_ref[0]).astype(BF16)
    for o_ref, (a, b) in zip(o_refs, splits):
        o_ref[...] = _dot(h, w_ref[:, a:b]).astype(o_ref.dtype)


def modulated_matmul(x, g, sh, sc, w, splits, dtypes, rows_per_set, tm=512):
    rows = x.shape[0]
    n_out = w.shape[1]
    si = _set_index(tm, rows_per_set)
    vec = pl.BlockSpec((1, 1, D_MODEL), lambda i: (si(i), 0, 0))
    return pl.pallas_call(
        functools.partial(_modmm_kernel, splits=splits), name="modulated_matmul",
        grid=(rows // tm,),
        in_specs=[pl.BlockSpec((tm, D_MODEL), lambda i: (i, 0)),
                  pl.BlockSpec((1, D_MODEL), lambda i: (0, 0)),
                  vec, vec,
                  pl.BlockSpec((D_MODEL, n_out), lambda i: (0, 0))],
        out_specs=[pl.BlockSpec((tm, b - a), lambda i: (i, 0)) for a, b in splits],
        out_shape=[jax.ShapeDtypeStruct((rows, b - a), dt) for (a, b), dt in zip(splits, dtypes)],
        compiler_params=_cparams(("parallel",)),
    )(x, g.reshape(1, D_MODEL), sh, sc, w)


def _dft_mats(n):
    t = np.arange(n)
    ang = 2.0 * np.pi * ((t[:, None] * t[None, :]) % n) / n
    cn, sn = np.cos(ang) / np.sqrt(n), np.sin(ang) / np.sqrt(n)
    c = np.arange(FOURIER_GW)
    angc = 2.0 * np.pi * ((c[:, None] * c[None, :]) % FOURIER_GW) / FOURIER_GW
    eye = np.eye(FOURIER_GROUPS)
    cc = np.kron(eye, np.cos(angc) / np.sqrt(FOURIER_GW))
    sc = np.kron(eye, np.sin(angc) / np.sqrt(FOURIER_GW))
    as_bf = lambda a: jnp.asarray(a, dtype=F32).astype(BF16)
    return as_bf(cn), as_bf(sn), as_bf(cc), as_bf(sc)


def _fourier_kernel(f_ref, cn_ref, sn_ref, cc_ref, sc_ref, w_ref, o_ref):
    x = f_ref[...].astype(BF16)
    a = _dot(x, cc_ref[...]).astype(BF16)
    b = _dot(x, sc_ref[...]).astype(BF16)
    re = _dot(cn_ref[...], a) - _dot(sn_ref[...], b)
    o_ref[...] = _dot(re.astype(BF16), w_ref[...]).astype(o_ref.dtype)


def _block_diag(w):
    g, c, _ = w.shape
    eye = jnp.eye(g, dtype=w.dtype)
    return (eye[:, None, :, None] * w[:, :, None, :]).reshape(g * c, g * c)


def fourier_mix(f, n, w_bd):
    rows = f.shape[0]
    cn, sn, cc, sc = _dft_mats(n)
    full = lambda shape: pl.BlockSpec(shape, lambda b: (0, 0))
    return pl.pallas_call(
        _fourier_kernel, name="fourier",
        grid=(rows // n,),
        in_specs=[pl.BlockSpec((n, FOURIER_CH), lambda b: (b, 0)),
                  full((n, n)), full((n, n)),
                  full((FOURIER_CH, FOURIER_CH)), full((FOURIER_CH, FOURIER_CH)),
                  full((FOURIER_CH, FOURIER_CH))],
        out_specs=pl.BlockSpec((n, FOURIER_CH), lambda b: (b, 0)),
        out_shape=jax.ShapeDtypeStruct((rows, FOURIER_CH), BF16),
        compiler_params=_cparams(("parallel",)),
    )(f, cn, sn, cc, sc, w_bd)


def _head_pair(ref, c):
    x = ref[:, c * LANES:(c + 1) * LANES]
    low = lax.broadcasted_iota(jnp.int32, x.shape, 1) < HEAD_DIM
    return jnp.where(low, x, 0.0).astype(BF16), jnp.where(low, 0.0, x).astype(BF16), low


def _value_pair(ref, c):
    x = ref[:, c * LANES:(c + 1) * LANES]
    low = lax.broadcasted_iota(jnp.int32, x.shape, 1) < HEAD_DIM
    return jnp.where(low, x, 1.0).astype(BF16), jnp.where(low, 1.0, x).astype(BF16)


def _normalised_pair(res_even, res_odd, low):
    even = res_even / pltpu.roll(res_even, HEAD_DIM, 1)
    odd = res_odd / pltpu.roll(res_odd, HEAD_DIM, 1)
    return jnp.where(low, even, odd)


def _ctx_attn_kernel(q_ref, k_ref, v_ref, o_ref):
    scale = HEAD_DIM ** -0.5

    def scores(c):
        q_even, q_odd, _ = _head_pair(q_ref, c)
        k = k_ref[:, c * LANES:(c + 1) * LANES].astype(BF16)
        return _dot_nt(q_even, k) * scale, _dot_nt(q_odd, k) * scale

    def softmax_numerator(s):
        return jnp.exp(s - jnp.max(s, axis=-1, keepdims=True)).astype(BF16)

    n_pairs = NA_HEADS // 2
    pending = [scores(0)]
    for c in range(n_pairs):
        if c + 1 < n_pairs:
            pending.append(scores(c + 1))
        s_even, s_odd = pending.pop(0)
        v_even, v_odd = _value_pair(v_ref, c)
        low = lax.broadcasted_iota(jnp.int32, (SEQ, LANES), 1) < HEAD_DIM
        o_ref[:, c * LANES:(c + 1) * LANES] = _normalised_pair(
            _dot(softmax_numerator(s_even), v_even), _dot(softmax_numerator(s_odd), v_odd), low).astype(o_ref.dtype)


def context_attention(q, k, v):
    blk = pl.BlockSpec((SEQ, NA_DIM), lambda b: (b, 0))
    return pl.pallas_call(
        _ctx_attn_kernel, name="ctx_attn",
        grid=(BATCH,),
        in_specs=[blk, blk, blk],
        out_specs=blk,
        out_shape=jax.ShapeDtypeStruct((P_ROWS, NA_DIM), BF16),
        compiler_params=_cparams(("parallel",)),
    )(q, k, v)


NA_ROWS = DEC_SEQ // GRID_W
NA_WIN = NA_MAX_ROWS * GRID_W


def _na_bias_table(rel_bias):
    cols = np.arange(GRID_W)
    c0 = np.clip(cols - NA_COLS // 2, 0, GRID_W - NA_COLS)
    col_ok = (cols[None, :] >= c0[:, None]) & (cols[None, :] < c0[:, None] + NA_COLS)
    dc = np.clip(cols[None, :] - cols[:, None] + NA_COLS - 1, 0, 2 * NA_COLS - 2)
    onehot = (dc[None] == np.arange(2 * NA_COLS - 1)[:, None, None]).astype(np.float32)
    toe = jnp.einsum("hrj,jqk->hrqk", rel_bias, jnp.asarray(onehot), precision=lax.Precision.HIGHEST)
    toe = jnp.where(col_ok[None, None], toe, NEG_INF)
    tabs = [jnp.transpose(toe[:, NA_MAX_ROWS - 1 - o: 2 * NA_MAX_ROWS - 1 - o], (0, 2, 1, 3))
            for o in range(NA_MAX_ROWS)]
    return jnp.stack(tabs).reshape(NA_MAX_ROWS, NA_HEADS, GRID_W, NA_WIN)


def _na_row_start(i):
    return jnp.clip(i - NA_MAX_ROWS // 2, 0, NA_ROWS - NA_MAX_ROWS)


def _na_kernel(q_ref, k_ref, v_ref, ck_ref, cv_ref, bias_ref, o_ref):
    scale = HEAD_DIM ** -0.5
    i = pl.program_id(1)
    start = pl.multiple_of(_na_row_start(i) * GRID_W, GRID_W)
    k_win = k_ref.at[pl.ds(start, NA_WIN)]
    v_win = v_ref.at[pl.ds(start, NA_WIN)]

    def scores(c):
        q_even, q_odd, _ = _head_pair(q_ref, c)
        kw = k_win[:, c * LANES:(c + 1) * LANES].astype(BF16)
        ck = ck_ref[:, c * LANES:(c + 1) * LANES].astype(BF16)
        return [(_dot_nt(q, kw) * scale + bias_ref[0, 2 * c + par], _dot_nt(q, ck) * scale)
                for par, q in enumerate((q_even, q_odd))]

    def weighted_values(s, vw, cv):
        s_loc, s_ctx = s
        m = jnp.maximum(jnp.max(s_loc, axis=-1, keepdims=True), jnp.max(s_ctx, axis=-1, keepdims=True))
        return _dot(jnp.exp(s_loc - m).astype(BF16), vw) + _dot(jnp.exp(s_ctx - m).astype(BF16), cv)

    n_pairs = NA_HEADS // 2
    pending = [scores(0)]
    for c in range(n_pairs):
        if c + 1 < n_pairs:
            pending.append(scores(c + 1))
        s_even, s_odd = pending.pop(0)
        vw_even, vw_odd = _value_pair(v_win, c)
        cv_even, cv_odd = _value_pair(cv_ref, c)
        low = lax.broadcasted_iota(jnp.int32, (GRID_W, LANES), 1) < HEAD_DIM
        o_ref[:, c * LANES:(c + 1) * LANES] = _normalised_pair(
            weighted_values(s_even, vw_even, cv_even), weighted_values(s_odd, vw_odd, cv_odd), low).astype(o_ref.dtype)


def neighbourhood_attention(q, k, v, ck, cv, bias_tab):
    seq = pl.BlockSpec((DEC_SEQ, NA_DIM), lambda b, i: (b, 0))
    ctx = pl.BlockSpec((PAST_LEN, NA_DIM), lambda b, i: (b, 0))
    row = pl.BlockSpec((GRID_W, NA_DIM), lambda b, i: (b * NA_ROWS + i, 0))
    return pl.pallas_call(
        _na_kernel, name="na_attn",
        grid=(DEC_BATCH, NA_ROWS),
        in_specs=[row, seq, seq, ctx, ctx,
                  pl.BlockSpec((1, NA_HEADS, GRID_W, NA_WIN), lambda b, i: (i - _na_row_start(i), 0, 0, 0))],
        out_specs=row,
        out_shape=jax.ShapeDtypeStruct((S_ROWS, NA_DIM), BF16),
        compiler_params=_cparams(("parallel", "arbitrary")),
    )(q, k, v, ck, cv, bias_tab)


def _mixer_residual(a_ref, b_ref, x_ref, gate_ref, wo_ref):
    na = a_ref.shape[1]
    y = _dot(a_ref[...].astype(BF16), wo_ref[:na, :]) + _dot(b_ref[...].astype(BF16), wo_ref[na:, :])
    return x_ref[...] + gate_ref[0] * y


def _ffn_kernel(a_ref, b_ref, x_ref, gate1_ref, wo_ref, g_ref, sh_ref, sc_ref, gate2_ref, wg_ref, wu_ref, wd_ref,
                o_ref, y_scr, h_scr, acc_scr):
    j = pl.program_id(1)

    @pl.when(j == 0)
    def _():
        y = _mixer_residual(a_ref, b_ref, x_ref, gate1_ref, wo_ref)
        y_scr[...] = y
        h_scr[...] = _modulated(y, g_ref[...], sh_ref[0], sc_ref[0]).astype(BF16)
        acc_scr[...] = jnp.zeros_like(acc_scr)

    h = h_scr[...]
    gt = _dot(h, wg_ref[...])
    act = (gt * _sigmoid(gt)) * _dot(h, wu_ref[...])
    acc_scr[...] += _dot(act.astype(BF16), wd_ref[...])

    @pl.when(j == pl.num_programs(1) - 1)
    def _():
        o_ref[...] = y_scr[...] + gate2_ref[0] * acc_scr[...]


def mixer_ffn_residual(a, b, x, gate1, w_out, g, sh, sc, gate2, wg, wu, wd, rows_per_set, tm=512, tf=D_FF // 2):
    rows = x.shape[0]
    na, nb = a.shape[1], b.shape[1]
    si = _set_index(tm, rows_per_set)
    vec = pl.BlockSpec((1, 1, D_MODEL), lambda i, j: (si(i), 0, 0))
    row = lambda n: pl.BlockSpec((tm, n), lambda i, j: (i, 0))
    return pl.pallas_call(
        _ffn_kernel, name="ffn",
        grid=(rows // tm, D_FF // tf),
        in_specs=[row(na), row(nb), row(D_MODEL), vec,
                  pl.BlockSpec((na + nb, D_MODEL), lambda i, j: (0, 0)),
                  pl.BlockSpec((1, D_MODEL), lambda i, j: (0, 0)),
                  vec, vec, vec,
                  pl.BlockSpec((D_MODEL, tf), lambda i, j: (0, j)),
                  pl.BlockSpec((D_MODEL, tf), lambda i, j: (0, j)),
                  pl.BlockSpec((tf, D_MODEL), lambda i, j: (j, 0))],
        out_specs=row(D_MODEL),
        out_shape=jax.ShapeDtypeStruct((rows, D_MODEL), F32),
        scratch_shapes=[pltpu.VMEM((tm, D_MODEL), F32), pltpu.VMEM((tm, D_MODEL), BF16),
                        pltpu.VMEM((tm, D_MODEL), F32)],
        compiler_params=_cparams(("parallel", "arbitrary")),
    )(a, b, x, gate1, w_out, g.reshape(1, D_MODEL), sh, sc, gate2, wg, wu, wd)


def _pool_consts(n):
    t = np.arange(n)
    mats, cnts = [], []
    for win in POOL_WINDOWS:
        lo = np.clip(t - win // 2, 0, n)
        hi = np.clip(t + win - win // 2, 0, n)
        mats.append(((t[None, :] >= lo[:, None]) & (t[None, :] < hi[:, None])).astype(np.float32))
        cnts.append(np.repeat((hi - lo).astype(np.float32)[:, None], POOL_GW, axis=1))
    return jnp.asarray(np.stack(mats)).astype(BF16), jnp.asarray(np.concatenate(cnts, axis=1))


def _pool_kernel(x_ref, pm_ref, cnt_ref, w_ref, scale_ref, o_ref):
    x = x_ref[...]
    hi = x.astype(BF16)
    lo = (x - hi.astype(F32)).astype(BF16)
    sums = []
    for g in range(len(POOL_WINDOWS)):
        sl = slice(g * POOL_GW, (g + 1) * POOL_GW)
        sums.append(_dot(pm_ref[g], hi[:, sl]) + _dot(pm_ref[g], lo[:, sl]))
    y = jnp.concatenate(sums, axis=-1) / cnt_ref[...] - x
    o_ref[...] = (_dot(y.astype(BF16), w_ref[...]) * scale_ref[...]).astype(o_ref.dtype)


def pool_mix(x, n, w_bd, scale):
    rows = x.shape[0]
    pm, cnt = _pool_consts(n)
    return pl.pallas_call(
        _pool_kernel, name="pool",
        grid=(rows // n,),
        in_specs=[pl.BlockSpec((n, POOL_CH), lambda b: (b, 0)),
                  pl.BlockSpec((len(POOL_WINDOWS), n, n), lambda b: (0, 0, 0)),
                  pl.BlockSpec((n, POOL_CH), lambda b: (0, 0)),
                  pl.BlockSpec((POOL_CH, POOL_CH), lambda b: (0, 0)),
                  pl.BlockSpec((1, POOL_CH), lambda b: (0, 0))],
        out_specs=pl.BlockSpec((n, POOL_CH), lambda b: (b, 0)),
        out_shape=jax.ShapeDtypeStruct((rows, POOL_CH), BF16),
        compiler_params=_cparams(("parallel",)),
    )(x, pm, cnt, w_bd, scale.reshape(1, POOL_CH))


RW_TILE = 256
HALO = 8
SUB = 8
PACK_R = RWKV_HEADS * SUB


def _head_ones():
    h = np.arange(RWKV_DIM) // HEAD_DIM
    return jnp.asarray((h[:, None] == h[None, :]).astype(np.float32)).astype(BF16)


def _head_sum(x, ones):
    hi = x.astype(BF16)
    lo = (x - hi.astype(F32)).astype(BF16)
    return _dot(hi, ones) + _dot(lo, ones)


def _pack_heads(a, b, o_ref, lead):
    n = a.shape[0]
    lane = lax.broadcasted_iota(jnp.int32, (n, LANES), 1)
    low = lane < HEAD_DIM
    for c in range(RWKV_DIM // LANES):
        ac = a[:, c * LANES:(c + 1) * LANES]
        bc = b[:, c * LANES:(c + 1) * LANES]
        even = jnp.where(low, ac, pltpu.roll(bc, HEAD_DIM, 1))
        odd = jnp.where(low, pltpu.roll(ac, HEAD_DIM, 1), bc)
        for h, val in ((2 * c, even), (2 * c + 1, odd)):
            o_ref[lead + (slice(None), slice(h * SUB, (h + 1) * SUB), slice(None))] = val.reshape(n // SUB, SUB, LANES)


def _rwkv_prep_kernel(z_ref, zp_ref, zn_ref, mu_ref, kk_w_ref, ka_ref, rk_ref, w0_ref, a0_ref,
                      dup_ref, iup_ref, gup_ref, ones_ref,
                      g1_ref, g2_ref, g3_ref, g_ref, bonus_ref, *, tiles_per_seq):
    i = pl.program_id(0)
    pos = i % tiles_per_seq
    z = z_ref[...]
    row = lax.broadcasted_iota(jnp.int32, (HALO, 1), 0)
    prev_edge = jnp.where(pos == 0, 0.0, zp_ref[HALO - 1:HALO, :])
    next_edge = jnp.where(pos == tiles_per_seq - 1, 0.0, zn_ref[0:1, :])
    prev = pltpu.roll(z, 1, 0)
    prev = jnp.concatenate([jnp.where(row == 0, prev_edge, prev[:HALO]), prev[HALO:]], axis=0)
    nxt = pltpu.roll(z, RW_TILE - 1, 0)
    nxt = jnp.concatenate([nxt[:-HALO], jnp.where(row == HALO - 1, next_edge, nxt[-HALO:])], axis=0)
    mu_prev, mu_next = mu_ref[0:1, :], mu_ref[1:2, :]
    zr = z * (1.0 - mu_prev - mu_next) + mu_prev * prev + mu_next * nxt

    d = RWKV_DIM
    r, k, v = zr[:, :d], zr[:, d:2 * d], zr[:, 2 * d:3 * d]
    lora = 3 * d
    ones = ones_ref[...]
    kk = k * kk_w_ref[...]
    kk = kk * lax.rsqrt(_head_sum(kk * kk, ones) + L2_EPS)
    _pack_heads(r, v, g3_ref, ())
    for dr in range(2):
        wl = zr[:, lora + dr * DECAY_LORA: lora + (dr + 1) * DECAY_LORA]
        al = zr[:, lora + 2 * DECAY_LORA + dr * ICLR_LORA: lora + 2 * DECAY_LORA + (dr + 1) * ICLR_LORA]
        lw = w0_ref[dr:dr + 1, :] + _dot(jnp.tanh(wl).astype(BF16), dup_ref[dr])
        w = jnp.exp(-DECAY_SCALE * _sigmoid(lw))
        a = _sigmoid(a0_ref[dr:dr + 1, :] + _dot(al.astype(BF16), iup_ref[dr]))
        _pack_heads(w, kk * a, g1_ref, (dr,))
        _pack_heads(k * (1.0 + (a - 1.0) * ka_ref[...]), kk, g2_ref, (dr,))
    gl = zr[:, lora + 2 * DECAY_LORA + 2 * ICLR_LORA:]
    g_ref[...] = _dot(_sigmoid(gl).astype(BF16), gup_ref[...])
    bonus_ref[...] = _head_sum(r * k * rk_ref[...], ones) * v


def rwkv_prep(z, n, mu, k_k, k_a, r_k, w0, a0, dup, iup, gup):
    rows = z.shape[0]
    nb = rows // n
    tps = n // RW_TILE
    hb = RW_TILE // HALO
    last = rows // HALO - 1
    d = RWKV_DIM
    full2 = lambda shape: pl.BlockSpec(shape, lambda i: (0, 0))
    full3 = lambda shape: pl.BlockSpec(shape, lambda i: (0, 0, 0))
    tile = pl.BlockSpec((RW_TILE, d), lambda i: (i, 0))
    pk2 = pl.BlockSpec((2, RW_TILE // SUB, PACK_R, LANES), lambda i: (0, i % tps, i // tps, 0))
    pk1 = pl.BlockSpec((RW_TILE // SUB, PACK_R, LANES), lambda i: (i % tps, i // tps, 0))
    return pl.pallas_call(
        functools.partial(_rwkv_prep_kernel, tiles_per_seq=tps), name="rwkv_prep",
        grid=(rows // RW_TILE,),
        in_specs=[pl.BlockSpec((RW_TILE, RWKV_IN), lambda i: (i, 0)),
                  pl.BlockSpec((HALO, RWKV_IN), lambda i: (jnp.maximum(i * hb - 1, 0), 0)),
                  pl.BlockSpec((HALO, RWKV_IN), lambda i: (jnp.minimum((i + 1) * hb, last), 0)),
                  full2((2, RWKV_IN)), full2((1, d)), full2((1, d)), full2((1, d)),
                  full2((2, d)), full2((2, d)),
                  full3((2, DECAY_LORA, d)), full3((2, ICLR_LORA, d)), full2((GATE_LORA, d)),
                  full2((d, d))],
        out_specs=[pk2, pk2, pk1, tile, tile],
        out_shape=[jax.ShapeDtypeStruct((2, n // SUB, nb * PACK_R, LANES), F32),
                   jax.ShapeDtypeStruct((2, n // SUB, nb * PACK_R, LANES), F32),
                   jax.ShapeDtypeStruct((n // SUB, nb * PACK_R, LANES), F32),
                   jax.ShapeDtypeStruct((rows, d), F32),
                   jax.ShapeDtypeStruct((rows, d), F32)],
        compiler_params=_cparams(("parallel",)),
    )(z, z, z, mu, k_k.reshape(1, d), k_a.reshape(1, d), r_k.reshape(1, d), w0, a0,
      dup.astype(BF16), iup.astype(BF16), gup.astype(BF16), _head_ones())


SCAN_TC = 64
SCAN_G = SCAN_TC // SUB
SLOTS = 4
V_BLOCK = 32
K_CHUNK = 16
N_KC = HEAD_DIM // K_CHUNK
PEEL = 2


def _wkv_first_sa(s_scr, t2, vs):
    sas = []
    for vb in range(vs // V_BLOCK):
        rows = slice(vb * V_BLOCK, (vb + 1) * V_BLOCK)

        def chunk(kc, sa):
            for j in range(K_CHUNK):
                sa = sa + s_scr[kc * K_CHUNK + j, rows, :] * t2[N_KC + kc, j:j + 1, :]
            return sa

        sas.append(lax.fori_loop(0, N_KC, chunk, jnp.zeros((V_BLOCK, LANES), F32)))
    return tuple(sas)


def _wkv_step(s_scr, t1, t2, t3, t2_next, v_blocks, o_ref, sas):
    nxt = []
    for vb, v_blk in enumerate(v_blocks):
        rows = slice(vb * V_BLOCK, (vb + 1) * V_BLOCK)
        sa = sas[vb]

        def chunk(kc, carry):
            o, sa_n = carry
            for j in range(K_CHUNK):
                k = kc * K_CHUNK + j
                s_new = s_scr[k, rows, :] - sa * t1[N_KC + kc, j:j + 1, :] + v_blk * t2[kc, j:j + 1, :]
                s_scr[k, rows, :] = s_new
                o = o + s_new * t3[kc, j:j + 1, :]
                sa_n = sa_n + s_new * t2_next[N_KC + kc, j:j + 1, :]
            return o, sa_n

        zero = jnp.zeros((V_BLOCK, LANES), F32)
        carry = (zero, zero)
        for kc in range(PEEL):
            carry = chunk(kc, carry)
        o, sa_n = lax.fori_loop(PEEL, N_KC, chunk, carry)
        o_ref[rows, :] = o
        nxt.append(sa_n)
    return tuple(nxt)


def _store_tile(ref, idx, x):
    ref[idx] = x.reshape(2 * N_KC, K_CHUNK, LANES)


def _store_scaled(tile_ref, t1, t2, t3, g, advance):
    d = HEAD_DIM
    g_new = g * t1[:d]
    inv = 1.0 / g_new
    _store_tile(tile_ref, 0, jnp.concatenate([t1[:d], t1[d:] * inv], axis=0))
    _store_tile(tile_ref, 1, jnp.concatenate([t2[:d] * inv, t2[d:] * g], axis=0))
    _store_tile(tile_ref, 2, jnp.concatenate([t3[:d] * g_new, t3[d:]], axis=0))
    return g_new if advance is True else jnp.where(advance, g_new, g)


def _unscale_state(s_scr, g_scr, g):
    g_scr[...] = g.reshape(N_KC, K_CHUNK, LANES)

    def chunk(kc, carry):
        for j in range(K_CHUNK):
            k = kc * K_CHUNK + j
            s_scr[k] = s_scr[k] * g_scr[kc, j:j + 1, :]
        return carry

    lax.fori_loop(0, N_KC, chunk, 0)


def _step_rows(ref, lead, grp, sub, n):
    return ref.at[lead + (grp,)][pl.ds(sub, n, stride=SUB), :]


def _scan_prompt_kernel(g1_ref, g2_ref, g3_ref, o_ref, st_ref, s_scr, g_scr, ta, tb, tc, td, oa, ob, *, reverse):
    c = pl.program_id(1)
    tiles = (ta, tb, tc, td)
    outs = (oa, ob)

    @pl.when(c == 0)
    def _():
        s_scr[...] = jnp.zeros_like(s_scr)

    for o_scr in outs:
        o_scr[...] = jnp.zeros_like(o_scr)

    def where(grp, sub):
        return (SCAN_G - 1 - grp, SUB - 1 - sub) if reverse else (grp, sub)

    def load_tiles(grp, sub, slot, dec, advance=True):
        g, s = where(grp, sub)
        return _store_scaled(tiles[slot], _step_rows(g1_ref, (0,), g, s, LANES).T,
                             _step_rows(g2_ref, (0,), g, s, LANES).T,
                             _step_rows(g3_ref, (), g, s, LANES).T, dec, advance)

    def flush(grp, sub, o_scr):
        g, s = where(grp, sub)
        o_ref.at[0, g][pl.ds(s, LANES, stride=SUB), :] = o_scr[...].T

    dec0 = load_tiles(0, 0, 0, jnp.ones((HEAD_DIM, LANES), F32))
    dec0 = load_tiles(0, 1, 1, dec0)
    sas0 = _wkv_first_sa(s_scr, tiles[0].at[1], HEAD_DIM)
    per_blk = V_BLOCK // K_CHUNK

    def group(grp, carry):
        sas, dec = carry
        for j in range(SUB):
            cur, nxt = tiles[j % SLOTS], tiles[(j + 1) % SLOTS]
            ahead = j + 2
            if ahead < SUB:
                dec = load_tiles(grp, ahead, ahead % SLOTS, dec)
            else:
                dec = load_tiles(jnp.minimum(grp + 1, SCAN_G - 1), ahead % SUB, ahead % SLOTS, dec,
                                 advance=grp + 1 < SCAN_G)
            if j > 0:
                flush(grp, j - 1, outs[(j - 1) % 2])
            v_blocks = [cur[2, N_KC + vb * per_blk: N_KC + (vb + 1) * per_blk].reshape(V_BLOCK, LANES)
                        for vb in range(HEAD_DIM // V_BLOCK)]
            sas = _wkv_step(s_scr, cur.at[0], cur.at[1], cur.at[2], nxt.at[1], v_blocks, outs[j % 2], sas)
        flush(grp, SUB - 1, outs[(SUB - 1) % 2])
        return sas, dec

    _, dec = lax.fori_loop(0, SCAN_G, group, (sas0, dec0))
    _unscale_state(s_scr, g_scr, dec)

    @pl.when(c == pl.num_programs(1) - 1)
    def _():
        st_ref[0] = s_scr[...]


def wkv_scan_prompt(g1, g2, g3, direction):
    n_grp, rows = g3.shape[0], g3.shape[1]
    groups = rows // (LANES * SUB)
    nblk = n_grp // SCAN_G
    tb = (lambda s: nblk - 1 - s) if direction else (lambda s: s)
    blk = (SCAN_G, LANES * SUB, LANES)
    dir_blk = pl.BlockSpec((1,) + blk, lambda g, s: (direction, tb(s), g, 0))
    return pl.pallas_call(
        functools.partial(_scan_prompt_kernel, reverse=bool(direction)), name="wkv_scan_prompt",
        grid=(groups, nblk),
        in_specs=[dir_blk, dir_blk, pl.BlockSpec(blk, lambda g, s: (tb(s), g, 0))],
        out_specs=[pl.BlockSpec((1,) + blk, lambda g, s: (0, tb(s), g, 0)),
                   pl.BlockSpec((1, HEAD_DIM, HEAD_DIM, LANES), lambda g, s: (g, 0, 0, 0))],
        out_shape=[jax.ShapeDtypeStruct((1, n_grp, rows, LANES), F32),
                   jax.ShapeDtypeStruct((groups, HEAD_DIM, HEAD_DIM, LANES), F32)],
        scratch_shapes=([pltpu.VMEM((HEAD_DIM, HEAD_DIM, LANES), F32),
                         pltpu.VMEM((N_KC, K_CHUNK, LANES), F32)]
                        + [pltpu.VMEM((3, 2 * N_KC, K_CHUNK, LANES), F32)] * SLOTS
                        + [pltpu.VMEM((LANES, LANES), F32)] * 2),
        compiler_params=_cparams(("parallel", "arbitrary")),
    )(g1, g2, g3)


S_CHAINS = DEC_BATCH * RWKV_HEADS
S_VS = HEAD_DIM // 2


def _scan_sample_kernel(g1f_ref, g1b_ref, g2f_ref, g2b_ref, g3f_ref, g3b_ref, s0_ref,
                        of_ref, ob_ref, s_scr, g_scr, ta, tb, tc, td, va, vb, vc, vd, oa, ob):
    c = pl.program_id(0)
    tiles = (ta, tb, tc, td)
    vals = (va, vb, vc, vd)
    outs = (oa, ob)

    @pl.when(c == 0)
    def _():
        s_scr[...] = s0_ref[...]

    nc = 2 * S_CHAINS
    n_pad = LANES - 2 * nc
    zpad = jnp.zeros((n_pad, LANES), F32)
    wpad = jnp.where(lax.broadcasted_iota(jnp.int32, (n_pad, LANES), 1) < HEAD_DIM, 1.0, 0.0)
    lane = lax.broadcasted_iota(jnp.int32, (S_VS, LANES), 1)

    def stacked_t(f_ref, b_ref, grp, sub, pad):
        f = _step_rows(f_ref, (0,), grp, sub, S_CHAINS)
        b = _step_rows(b_ref, (0,), SCAN_G - 1 - grp, SUB - 1 - sub, S_CHAINS)
        return jnp.concatenate([f, b, f, b, pad], axis=0).T

    def load_tiles(grp, sub, slot, dec, advance=True):
        t3 = stacked_t(g3f_ref, g3b_ref, grp, sub, zpad)
        vals[slot][...] = jnp.where(lane < nc, t3[HEAD_DIM:HEAD_DIM + S_VS], t3[HEAD_DIM + S_VS:])
        return _store_scaled(tiles[slot], stacked_t(g1f_ref, g1b_ref, grp, sub, wpad),
                             stacked_t(g2f_ref, g2b_ref, grp, sub, zpad), t3, dec, advance)

    def flush(grp, sub, o_scr):
        o = o_scr[...]
        full = jnp.concatenate([o, pltpu.roll(o, LANES - nc, 1), jnp.zeros((LANES - HEAD_DIM, LANES), F32)], axis=0)
        ot = full.T
        of_ref.at[0, grp][pl.ds(sub, S_CHAINS, stride=SUB), :] = ot[0:S_CHAINS]
        ob_ref.at[0, SCAN_G - 1 - grp][pl.ds(SUB - 1 - sub, S_CHAINS, stride=SUB), :] = ot[S_CHAINS:nc]

    dec0 = load_tiles(0, 0, 0, jnp.ones((HEAD_DIM, LANES), F32))
    dec0 = load_tiles(0, 1, 1, dec0)
    sas0 = _wkv_first_sa(s_scr, tiles[0].at[1], S_VS)

    def group(grp, carry):
        sas, dec = carry
        for j in range(SUB):
            cur, nxt = tiles[j % SLOTS], tiles[(j + 1) % SLOTS]
            ahead = j + 2
            if ahead < SUB:
                dec = load_tiles(grp, ahead, ahead % SLOTS, dec)
            else:
                dec = load_tiles(jnp.minimum(grp + 1, SCAN_G - 1), ahead % SUB, ahead % SLOTS, dec,
                                 advance=grp + 1 < SCAN_G)
            if j > 0:
                flush(grp, j - 1, outs[(j - 1) % 2])
            sas = _wkv_step(s_scr, cur.at[0], cur.at[1], cur.at[2], nxt.at[1], [vals[j % SLOTS][...]],
                            outs[j % 2], sas)
        flush(grp, SUB - 1, outs[(SUB - 1) % 2])
        return sas, dec

    _, dec = lax.fori_loop(0, SCAN_G, group, (sas0, dec0))
    _unscale_state(s_scr, g_scr, dec)


def wkv_scan_sample(g1, g2, g3, s0):
    n_grp, rows = g3.shape[0], g3.shape[1]
    g3 = g3.reshape(1, n_grp, rows, LANES)
    nblk = n_grp // SCAN_G
    blk = (1, SCAN_G, rows, LANES)
    fwd = lambda d: pl.BlockSpec(blk, lambda s: (d, s, 0, 0))
    bwd = lambda d: pl.BlockSpec(blk, lambda s: (d, nblk - 1 - s, 0, 0))
    return pl.pallas_call(
        _scan_sample_kernel, name="wkv_scan_sample",
        grid=(nblk,),
        in_specs=[fwd(0), bwd(1), fwd(0), bwd(1), fwd(0), bwd(0),
                  pl.BlockSpec((HEAD_DIM, S_VS, LANES), lambda s: (0, 0, 0))],
        out_specs=[fwd(0), bwd(0)],
        out_shape=[jax.ShapeDtypeStruct((1, n_grp, rows, LANES), F32)] * 2,
        scratch_shapes=([pltpu.VMEM((HEAD_DIM, S_VS, LANES), F32),
                         pltpu.VMEM((N_KC, K_CHUNK, LANES), F32)]
                        + [pltpu.VMEM((3, 2 * N_KC, K_CHUNK, LANES), F32)] * SLOTS
                        + [pltpu.VMEM((S_VS, LANES), F32)] * SLOTS
                        + [pltpu.VMEM((S_VS, LANES), F32)] * 2),
        compiler_params=_cparams(("arbitrary",)),
    )(g1, g1, g2, g2, g3, g3, s0)


def _sample_state_lanes(s0):
    nc = 2 * S_CHAINS
    st = jnp.transpose(s0, (4, 3, 1, 0, 2)).reshape(HEAD_DIM, HEAD_DIM, nc)
    st = jnp.concatenate([st[:, :S_VS], st[:, S_VS:]], axis=-1)
    return jnp.pad(st, ((0, 0), (0, 0), (0, LANES - 2 * nc)))


def _rwkv_post_kernel(of_ref, ob_ref, g_ref, bonus_ref, gw_ref, gb_ref, ones_ref, y_ref):
    head_sum = functools.partial(_head_sum, ones=ones_ref[...])
    lane = lax.broadcasted_iota(jnp.int32, (RW_TILE, LANES), 1)
    low = lane < HEAD_DIM

    def head(h):
        rows = slice(h * SUB, (h + 1) * SUB)
        return (of_ref[0, :, rows, :] + ob_ref[0, :, rows, :]).reshape(RW_TILE, LANES)

    cols = [jnp.where(low, head(2 * c), pltpu.roll(head(2 * c + 1), HEAD_DIM, 1))
            for c in range(RWKV_DIM // LANES)]
    o = jnp.concatenate(cols, axis=-1)
    mu = head_sum(o) / HEAD_DIM
    oc = o - mu
    var = head_sum(oc * oc) / HEAD_DIM
    on = (oc * lax.rsqrt(var + GN_EPS)) * gw_ref[...] + gb_ref[...]
    y_ref[...] = ((on + bonus_ref[...]) * g_ref[...]).astype(y_ref.dtype)


def rwkv_post(o_f, o_b, n, g, bonus, gn_w, gn_b):
    rows, d = g.shape
    tps = n // RW_TILE
    tile = pl.BlockSpec((RW_TILE, d), lambda i: (i, 0))
    vec = pl.BlockSpec((1, d), lambda i: (0, 0))
    pk = pl.BlockSpec((1, RW_TILE // SUB, PACK_R, LANES), lambda i: (0, i % tps, i // tps, 0))
    return pl.pallas_call(
        _rwkv_post_kernel, name="rwkv_post",
        grid=(rows // RW_TILE,),
        in_specs=[pk, pk, tile, tile, vec, vec, pl.BlockSpec((d, d), lambda i: (0, 0))],
        out_specs=tile,
        out_shape=jax.ShapeDtypeStruct((rows, d), BF16),
        compiler_params=_cparams(("parallel",)),
    )(o_f, o_b, g, bonus, gn_w.reshape(1, d), gn_b.reshape(1, d), _head_ones())


MOE_R = 512
MOE_M = 128


def _split3(x):
    a = x.astype(BF16)
    r1 = x - a.astype(F32)
    b = r1.astype(BF16)
    c = (r1 - b.astype(F32)).astype(BF16)
    return a, b, c


def _router_kernel(mix_a_ref, mix_b_ref, x_ref, gate1_ref, wo_ref, g_ref, sh_ref, sc_ref, w_ref, b_ref, tri_ref,
                   y_ref, h_ref, comb_ref, rank_ref, rank_t_ref, cnt_ref):
    y = _mixer_residual(mix_a_ref, mix_b_ref, x_ref, gate1_ref, wo_ref)
    y_ref[...] = y
    h = _modulated(y, g_ref[...], sh_ref[0], sc_ref[0])
    h_ref[...] = h.astype(BF16)
    h1, h2, h3 = _split3(h)
    w1, w2, w3 = _split3(w_ref[...])
    logits = (_dot(h1, w1) + (_dot(h1, w2) + _dot(h2, w1))
              + (_dot(h1, w3) + _dot(h2, w2) + _dot(h3, w1))) + b_ref[...]
    col = lax.broadcasted_iota(jnp.int32, logits.shape, 1)
    logits = jnp.where(col < N_EXPERTS, logits, -jnp.inf)
    m1 = jnp.max(logits, axis=-1, keepdims=True)
    i1 = jnp.min(jnp.where(logits == m1, col, LANES), axis=-1, keepdims=True)
    rest = jnp.where(col == i1, -jnp.inf, logits)
    m2 = jnp.max(rest, axis=-1, keepdims=True)
    i2 = jnp.min(jnp.where(rest == m2, col, LANES), axis=-1, keepdims=True)
    e2 = jnp.exp(m2 - m1)
    den = 1.0 + e2
    comb_ref[...] = jnp.where(col == i1, 1.0 / den, 0.0) + jnp.where(col == i2, e2 / den, 0.0)
    chosen = (col == i1) | (col == i2)
    upto = _dot(tri_ref[...], jnp.where(chosen, 1.0, 0.0).astype(BF16))
    rank = jnp.where(chosen, upto - 1.0, -1.0)
    rank_ref[...] = rank
    rank_t_ref[0] = rank.T[0:N_EXPERTS, :]
    cnt_ref[0] = jnp.broadcast_to(upto[MOE_R - 1:MOE_R, :], (8, LANES))


def mixer_residual_router(a, b, x, gate1, w_out, g, sh, sc, router_w, router_b, rows_per_set):
    rows = x.shape[0]
    na, nb = a.shape[1], b.shape[1]
    tm = MOE_R
    nblk = rows // tm
    si = _set_index(tm, rows_per_set)
    vec = pl.BlockSpec((1, 1, D_MODEL), lambda i: (si(i), 0, 0))
    row = lambda n: pl.BlockSpec((tm, n), lambda i: (i, 0))
    w = jnp.pad(router_w, ((0, 0), (0, LANES - N_EXPERTS)))
    bias = jnp.pad(router_b, (0, LANES - N_EXPERTS)).reshape(1, LANES)
    tri = jnp.asarray(np.tril(np.ones((tm, tm), np.float32))).astype(BF16)
    return pl.pallas_call(
        _router_kernel, name="moe_router",
        grid=(nblk,),
        in_specs=[row(na), row(nb), row(D_MODEL), vec,
                  pl.BlockSpec((na + nb, D_MODEL), lambda i: (0, 0)),
                  pl.BlockSpec((1, D_MODEL), lambda i: (0, 0)),
                  vec, vec,
                  pl.BlockSpec((D_MODEL, LANES), lambda i: (0, 0)),
                  pl.BlockSpec((1, LANES), lambda i: (0, 0)),
                  pl.BlockSpec((tm, tm), lambda i: (0, 0))],
        out_specs=[row(D_MODEL), row(D_MODEL), row(LANES), row(LANES),
                   pl.BlockSpec((1, N_EXPERTS, tm), lambda i: (i, 0, 0)),
                   pl.BlockSpec((1, 8, LANES), lambda i: (i, 0, 0))],
        out_shape=[jax.ShapeDtypeStruct((rows, D_MODEL), F32),
                   jax.ShapeDtypeStruct((rows, D_MODEL), BF16),
                   jax.ShapeDtypeStruct((rows, LANES), F32),
                   jax.ShapeDtypeStruct((rows, LANES), F32),
                   jax.ShapeDtypeStruct((nblk, N_EXPERTS, tm), F32),
                   jax.ShapeDtypeStruct((nblk, 8, LANES), F32)],
        compiler_params=_cparams(("parallel",)),
    )(a, b, x, gate1, w_out, g.reshape(1, D_MODEL), sh, sc, w, bias, tri)


MOE_TM = 1024


def _moe_kernel(cnt_ref, x_ref, h_ref, comb_ref, rank_ref, rank_t_ref, gate_ref, gfin_ref, wg_ref, wu_ref, wd_ref,
                o_ref, acc_scr):
    i = pl.program_id(0)
    e = pl.program_id(1)

    @pl.when(e == 0)
    def _():
        acc_scr[...] = jnp.zeros_like(acc_scr)

    col = lax.broadcasted_iota(jnp.int32, (MOE_R, LANES), 1)
    slot_rows = lax.broadcasted_iota(jnp.int32, (MOE_M, MOE_R), 0).astype(F32)
    slot_cols = lax.broadcasted_iota(jnp.int32, (MOE_R, MOE_M), 1).astype(F32)
    for s in range(MOE_TM // MOE_R):
        blk = slice(s * MOE_R, (s + 1) * MOE_R)
        count = cnt_ref[(i * (MOE_TM // MOE_R) + s) * N_EXPERTS + e]
        for m in range(MOE_R // MOE_M):
            @pl.when(count > m * MOE_M)
            def _():
                take = (rank_t_ref[s, pl.ds(e, 1), :] == slot_rows + float(m * MOE_M))
                hc = _dot(jnp.where(take, 1.0, 0.0).astype(BF16), h_ref[blk, :]).astype(BF16)
                gt = _dot(hc, wg_ref[0])
                act = (gt * _sigmoid(gt)) * _dot(hc, wu_ref[0])
                y = _dot(act.astype(BF16), wd_ref[0]).astype(BF16)
                mine = col == e
                rank_e = jnp.sum(jnp.where(mine, rank_ref[blk, :], 0.0), axis=-1, keepdims=True)
                ce = jnp.sum(jnp.where(mine, comb_ref[blk, :], 0.0), axis=-1, keepdims=True)
                put = rank_e == slot_cols + float(m * MOE_M)
                acc_scr[blk, :] += ce * _dot(jnp.where(put, 1.0, 0.0).astype(BF16), y)

    @pl.when(e == pl.num_programs(1) - 1)
    def _():
        y = x_ref[...] + gate_ref[0] * acc_scr[...]
        ms = jnp.mean(y * y, axis=-1, keepdims=True)
        o_ref[...] = y * lax.rsqrt(ms + RMS_EPS) * gfin_ref[...]


def moe_residual_norm(x, h, comb, rank, rank_t, counts, gate, g_final, wg, wu, wd, rows_per_set):
    rows = x.shape[0]
    tm = MOE_TM
    sub = tm // MOE_R
    si = _set_index(tm, rows_per_set)
    cnt = counts[:, 0, :N_EXPERTS].astype(jnp.int32).reshape(-1)
    grid_spec = pltpu.PrefetchScalarGridSpec(
        num_scalar_prefetch=1,
        grid=(rows // tm, N_EXPERTS),
        in_specs=[pl.BlockSpec((tm, D_MODEL), lambda i, e, c: (i, 0)),
                  pl.BlockSpec((tm, D_MODEL), lambda i, e, c: (i, 0)),
                  pl.BlockSpec((tm, LANES), lambda i, e, c: (i, 0)),
                  pl.BlockSpec((tm, LANES), lambda i, e, c: (i, 0)),
                  pl.BlockSpec((sub, N_EXPERTS, MOE_R), lambda i, e, c: (i, 0, 0)),
                  pl.BlockSpec((1, 1, D_MODEL), lambda i, e, c: (si(i), 0, 0)),
                  pl.BlockSpec((1, D_MODEL), lambda i, e, c: (0, 0)),
                  pl.BlockSpec((1, D_MODEL, D_FF_EXPERT), lambda i, e, c: (e, 0, 0)),
                  pl.BlockSpec((1, D_MODEL, D_FF_EXPERT), lambda i, e, c: (e, 0, 0)),
                  pl.BlockSpec((1, D_FF_EXPERT, D_MODEL), lambda i, e, c: (e, 0, 0))],
        out_specs=pl.BlockSpec((tm, D_MODEL), lambda i, e, c: (i, 0)),
        scratch_shapes=[pltpu.VMEM((tm, D_MODEL), F32)])
    return pl.pallas_call(
        _moe_kernel, name="moe_experts",
        grid_spec=grid_spec,
        out_shape=jax.ShapeDtypeStruct((rows, D_MODEL), F32),
        compiler_params=_cparams(("parallel", "arbitrary")),
    )(cnt, x, h, comb, rank, rank_t, gate, g_final.reshape(1, D_MODEL), wg, wu, wd)


def kernel(x_prompt, x_sample, cache_na_k, cache_na_v, state_wkv, c, c_ctx, mod_w, mod_b, norm_mix, norm_ffn, norm_final, na_w_in, fourier_w, na_rel_bias, na_w_out, ffn_w_gate, ffn_w_up, ffn_w_down, rw_w_in, pool_w, pool_scale, shift_mu, decay_w0, decay_up, iclr_a0, iclr_up, gate_up, k_k, k_a, r_k, gn_w, gn_b, rw_w_out, router_w, router_b, moe_w_gate, moe_w_up, moe_w_down):
    cond = jnp.concatenate([c_ctx[None, :], c, jnp.zeros((8 - N_SETS, D_MODEL), F32)], axis=0)
    mods = adaln_all(cond, mod_w, mod_b)[:, :N_SETS].reshape(DEPTH, N_SETS, 6, 1, D_MODEL)
    bf = lambda w: w.astype(BF16)

    xp = x_prompt.reshape(P_ROWS, D_MODEL)
    xs = x_sample.reshape(S_ROWS, D_MODEL)
    streams = {"p": (SEQ, P_ROWS, slice(0, 1)), "s": (DEC_SEQ, DEC_SEQ, slice(1, N_SETS))}
    x = {"p": xp, "s": xs}

    splits = ((0, FOURIER_CH), (FOURIER_CH, FOURIER_CH + NA_DIM),
              (FOURIER_CH + NA_DIM, FOURIER_CH + 2 * NA_DIM), (FOURIER_CH + 2 * NA_DIM, FOURIER_CH + 3 * NA_DIM))
    w_in, w_out = bf(na_w_in[0]), bf(na_w_out[0])
    f_bd = bf(_block_diag(fourier_w[0]))
    ffn_w = (bf(ffn_w_gate[0]), bf(ffn_w_up[0]), bf(ffn_w_down[0]))
    ck = cache_na_k[:, 0].reshape(DEC_BATCH * PAST_LEN, NA_DIM)
    cv = cache_na_v[:, 0].reshape(DEC_BATCH * PAST_LEN, NA_DIM)
    for name, (n, rps, sets) in streams.items():
        sh1, sc1, g1, sh2, sc2, g2 = [mods[0, sets, m] for m in range(6)]
        kv_dtype = F32 if name == "p" else BF16
        f, q, k, v = modulated_matmul(x[name], norm_mix[0], sh1, sc1, w_in, splits, (BF16, BF16, kv_dtype, kv_dtype), rps)
        if name == "p":
            attn = context_attention(q, k, v)
            new_k = k.reshape(BATCH, 1, SEQ, NA_HEADS, HEAD_DIM)
            new_v = v.reshape(BATCH, 1, SEQ, NA_HEADS, HEAD_DIM)
        else:
            attn = neighbourhood_attention(q, k, v, ck, cv, _na_bias_table(na_rel_bias[0]))
        x[name] = mixer_ffn_residual(fourier_mix(f, n, f_bd), attn, x[name], g1, w_out,
                                     norm_ffn[0], sh2, sc2, g2, *ffn_w, rps)

    w_in, w_out = bf(rw_w_in[0]), bf(rw_w_out[0])
    p_bd = bf(_block_diag(pool_w[0]))
    moe_w = (bf(moe_w_gate[0]), bf(moe_w_up[0]), bf(moe_w_down[0]))
    rw = (shift_mu[0], k_k[0], k_a[0], r_k[0], decay_w0[0], iclr_a0[0], decay_up[0], iclr_up[0], gate_up[0])
    out = {}
    for name, (n, rps, sets) in streams.items():
        sh1, sc1, g1, sh2, sc2, g2 = [mods[1, sets, m] for m in range(6)]
        pc, z = modulated_matmul(x[name], norm_mix[1], sh1, sc1, w_in, ((0, POOL_CH), (POOL_CH, POOL_CH + RWKV_IN)),
                                 (F32, F32), rps)
        t1, t2, t3, gate, bonus = rwkv_prep(z, n, *rw)
        if name == "p":
            o_f, st_f = wkv_scan_prompt(t1, t2, t3, 0)
            o_b, st_b = wkv_scan_prompt(t1, t2, t3, 1)
            st = jnp.transpose(jnp.stack([st_f, st_b]), (0, 1, 4, 3, 2))
            st = jnp.transpose(st.reshape(2, BATCH, RWKV_HEADS, HEAD_DIM, HEAD_DIM), (1, 0, 2, 3, 4))
        else:
            o_f, o_b = wkv_scan_sample(t1, t2, t3, _sample_state_lanes(state_wkv[:, 0]))
        mixed = rwkv_post(o_f, o_b, n, gate, bonus, gn_w[0], gn_b[0])
        y, *routed = mixer_residual_router(pool_mix(pc, n, p_bd, pool_scale[0]), mixed, x[name], g1, w_out,
                                           norm_ffn[1], sh2, sc2, router_w[0], router_b[0], rps)
        out[name] = moe_residual_norm(y, *routed, g2, norm_final, *moe_w, rps)

    return (out["p"].reshape(BATCH, SEQ, D_MODEL), out["s"].reshape(DEC_BATCH, DEC_SEQ, D_MODEL),
            new_k, new_v, st[:, None])
```

```python
import functools
import math

import numpy as np
import jax
import jax.numpy as jnp
from jax import lax
from jax.experimental import pallas as pl
from jax.experimental.pallas import tpu as pltpu

F32 = jnp.float32
BF16 = jnp.bfloat16

D_MODEL = 1024
BATCH = 32
SEQ = 256
DEPTH = 2
DEC_BATCH = 2
DEC_SEQ = 1024
PAST_LEN = 512
GRID_W = 64
HEAD_DIM = 64
FOURIER_CH = D_MODEL // 4
FOURIER_GROUPS = 4
FOURIER_GW = FOURIER_CH // FOURIER_GROUPS
NA_DIM = D_MODEL - FOURIER_CH
NA_HEADS = NA_DIM // HEAD_DIM
NA_MAX_ROWS = 8
NA_COLS = 16
POOL_WINDOWS = (2, 4, 8, 16)
POOL_CH = D_MODEL // 4
POOL_GW = POOL_CH // len(POOL_WINDOWS)
RWKV_DIM = D_MODEL - POOL_CH
RWKV_HEADS = RWKV_DIM // HEAD_DIM
DECAY_LORA = 64
ICLR_LORA = 64
GATE_LORA = 128
RWKV_IN = 3 * RWKV_DIM + 2 * DECAY_LORA + 2 * ICLR_LORA + GATE_LORA
D_FF = 2816
N_EXPERTS = 8
D_FF_EXPERT = 1408
RMS_EPS = 1e-6
GN_EPS = 64e-5
L2_EPS = 1e-12
DECAY_SCALE = math.exp(-0.5)
NEG_INF = -1e30

P_ROWS = BATCH * SEQ
S_ROWS = DEC_BATCH * DEC_SEQ
N_ROWS = P_ROWS + S_ROWS
N_SETS = 1 + DEC_BATCH
LANES = 128
VMEM_LIMIT = 56 * 1024 * 1024


def _cparams(sem):
    return pltpu.CompilerParams(dimension_semantics=sem, vmem_limit_bytes=VMEM_LIMIT)


def _sigmoid(x):
    return 0.5 * jnp.tanh(0.5 * x) + 0.5


def _dot(a, b):
    return jnp.dot(a, b, preferred_element_type=F32)


def _dot_nt(a, b):
    return lax.dot_general(a, b, (((1,), (1,)), ((), ())), preferred_element_type=F32)


def _set_index(tm, rows_per_set):
    q = rows_per_set // tm
    return lambda i: i // q


def _modulated(x, g, sh, sc):
    ms = jnp.mean(x * x, axis=-1, keepdims=True)
    return (x * lax.rsqrt(ms + RMS_EPS) * g) * (1.0 + sc) + sh


def _adaln_kernel(c_ref, w_ref, b_ref, o_ref):
    c = c_ref[...]
    s = (c * _sigmoid(c)).astype(BF16)
    o_ref[0] = _dot(s, w_ref[0].astype(BF16)) + b_ref[0]


def adaln_all(cond, mod_w, mod_b):
    tn = 1536
    n = 6 * D_MODEL
    return pl.pallas_call(
        _adaln_kernel, name="adaln",
        grid=(DEPTH, n // tn),
        in_specs=[pl.BlockSpec((8, D_MODEL), lambda l, j: (0, 0)),
                  pl.BlockSpec((1, D_MODEL, tn), lambda l, j: (l, 0, j)),
                  pl.BlockSpec((1, 1, tn), lambda l, j: (l, 0, j))],
        out_specs=pl.BlockSpec((1, 8, tn), lambda l, j: (l, 0, j)),
        out_shape=jax.ShapeDtypeStruct((DEPTH, 8, n), F32),
        compiler_params=_cparams(("parallel", "parallel")),
    )(cond, mod_w, mod_b.reshape(DEPTH, 1, n))


def _modmm_kernel(x_ref, g_ref, sh_ref, sc_ref, w_ref, *o_refs, splits):
    h = _modulated(x_ref[...], g_ref[...], sh_ref[0], sc_ref[0]).astype(BF16)
    for o_ref, (a, b) in zip(o_refs, splits):
        o_ref[...] = _dot(h, w_ref[:, a:b]).astype(o_ref.dtype)


def modulated_matmul(x, g, sh, sc, w, splits, dtypes, rows_per_set, tm=512):
    rows = x.shape[0]
    n_out = w.shape[1]
    si = _set_index(tm, rows_per_set)
    vec = pl.BlockSpec((1, 1, D_MODEL), lambda i: (si(i), 0, 0))
    return pl.pallas_call(
        functools.partial(_modmm_kernel, splits=splits), name="modulated_matmul",
        grid=(rows // tm,),
        in_specs=[pl.BlockSpec((tm, D_MODEL), lambda i: (i, 0)),
                  pl.BlockSpec((1, D_MODEL), lambda i: (0, 0)),
                  vec, vec,
                  pl.BlockSpec((D_MODEL, n_out), lambda i: (0, 0))],
        out_specs=[pl.BlockSpec((tm, b - a), lambda i: (i, 0)) for a, b in splits],
        out_shape=[jax.ShapeDtypeStruct((rows, b - a), dt) for (a, b), dt in zip(splits, dtypes)],
        compiler_params=_cparams(("parallel",)),
    )(x, g.reshape(1, D_MODEL), sh, sc, w)


def _dft_mats(n):
    t = np.arange(n)
    ang = 2.0 * np.pi * ((t[:, None] * t[None, :]) % n) / n
    cn, sn = np.cos(ang) / np.sqrt(n), np.sin(ang) / np.sqrt(n)
    c = np.arange(FOURIER_GW)
    angc = 2.0 * np.pi * ((c[:, None] * c[None, :]) % FOURIER_GW) / FOURIER_GW
    eye = np.eye(FOURIER_GROUPS)
    cc = np.kron(eye, np.cos(angc) / np.sqrt(FOURIER_GW))
    sc = np.kron(eye, np.sin(angc) / np.sqrt(FOURIER_GW))
    as_bf = lambda a: jnp.asarray(a, dtype=F32).astype(BF16)
    return as_bf(cn), as_bf(sn), as_bf(cc), as_bf(sc)


def _fourier_kernel(f_ref, cn_ref, sn_ref, cc_ref, sc_ref, w_ref, o_ref):
    x = f_ref[...].astype(BF16)
    a = _dot(x, cc_ref[...]).astype(BF16)
    b = _dot(x, sc_ref[...]).astype(BF16)
    re = _dot(cn_ref[...], a) - _dot(sn_ref[...], b)
    o_ref[...] = _dot(re.astype(BF16), w_ref[...]).astype(o_ref.dtype)


def _block_diag(w):
    g, c, _ = w.shape
    eye = jnp.eye(g, dtype=w.dtype)
    return (eye[:, None, :, None] * w[:, :, None, :]).reshape(g * c, g * c)


def fourier_mix(f, n, w_bd):
    rows = f.shape[0]
    cn, sn, cc, sc = _dft_mats(n)
    full = lambda shape: pl.BlockSpec(shape, lambda b: (0, 0))
    return pl.pallas_call(
        _fourier_kernel, name="fourier",
        grid=(rows // n,),
        in_specs=[pl.BlockSpec((n, FOURIER_CH), lambda b: (b, 0)),
                  full((n, n)), full((n, n)),
                  full((FOURIER_CH, FOURIER_CH)), full((FOURIER_CH, FOURIER_CH)),
                  full((FOURIER_CH, FOURIER_CH))],
        out_specs=pl.BlockSpec((n, FOURIER_CH), lambda b: (b, 0)),
        out_shape=jax.ShapeDtypeStruct((rows, FOURIER_CH), BF16),
        compiler_params=_cparams(("parallel",)),
    )(f, cn, sn, cc, sc, w_bd)


def _head_pair(ref, c):
    x = ref[:, c * LANES:(c + 1) * LANES]
    low = lax.broadcasted_iota(jnp.int32, x.shape, 1) < HEAD_DIM
    return jnp.where(low, x, 0.0).astype(BF16), jnp.where(low, 0.0, x).astype(BF16), low


def _value_pair(ref, c):
    x = ref[:, c * LANES:(c + 1) * LANES]
    low = lax.broadcasted_iota(jnp.int32, x.shape, 1) < HEAD_DIM
    return jnp.where(low, x, 1.0).astype(BF16), jnp.where(low, 1.0, x).astype(BF16)


def _normalised_pair(res_even, res_odd, low):
    even = res_even / pltpu.roll(res_even, HEAD_DIM, 1)
    odd = res_odd / pltpu.roll(res_odd, HEAD_DIM, 1)
    return jnp.where(low, even, odd)


def _ctx_attn_kernel(q_ref, k_ref, v_ref, o_ref):
    scale = HEAD_DIM ** -0.5

    def scores(c):
        q_even, q_odd, _ = _head_pair(q_ref, c)
        k = k_ref[:, c * LANES:(c + 1) * LANES].astype(BF16)
        return _dot_nt(q_even, k) * scale, _dot_nt(q_odd, k) * scale

    def softmax_numerator(s):
        return jnp.exp(s - jnp.max(s, axis=-1, keepdims=True)).astype(BF16)

    n_pairs = NA_HEADS // 2
    pending = [scores(0)]
    for c in range(n_pairs):
        if c + 1 < n_pairs:
            pending.append(scores(c + 1))
        s_even, s_odd = pending.pop(0)
        v_even, v_odd = _value_pair(v_ref, c)
        low = lax.broadcasted_iota(jnp.int32, (SEQ, LANES), 1) < HEAD_DIM
        o_ref[:, c * LANES:(c + 1) * LANES] = _normalised_pair(
            _dot(softmax_numerator(s_even), v_even), _dot(softmax_numerator(s_odd), v_odd), low).astype(o_ref.dtype)


def context_attention(q, k, v):
    blk = pl.BlockSpec((SEQ, NA_DIM), lambda b: (b, 0))
    return pl.pallas_call(
        _ctx_attn_kernel, name="ctx_attn",
        grid=(BATCH,),
        in_specs=[blk, blk, blk],
        out_specs=blk,
        out_shape=jax.ShapeDtypeStruct((P_ROWS, NA_DIM), BF16),
        compiler_params=_cparams(("parallel",)),
    )(q, k, v)


NA_ROWS = DEC_SEQ // GRID_W
NA_WIN = NA_MAX_ROWS * GRID_W


def _na_bias_table(rel_bias):
    cols = np.arange(GRID_W)
    c0 = np.clip(cols - NA_COLS // 2, 0, GRID_W - NA_COLS)
    col_ok = (cols[None, :] >= c0[:, None]) & (cols[None, :] < c0[:, None] + NA_COLS)
    dc = np.clip(cols[None, :] - cols[:, None] + NA_COLS - 1, 0, 2 * NA_COLS - 2)
    onehot = (dc[None] == np.arange(2 * NA_COLS - 1)[:, None, None]).astype(np.float32)
    toe = jnp.einsum("hrj,jqk->hrqk", rel_bias, jnp.asarray(onehot), precision=lax.Precision.HIGHEST)
    toe = jnp.where(col_ok[None, None], toe, NEG_INF)
    tabs = [jnp.transpose(toe[:, NA_MAX_ROWS - 1 - o: 2 * NA_MAX_ROWS - 1 - o], (0, 2, 1, 3))
            for o in range(NA_MAX_ROWS)]
    return jnp.stack(tabs).reshape(NA_MAX_ROWS, NA_HEADS, GRID_W, NA_WIN)


def _na_row_start(i):
    return jnp.clip(i - NA_MAX_ROWS // 2, 0, NA_ROWS - NA_MAX_ROWS)


def _na_kernel(q_ref, k_ref, v_ref, ck_ref, cv_ref, bias_ref, o_ref):
    scale = HEAD_DIM ** -0.5
    i = pl.program_id(1)
    start = pl.multiple_of(_na_row_start(i) * GRID_W, GRID_W)
    k_win = k_ref.at[pl.ds(start, NA_WIN)]
    v_win = v_ref.at[pl.ds(start, NA_WIN)]

    def scores(c):
        q_even, q_odd, _ = _head_pair(q_ref, c)
        kw = k_win[:, c * LANES:(c + 1) * LANES].astype(BF16)
        ck = ck_ref[:, c * LANES:(c + 1) * LANES].astype(BF16)
        return [(_dot_nt(q, kw) * scale + bias_ref[0, 2 * c + par], _dot_nt(q, ck) * scale)
                for par, q in enumerate((q_even, q_odd))]

    def weighted_values(s, vw, cv):
        s_loc, s_ctx = s
        m = jnp.maximum(jnp.max(s_loc, axis=-1, keepdims=True), jnp.max(s_ctx, axis=-1, keepdims=True))
        return _dot(jnp.exp(s_loc - m).astype(BF16), vw) + _dot(jnp.exp(s_ctx - m).astype(BF16), cv)

    n_pairs = NA_HEADS // 2
    pending = [scores(0)]
    for c in range(n_pairs):
        if c + 1 < n_pairs:
            pending.append(scores(c + 1))
        s_even, s_odd = pending.pop(0)
        vw_even, vw_odd = _value_pair(v_win, c)
        cv_even, cv_odd = _value_pair(cv_ref, c)
        low = lax.broadcasted_iota(jnp.int32, (GRID_W, LANES), 1) < HEAD_DIM
        o_ref[:, c * LANES:(c + 1) * LANES] = _normalised_pair(
            weighted_values(s_even, vw_even, cv_even), weighted_values(s_odd, vw_odd, cv_odd), low).astype(o_ref.dtype)


def neighbourhood_attention(q, k, v, ck, cv, bias_tab):
    seq = pl.BlockSpec((DEC_SEQ, NA_DIM), lambda b, i: (b, 0))
    ctx = pl.BlockSpec((PAST_LEN, NA_DIM), lambda b, i: (b, 0))
    row = pl.BlockSpec((GRID_W, NA_DIM), lambda b, i: (b * NA_ROWS + i, 0))
    return pl.pallas_call(
        _na_kernel, name="na_attn",
        grid=(DEC_BATCH, NA_ROWS),
        in_specs=[row, seq, seq, ctx, ctx,
                  pl.BlockSpec((1, NA_HEADS, GRID_W, NA_WIN), lambda b, i: (i - _na_row_start(i), 0, 0, 0))],
        out_specs=row,
        out_shape=jax.ShapeDtypeStruct((S_ROWS, NA_DIM), BF16),
        compiler_params=_cparams(("parallel", "arbitrary")),
    )(q, k, v, ck, cv, bias_tab)


def _mixer_residual(a_ref, b_ref, x_ref, gate_ref, wo_ref):
    na = a_ref.shape[1]
    y = _dot(a_ref[...].astype(BF16), wo_ref[:na, :]) + _dot(b_ref[...].astype(BF16), wo_ref[na:, :])
    return x_ref[...] + gate_ref[0] * y


def _ffn_kernel(a_ref, b_ref, x_ref, gate1_ref, wo_ref, g_ref, sh_ref, sc_ref, gate2_ref, wg_ref, wu_ref, wd_ref,
                o_ref, y_scr, h_scr, acc_scr):
    j = pl.program_id(1)

    @pl.when(j == 0)
    def _():
        y = _mixer_residual(a_ref, b_ref, x_ref, gate1_ref, wo_ref)
        y_scr[...] = y
        h_scr[...] = _modulated(y, g_ref[...], sh_ref[0], sc_ref[0]).astype(BF16)
        acc_scr[...] = jnp.zeros_like(acc_scr)

    h = h_scr[...]
    gt = _dot(h, wg_ref[...])
    act = (gt * _sigmoid(gt)) * _dot(h, wu_ref[...])
    acc_scr[...] += _dot(act.astype(BF16), wd_ref[...])

    @pl.when(j == pl.num_programs(1) - 1)
    def _():
        o_ref[...] = y_scr[...] + gate2_ref[0] * acc_scr[...]


def mixer_ffn_residual(a, b, x, gate1, w_out, g, sh, sc, gate2, wg, wu, wd, rows_per_set, tm=256, tf=D_FF):
    rows = x.shape[0]
    na, nb = a.shape[1], b.shape[1]
    si = _set_index(tm, rows_per_set)
    vec = pl.BlockSpec((1, 1, D_MODEL), lambda i, j: (si(i), 0, 0))
    row = lambda n: pl.BlockSpec((tm, n), lambda i, j: (i, 0))
    return pl.pallas_call(
        _ffn_kernel, name="ffn",
        grid=(rows // tm, D_FF // tf),
        in_specs=[row(na), row(nb), row(D_MODEL), vec,
                  pl.BlockSpec((na + nb, D_MODEL), lambda i, j: (0, 0)),
                  pl.BlockSpec((1, D_MODEL), lambda i, j: (0, 0)),
                  vec, vec, vec,
                  pl.BlockSpec((D_MODEL, tf), lambda i, j: (0, j)),
                  pl.BlockSpec((D_MODEL, tf), lambda i, j: (0, j)),
                  pl.BlockSpec((tf, D_MODEL), lambda i, j: (j, 0))],
        out_specs=row(D_MODEL),
        out_shape=jax.ShapeDtypeStruct((rows, D_MODEL), F32),
        scratch_shapes=[pltpu.VMEM((tm, D_MODEL), F32), pltpu.VMEM((tm, D_MODEL), BF16),
                        pltpu.VMEM((tm, D_MODEL), F32)],
        compiler_params=_cparams(("parallel", "arbitrary")),
    )(a, b, x, gate1, w_out, g.reshape(1, D_MODEL), sh, sc, gate2, wg, wu, wd)


def _pool_consts(n):
    t = np.arange(n)
    mats, cnts = [], []
    for win in POOL_WINDOWS:
        lo = np.clip(t - win // 2, 0, n)
        hi = np.clip(t + win - win // 2, 0, n)
        mats.append(((t[None, :] >= lo[:, None]) & (t[None, :] < hi[:, None])).astype(np.float32))
        cnts.append(np.repeat((hi - lo).astype(np.float32)[:, None], POOL_GW, axis=1))
    return jnp.asarray(np.stack(mats)).astype(BF16), jnp.asarray(np.concatenate(cnts, axis=1))


def _pool_kernel(x_ref, pm_ref, cnt_ref, w_ref, scale_ref, o_ref):
    x = x_ref[...]
    hi = x.astype(BF16)
    lo = (x - hi.astype(F32)).astype(BF16)
    sums = []
    for g in range(len(POOL_WINDOWS)):
        sl = slice(g * POOL_GW, (g + 1) * POOL_GW)
        sums.append(_dot(pm_ref[g], hi[:, sl]) + _dot(pm_ref[g], lo[:, sl]))
    y = jnp.concatenate(sums, axis=-1) / cnt_ref[...] - x
    o_ref[...] = (_dot(y.astype(BF16), w_ref[...]) * scale_ref[...]).astype(o_ref.dtype)


def pool_mix(x, n, w_bd, scale):
    rows = x.shape[0]
    pm, cnt = _pool_consts(n)
    return pl.pallas_call(
        _pool_kernel, name="pool",
        grid=(rows // n,),
        in_specs=[pl.BlockSpec((n, POOL_CH), lambda b: (b, 0)),
                  pl.BlockSpec((len(POOL_WINDOWS), n, n), lambda b: (0, 0, 0)),
                  pl.BlockSpec((n, POOL_CH), lambda b: (0, 0)),
                  pl.BlockSpec((POOL_CH, POOL_CH), lambda b: (0, 0)),
                  pl.BlockSpec((1, POOL_CH), lambda b: (0, 0))],
        out_specs=pl.BlockSpec((n, POOL_CH), lambda b: (b, 0)),
        out_shape=jax.ShapeDtypeStruct((rows, POOL_CH), BF16),
        compiler_params=_cparams(("parallel",)),
    )(x, pm, cnt, w_bd, scale.reshape(1, POOL_CH))


RW_TILE = 256
HALO = 8
SUB = 8
PACK_R = RWKV_HEADS * SUB


def _head_ones():
    h = np.arange(RWKV_DIM) // HEAD_DIM
    return jnp.asarray((h[:, None] == h[None, :]).astype(np.float32)).astype(BF16)


def _head_sum(x, ones):
    hi = x.astype(BF16)
    lo = (x - hi.astype(F32)).astype(BF16)
    return _dot(hi, ones) + _dot(lo, ones)


def _pack_heads(a, b, o_ref, lead):
    n = a.shape[0]
    lane = lax.broadcasted_iota(jnp.int32, (n, LANES), 1)
    low = lane < HEAD_DIM
    for c in range(RWKV_DIM // LANES):
        ac = a[:, c * LANES:(c + 1) * LANES]
        bc = b[:, c * LANES:(c + 1) * LANES]
        even = jnp.where(low, ac, pltpu.roll(bc, HEAD_DIM, 1))
        odd = jnp.where(low, pltpu.roll(ac, HEAD_DIM, 1), bc)
        for h, val in ((2 * c, even), (2 * c + 1, odd)):
            o_ref[lead + (slice(None), slice(h * SUB, (h + 1) * SUB), slice(None))] = val.reshape(n // SUB, SUB, LANES)


def _rwkv_prep_kernel(z_ref, zp_ref, zn_ref, mu_ref, kk_w_ref, ka_ref, rk_ref, w0_ref, a0_ref,
                      dup_ref, iup_ref, gup_ref, ones_ref,
                      g1_ref, g2_ref, g3_ref, g_ref, bonus_ref, *, tiles_per_seq):
    i = pl.program_id(0)
    pos = i % tiles_per_seq
    z = z_ref[...]
    row = lax.broadcasted_iota(jnp.int32, (HALO, 1), 0)
    prev_edge = jnp.where(pos == 0, 0.0, zp_ref[HALO - 1:HALO, :])
    next_edge = jnp.where(pos == tiles_per_seq - 1, 0.0, zn_ref[0:1, :])
    prev = pltpu.roll(z, 1, 0)
    prev = jnp.concatenate([jnp.where(row == 0, prev_edge, prev[:HALO]), prev[HALO:]], axis=0)
    nxt = pltpu.roll(z, RW_TILE - 1, 0)
    nxt = jnp.concatenate([nxt[:-HALO], jnp.where(row == HALO - 1, next_edge, nxt[-HALO:])], axis=0)
    mu_prev, mu_next = mu_ref[0:1, :], mu_ref[1:2, :]
    zr = z * (1.0 - mu_prev - mu_next) + mu_prev * prev + mu_next * nxt

    d = RWKV_DIM
    r, k, v = zr[:, :d], zr[:, d:2 * d], zr[:, 2 * d:3 * d]
    lora = 3 * d
    ones = ones_ref[...]
    kk = k * kk_w_ref[...]
    kk = kk * lax.rsqrt(_head_sum(kk * kk, ones) + L2_EPS)
    _pack_heads(r, v, g3_ref, ())
    for dr in range(2):
        wl = zr[:, lora + dr * DECAY_LORA: lora + (dr + 1) * DECAY_LORA]
        al = zr[:, lora + 2 * DECAY_LORA + dr * ICLR_LORA: lora + 2 * DECAY_LORA + (dr + 1) * ICLR_LORA]
        lw = w0_ref[dr:dr + 1, :] + _dot(jnp.tanh(wl).astype(BF16), dup_ref[dr])
        w = jnp.exp(-DECAY_SCALE * _sigmoid(lw))
        a = _sigmoid(a0_ref[dr:dr + 1, :] + _dot(al.astype(BF16), iup_ref[dr]))
        _pack_heads(w, kk * a, g1_ref, (dr,))
        _pack_heads(k * (1.0 + (a - 1.0) * ka_ref[...]), kk, g2_ref, (dr,))
    gl = zr[:, lora + 2 * DECAY_LORA + 2 * ICLR_LORA:]
    g_ref[...] = _dot(_sigmoid(gl).astype(BF16), gup_ref[...])
    bonus_ref[...] = _head_sum(r * k * rk_ref[...], ones) * v


def rwkv_prep(z, n, mu, k_k, k_a, r_k, w0, a0, dup, iup, gup):
    rows = z.shape[0]
    nb = rows // n
    tps = n // RW_TILE
    hb = RW_TILE // HALO
    last = rows // HALO - 1
    d = RWKV_DIM
    full2 = lambda shape: pl.BlockSpec(shape, lambda i: (0, 0))
    full3 = lambda shape: pl.BlockSpec(shape, lambda i: (0, 0, 0))
    tile = pl.BlockSpec((RW_TILE, d), lambda i: (i, 0))
    pk2 = pl.BlockSpec((2, RW_TILE // SUB, PACK_R, LANES), lambda i: (0, i % tps, i // tps, 0))
    pk1 = pl.BlockSpec((RW_TILE // SUB, PACK_R, LANES), lambda i: (i % tps, i // tps, 0))
    return pl.pallas_call(
        functools.partial(_rwkv_prep_kernel, tiles_per_seq=tps), name="rwkv_prep",
        grid=(rows // RW_TILE,),
        in_specs=[pl.BlockSpec((RW_TILE, RWKV_IN), lambda i: (i, 0)),
                  pl.BlockSpec((HALO, RWKV_IN), lambda i: (jnp.maximum(i * hb - 1, 0), 0)),
                  pl.BlockSpec((HALO, RWKV_IN), lambda i: (jnp.minimum((i + 1) * hb, last), 0)),
                  full2((2, RWKV_IN)), full2((1, d)), full2((1, d)), full2((1, d)),
                  full2((2, d)), full2((2, d)),
                  full3((2, DECAY_LORA, d)), full3((2, ICLR_LORA, d)), full2((GATE_LORA, d)),
                  full2((d, d))],
        out_specs=[pk2, pk2, pk1, tile, tile],
        out_shape=[jax.ShapeDtypeStruct((2, n // SUB, nb * PACK_R, LANES), F32),
                   jax.ShapeDtypeStruct((2, n // SUB, nb * PACK_R, LANES), F32),
                   jax.ShapeDtypeStruct((n // SUB, nb * PACK_R, LANES), F32),
                   jax.ShapeDtypeStruct((rows, d), F32),
                   jax.ShapeDtypeStruct((rows, d), F32)],
        compiler_params=_cparams(("parallel",)),
    )(z, z, z, mu, k_k.reshape(1, d), k_a.reshape(1, d), r_k.reshape(1, d), w0, a0,
      dup.astype(BF16), iup.astype(BF16), gup.astype(BF16), _head_ones())


SCAN_TC = 64
SCAN_G = SCAN_TC // SUB
SLOTS = 4
V_BLOCK = 32
K_CHUNK = 16
N_KC = HEAD_DIM // K_CHUNK
PEEL = 2


def _wkv_first_sa(s_scr, t2, vs):
    sas = []
    for vb in range(vs // V_BLOCK):
        rows = slice(vb * V_BLOCK, (vb + 1) * V_BLOCK)

        def chunk(kc, sa):
            for j in range(K_CHUNK):
                sa = sa + s_scr[kc * K_CHUNK + j, rows, :] * t2[N_KC + kc, j:j + 1, :]
            return sa

        sas.append(lax.fori_loop(0, N_KC, chunk, jnp.zeros((V_BLOCK, LANES), F32)))
    return tuple(sas)


def _wkv_step(s_scr, t1, t2, t3, t2_next, v_blocks, o_ref, sas):
    nxt = []
    for vb, v_blk in enumerate(v_blocks):
        rows = slice(vb * V_BLOCK, (vb + 1) * V_BLOCK)
        sa = sas[vb]

        def chunk(kc, carry):
            o, sa_n = carry
            for j in range(K_CHUNK):
                k = kc * K_CHUNK + j
                s_new = s_scr[k, rows, :] - sa * t1[N_KC + kc, j:j + 1, :] + v_blk * t2[kc, j:j + 1, :]
                s_scr[k, rows, :] = s_new
                o = o + s_new * t3[kc, j:j + 1, :]
                sa_n = sa_n + s_new * t2_next[N_KC + kc, j:j + 1, :]
            return o, sa_n

        zero = jnp.zeros((V_BLOCK, LANES), F32)
        carry = (zero, zero)
        for kc in range(PEEL):
            carry = chunk(kc, carry)
        o, sa_n = lax.fori_loop(PEEL, N_KC, chunk, carry)
        o_ref[rows, :] = o
        nxt.append(sa_n)
    return tuple(nxt)


def _store_tile(ref, idx, x):
    ref[idx] = x.reshape(2 * N_KC, K_CHUNK, LANES)


def _store_scaled(tile_ref, t1, t2, t3, g, advance):
    d = HEAD_DIM
    g_new = g * t1[:d]
    inv = 1.0 / g_new
    _store_tile(tile_ref, 0, jnp.concatenate([t1[:d], t1[d:] * inv], axis=0))
    _store_tile(tile_ref, 1, jnp.concatenate([t2[:d] * inv, t2[d:] * g], axis=0))
    _store_tile(tile_ref, 2, jnp.concatenate([t3[:d] * g_new, t3[d:]], axis=0))
    return g_new if advance is True else jnp.where(advance, g_new, g)


def _unscale_state(s_scr, g_scr, g):
    g_scr[...] = g.reshape(N_KC, K_CHUNK, LANES)

    def chunk(kc, carry):
        for j in range(K_CHUNK):
            k = kc * K_CHUNK + j
            s_scr[k] = s_scr[k] * g_scr[kc, j:j + 1, :]
        return carry

    lax.fori_loop(0, N_KC, chunk, 0)


def _step_rows(ref, lead, grp, sub, n):
    return ref.at[lead + (grp,)][pl.ds(sub, n, stride=SUB), :]


def _scan_prompt_kernel(g1_ref, g2_ref, g3_ref, o_ref, st_ref, s_scr, g_scr, ta, tb, tc, td, oa, ob, *, reverse):
    c = pl.program_id(1)
    tiles = (ta, tb, tc, td)
    outs = (oa, ob)

    @pl.when(c == 0)
    def _():
        s_scr[...] = jnp.zeros_like(s_scr)

    for o_scr in outs:
        o_scr[...] = jnp.zeros_like(o_scr)

    def where(grp, sub):
        return (SCAN_G - 1 - grp, SUB - 1 - sub) if reverse else (grp, sub)

    def load_tiles(grp, sub, slot, dec, advance=True):
        g, s = where(grp, sub)
        return _store_scaled(tiles[slot], _step_rows(g1_ref, (0,), g, s, LANES).T,
                             _step_rows(g2_ref, (0,), g, s, LANES).T,
                             _step_rows(g3_ref, (), g, s, LANES).T, dec, advance)

    def flush(grp, sub, o_scr):
        g, s = where(grp, sub)
        o_ref.at[0, g][pl.ds(s, LANES, stride=SUB), :] = o_scr[...].T

    dec0 = load_tiles(0, 0, 0, jnp.ones((HEAD_DIM, LANES), F32))
    dec0 = load_tiles(0, 1, 1, dec0)
    sas0 = _wkv_first_sa(s_scr, tiles[0].at[1], HEAD_DIM)
    per_blk = V_BLOCK // K_CHUNK

    def group(grp, carry):
        sas, dec = carry
        for j in range(SUB):
            cur, nxt = tiles[j % SLOTS], tiles[(j + 1) % SLOTS]
            ahead = j + 2
            if ahead < SUB:
                dec = load_tiles(grp, ahead, ahead % SLOTS, dec)
            else:
                dec = load_tiles(jnp.minimum(grp + 1, SCAN_G - 1), ahead % SUB, ahead % SLOTS, dec,
                                 advance=grp + 1 < SCAN_G)
            if j > 0:
                flush(grp, j - 1, outs[(j - 1) % 2])
            v_blocks = [cur[2, N_KC + vb * per_blk: N_KC + (vb + 1) * per_blk].reshape(V_BLOCK, LANES)
                        for vb in range(HEAD_DIM // V_BLOCK)]
            sas = _wkv_step(s_scr, cur.at[0], cur.at[1], cur.at[2], nxt.at[1], v_blocks, outs[j % 2], sas)
        flush(grp, SUB - 1, outs[(SUB - 1) % 2])
        return sas, dec

    _, dec = lax.fori_loop(0, SCAN_G, group, (sas0, dec0))
    _unscale_state(s_scr, g_scr, dec)

    @pl.when(c == pl.num_programs(1) - 1)
    def _():
        st_ref[0] = s_scr[...]


def wkv_scan_prompt(g1, g2, g3, direction):
    n_grp, rows = g3.shape[0], g3.shape[1]
    groups = rows // (LANES * SUB)
    nblk = n_grp // SCAN_G
    tb = (lambda s: nblk - 1 - s) if direction else (lambda s: s)
    blk = (SCAN_G, LANES * SUB, LANES)
    dir_blk = pl.BlockSpec((1,) + blk, lambda g, s: (direction, tb(s), g, 0))
    return pl.pallas_call(
        functools.partial(_scan_prompt_kernel, reverse=bool(direction)), name="wkv_scan_prompt",
        grid=(groups, nblk),
        in_specs=[dir_blk, dir_blk, pl.BlockSpec(blk, lambda g, s: (tb(s), g, 0))],
        out_specs=[pl.BlockSpec((1,) + blk, lambda g, s: (0, tb(s), g, 0)),
                   pl.BlockSpec((1, HEAD_DIM, HEAD_DIM, LANES), lambda g, s: (g, 0, 0, 0))],
        out_shape=[jax.ShapeDtypeStruct((1, n_grp, rows, LANES), F32),
                   jax.ShapeDtypeStruct((groups, HEAD_DIM, HEAD_DIM, LANES), F32)],
        scratch_shapes=([pltpu.VMEM((HEAD_DIM, HEAD_DIM, LANES), F32),
                         pltpu.VMEM((N_KC, K_CHUNK, LANES), F32)]
                        + [pltpu.VMEM((3, 2 * N_KC, K_CHUNK, LANES), F32)] * SLOTS
                        + [pltpu.VMEM((LANES, LANES), F32)] * 2),
        compiler_params=_cparams(("parallel", "arbitrary")),
    )(g1, g2, g3)


S_CHAINS = DEC_BATCH * RWKV_HEADS
S_VS = HEAD_DIM // 2


def _scan_sample_kernel(g1f_ref, g1b_ref, g2f_ref, g2b_ref, g3f_ref, g3b_ref, s0_ref,
                        of_ref, ob_ref, s_scr, g_scr, ta, tb, tc, td, va, vb, vc, vd, oa, ob):
    c = pl.program_id(0)
    tiles = (ta, tb, tc, td)
    vals = (va, vb, vc, vd)
    outs = (oa, ob)

    @pl.when(c == 0)
    def _():
        s_scr[...] = s0_ref[...]

    nc = 2 * S_CHAINS
    n_pad = LANES - 2 * nc
    zpad = jnp.zeros((n_pad, LANES), F32)
    wpad = jnp.where(lax.broadcasted_iota(jnp.int32, (n_pad, LANES), 1) < HEAD_DIM, 1.0, 0.0)
    lane = lax.broadcasted_iota(jnp.int32, (S_VS, LANES), 1)

    def stacked_t(f_ref, b_ref, grp, sub, pad):
        f = _step_rows(f_ref, (0,), grp, sub, S_CHAINS)
        b = _step_rows(b_ref, (0,), SCAN_G - 1 - grp, SUB - 1 - sub, S_CHAINS)
        return jnp.concatenate([f, b, f, b, pad], axis=0).T

    def load_tiles(grp, sub, slot, dec, advance=True):
        t3 = stacked_t(g3f_ref, g3b_ref, grp, sub, zpad)
        vals[slot][...] = jnp.where(lane < nc, t3[HEAD_DIM:HEAD_DIM + S_VS], t3[HEAD_DIM + S_VS:])
        return _store_scaled(tiles[slot], stacked_t(g1f_ref, g1b_ref, grp, sub, wpad),
                             stacked_t(g2f_ref, g2b_ref, grp, sub, zpad), t3, dec, advance)

    def flush(grp, sub, o_scr):
        o = o_scr[...]
        full = jnp.concatenate([o, pltpu.roll(o, LANES - nc, 1), jnp.zeros((LANES - HEAD_DIM, LANES), F32)], axis=0)
        ot = full.T
        of_ref.at[0, grp][pl.ds(sub, S_CHAINS, stride=SUB), :] = ot[0:S_CHAINS]
        ob_ref.at[0, SCAN_G - 1 - grp][pl.ds(SUB - 1 - sub, S_CHAINS, stride=SUB), :] = ot[S_CHAINS:nc]

    dec0 = load_tiles(0, 0, 0, jnp.ones((HEAD_DIM, LANES), F32))
    dec0 = load_tiles(0, 1, 1, dec0)
    sas0 = _wkv_first_sa(s_scr, tiles[0].at[1], S_VS)

    def group(grp, carry):
        sas, dec = carry
        for j in range(SUB):
            cur, nxt = tiles[j % SLOTS], tiles[(j + 1) % SLOTS]
            ahead = j + 2
            if ahead < SUB:
                dec = load_tiles(grp, ahead, ahead % SLOTS, dec)
            else:
                dec = load_tiles(jnp.minimum(grp + 1, SCAN_G - 1), ahead % SUB, ahead % SLOTS, dec,
                                 advance=grp + 1 < SCAN_G)
            if j > 0:
                flush(grp, j - 1, outs[(j - 1) % 2])
            sas = _wkv_step(s_scr, cur.at[0], cur.at[1], cur.at[2], nxt.at[1], [vals[j % SLOTS][...]],
                            outs[j % 2], sas)
        flush(grp, SUB - 1, outs[(SUB - 1) % 2])
        return sas, dec

    _, dec = lax.fori_loop(0, SCAN_G, group, (sas0, dec0))
    _unscale_state(s_scr, g_scr, dec)


def wkv_scan_sample(g1, g2, g3, s0):
    n_grp, rows = g3.shape[0], g3.shape[1]
    g3 = g3.reshape(1, n_grp, rows, LANES)
    nblk = n_grp // SCAN_G
    blk = (1, SCAN_G, rows, LANES)
    fwd = lambda d: pl.BlockSpec(blk, lambda s: (d, s, 0, 0))
    bwd = lambda d: pl.BlockSpec(blk, lambda s: (d, nblk - 1 - s, 0, 0))
    return pl.pallas_call(
        _scan_sample_kernel, name="wkv_scan_sample",
        grid=(nblk,),
        in_specs=[fwd(0), bwd(1), fwd(0), bwd(1), fwd(0), bwd(0),
                  pl.BlockSpec((HEAD_DIM, S_VS, LANES), lambda s: (0, 0, 0))],
        out_specs=[fwd(0), bwd(0)],
        out_shape=[jax.ShapeDtypeStruct((1, n_grp, rows, LANES), F32)] * 2,
        scratch_shapes=([pltpu.VMEM((HEAD_DIM, S_VS, LANES), F32),
                         pltpu.VMEM((N_KC, K_CHUNK, LANES), F32)]
                        + [pltpu.VMEM((3, 2 * N_KC, K_CHUNK, LANES), F32)] * SLOTS
                        + [pltpu.VMEM((S_VS, LANES), F32)] * SLOTS
                        + [pltpu.VMEM((S_VS, LANES), F32)] * 2),
        compiler_params=_cparams(("arbitrary",)),
    )(g1, g1, g2, g2, g3, g3, s0)


def _sample_state_lanes(s0):
    nc = 2 * S_CHAINS
    st = jnp.transpose(s0, (4, 3, 1, 0, 2)).reshape(HEAD_DIM, HEAD_DIM, nc)
    st = jnp.concatenate([st[:, :S_VS], st[:, S_VS:]], axis=-1)
    return jnp.pad(st, ((0, 0), (0, 0), (0, LANES - 2 * nc)))


def _rwkv_post_kernel(of_ref, ob_ref, g_ref, bonus_ref, gw_ref, gb_ref, ones_ref, y_ref):
    head_sum = functools.partial(_head_sum, ones=ones_ref[...])
    lane = lax.broadcasted_iota(jnp.int32, (RW_TILE, LANES), 1)
    low = lane < HEAD_DIM

    def head(h):
        rows = slice(h * SUB, (h + 1) * SUB)
        return (of_ref[0, :, rows, :] + ob_ref[0, :, rows, :]).reshape(RW_TILE, LANES)

    cols = [jnp.where(low, head(2 * c), pltpu.roll(head(2 * c + 1), HEAD_DIM, 1))
            for c in range(RWKV_DIM // LANES)]
    o = jnp.concatenate(cols, axis=-1)
    mu = head_sum(o) / HEAD_DIM
    oc = o - mu
    var = head_sum(oc * oc) / HEAD_DIM
    on = (oc * lax.rsqrt(var + GN_EPS)) * gw_ref[...] + gb_ref[...]
    y_ref[...] = ((on + bonus_ref[...]) * g_ref[...]).astype(y_ref.dtype)


def rwkv_post(o_f, o_b, n, g, bonus, gn_w, gn_b):
    rows, d = g.shape
    tps = n // RW_TILE
    tile = pl.BlockSpec((RW_TILE, d), lambda i: (i, 0))
    vec = pl.BlockSpec((1, d), lambda i: (0, 0))
    pk = pl.BlockSpec((1, RW_TILE // SUB, PACK_R, LANES), lambda i: (0, i % tps, i // tps, 0))
    return pl.pallas_call(
        _rwkv_post_kernel, name="rwkv_post",
        grid=(rows // RW_TILE,),
        in_specs=[pk, pk, tile, tile, vec, vec, pl.BlockSpec((d, d), lambda i: (0, 0))],
        out_specs=tile,
        out_shape=jax.ShapeDtypeStruct((rows, d), BF16),
        compiler_params=_cparams(("parallel",)),
    )(o_f, o_b, g, bonus, gn_w.reshape(1, d), gn_b.reshape(1, d), _head_ones())


MOE_R = 512
MOE_M = 128


def _split3(x):
    a = x.astype(BF16)
    r1 = x - a.astype(F32)
    b = r1.astype(BF16)
    c = (r1 - b.astype(F32)).astype(BF16)
    return a, b, c


def _router_kernel(mix_a_ref, mix_b_ref, x_ref, gate1_ref, wo_ref, g_ref, sh_ref, sc_ref, w_ref, b_ref, tri_ref,
                   y_ref, h_ref, comb_ref, rank_ref, rank_t_ref, cnt_ref):
    y = _mixer_residual(mix_a_ref, mix_b_ref, x_ref, gate1_ref, wo_ref)
    y_ref[...] = y
    h = _modulated(y, g_ref[...], sh_ref[0], sc_ref[0])
    h_ref[...] = h.astype(BF16)
    h1, h2, h3 = _split3(h)
    w1, w2, w3 = _split3(w_ref[...])
    logits = (_dot(h1, w1) + (_dot(h1, w2) + _dot(h2, w1))
              + (_dot(h1, w3) + _dot(h2, w2) + _dot(h3, w1))) + b_ref[...]
    col = lax.broadcasted_iota(jnp.int32, logits.shape, 1)
    logits = jnp.where(col < N_EXPERTS, logits, -jnp.inf)
    m1 = jnp.max(logits, axis=-1, keepdims=True)
    i1 = jnp.min(jnp.where(logits == m1, col, LANES), axis=-1, keepdims=True)
    rest = jnp.where(col == i1, -jnp.inf, logits)
    m2 = jnp.max(rest, axis=-1, keepdims=True)
    i2 = jnp.min(jnp.where(rest == m2, col, LANES), axis=-1, keepdims=True)
    e2 = jnp.exp(m2 - m1)
    den = 1.0 + e2
    comb_ref[...] = jnp.where(col == i1, 1.0 / den, 0.0) + jnp.where(col == i2, e2 / den, 0.0)
    chosen = (col == i1) | (col == i2)
    upto = _dot(tri_ref[...], jnp.where(chosen, 1.0, 0.0).astype(BF16))
    rank = jnp.where(chosen, upto - 1.0, -1.0)
    rank_ref[...] = rank
    rank_t_ref[0] = rank.T[0:N_EXPERTS, :]
    cnt_ref[0] = jnp.broadcast_to(upto[MOE_R - 1:MOE_R, :], (8, LANES))


def mixer_residual_router(a, b, x, gate1, w_out, g, sh, sc, router_w, router_b, rows_per_set):
    rows = x.shape[0]
    na, nb = a.shape[1], b.shape[1]
    tm = MOE_R
    nblk = rows // tm
    si = _set_index(tm, rows_per_set)
    vec = pl.BlockSpec((1, 1, D_MODEL), lambda i: (si(i), 0, 0))
    row = lambda n: pl.BlockSpec((tm, n), lambda i: (i, 0))
    w = jnp.pad(router_w, ((0, 0), (0, LANES - N_EXPERTS)))
    bias = jnp.pad(router_b, (0, LANES - N_EXPERTS)).reshape(1, LANES)
    tri = jnp.asarray(np.tril(np.ones((tm, tm), np.float32))).astype(BF16)
    return pl.pallas_call(
        _router_kernel, name="moe_router",
        grid=(nblk,),
        in_specs=[row(na), row(nb), row(D_MODEL), vec,
                  pl.BlockSpec((na + nb, D_MODEL), lambda i: (0, 0)),
                  pl.BlockSpec((1, D_MODEL), lambda i: (0, 0)),
                  vec, vec,
                  pl.BlockSpec((D_MODEL, LANES), lambda i: (0, 0)),
                  pl.BlockSpec((1, LANES), lambda i: (0, 0)),
                  pl.BlockSpec((tm, tm), lambda i: (0, 0))],
        out_specs=[row(D_MODEL), row(D_MODEL), row(LANES), row(LANES),
                   pl.BlockSpec((1, N_EXPERTS, tm), lambda i: (i, 0, 0)),
                   pl.BlockSpec((1, 8, LANES), lambda i: (i, 0, 0))],
        out_shape=[jax.ShapeDtypeStruct((rows, D_MODEL), F32),
                   jax.ShapeDtypeStruct((rows, D_MODEL), BF16),
                   jax.ShapeDtypeStruct((rows, LANES), F32),
                   jax.ShapeDtypeStruct((rows, LANES), F32),
                   jax.ShapeDtypeStruct((nblk, N_EXPERTS, tm), F32),
                   jax.ShapeDtypeStruct((nblk, 8, LANES), F32)],
        compiler_params=_cparams(("parallel",)),
    )(a, b, x, gate1, w_out, g.reshape(1, D_MODEL), sh, sc, w, bias, tri)


MOE_TM = 1024


def _moe_kernel(cnt_ref, x_ref, h_ref, comb_ref, rank_ref, rank_t_ref, gate_ref, gfin_ref, wg_ref, wu_ref, wd_ref,
                o_ref, acc_scr):
    i = pl.program_id(0)
    e = pl.program_id(1)

    @pl.when(e == 0)
    def _():
        acc_scr[...] = jnp.zeros_like(acc_scr)

    col = lax.broadcasted_iota(jnp.int32, (MOE_R, LANES), 1)
    slot_rows = lax.broadcasted_iota(jnp.int32, (MOE_M, MOE_R), 0).astype(F32)
    slot_cols = lax.broadcasted_iota(jnp.int32, (MOE_R, MOE_M), 1).astype(F32)
    for s in range(MOE_TM // MOE_R):
        blk = slice(s * MOE_R, (s + 1) * MOE_R)
        count = cnt_ref[(i * (MOE_TM // MOE_R) + s) * N_EXPERTS + e]
        for m in range(MOE_R // MOE_M):
            @pl.when(count > m * MOE_M)
            def _():
                take = (rank_t_ref[s, pl.ds(e, 1), :] == slot_rows + float(m * MOE_M))
                hc = _dot(jnp.where(take, 1.0, 0.0).astype(BF16), h_ref[blk, :]).astype(BF16)
                gt = _dot(hc, wg_ref[0])
                act = (gt * _sigmoid(gt)) * _dot(hc, wu_ref[0])
                y = _dot(act.astype(BF16), wd_ref[0]).astype(BF16)
                mine = col == e
                rank_e = jnp.sum(jnp.where(mine, rank_ref[blk, :], 0.0), axis=-1, keepdims=True)
                ce = jnp.sum(jnp.where(mine, comb_ref[blk, :], 0.0), axis=-1, keepdims=True)
                put = rank_e == slot_cols + float(m * MOE_M)
                acc_scr[blk, :] += ce * _dot(jnp.where(put, 1.0, 0.0).astype(BF16), y)

    @pl.when(e == pl.num_programs(1) - 1)
    def _():
        y = x_ref[...] + gate_ref[0] * acc_scr[...]
        ms = jnp.mean(y * y, axis=-1, keepdims=True)
        o_ref[...] = y * lax.rsqrt(ms + RMS_EPS) * gfin_ref[...]


def moe_residual_norm(x, h, comb, rank, rank_t, counts, gate, g_final, wg, wu, wd, rows_per_set):
    rows = x.shape[0]
    tm = MOE_TM
    sub = tm // MOE_R
    si = _set_index(tm, rows_per_set)
    cnt = counts[:, 0, :N_EXPERTS].astype(jnp.int32).reshape(-1)
    grid_spec = pltpu.PrefetchScalarGridSpec(
        num_scalar_prefetch=1,
        grid=(rows // tm, N_EXPERTS),
        in_specs=[pl.BlockSpec((tm, D_MODEL), lambda i, e, c: (i, 0)),
                  pl.BlockSpec((tm, D_MODEL), lambda i, e, c: (i, 0)),
                  pl.BlockSpec((tm, LANES), lambda i, e, c: (i, 0)),
                  pl.BlockSpec((tm, LANES), lambda i, e, c: (i, 0)),
                  pl.BlockSpec((sub, N_EXPERTS, MOE_R), lambda i, e, c: (i, 0, 0)),
                  pl.BlockSpec((1, 1, D_MODEL), lambda i, e, c: (si(i), 0, 0)),
                  pl.BlockSpec((1, D_MODEL), lambda i, e, c: (0, 0)),
                  pl.BlockSpec((1, D_MODEL, D_FF_EXPERT), lambda i, e, c: (e, 0, 0)),
                  pl.BlockSpec((1, D_MODEL, D_FF_EXPERT), lambda i, e, c: (e, 0, 0)),
                  pl.BlockSpec((1, D_FF_EXPERT, D_MODEL), lambda i, e, c: (e, 0, 0))],
        out_specs=pl.BlockSpec((tm, D_MODEL), lambda i, e, c: (i, 0)),
        scratch_shapes=[pltpu.VMEM((tm, D_MODEL), F32)])
    return pl.pallas_call(
        _moe_kernel, name="moe_experts",
        grid_spec=grid_spec,
        out_shape=jax.ShapeDtypeStruct((rows, D_MODEL), F32),
        compiler_params=_cparams(("parallel", "arbitrary")),
    )(cnt, x, h, comb, rank, rank_t, gate, g_final.reshape(1, D_MODEL), wg, wu, wd)


def kernel(x_prompt, x_sample, cache_na_k, cache_na_v, state_wkv, c, c_ctx, mod_w, mod_b, norm_mix, norm_ffn, norm_final, na_w_in, fourier_w, na_rel_bias, na_w_out, ffn_w_gate, ffn_w_up, ffn_w_down, rw_w_in, pool_w, pool_scale, shift_mu, decay_w0, decay_up, iclr_a0, iclr_up, gate_up, k_k, k_a, r_k, gn_w, gn_b, rw_w_out, router_w, router_b, moe_w_gate, moe_w_up, moe_w_down):
    cond = jnp.concatenate([c_ctx[None, :], c, jnp.zeros((8 - N_SETS, D_MODEL), F32)], axis=0)
    mods = adaln_all(cond, mod_w, mod_b)[:, :N_SETS].reshape(DEPTH, N_SETS, 6, 1, D_MODEL)
    bf = lambda w: w.astype(BF16)

    xp = x_prompt.reshape(P_ROWS, D_MODEL)
    xs = x_sample.reshape(S_ROWS, D_MODEL)
    streams = {"p": (SEQ, P_ROWS, slice(0, 1)), "s": (DEC_SEQ, DEC_SEQ, slice(1, N_SETS))}
    x = {"p": xp, "s": xs}

    splits = ((0, FOURIER_CH), (FOURIER_CH, FOURIER_CH + NA_DIM),
              (FOURIER_CH + NA_DIM, FOURIER_CH + 2 * NA_DIM), (FOURIER_CH + 2 * NA_DIM, FOURIER_CH + 3 * NA_DIM))
    w_in, w_out = bf(na_w_in[0]), bf(na_w_out[0])
    f_bd = bf(_block_diag(fourier_w[0]))
    ffn_w = (bf(ffn_w_gate[0]), bf(ffn_w_up[0]), bf(ffn_w_down[0]))
    ck = cache_na_k[:, 0].reshape(DEC_BATCH * PAST_LEN, NA_DIM)
    cv = cache_na_v[:, 0].reshape(DEC_BATCH * PAST_LEN, NA_DIM)
    for name, (n, rps, sets) in streams.items():
        sh1, sc1, g1, sh2, sc2, g2 = [mods[0, sets, m] for m in range(6)]
        kv_dtype = F32 if name == "p" else BF16
        f, q, k, v = modulated_matmul(x[name], norm_mix[0], sh1, sc1, w_in, splits, (BF16, BF16, kv_dtype, kv_dtype), rps)
        if name == "p":
            attn = context_attention(q, k, v)
            new_k = k.reshape(BATCH, 1, SEQ, NA_HEADS, HEAD_DIM)
            new_v = v.reshape(BATCH, 1, SEQ, NA_HEADS, HEAD_DIM)
        else:
            attn = neighbourhood_attention(q, k, v, ck, cv, _na_bias_table(na_rel_bias[0]))
        x[name] = mixer_ffn_residual(fourier_mix(f, n, f_bd), attn, x[name], g1, w_out,
                                     norm_ffn[0], sh2, sc2, g2, *ffn_w, rps)

    w_in, w_out = bf(rw_w_in[0]), bf(rw_w_out[0])
    p_bd = bf(_block_diag(pool_w[0]))
    moe_w = (bf(moe_w_gate[0]), bf(moe_w_up[0]), bf(moe_w_down[0]))
    rw = (shift_mu[0], k_k[0], k_a[0], r_k[0], decay_w0[0], iclr_a0[0], decay_up[0], iclr_up[0], gate_up[0])
    out = {}
    for name, (n, rps, sets) in streams.items():
        sh1, sc1, g1, sh2, sc2, g2 = [mods[1, sets, m] for m in range(6)]
        pc, z = modulated_matmul(x[name], norm_mix[1], sh1, sc1, w_in, ((0, POOL_CH), (POOL_CH, POOL_CH + RWKV_IN)),
                                 (F32, F32), rps)
        t1, t2, t3, gate, bonus = rwkv_prep(z, n, *rw)
        if name == "p":
            o_f, st_f = wkv_scan_prompt(t1, t2, t3, 0)
            o_b, st_b = wkv_scan_prompt(t1, t2, t3, 1)
            st = jnp.transpose(jnp.stack([st_f, st_b]), (0, 1, 4, 3, 2))
            st = jnp.transpose(st.reshape(2, BATCH, RWKV_HEADS, HEAD_DIM, HEAD_DIM), (1, 0, 2, 3, 4))
        else:
            o_f, o_b = wkv_scan_sample(t1, t2, t3, _sample_state_lanes(state_wkv[:, 0]))
        mixed = rwkv_post(o_f, o_b, n, gate, bonus, gn_w[0], gn_b[0])
        y, *routed = mixer_residual_router(pool_mix(pc, n, p_bd, pool_scale[0]), mixed, x[name], g1, w_out,
                                           norm_ffn[1], sh2, sc2, router_w[0], router_b[0], rps)
        out[name] = moe_residual_norm(y, *routed, g2, norm_final, *moe_w, rps)

    return (out["p"].reshape(BATCH, SEQ, D_MODEL), out["s"].reshape(DEC_BATCH, DEC_SEQ, D_MODEL),
            new_k, new_v, st[:, None])
```

```python
import functools
import math

import numpy as np
import jax
import jax.numpy as jnp
from jax import lax
from jax.experimental import pallas as pl
from jax.experimental.pallas import tpu as pltpu

F32 = jnp.float32
BF16 = jnp.bfloat16

D_MODEL = 1024
BATCH = 32
SEQ = 256
DEPTH = 2
DEC_BATCH = 2
DEC_SEQ = 1024
PAST_LEN = 512
GRID_W = 64
HEAD_DIM = 64
FOURIER_CH = D_MODEL // 4
FOURIER_GROUPS = 4
FOURIER_GW = FOURIER_CH // FOURIER_GROUPS
NA_DIM = D_MODEL - FOURIER_CH
NA_HEADS = NA_DIM // HEAD_DIM
NA_MAX_ROWS = 8
NA_COLS = 16
POOL_WINDOWS = (2, 4, 8, 16)
POOL_CH = D_MODEL // 4
POOL_GW = POOL_CH // len(POOL_WINDOWS)
RWKV_DIM = D_MODEL - POOL_CH
RWKV_HEADS = RWKV_DIM // HEAD_DIM
DECAY_LORA = 64
ICLR_LORA = 64
GATE_LORA = 128
RWKV_IN = 3 * RWKV_DIM + 2 * DECAY_LORA + 2 * ICLR_LORA + GATE_LORA
D_FF = 2816
N_EXPERTS = 8
D_FF_EXPERT = 1408
RMS_EPS = 1e-6
GN_EPS = 64e-5
L2_EPS = 1e-12
DECAY_SCALE = math.exp(-0.5)
NEG_INF = -1e30

P_ROWS = BATCH * SEQ
S_ROWS = DEC_BATCH * DEC_SEQ
N_ROWS = P_ROWS + S_ROWS
N_SETS = 1 + DEC_BATCH
LANES = 128
VMEM_LIMIT = 56 * 1024 * 1024


def _cparams(sem):
    return pltpu.CompilerParams(dimension_semantics=sem, vmem_limit_bytes=VMEM_LIMIT)


def _sigmoid(x):
    return 0.5 * jnp.tanh(0.5 * x) + 0.5


def _dot(a, b):
    return jnp.dot(a, b, preferred_element_type=F32)


def _dot_nt(a, b):
    return lax.dot_general(a, b, (((1,), (1,)), ((), ())), preferred_element_type=F32)


def _set_index(tm, rows_per_set):
    q = rows_per_set // tm
    return lambda i: i // q


def _modulated(x, g, sh, sc):
    ms = jnp.mean(x * x, axis=-1, keepdims=True)
    return (x * lax.rsqrt(ms + RMS_EPS) * g) * (1.0 + sc) + sh


def _adaln_kernel(c_ref, w_ref, b_ref, o_ref):
    c = c_ref[...]
    s = (c * _sigmoid(c)).astype(BF16)
    o_ref[0] = _dot(s, w_ref[0].astype(BF16)) + b_ref[0]


def adaln_all(cond, mod_w, mod_b):
    tn = 1536
    n = 6 * D_MODEL
    return pl.pallas_call(
        _adaln_kernel, name="adaln",
        grid=(DEPTH, n // tn),
        in_specs=[pl.BlockSpec((8, D_MODEL), lambda l, j: (0, 0)),
                  pl.BlockSpec((1, D_MODEL, tn), lambda l, j: (l, 0, j)),
                  pl.BlockSpec((1, 1, tn), lambda l, j: (l, 0, j))],
        out_specs=pl.BlockSpec((1, 8, tn), lambda l, j: (l, 0, j)),
        out_shape=jax.ShapeDtypeStruct((DEPTH, 8, n), F32),
        compiler_params=_cparams(("parallel", "parallel")),
    )(cond, mod_w, mod_b.reshape(DEPTH, 1, n))


def _modmm_kernel(x_ref, g_ref, sh_ref, sc_ref, w_ref, *o_refs, splits):
    h = _modulated(x_ref[...], g_ref[...], sh_ref[0], sc_ref[0]).astype(BF16)
    for o_ref, (a, b) in zip(o_refs, splits):
        o_ref[...] = _dot(h, w_ref[:, a:b]).astype(o_ref.dtype)


def modulated_matmul(x, g, sh, sc, w, splits, dtypes, rows_per_set, tm=512):
    rows = x.shape[0]
    n_out = w.shape[1]
    si = _set_index(tm, rows_per_set)
    vec = pl.BlockSpec((1, 1, D_MODEL), lambda i: (si(i), 0, 0))
    return pl.pallas_call(
        functools.partial(_modmm_kernel, splits=splits), name="modulated_matmul",
        grid=(rows // tm,),
        in_specs=[pl.BlockSpec((tm, D_MODEL), lambda i: (i, 0)),
                  pl.BlockSpec((1, D_MODEL), lambda i: (0, 0)),
                  vec, vec,
                  pl.BlockSpec((D_MODEL, n_out), lambda i: (0, 0))],
        out_specs=[pl.BlockSpec((tm, b - a), lambda i: (i, 0)) for a, b in splits],
        out_shape=[jax.ShapeDtypeStruct((rows, b - a), dt) for (a, b), dt in zip(splits, dtypes)],
        compiler_params=_cparams(("parallel",)),
    )(x, g.reshape(1, D_MODEL), sh, sc, w)


def _dft_mats(n):
    t = np.arange(n)
    ang = 2.0 * np.pi * ((t[:, None] * t[None, :]) % n) / n
    cn, sn = np.cos(ang) / np.sqrt(n), np.sin(ang) / np.sqrt(n)
    c = np.arange(FOURIER_GW)
    angc = 2.0 * np.pi * ((c[:, None] * c[None, :]) % FOURIER_GW) / FOURIER_GW
    eye = np.eye(FOURIER_GROUPS)
    cc = np.kron(eye, np.cos(angc) / np.sqrt(FOURIER_GW))
    sc = np.kron(eye, np.sin(angc) / np.sqrt(FOURIER_GW))
    as_bf = lambda a: jnp.asarray(a, dtype=F32).astype(BF16)
    return as_bf(cn), as_bf(sn), as_bf(cc), as_bf(sc)


def _fourier_kernel(f_ref, cn_ref, sn_ref, cc_ref, sc_ref, w_ref, o_ref):
    x = f_ref[...].astype(BF16)
    a = _dot(x, cc_ref[...]).astype(BF16)
    b = _dot(x, sc_ref[...]).astype(BF16)
    re = _dot(cn_ref[...], a) - _dot(sn_ref[...], b)
    o_ref[...] = _dot(re.astype(BF16), w_ref[...]).astype(o_ref.dtype)


def _block_diag(w):
    g, c, _ = w.shape
    eye = jnp.eye(g, dtype=w.dtype)
    return (eye[:, None, :, None] * w[:, :, None, :]).reshape(g * c, g * c)


def fourier_mix(f, n, w_bd):
    rows = f.shape[0]
    cn, sn, cc, sc = _dft_mats(n)
    full = lambda shape: pl.BlockSpec(shape, lambda b: (0, 0))
    return pl.pallas_call(
        _fourier_kernel, name="fourier",
        grid=(rows // n,),
        in_specs=[pl.BlockSpec((n, FOURIER_CH), lambda b: (b, 0)),
                  full((n, n)), full((n, n)),
                  full((FOURIER_CH, FOURIER_CH)), full((FOURIER_CH, FOURIER_CH)),
                  full((FOURIER_CH, FOURIER_CH))],
        out_specs=pl.BlockSpec((n, FOURIER_CH), lambda b: (b, 0)),
        out_shape=jax.ShapeDtypeStruct((rows, FOURIER_CH), BF16),
        compiler_params=_cparams(("parallel",)),
    )(f, cn, sn, cc, sc, w_bd)


def _head_pair(ref, c):
    x = ref[:, c * LANES:(c + 1) * LANES]
    low = lax.broadcasted_iota(jnp.int32, x.shape, 1) < HEAD_DIM
    return jnp.where(low, x, 0.0).astype(BF16), jnp.where(low, 0.0, x).astype(BF16), low


def _value_pair(ref, c):
    x = ref[:, c * LANES:(c + 1) * LANES]
    low = lax.broadcasted_iota(jnp.int32, x.shape, 1) < HEAD_DIM
    return jnp.where(low, x, 1.0).astype(BF16), jnp.where(low, 1.0, x).astype(BF16)


def _normalised_pair(res_even, res_odd, low):
    even = res_even / pltpu.roll(res_even, HEAD_DIM, 1)
    odd = res_odd / pltpu.roll(res_odd, HEAD_DIM, 1)
    return jnp.where(low, even, odd)


def _ctx_attn_kernel(q_ref, k_ref, v_ref, o_ref):
    scale = HEAD_DIM ** -0.5

    def scores(c):
        q_even, q_odd, _ = _head_pair(q_ref, c)
        k = k_ref[:, c * LANES:(c + 1) * LANES].astype(BF16)
        return _dot_nt(q_even, k) * scale, _dot_nt(q_odd, k) * scale

    def softmax_numerator(s):
        return jnp.exp(s - jnp.max(s, axis=-1, keepdims=True)).astype(BF16)

    n_pairs = NA_HEADS // 2
    pending = [scores(0)]
    for c in range(n_pairs):
        if c + 1 < n_pairs:
            pending.append(scores(c + 1))
        s_even, s_odd = pending.pop(0)
        v_even, v_odd = _value_pair(v_ref, c)
        low = lax.broadcasted_iota(jnp.int32, (SEQ, LANES), 1) < HEAD_DIM
        o_ref[:, c * LANES:(c + 1) * LANES] = _normalised_pair(
            _dot(softmax_numerator(s_even), v_even), _dot(softmax_numerator(s_odd), v_odd), low).astype(o_ref.dtype)


def context_attention(q, k, v):
    blk = pl.BlockSpec((SEQ, NA_DIM), lambda b: (b, 0))
    return pl.pallas_call(
        _ctx_attn_kernel, name="ctx_attn",
        grid=(BATCH,),
        in_specs=[blk, blk, blk],
        out_specs=blk,
        out_shape=jax.ShapeDtypeStruct((P_ROWS, NA_DIM), BF16),
        compiler_params=_cparams(("parallel",)),
    )(q, k, v)


NA_ROWS = DEC_SEQ // GRID_W
NA_WIN = NA_MAX_ROWS * GRID_W


def _na_bias_table(rel_bias):
    cols = np.arange(GRID_W)
    c0 = np.clip(cols - NA_COLS // 2, 0, GRID_W - NA_COLS)
    col_ok = (cols[None, :] >= c0[:, None]) & (cols[None, :] < c0[:, None] + NA_COLS)
    dc = np.clip(cols[None, :] - cols[:, None] + NA_COLS - 1, 0, 2 * NA_COLS - 2)
    onehot = (dc[None] == np.arange(2 * NA_COLS - 1)[:, None, None]).astype(np.float32)
    toe = jnp.einsum("hrj,jqk->hrqk", rel_bias, jnp.asarray(onehot), precision=lax.Precision.HIGHEST)
    toe = jnp.where(col_ok[None, None], toe, NEG_INF)
    tabs = [jnp.transpose(toe[:, NA_MAX_ROWS - 1 - o: 2 * NA_MAX_ROWS - 1 - o], (0, 2, 1, 3))
            for o in range(NA_MAX_ROWS)]
    return jnp.stack(tabs).reshape(NA_MAX_ROWS, NA_HEADS, GRID_W, NA_WIN)


def _na_row_start(i):
    return jnp.clip(i - NA_MAX_ROWS // 2, 0, NA_ROWS - NA_MAX_ROWS)


def _na_kernel(q_ref, k_ref, v_ref, ck_ref, cv_ref, bias_ref, o_ref):
    scale = HEAD_DIM ** -0.5
    i = pl.program_id(1)
    start = pl.multiple_of(_na_row_start(i) * GRID_W, GRID_W)
    k_win = k_ref.at[pl.ds(start, NA_WIN)]
    v_win = v_ref.at[pl.ds(start, NA_WIN)]

    def scores(c):
        q_even, q_odd, _ = _head_pair(q_ref, c)
        kw = k_win[:, c * LANES:(c + 1) * LANES].astype(BF16)
        ck = ck_ref[:, c * LANES:(c + 1) * LANES].astype(BF16)
        return [(_dot_nt(q, kw) * scale + bias_ref[0, 2 * c + par], _dot_nt(q, ck) * scale)
                for par, q in enumerate((q_even, q_odd))]

    def weighted_values(s, vw, cv):
        s_loc, s_ctx = s
        m = jnp.maximum(jnp.max(s_loc, axis=-1, keepdims=True), jnp.max(s_ctx, axis=-1, keepdims=True))
        return _dot(jnp.exp(s_loc - m).astype(BF16), vw) + _dot(jnp.exp(s_ctx - m).astype(BF16), cv)

    n_pairs = NA_HEADS // 2
    pending = [scores(0)]
    for c in range(n_pairs):
        if c + 1 < n_pairs:
            pending.append(scores(c + 1))
        s_even, s_odd = pending.pop(0)
        vw_even, vw_odd = _value_pair(v_win, c)
        cv_even, cv_odd = _value_pair(cv_ref, c)
        low = lax.broadcasted_iota(jnp.int32, (GRID_W, LANES), 1) < HEAD_DIM
        o_ref[:, c * LANES:(c + 1) * LANES] = _normalised_pair(
            weighted_values(s_even, vw_even, cv_even), weighted_values(s_odd, vw_odd, cv_odd), low).astype(o_ref.dtype)


def neighbourhood_attention(q, k, v, ck, cv, bias_tab):
    seq = pl.BlockSpec((DEC_SEQ, NA_DIM), lambda b, i: (b, 0))
    ctx = pl.BlockSpec((PAST_LEN, NA_DIM), lambda b, i: (b, 0))
    row = pl.BlockSpec((GRID_W, NA_DIM), lambda b, i: (b * NA_ROWS + i, 0))
    return pl.pallas_call(
        _na_kernel, name="na_attn",
        grid=(DEC_BATCH, NA_ROWS),
        in_specs=[row, seq, seq, ctx, ctx,
                  pl.BlockSpec((1, NA_HEADS, GRID_W, NA_WIN), lambda b, i: (i - _na_row_start(i), 0, 0, 0))],
        out_specs=row,
        out_shape=jax.ShapeDtypeStruct((S_ROWS, NA_DIM), BF16),
        compiler_params=_cparams(("parallel", "arbitrary")),
    )(q, k, v, ck, cv, bias_tab)


def _mixer_residual(a_ref, b_ref, x_ref, gate_ref, wo_ref):
    na = a_ref.shape[1]
    y = _dot(a_ref[...].astype(BF16), wo_ref[:na, :]) + _dot(b_ref[...].astype(BF16), wo_ref[na:, :])
    return x_ref[...] + gate_ref[0] * y


def _ffn_kernel(a_ref, b_ref, x_ref, gate1_ref, wo_ref, g_ref, sh_ref, sc_ref, gate2_ref, wg_ref, wu_ref, wd_ref,
                o_ref, y_scr, h_scr, acc_scr):
    j = pl.program_id(1)

    @pl.when(j == 0)
    def _():
        y = _mixer_residual(a_ref, b_ref, x_ref, gate1_ref, wo_ref)
        y_scr[...] = y
        h_scr[...] = _modulated(y, g_ref[...], sh_ref[0], sc_ref[0]).astype(BF16)
        acc_scr[...] = jnp.zeros_like(acc_scr)

    h = h_scr[...]
    gt = _dot(h, wg_ref[...])
    act = (gt * _sigmoid(gt)) * _dot(h, wu_ref[...])
    acc_scr[...] += _dot(act.astype(BF16), wd_ref[...])

    @pl.when(j == pl.num_programs(1) - 1)
    def _():
        o_ref[...] = y_scr[...] + gate2_ref[0] * acc_scr[...]


def mixer_ffn_residual(a, b, x, gate1, w_out, g, sh, sc, gate2, wg, wu, wd, rows_per_set, tm=512, tf=D_FF):
    rows = x.shape[0]
    na, nb = a.shape[1], b.shape[1]
    si = _set_index(tm, rows_per_set)
    vec = pl.BlockSpec((1, 1, D_MODEL), lambda i, j: (si(i), 0, 0))
    row = lambda n: pl.BlockSpec((tm, n), lambda i, j: (i, 0))
    return pl.pallas_call(
        _ffn_kernel, name="ffn",
        grid=(rows // tm, D_FF // tf),
        in_specs=[row(na), row(nb), row(D_MODEL), vec,
                  pl.BlockSpec((na + nb, D_MODEL), lambda i, j: (0, 0)),
                  pl.BlockSpec((1, D_MODEL), lambda i, j: (0, 0)),
                  vec, vec, vec,
                  pl.BlockSpec((D_MODEL, tf), lambda i, j: (0, j)),
                  pl.BlockSpec((D_MODEL, tf), lambda i, j: (0, j)),
                  pl.BlockSpec((tf, D_MODEL), lambda i, j: (j, 0))],
        out_specs=row(D_MODEL),
        out_shape=jax.ShapeDtypeStruct((rows, D_MODEL), F32),
        scratch_shapes=[pltpu.VMEM((tm, D_MODEL), F32), pltpu.VMEM((tm, D_MODEL), BF16),
                        pltpu.VMEM((tm, D_MODEL), F32)],
        compiler_params=_cparams(("parallel", "arbitrary")),
    )(a, b, x, gate1, w_out, g.reshape(1, D_MODEL), sh, sc, gate2, wg, wu, wd)


def _pool_consts(n):
    t = np.arange(n)
    mats, cnts = [], []
    for win in POOL_WINDOWS:
        lo = np.clip(t - win // 2, 0, n)
        hi = np.clip(t + win - win // 2, 0, n)
        mats.append(((t[None, :] >= lo[:, None]) & (t[None, :] < hi[:, None])).astype(np.float32))
        cnts.append(np.repeat((hi - lo).astype(np.float32)[:, None], POOL_GW, axis=1))
    return jnp.asarray(np.stack(mats)).astype(BF16), jnp.asarray(np.concatenate(cnts, axis=1))


def _pool_kernel(x_ref, pm_ref, cnt_ref, w_ref, scale_ref, o_ref):
    x = x_ref[...]
    hi = x.astype(BF16)
    lo = (x - hi.astype(F32)).astype(BF16)
    sums = []
    for g in range(len(POOL_WINDOWS)):
        sl = slice(g * POOL_GW, (g + 1) * POOL_GW)
        sums.append(_dot(pm_ref[g], hi[:, sl]) + _dot(pm_ref[g], lo[:, sl]))
    y = jnp.concatenate(sums, axis=-1) / cnt_ref[...] - x
    o_ref[...] = (_dot(y.astype(BF16), w_ref[...]) * scale_ref[...]).astype(o_ref.dtype)


def pool_mix(x, n, w_bd, scale):
    rows = x.shape[0]
    pm, cnt = _pool_consts(n)
    return pl.pallas_call(
        _pool_kernel, name="pool",
        grid=(rows // n,),
        in_specs=[pl.BlockSpec((n, POOL_CH), lambda b: (b, 0)),
                  pl.BlockSpec((len(POOL_WINDOWS), n, n), lambda b: (0, 0, 0)),
                  pl.BlockSpec((n, POOL_CH), lambda b: (0, 0)),
                  pl.BlockSpec((POOL_CH, POOL_CH), lambda b: (0, 0)),
                  pl.BlockSpec((1, POOL_CH), lambda b: (0, 0))],
        out_specs=pl.BlockSpec((n, POOL_CH), lambda b: (b, 0)),
        out_shape=jax.ShapeDtypeStruct((rows, POOL_CH), BF16),
        compiler_params=_cparams(("parallel",)),
    )(x, pm, cnt, w_bd, scale.reshape(1, POOL_CH))


RW_TILE = 256
HALO = 8
SUB = 8
PACK_R = RWKV_HEADS * SUB


def _head_ones():
    h = np.arange(RWKV_DIM) // HEAD_DIM
    return jnp.asarray((h[:, None] == h[None, :]).astype(np.float32)).astype(BF16)


def _head_sum(x, ones):
    hi = x.astype(BF16)
    lo = (x - hi.astype(F32)).astype(BF16)
    return _dot(hi, ones) + _dot(lo, ones)


def _pack_heads(a, b, o_ref, lead):
    n = a.shape[0]
    lane = lax.broadcasted_iota(jnp.int32, (n, LANES), 1)
    low = lane < HEAD_DIM
    for c in range(RWKV_DIM // LANES):
        ac = a[:, c * LANES:(c + 1) * LANES]
        bc = b[:, c * LANES:(c + 1) * LANES]
        even = jnp.where(low, ac, pltpu.roll(bc, HEAD_DIM, 1))
        odd = jnp.where(low, pltpu.roll(ac, HEAD_DIM, 1), bc)
        for h, val in ((2 * c, even), (2 * c + 1, odd)):
            o_ref[lead + (slice(None), slice(h * SUB, (h + 1) * SUB), slice(None))] = val.reshape(n // SUB, SUB, LANES)


def _rwkv_prep_kernel(z_ref, zp_ref, zn_ref, mu_ref, kk_w_ref, ka_ref, rk_ref, w0_ref, a0_ref,
                      dup_ref, iup_ref, gup_ref, ones_ref,
                      g1_ref, g2_ref, g3_ref, g_ref, bonus_ref, *, tiles_per_seq):
    i = pl.program_id(0)
    pos = i % tiles_per_seq
    z = z_ref[...]
    row = lax.broadcasted_iota(jnp.int32, (HALO, 1), 0)
    prev_edge = jnp.where(pos == 0, 0.0, zp_ref[HALO - 1:HALO, :])
    next_edge = jnp.where(pos == tiles_per_seq - 1, 0.0, zn_ref[0:1, :])
    prev = pltpu.roll(z, 1, 0)
    prev = jnp.concatenate([jnp.where(row == 0, prev_edge, prev[:HALO]), prev[HALO:]], axis=0)
    nxt = pltpu.roll(z, RW_TILE - 1, 0)
    nxt = jnp.concatenate([nxt[:-HALO], jnp.where(row == HALO - 1, next_edge, nxt[-HALO:])], axis=0)
    mu_prev, mu_next = mu_ref[0:1, :], mu_ref[1:2, :]
    zr = z * (1.0 - mu_prev - mu_next) + mu_prev * prev + mu_next * nxt

    d = RWKV_DIM
    r, k, v = zr[:, :d], zr[:, d:2 * d], zr[:, 2 * d:3 * d]
    lora = 3 * d
    ones = ones_ref[...]
    kk = k * kk_w_ref[...]
    kk = kk * lax.rsqrt(_head_sum(kk * kk, ones) + L2_EPS)
    _pack_heads(r, v, g3_ref, ())
    for dr in range(2):
        wl = zr[:, lora + dr * DECAY_LORA: lora + (dr + 1) * DECAY_LORA]
        al = zr[:, lora + 2 * DECAY_LORA + dr * ICLR_LORA: lora + 2 * DECAY_LORA + (dr + 1) * ICLR_LORA]
        lw = w0_ref[dr:dr + 1, :] + _dot(jnp.tanh(wl).astype(BF16), dup_ref[dr])
        w = jnp.exp(-DECAY_SCALE * _sigmoid(lw))
        a = _sigmoid(a0_ref[dr:dr + 1, :] + _dot(al.astype(BF16), iup_ref[dr]))
        _pack_heads(w, kk * a, g1_ref, (dr,))
        _pack_heads(k * (1.0 + (a - 1.0) * ka_ref[...]), kk, g2_ref, (dr,))
    gl = zr[:, lora + 2 * DECAY_LORA + 2 * ICLR_LORA:]
    g_ref[...] = _dot(_sigmoid(gl).astype(BF16), gup_ref[...])
    bonus_ref[...] = _head_sum(r * k * rk_ref[...], ones) * v


def rwkv_prep(z, n, mu, k_k, k_a, r_k, w0, a0, dup, iup, gup):
    rows = z.shape[0]
    nb = rows // n
    tps = n // RW_TILE
    hb = RW_TILE // HALO
    last = rows // HALO - 1
    d = RWKV_DIM
    full2 = lambda shape: pl.BlockSpec(shape, lambda i: (0, 0))
    full3 = lambda shape: pl.BlockSpec(shape, lambda i: (0, 0, 0))
    tile = pl.BlockSpec((RW_TILE, d), lambda i: (i, 0))
    pk2 = pl.BlockSpec((2, RW_TILE // SUB, PACK_R, LANES), lambda i: (0, i % tps, i // tps, 0))
    pk1 = pl.BlockSpec((RW_TILE // SUB, PACK_R, LANES), lambda i: (i % tps, i // tps, 0))
    return pl.pallas_call(
        functools.partial(_rwkv_prep_kernel, tiles_per_seq=tps), name="rwkv_prep",
        grid=(rows // RW_TILE,),
        in_specs=[pl.BlockSpec((RW_TILE, RWKV_IN), lambda i: (i, 0)),
                  pl.BlockSpec((HALO, RWKV_IN), lambda i: (jnp.maximum(i * hb - 1, 0), 0)),
                  pl.BlockSpec((HALO, RWKV_IN), lambda i: (jnp.minimum((i + 1) * hb, last), 0)),
                  full2((2, RWKV_IN)), full2((1, d)), full2((1, d)), full2((1, d)),
                  full2((2, d)), full2((2, d)),
                  full3((2, DECAY_LORA, d)), full3((2, ICLR_LORA, d)), full2((GATE_LORA, d)),
                  full2((d, d))],
        out_specs=[pk2, pk2, pk1, tile, tile],
        out_shape=[jax.ShapeDtypeStruct((2, n // SUB, nb * PACK_R, LANES), F32),
                   jax.ShapeDtypeStruct((2, n // SUB, nb * PACK_R, LANES), F32),
                   jax.ShapeDtypeStruct((n // SUB, nb * PACK_R, LANES), F32),
                   jax.ShapeDtypeStruct((rows, d), F32),
                   jax.ShapeDtypeStruct((rows, d), F32)],
        compiler_params=_cparams(("parallel",)),
    )(z, z, z, mu, k_k.reshape(1, d), k_a.reshape(1, d), r_k.reshape(1, d), w0, a0,
      dup.astype(BF16), iup.astype(BF16), gup.astype(BF16), _head_ones())


SCAN_TC = 64
SCAN_G = SCAN_TC // SUB
SLOTS = 4
V_BLOCK = 32
K_CHUNK = 16
N_KC = HEAD_DIM // K_CHUNK
PEEL = 2


def _wkv_first_sa(s_scr, t2, vs):
    sas = []
    for vb in range(vs // V_BLOCK):
        rows = slice(vb * V_BLOCK, (vb + 1) * V_BLOCK)

        def chunk(kc, sa):
            for j in range(K_CHUNK):
                sa = sa + s_scr[kc * K_CHUNK + j, rows, :] * t2[N_KC + kc, j:j + 1, :]
            return sa

        sas.append(lax.fori_loop(0, N_KC, chunk, jnp.zeros((V_BLOCK, LANES), F32)))
    return tuple(sas)


def _wkv_step(s_scr, t1, t2, t3, t2_next, v_blocks, o_ref, sas):
    nxt = []
    for vb, v_blk in enumerate(v_blocks):
        rows = slice(vb * V_BLOCK, (vb + 1) * V_BLOCK)
        sa = sas[vb]

        def chunk(kc, carry):
            o, sa_n = carry
            for j in range(K_CHUNK):
                k = kc * K_CHUNK + j
                s_new = s_scr[k, rows, :] - sa * t1[N_KC + kc, j:j + 1, :] + v_blk * t2[kc, j:j + 1, :]
                s_scr[k, rows, :] = s_new
                o = o + s_new * t3[kc, j:j + 1, :]
                sa_n = sa_n + s_new * t2_next[N_KC + kc, j:j + 1, :]
            return o, sa_n

        zero = jnp.zeros((V_BLOCK, LANES), F32)
        carry = (zero, zero)
        for kc in range(PEEL):
            carry = chunk(kc, carry)
        o, sa_n = lax.fori_loop(PEEL, N_KC, chunk, carry)
        o_ref[rows, :] = o
        nxt.append(sa_n)
    return tuple(nxt)


def _store_tile(ref, idx, x):
    ref[idx] = x.reshape(2 * N_KC, K_CHUNK, LANES)


def _store_scaled(tile_ref, t1, t2, t3, g, advance):
    d = HEAD_DIM
    g_new = g * t1[:d]
    inv = 1.0 / g_new
    _store_tile(tile_ref, 0, jnp.concatenate([t1[:d], t1[d:] * inv], axis=0))
    _store_tile(tile_ref, 1, jnp.concatenate([t2[:d] * inv, t2[d:] * g], axis=0))
    _store_tile(tile_ref, 2, jnp.concatenate([t3[:d] * g_new, t3[d:]], axis=0))
    return g_new if advance is True else jnp.where(advance, g_new, g)


def _unscale_state(s_scr, g_scr, g):
    g_scr[...] = g.reshape(N_KC, K_CHUNK, LANES)

    def chunk(kc, carry):
        for j in range(K_CHUNK):
            k = kc * K_CHUNK + j
            s_scr[k] = s_scr[k] * g_scr[kc, j:j + 1, :]
        return carry

    lax.fori_loop(0, N_KC, chunk, 0)


def _step_rows(ref, lead, grp, sub, n):
    return ref.at[lead + (grp,)][pl.ds(sub, n, stride=SUB), :]


def _scan_prompt_kernel(g1_ref, g2_ref, g3_ref, o_ref, st_ref, s_scr, g_scr, ta, tb, tc, td, oa, ob, *, reverse):
    c = pl.program_id(1)
    tiles = (ta, tb, tc, td)
    outs = (oa, ob)

    @pl.when(c == 0)
    def _():
        s_scr[...] = jnp.zeros_like(s_scr)

    for o_scr in outs:
        o_scr[...] = jnp.zeros_like(o_scr)

    def where(grp, sub):
        return (SCAN_G - 1 - grp, SUB - 1 - sub) if reverse else (grp, sub)

    def load_tiles(grp, sub, slot, dec, advance=True):
        g, s = where(grp, sub)
        return _store_scaled(tiles[slot], _step_rows(g1_ref, (0,), g, s, LANES).T,
                             _step_rows(g2_ref, (0,), g, s, LANES).T,
                             _step_rows(g3_ref, (), g, s, LANES).T, dec, advance)

    def flush(grp, sub, o_scr):
        g, s = where(grp, sub)
        o_ref.at[0, g][pl.ds(s, LANES, stride=SUB), :] = o_scr[...].T

    dec0 = load_tiles(0, 0, 0, jnp.ones((HEAD_DIM, LANES), F32))
    dec0 = load_tiles(0, 1, 1, dec0)
    sas0 = _wkv_first_sa(s_scr, tiles[0].at[1], HEAD_DIM)
    per_blk = V_BLOCK // K_CHUNK

    def group(grp, carry):
        sas, dec = carry
        for j in range(SUB):
            cur, nxt = tiles[j % SLOTS], tiles[(j + 1) % SLOTS]
            ahead = j + 2
            if ahead < SUB:
                dec = load_tiles(grp, ahead, ahead % SLOTS, dec)
            else:
                dec = load_tiles(jnp.minimum(grp + 1, SCAN_G - 1), ahead % SUB, ahead % SLOTS, dec,
                                 advance=grp + 1 < SCAN_G)
            if j > 0:
                flush(grp, j - 1, outs[(j - 1) % 2])
            v_blocks = [cur[2, N_KC + vb * per_blk: N_KC + (vb + 1) * per_blk].reshape(V_BLOCK, LANES)
                        for vb in range(HEAD_DIM // V_BLOCK)]
            sas = _wkv_step(s_scr, cur.at[0], cur.at[1], cur.at[2], nxt.at[1], v_blocks, outs[j % 2], sas)
        flush(grp, SUB - 1, outs[(SUB - 1) % 2])
        return sas, dec

    _, dec = lax.fori_loop(0, SCAN_G, group, (sas0, dec0))
    _unscale_state(s_scr, g_scr, dec)

    @pl.when(c == pl.num_programs(1) - 1)
    def _():
        st_ref[0] = s_scr[...]


def wkv_scan_prompt(g1, g2, g3, direction):
    n_grp, rows = g3.shape[0], g3.shape[1]
    groups = rows // (LANES * SUB)
    nblk = n_grp // SCAN_G
    tb = (lambda s: nblk - 1 - s) if direction else (lambda s: s)
    blk = (SCAN_G, LANES * SUB, LANES)
    dir_blk = pl.BlockSpec((1,) + blk, lambda g, s: (direction, tb(s), g, 0))
    return pl.pallas_call(
        functools.partial(_scan_prompt_kernel, reverse=bool(direction)), name="wkv_scan_prompt",
        grid=(groups, nblk),
        in_specs=[dir_blk, dir_blk, pl.BlockSpec(blk, lambda g, s: (tb(s), g, 0))],
        out_specs=[pl.BlockSpec((1,) + blk, lambda g, s: (0, tb(s), g, 0)),
                   pl.BlockSpec((1, HEAD_DIM, HEAD_DIM, LANES), lambda g, s: (g, 0, 0, 0))],
        out_shape=[jax.ShapeDtypeStruct((1, n_grp, rows, LANES), F32),
                   jax.ShapeDtypeStruct((groups, HEAD_DIM, HEAD_DIM, LANES), F32)],
        scratch_shapes=([pltpu.VMEM((HEAD_DIM, HEAD_DIM, LANES), F32),
                         pltpu.VMEM((N_KC, K_CHUNK, LANES), F32)]
                        + [pltpu.VMEM((3, 2 * N_KC, K_CHUNK, LANES), F32)] * SLOTS
                        + [pltpu.VMEM((LANES, LANES), F32)] * 2),
        compiler_params=_cparams(("parallel", "arbitrary")),
    )(g1, g2, g3)


S_CHAINS = DEC_BATCH * RWKV_HEADS
S_VS = HEAD_DIM // 2


def _scan_sample_kernel(g1f_ref, g1b_ref, g2f_ref, g2b_ref, g3f_ref, g3b_ref, s0_ref,
                        of_ref, ob_ref, s_scr, g_scr, ta, tb, tc, td, va, vb, vc, vd, oa, ob):
    c = pl.program_id(0)
    tiles = (ta, tb, tc, td)
    vals = (va, vb, vc, vd)
    outs = (oa, ob)

    @pl.when(c == 0)
    def _():
        s_scr[...] = s0_ref[...]

    nc = 2 * S_CHAINS
    n_pad = LANES - 2 * nc
    zpad = jnp.zeros((n_pad, LANES), F32)
    wpad = jnp.where(lax.broadcasted_iota(jnp.int32, (n_pad, LANES), 1) < HEAD_DIM, 1.0, 0.0)
    lane = lax.broadcasted_iota(jnp.int32, (S_VS, LANES), 1)

    def stacked_t(f_ref, b_ref, grp, sub, pad):
        f = _step_rows(f_ref, (0,), grp, sub, S_CHAINS)
        b = _step_rows(b_ref, (0,), SCAN_G - 1 - grp, SUB - 1 - sub, S_CHAINS)
        return jnp.concatenate([f, b, f, b, pad], axis=0).T

    def load_tiles(grp, sub, slot, dec, advance=True):
        t3 = stacked_t(g3f_ref, g3b_ref, grp, sub, zpad)
        vals[slot][...] = jnp.where(lane < nc, t3[HEAD_DIM:HEAD_DIM + S_VS], t3[HEAD_DIM + S_VS:])
        return _store_scaled(tiles[slot], stacked_t(g1f_ref, g1b_ref, grp, sub, wpad),
                             stacked_t(g2f_ref, g2b_ref, grp, sub, zpad), t3, dec, advance)

    def flush(grp, sub, o_scr):
        o = o_scr[...]
        full = jnp.concatenate([o, pltpu.roll(o, LANES - nc, 1), jnp.zeros((LANES - HEAD_DIM, LANES), F32)], axis=0)
        ot = full.T
        of_ref.at[0, grp][pl.ds(sub, S_CHAINS, stride=SUB), :] = ot[0:S_CHAINS]
        ob_ref.at[0, SCAN_G - 1 - grp][pl.ds(SUB - 1 - sub, S_CHAINS, stride=SUB), :] = ot[S_CHAINS:nc]

    dec0 = load_tiles(0, 0, 0, jnp.ones((HEAD_DIM, LANES), F32))
    dec0 = load_tiles(0, 1, 1, dec0)
    sas0 = _wkv_first_sa(s_scr, tiles[0].at[1], S_VS)

    def group(grp, carry):
        sas, dec = carry
        for j in range(SUB):
            cur, nxt = tiles[j % SLOTS], tiles[(j + 1) % SLOTS]
            ahead = j + 2
            if ahead < SUB:
                dec = load_tiles(grp, ahead, ahead % SLOTS, dec)
            else:
                dec = load_tiles(jnp.minimum(grp + 1, SCAN_G - 1), ahead % SUB, ahead % SLOTS, dec,
                                 advance=grp + 1 < SCAN_G)
            if j > 0:
                flush(grp, j - 1, outs[(j - 1) % 2])
            sas = _wkv_step(s_scr, cur.at[0], cur.at[1], cur.at[2], nxt.at[1], [vals[j % SLOTS][...]],
                            outs[j % 2], sas)
        flush(grp, SUB - 1, outs[(SUB - 1) % 2])
        return sas, dec

    _, dec = lax.fori_loop(0, SCAN_G, group, (sas0, dec0))
    _unscale_state(s_scr, g_scr, dec)


def wkv_scan_sample(g1, g2, g3, s0):
    n_grp, rows = g3.shape[0], g3.shape[1]
    g3 = g3.reshape(1, n_grp, rows, LANES)
    nblk = n_grp // SCAN_G
    blk = (1, SCAN_G, rows, LANES)
    fwd = lambda d: pl.BlockSpec(blk, lambda s: (d, s, 0, 0))
    bwd = lambda d: pl.BlockSpec(blk, lambda s: (d, nblk - 1 - s, 0, 0))
    return pl.pallas_call(
        _scan_sample_kernel, name="wkv_scan_sample",
        grid=(nblk,),
        in_specs=[fwd(0), bwd(1), fwd(0), bwd(1), fwd(0), bwd(0),
                  pl.BlockSpec((HEAD_DIM, S_VS, LANES), lambda s: (0, 0, 0))],
        out_specs=[fwd(0), bwd(0)],
        out_shape=[jax.ShapeDtypeStruct((1, n_grp, rows, LANES), F32)] * 2,
        scratch_shapes=([pltpu.VMEM((HEAD_DIM, S_VS, LANES), F32),
                         pltpu.VMEM((N_KC, K_CHUNK, LANES), F32)]
                        + [pltpu.VMEM((3, 2 * N_KC, K_CHUNK, LANES), F32)] * SLOTS
                        + [pltpu.VMEM((S_VS, LANES), F32)] * SLOTS
                        + [pltpu.VMEM((S_VS, LANES), F32)] * 2),
        compiler_params=_cparams(("arbitrary",)),
    )(g1, g1, g2, g2, g3, g3, s0)


def _sample_state_lanes(s0):
    nc = 2 * S_CHAINS
    st = jnp.transpose(s0, (4, 3, 1, 0, 2)).reshape(HEAD_DIM, HEAD_DIM, nc)
    st = jnp.concatenate([st[:, :S_VS], st[:, S_VS:]], axis=-1)
    return jnp.pad(st, ((0, 0), (0, 0), (0, LANES - 2 * nc)))


def _rwkv_post_kernel(of_ref, ob_ref, g_ref, bonus_ref, gw_ref, gb_ref, ones_ref, y_ref):
    head_sum = functools.partial(_head_sum, ones=ones_ref[...])
    lane = lax.broadcasted_iota(jnp.int32, (RW_TILE, LANES), 1)
    low = lane < HEAD_DIM

    def head(h):
        rows = slice(h * SUB, (h + 1) * SUB)
        return (of_ref[0, :, rows, :] + ob_ref[0, :, rows, :]).reshape(RW_TILE, LANES)

    cols = [jnp.where(low, head(2 * c), pltpu.roll(head(2 * c + 1), HEAD_DIM, 1))
            for c in range(RWKV_DIM // LANES)]
    o = jnp.concatenate(cols, axis=-1)
    mu = head_sum(o) / HEAD_DIM
    oc = o - mu
    var = head_sum(oc * oc) / HEAD_DIM
    on = (oc * lax.rsqrt(var + GN_EPS)) * gw_ref[...] + gb_ref[...]
    y_ref[...] = ((on + bonus_ref[...]) * g_ref[...]).astype(y_ref.dtype)


def rwkv_post(o_f, o_b, n, g, bonus, gn_w, gn_b):
    rows, d = g.shape
    tps = n // RW_TILE
    tile = pl.BlockSpec((RW_TILE, d), lambda i: (i, 0))
    vec = pl.BlockSpec((1, d), lambda i: (0, 0))
    pk = pl.BlockSpec((1, RW_TILE // SUB, PACK_R, LANES), lambda i: (0, i % tps, i // tps, 0))
    return pl.pallas_call(
        _rwkv_post_kernel, name="rwkv_post",
        grid=(rows // RW_TILE,),
        in_specs=[pk, pk, tile, tile, vec, vec, pl.BlockSpec((d, d), lambda i: (0, 0))],
        out_specs=tile,
        out_shape=jax.ShapeDtypeStruct((rows, d), BF16),
        compiler_params=_cparams(("parallel",)),
    )(o_f, o_b, g, bonus, gn_w.reshape(1, d), gn_b.reshape(1, d), _head_ones())


MOE_R = 512
MOE_M = 128


def _split3(x):
    a = x.astype(BF16)
    r1 = x - a.astype(F32)
    b = r1.astype(BF16)
    c = (r1 - b.astype(F32)).astype(BF16)
    return a, b, c


def _router_kernel(mix_a_ref, mix_b_ref, x_ref, gate1_ref, wo_ref, g_ref, sh_ref, sc_ref, w_ref, b_ref, tri_ref,
                   y_ref, h_ref, comb_ref, rank_ref, rank_t_ref, cnt_ref):
    y = _mixer_residual(mix_a_ref, mix_b_ref, x_ref, gate1_ref, wo_ref)
    y_ref[...] = y
    h = _modulated(y, g_ref[...], sh_ref[0], sc_ref[0])
    h_ref[...] = h.astype(BF16)
    h1, h2, h3 = _split3(h)
    w1, w2, w3 = _split3(w_ref[...])
    logits = (_dot(h1, w1) + (_dot(h1, w2) + _dot(h2, w1))
              + (_dot(h1, w3) + _dot(h2, w2) + _dot(h3, w1))) + b_ref[...]
    col = lax.broadcasted_iota(jnp.int32, logits.shape, 1)
    logits = jnp.where(col < N_EXPERTS, logits, -jnp.inf)
    m1 = jnp.max(logits, axis=-1, keepdims=True)
    i1 = jnp.min(jnp.where(logits == m1, col, LANES), axis=-1, keepdims=True)
    rest = jnp.where(col == i1, -jnp.inf, logits)
    m2 = jnp.max(rest, axis=-1, keepdims=True)
    i2 = jnp.min(jnp.where(rest == m2, col, LANES), axis=-1, keepdims=True)
    e2 = jnp.exp(m2 - m1)
    den = 1.0 + e2
    comb_ref[...] = jnp.where(col == i1, 1.0 / den, 0.0) + jnp.where(col == i2, e2 / den, 0.0)
    chosen = (col == i1) | (col == i2)
    upto = _dot(tri_ref[...], jnp.where(chosen, 1.0, 0.0).astype(BF16))
    rank = jnp.where(chosen, upto - 1.0, -1.0)
    rank_ref[...] = rank
    rank_t_ref[0] = rank.T[0:N_EXPERTS, :]
    cnt_ref[0] = jnp.broadcast_to(upto[MOE_R - 1:MOE_R, :], (8, LANES))


def mixer_residual_router(a, b, x, gate1, w_out, g, sh, sc, router_w, router_b, rows_per_set):
    rows = x.shape[0]
    na, nb = a.shape[1], b.shape[1]
    tm = MOE_R
    nblk = rows // tm
    si = _set_index(tm, rows_per_set)
    vec = pl.BlockSpec((1, 1, D_MODEL), lambda i: (si(i), 0, 0))
    row = lambda n: pl.BlockSpec((tm, n), lambda i: (i, 0))
    w = jnp.pad(router_w, ((0, 0), (0, LANES - N_EXPERTS)))
    bias = jnp.pad(router_b, (0, LANES - N_EXPERTS)).reshape(1, LANES)
    tri = jnp.asarray(np.tril(np.ones((tm, tm), np.float32))).astype(BF16)
    return pl.pallas_call(
        _router_kernel, name="moe_router",
        grid=(nblk,),
        in_specs=[row(na), row(nb), row(D_MODEL), vec,
                  pl.BlockSpec((na + nb, D_MODEL), lambda i: (0, 0)),
                  pl.BlockSpec((1, D_MODEL), lambda i: (0, 0)),
                  vec, vec,
                  pl.BlockSpec((D_MODEL, LANES), lambda i: (0, 0)),
                  pl.BlockSpec((1, LANES), lambda i: (0, 0)),
                  pl.BlockSpec((tm, tm), lambda i: (0, 0))],
        out_specs=[row(D_MODEL), row(D_MODEL), row(LANES), row(LANES),
                   pl.BlockSpec((1, N_EXPERTS, tm), lambda i: (i, 0, 0)),
                   pl.BlockSpec((1, 8, LANES), lambda i: (i, 0, 0))],
        out_shape=[jax.ShapeDtypeStruct((rows, D_MODEL), F32),
                   jax.ShapeDtypeStruct((rows, D_MODEL), BF16),
                   jax.ShapeDtypeStruct((rows, LANES), F32),
                   jax.ShapeDtypeStruct((rows, LANES), F32),
                   jax.ShapeDtypeStruct((nblk, N_EXPERTS, tm), F32),
                   jax.ShapeDtypeStruct((nblk, 8, LANES), F32)],
        compiler_params=_cparams(("parallel",)),
    )(a, b, x, gate1, w_out, g.reshape(1, D_MODEL), sh, sc, w, bias, tri)


MOE_TM = 1024


def _moe_kernel(cnt_ref, x_ref, h_ref, comb_ref, rank_ref, rank_t_ref, gate_ref, gfin_ref, wg_ref, wu_ref, wd_ref,
                o_ref, acc_scr):
    i = pl.program_id(0)
    e = pl.program_id(1)

    @pl.when(e == 0)
    def _():
        acc_scr[...] = jnp.zeros_like(acc_scr)

    col = lax.broadcasted_iota(jnp.int32, (MOE_R, LANES), 1)
    slot_rows = lax.broadcasted_iota(jnp.int32, (MOE_M, MOE_R), 0).astype(F32)
    slot_cols = lax.broadcasted_iota(jnp.int32, (MOE_R, MOE_M), 1).astype(F32)
    for s in range(MOE_TM // MOE_R):
        blk = slice(s * MOE_R, (s + 1) * MOE_R)
        count = cnt_ref[(i * (MOE_TM // MOE_R) + s) * N_EXPERTS + e]
        for m in range(MOE_R // MOE_M):
            @pl.when(count > m * MOE_M)
            def _():
                take = (rank_t_ref[s, pl.ds(e, 1), :] == slot_rows + float(m * MOE_M))
                hc = _dot(jnp.where(take, 1.0, 0.0).astype(BF16), h_ref[blk, :]).astype(BF16)
                gt = _dot(hc, wg_ref[0])
                act = (gt * _sigmoid(gt)) * _dot(hc, wu_ref[0])
                y = _dot(act.astype(BF16), wd_ref[0]).astype(BF16)
                mine = col == e
                rank_e = jnp.sum(jnp.where(mine, rank_ref[blk, :], 0.0), axis=-1, keepdims=True)
                ce = jnp.sum(jnp.where(mine, comb_ref[blk, :], 0.0), axis=-1, keepdims=True)
                put = rank_e == slot_cols + float(m * MOE_M)
                acc_scr[blk, :] += ce * _dot(jnp.where(put, 1.0, 0.0).astype(BF16), y)

    @pl.when(e == pl.num_programs(1) - 1)
    def _():
        y = x_ref[...] + gate_ref[0] * acc_scr[...]
        ms = jnp.mean(y * y, axis=-1, keepdims=True)
        o_ref[...] = y * lax.rsqrt(ms + RMS_EPS) * gfin_ref[...]


def moe_residual_norm(x, h, comb, rank, rank_t, counts, gate, g_final, wg, wu, wd, rows_per_set):
    rows = x.shape[0]
    tm = MOE_TM
    sub = tm // MOE_R
    si = _set_index(tm, rows_per_set)
    cnt = counts[:, 0, :N_EXPERTS].astype(jnp.int32).reshape(-1)
    grid_spec = pltpu.PrefetchScalarGridSpec(
        num_scalar_prefetch=1,
        grid=(rows // tm, N_EXPERTS),
        in_specs=[pl.BlockSpec((tm, D_MODEL), lambda i, e, c: (i, 0)),
                  pl.BlockSpec((tm, D_MODEL), lambda i, e, c: (i, 0)),
                  pl.BlockSpec((tm, LANES), lambda i, e, c: (i, 0)),
                  pl.BlockSpec((tm, LANES), lambda i, e, c: (i, 0)),
                  pl.BlockSpec((sub, N_EXPERTS, MOE_R), lambda i, e, c: (i, 0, 0)),
                  pl.BlockSpec((1, 1, D_MODEL), lambda i, e, c: (si(i), 0, 0)),
                  pl.BlockSpec((1, D_MODEL), lambda i, e, c: (0, 0)),
                  pl.BlockSpec((1, D_MODEL, D_FF_EXPERT), lambda i, e, c: (e, 0, 0)),
                  pl.BlockSpec((1, D_MODEL, D_FF_EXPERT), lambda i, e, c: (e, 0, 0)),
                  pl.BlockSpec((1, D_FF_EXPERT, D_MODEL), lambda i, e, c: (e, 0, 0))],
        out_specs=pl.BlockSpec((tm, D_MODEL), lambda i, e, c: (i, 0)),
        scratch_shapes=[pltpu.VMEM((tm, D_MODEL), F32)])
    return pl.pallas_call(
        _moe_kernel, name="moe_experts",
        grid_spec=grid_spec,
        out_shape=jax.ShapeDtypeStruct((rows, D_MODEL), F32),
        compiler_params=_cparams(("parallel", "arbitrary")),
    )(cnt, x, h, comb, rank, rank_t, gate, g_final.reshape(1, D_MODEL), wg, wu, wd)


def kernel(x_prompt, x_sample, cache_na_k, cache_na_v, state_wkv, c, c_ctx, mod_w, mod_b, norm_mix, norm_ffn, norm_final, na_w_in, fourier_w, na_rel_bias, na_w_out, ffn_w_gate, ffn_w_up, ffn_w_down, rw_w_in, pool_w, pool_scale, shift_mu, decay_w0, decay_up, iclr_a0, iclr_up, gate_up, k_k, k_a, r_k, gn_w, gn_b, rw_w_out, router_w, router_b, moe_w_gate, moe_w_up, moe_w_down):
    cond = jnp.concatenate([c_ctx[None, :], c, jnp.zeros((8 - N_SETS, D_MODEL), F32)], axis=0)
    mods = adaln_all(cond, mod_w, mod_b)[:, :N_SETS].reshape(DEPTH, N_SETS, 6, 1, D_MODEL)
    bf = lambda w: w.astype(BF16)

    xp = x_prompt.reshape(P_ROWS, D_MODEL)
    xs = x_sample.reshape(S_ROWS, D_MODEL)
    streams = {"p": (SEQ, P_ROWS, slice(0, 1)), "s": (DEC_SEQ, DEC_SEQ, slice(1, N_SETS))}
    x = {"p": xp, "s": xs}

    splits = ((0, FOURIER_CH), (FOURIER_CH, FOURIER_CH + NA_DIM),
              (FOURIER_CH + NA_DIM, FOURIER_CH + 2 * NA_DIM), (FOURIER_CH + 2 * NA_DIM, FOURIER_CH + 3 * NA_DIM))
    w_in, w_out = bf(na_w_in[0]), bf(na_w_out[0])
    f_bd = bf(_block_diag(fourier_w[0]))
    ffn_w = (bf(ffn_w_gate[0]), bf(ffn_w_up[0]), bf(ffn_w_down[0]))
    ck = cache_na_k[:, 0].reshape(DEC_BATCH * PAST_LEN, NA_DIM)
    cv = cache_na_v[:, 0].reshape(DEC_BATCH * PAST_LEN, NA_DIM)
    for name, (n, rps, sets) in streams.items():
        sh1, sc1, g1, sh2, sc2, g2 = [mods[0, sets, m] for m in range(6)]
        kv_dtype = F32 if name == "p" else BF16
        f, q, k, v = modulated_matmul(x[name], norm_mix[0], sh1, sc1, w_in, splits, (BF16, BF16, kv_dtype, kv_dtype), rps)
        if name == "p":
            attn = context_attention(q, k, v)
            new_k = k.reshape(BATCH, 1, SEQ, NA_HEADS, HEAD_DIM)
            new_v = v.reshape(BATCH, 1, SEQ, NA_HEADS, HEAD_DIM)
        else:
            attn = neighbourhood_attention(q, k, v, ck, cv, _na_bias_table(na_rel_bias[0]))
        x[name] = mixer_ffn_residual(fourier_mix(f, n, f_bd), attn, x[name], g1, w_out,
                                     norm_ffn[0], sh2, sc2, g2, *ffn_w, rps)

    w_in, w_out = bf(rw_w_in[0]), bf(rw_w_out[0])
    p_bd = bf(_block_diag(pool_w[0]))
    moe_w = (bf(moe_w_gate[0]), bf(moe_w_up[0]), bf(moe_w_down[0]))
    rw = (shift_mu[0], k_k[0], k_a[0], r_k[0], decay_w0[0], iclr_a0[0], decay_up[0], iclr_up[0], gate_up[0])
    out = {}
    for name, (n, rps, sets) in streams.items():
        sh1, sc1, g1, sh2, sc2, g2 = [mods[1, sets, m] for m in range(6)]
        pc, z = modulated_matmul(x[name], norm_mix[1], sh1, sc1, w_in, ((0, POOL_CH), (POOL_CH, POOL_CH + RWKV_IN)),
                                 (F32, F32), rps)
        t1, t2, t3, gate, bonus = rwkv_prep(z, n, *rw)
        if name == "p":
            o_f, st_f = wkv_scan_prompt(t1, t2, t3, 0)
            o_b, st_b = wkv_scan_prompt(t1, t2, t3, 1)
            st = jnp.transpose(jnp.stack([st_f, st_b]), (0, 1, 4, 3, 2))
            st = jnp.transpose(st.reshape(2, BATCH, RWKV_HEADS, HEAD_DIM, HEAD_DIM), (1, 0, 2, 3, 4))
        else:
            o_f, o_b = wkv_scan_sample(t1, t2, t3, _sample_state_lanes(state_wkv[:, 0]))
        mixed = rwkv_post(o_f, o_b, n, gate, bonus, gn_w[0], gn_b[0])
        y, *routed = mixer_residual_router(pool_mix(pc, n, p_bd, pool_scale[0]), mixed, x[name], g1, w_out,
                                           norm_ffn[1], sh2, sc2, router_w[0], router_b[0], rps)
        out[name] = moe_residual_norm(y, *routed, g2, norm_final, *moe_w, rps)

    return (out["p"].reshape(BATCH, SEQ, D_MODEL), out["s"].reshape(DEC_BATCH, DEC_SEQ, D_MODEL),
            new_k, new_v, st[:, None])
```
